```python
import math
import jax, jax.numpy as jnp
from jax import lax
import numpy as np

D_MODEL = 2048
BATCH = 2
SEQ = 4096
DEPTH = 1
DEC_BATCH = 8
DEC_SEQ = 4
PAST_LEN = 16384
PAGE_SIZE = 128

RET_HEADS = 8
RET_DK = 128
RET_DV = 128
RET_CHUNK = 128
RET_THETA = 10000.0
DSA_HEADS = 8
DSA_KV_HEADS = 2
DSA_DH = 128
IDX_HEADS = 16
IDX_DH = 64
TOPK_MAX = 256
Q_BLOCK = 128
ROPE_THETA = 500000.0
ROPE_FRAC = 4
MEM_LEN = 256
MEM_HEADS = 4
MEM_DH = 128
D_FF = 4 * D_MODEL
EPS = 1e-6

RET_QK = RET_HEADS * RET_DK
RET_VW = RET_HEADS * RET_DV
DSA_QW = DSA_HEADS * DSA_DH
DSA_KVW = DSA_KV_HEADS * DSA_DH
IDX_QW = IDX_HEADS * IDX_DH
IN_SPLITS = (RET_QK, RET_QK, RET_VW, RET_VW, DSA_QW, DSA_KVW, DSA_KVW, IDX_QW, IDX_DH, IDX_HEADS, D_MODEL, D_MODEL)
N_IN = 2 * RET_QK + 2 * RET_VW + DSA_QW + 2 * DSA_KVW + IDX_QW + IDX_DH + IDX_HEADS + 2 * D_MODEL
MEM_W = MEM_HEADS * MEM_DH

kernel_name = 'retention_dsa_gated_hybrid_step'


def rmsnorm(x, g):
    xf = x.astype(jnp.float32)
    y = xf * lax.rsqrt(jnp.mean(xf * xf, axis=-1, keepdims=True) + EPS)
    return (y * g.astype(jnp.float32)).astype(x.dtype)


def rope(x, pos, n_rot, theta):
    half = n_rot // 2
    inv = theta ** (-jnp.arange(half, dtype=jnp.float32) / half)
    ang = pos.astype(jnp.float32)[:, None] * inv[None, :]
    cos = jnp.cos(ang)[:, None, :]
    sin = jnp.sin(ang)[:, None, :]
    xr = x[..., :n_rot].astype(jnp.float32)
    x1, x2 = xr[..., :half], xr[..., half:]
    rot = jnp.concatenate([x1 * cos - x2 * sin, x2 * cos + x1 * sin], axis=-1)
    return jnp.concatenate([rot.astype(x.dtype), x[..., n_rot:]], axis=-1)


def mixer_inputs(u, pos, w_in):
    B, T, _ = u.shape
    h = jnp.einsum('btd,dn->btn', u, w_in)
    parts = []
    o = 0
    for n in IN_SPLITS:
        parts.append(h[..., o:o + n])
        o += n
    rq, rk, rv, rg, dq, dk, dv, iq, ik, iw, ga, gb = parts
    rq = rope(rq.reshape(B, T, RET_HEADS, RET_DK), pos, RET_DK, RET_THETA)
    rk = rope(rk.reshape(B, T, RET_HEADS, RET_DK), pos, RET_DK, RET_THETA) * (RET_DK ** -0.5)
    rv = rv.reshape(B, T, RET_HEADS, RET_DV)
    dq = rope(dq.reshape(B, T, DSA_HEADS, DSA_DH), pos, DSA_DH // ROPE_FRAC, ROPE_THETA)
    dk = rope(dk.reshape(B, T, DSA_KV_HEADS, DSA_DH), pos, DSA_DH // ROPE_FRAC, ROPE_THETA)
    dv = dv.reshape(B, T, DSA_KV_HEADS, DSA_DH)
    iq = rope(iq.reshape(B, T, IDX_HEADS, IDX_DH), pos, IDX_DH // ROPE_FRAC, ROPE_THETA)
    ik = rope(ik.reshape(B, T, 1, IDX_DH), pos, IDX_DH // ROPE_FRAC, ROPE_THETA)[:, :, 0]
    return rq, rk, rv, rg, dq, dk, dv, iq, ik, iw, ga, gb


def retention(q, k, v, state0):
    B, T, H, dk = q.shape
    dv = v.shape[-1]
    C = RET_CHUNK if T % RET_CHUNK == 0 else T
    n = T // C
    log_g = jnp.log1p(-jnp.exp2(-5.0 - jnp.arange(H, dtype=jnp.float32)))
    j = jnp.arange(C, dtype=jnp.float32)
    diff = j[:, None] - j[None, :]
    intra = jnp.where(diff >= 0, jnp.exp(log_g[:, None, None] * jnp.maximum(diff, 0.0)), 0.0)
    q_dec = jnp.exp(log_g[None, :] * (j[:, None] + 1.0))[None, :, :, None]
    k_dec = jnp.exp(log_g[None, :] * (C - 1.0 - j[:, None]))[None, :, :, None]
    c_dec = jnp.exp(log_g * C)[None, :, None, None]

    def to_chunks(a):
        return a.astype(jnp.float32).reshape(B, n, C, H, a.shape[-1]).swapaxes(0, 1)

    def step(S, inp):
        qc, kc, vc = inp
        s = jnp.einsum('bihd,bjhd->bhij', qc, kc) * intra[None]
        o = jnp.einsum('bhij,bjhe->bihe', s, vc) + jnp.einsum('bihd,bhde->bihe', qc, S) * q_dec
        S = S * c_dec + jnp.einsum('bjhd,bjhe->bhde', kc * k_dec, vc)
        return S, o

    S, o = lax.scan(step, state0.astype(jnp.float32), (to_chunks(q), to_chunks(k), to_chunks(v)))
    return o.swapaxes(0, 1).reshape(B, T, H, dv), S


def index_select(iq, iw, ik, qpos, topk):
    L = ik.shape[1]
    dots = jnp.einsum('bthd,bsd->bths', iq, ik, preferred_element_type=jnp.float32)
    score = jnp.einsum('bths,bth->bts', jax.nn.relu(dots), iw.astype(jnp.float32))
    admissible = jnp.arange(L)[None, None, :] <= qpos[None, :, None]
    score = jnp.where(admissible, score, -jnp.inf)
    _, idx = lax.top_k(score, topk)
    valid = idx <= qpos[None, :, None]
    return idx, valid


gather_rows = jax.vmap(lambda a, i: a[i])


def sparse_attend(q, kg, vg, valid):
    B, T, H, dh = q.shape
    hkv = kg.shape[3]
    qg = q.reshape(B, T, hkv, H // hkv, dh)
    logits = jnp.einsum('btkgd,btnkd->btkgn', qg, kg, preferred_element_type=jnp.float32) * (dh ** -0.5)
    logits = jnp.where(valid[:, :, None, None, :], logits, -jnp.inf)
    p = jax.nn.softmax(logits, axis=-1).astype(vg.dtype)
    o = jnp.einsum('btkgn,btnkd->btkgd', p, vg)
    return o.reshape(B, T, H, dh)


def dsa_prompt(q, k, v, iq, ik, iw):
    B, S, H, dh = q.shape
    topk = min(TOPK_MAX, S // 4)

    def block(bi):
        start = bi * Q_BLOCK
        sl = lambda a: lax.dynamic_slice_in_dim(a, start, Q_BLOCK, axis=1)
        qpos = start + jnp.arange(Q_BLOCK, dtype=jnp.int32)
        idx, valid = index_select(sl(iq), sl(iw), ik, qpos, topk)
        return sparse_attend(sl(q), gather_rows(k, idx), gather_rows(v, idx), valid)

    out = lax.map(block, jnp.arange(S // Q_BLOCK, dtype=jnp.int32))
    return out.swapaxes(0, 1).reshape(B, S, H, dh)


def dsa_sample(q, k, v, iq, ik, iw, pool_k, pool_v, pool_ik, page_table):
    DB, T = q.shape[:2]
    L = PAST_LEN + T
    topk = min(TOPK_MAX, L // 4)
    ik_past = pool_ik[page_table].reshape(DB, PAST_LEN, IDX_DH).astype(ik.dtype)
    ik_all = jnp.concatenate([ik_past, ik], axis=1)
    qpos = PAST_LEN + jnp.arange(T, dtype=jnp.int32)
    idx, valid = index_select(iq, iw, ik_all, qpos, topk)
    in_past = idx < PAST_LEN
    pidx = jnp.minimum(idx, PAST_LEN - 1)
    page = pidx // PAGE_SIZE
    off = pidx % PAGE_SIZE
    phys = jnp.take_along_axis(page_table, page.reshape(DB, -1), axis=1).reshape(page.shape)
    nidx = jnp.clip(idx - PAST_LEN, 0, T - 1)
    sel = in_past[..., None, None]
    kg = jnp.where(sel, pool_k[phys, off].astype(k.dtype), gather_rows(k, nidx))
    vg = jnp.where(sel, pool_v[phys, off].astype(v.dtype), gather_rows(v, nidx))
    return sparse_attend(q, kg, vg, valid)


def layer_tail(x, ret_o, rg, dsa_o, ga, gb, mk, mv, gn_ret, w_ret_out, w_dsa_out, w_o,
               g_mem, w_mq, w_mo, g_mlp, w_up, w_down):
    B, T, _ = x.shape
    mu = jnp.mean(ret_o, axis=-1, keepdims=True)
    var = jnp.mean(jnp.square(ret_o - mu), axis=-1, keepdims=True)
    ret_n = ((ret_o - mu) * lax.rsqrt(var + EPS)).reshape(B, T, -1) * gn_ret.astype(jnp.float32)
    ret_y = (jax.nn.silu(rg.astype(jnp.float32)) * ret_n).astype(x.dtype)
    ya = ret_y @ w_ret_out
    yb = dsa_o.reshape(B, T, -1) @ w_dsa_out
    merged = jax.nn.sigmoid(ga) * ya + jax.nn.sigmoid(gb) * yb
    h = x + merged @ w_o
    qm = (rmsnorm(h, g_mem) @ w_mq).reshape(B, T, MEM_HEADS, MEM_DH)
    logits = jnp.einsum('bthd,bmhd->bhtm', qm, mk.astype(qm.dtype), preferred_element_type=jnp.float32) * (MEM_DH ** -0.5)
    p = jax.nn.softmax(logits, axis=-1).astype(qm.dtype)
    om = jnp.einsum('bhtm,bmhd->bthd', p, mv.astype(qm.dtype)).reshape(B, T, -1)
    h = h + om @ w_mo
    u = rmsnorm(h, g_mlp)
    h = h + jnp.square(jax.nn.relu(u @ w_up)) @ w_down
    return h


def setup_inputs(seed: int = 0) -> dict:
    key = jax.random.key(seed)
    ks = iter(list(jax.random.split(key, 40)))
    f32 = jnp.float32
    n_pages = PAST_LEN // PAGE_SIZE
    n_used = DEC_BATCH * n_pages
    n_pool = n_used + n_used // 4

    def nrm(shape, scale):
        return jax.random.normal(next(ks), shape, f32) * scale

    def gain(shape):
        return 1.0 + 0.01 * jax.random.normal(next(ks), shape, f32)

    page_table = jax.random.permutation(next(ks), n_pool)[:n_used].reshape(DEC_BATCH, n_pages).astype(jnp.int32)
    return {
        'x_prompt': nrm((BATCH, SEQ, D_MODEL), 1.0),
        'x_sample': nrm((DEC_BATCH, DEC_SEQ, D_MODEL), 1.0),
        'mem_prompt': nrm((BATCH, MEM_LEN, D_MODEL), 1.0),
        'cache_k': nrm((DEPTH, n_pool, PAGE_SIZE, DSA_KV_HEADS, DSA_DH), 1.0),
        'cache_v': nrm((DEPTH, n_pool, PAGE_SIZE, DSA_KV_HEADS, DSA_DH), 1.0),
        'cache_idx_k': nrm((DEPTH, n_pool, PAGE_SIZE, IDX_DH), 1.0),
        'state_ret': nrm((DEPTH, DEC_BATCH, RET_HEADS, RET_DK, RET_DV), 0.1),
        'cache_mem_k': nrm((DEPTH, DEC_BATCH, MEM_LEN, MEM_HEADS, MEM_DH), 1.0),
        'cache_mem_v': nrm((DEPTH, DEC_BATCH, MEM_LEN, MEM_HEADS, MEM_DH), 1.0),
        'page_table': page_table,
        'g_mix': gain((DEPTH, D_MODEL)),
        'w_in': nrm((DEPTH, D_MODEL, N_IN), D_MODEL ** -0.5),
        'gn_ret': gain((DEPTH, RET_VW)),
        'w_ret_out': nrm((DEPTH, RET_VW, D_MODEL), RET_VW ** -0.5),
        'w_dsa_out': nrm((DEPTH, DSA_QW, D_MODEL), DSA_QW ** -0.5),
        'w_o': nrm((DEPTH, D_MODEL, D_MODEL), D_MODEL ** -0.5),
        'g_mem': gain((DEPTH, D_MODEL)),
        'g_memkv': gain((DEPTH, D_MODEL)),
        'w_mq': nrm((DEPTH, D_MODEL, MEM_W), D_MODEL ** -0.5),
        'w_mk': nrm((DEPTH, D_MODEL, MEM_W), D_MODEL ** -0.5),
        'w_mv': nrm((DEPTH, D_MODEL, MEM_W), D_MODEL ** -0.5),
        'w_mo': nrm((DEPTH, MEM_W, D_MODEL), MEM_W ** -0.5),
        'g_mlp': gain((DEPTH, D_MODEL)),
        'w_up': nrm((DEPTH, D_MODEL, D_FF), D_MODEL ** -0.5),
        'w_down': nrm((DEPTH, D_FF, D_MODEL), D_FF ** -0.5),
        'g_final': gain((D_MODEL,)),
    }


def reference(x_prompt, x_sample, mem_prompt, cache_k, cache_v, cache_idx_k, state_ret, cache_mem_k, cache_mem_v,
              page_table, g_mix, w_in, gn_ret, w_ret_out, w_dsa_out, w_o, g_mem, g_memkv, w_mq, w_mk, w_mv, w_mo,
              g_mlp, w_up, w_down, g_final):
    B, S, _ = x_prompt.shape
    DB, T, _ = x_sample.shape
    M = mem_prompt.shape[1]
    pos_p = jnp.arange(S, dtype=jnp.int32)
    pos_s = PAST_LEN + jnp.arange(T, dtype=jnp.int32)
    hp, hs = x_prompt, x_sample
    sp_l, kp_l, vp_l, ikp_l, mkp_l, mvp_l = [], [], [], [], [], []
    ss_l, ks_l, vs_l, iks_l = [], [], [], []
    for l in range(DEPTH):
        up = rmsnorm(hp, g_mix[l])
        rq, rk, rv, rg, dq, dk, dv, iq, ik, iw, ga, gb = mixer_inputs(up, pos_p, w_in[l])
        ret_o, st = retention(rq, rk, rv, jnp.zeros((B, RET_HEADS, RET_DK, RET_DV), jnp.float32))
        dsa_o = dsa_prompt(dq, dk, dv, iq, ik, iw)
        mn = rmsnorm(mem_prompt, g_memkv[l])
        mk = (mn @ w_mk[l]).reshape(B, M, MEM_HEADS, MEM_DH)
        mv = (mn @ w_mv[l]).reshape(B, M, MEM_HEADS, MEM_DH)
        hp = layer_tail(hp, ret_o, rg, dsa_o, ga, gb, mk, mv, gn_ret[l], w_ret_out[l], w_dsa_out[l], w_o[l],
                        g_mem[l], w_mq[l], w_mo[l], g_mlp[l], w_up[l], w_down[l])
        sp_l.append(st); kp_l.append(dk); vp_l.append(dv); ikp_l.append(ik); mkp_l.append(mk); mvp_l.append(mv)
        us = rmsnorm(hs, g_mix[l])
        rq, rk, rv, rg, dq, dk, dv, iq, ik, iw, ga, gb = mixer_inputs(us, pos_s, w_in[l])
        ret_o, st = retention(rq, rk, rv, state_ret[l])
        dsa_o = dsa_sample(dq, dk, dv, iq, ik, iw, cache_k[l], cache_v[l], cache_idx_k[l], page_table)
        hs = layer_tail(hs, ret_o, rg, dsa_o, ga, gb, cache_mem_k[l], cache_mem_v[l], gn_ret[l], w_ret_out[l],
                        w_dsa_out[l], w_o[l], g_mem[l], w_mq[l], w_mo[l], g_mlp[l], w_up[l], w_down[l])
        ss_l.append(st); ks_l.append(dk); vs_l.append(dv); iks_l.append(ik)
    y_prompt = rmsnorm(hp, g_final)
    y_sample = rmsnorm(hs, g_final)
    ret_state_prompt = jnp.stack(sp_l)
    k_rows_prompt = jnp.stack(kp_l)
    v_rows_prompt = jnp.stack(vp_l)
    idx_k_rows_prompt = jnp.stack(ikp_l)
    mem_k_prompt = jnp.stack(mkp_l)
    mem_v_prompt = jnp.stack(mvp_l)
    ret_state_sample = jnp.stack(ss_l)
    k_rows_sample = jnp.stack(ks_l)
    v_rows_sample = jnp.stack(vs_l)
    idx_k_rows_sample = jnp.stack(iks_l)
    return (y_prompt, y_sample, ret_state_prompt, k_rows_prompt, v_rows_prompt, idx_k_rows_prompt,
            mem_k_prompt, mem_v_prompt, ret_state_sample, k_rows_sample, v_rows_sample, idx_k_rows_sample)
```

```python
import functools
import math

import jax
import jax.numpy as jnp
from jax import lax
from jax.experimental import pallas as pl
from jax.experimental.pallas import tpu as pltpu

F32 = jnp.float32
BF16 = jnp.bfloat16

D_MODEL = 2048
RET_HEADS = 8
HEAD_DIM = 128
DSA_HEADS = 8
DSA_KV_HEADS = 2
IDX_HEADS = 16
IDX_DH = 64
TOPK_MAX = 256
PAGE_SIZE = 128
MEM_LEN = 256
MEM_HEADS = 4
MEM_W = MEM_HEADS * HEAD_DIM
D_FF = 4 * D_MODEL
RET_THETA = 10000.0
ROPE_THETA = 500000.0
EPS = 1e-6
LANES = 128
SUBLANES = 8
VMEM_LIMIT = 56 * 1024 * 1024
NEG = -1e30
INT_MIN = -(2 ** 31)

COL_GA = 0
COL_GB = 2048
COL_RQ = 4096
COL_RK = 5120
COL_RV = 6144
COL_RG = 7168
COL_DQ = 8192
COL_IQ = 9216
COL_DK = 10240
COL_DV = 10496
COL_IK = 10752
N_PROJ = 11264
PROJ_TN = 512
MODE_PLAIN, MODE_RET, MODE_RETK, MODE_DSA, MODE_IDX, MODE_DKV, MODE_IKW = range(7)
PROJ_MODES = ((MODE_PLAIN,) * 8 + (MODE_RET,) * 2 + (MODE_RETK,) * 2 + (MODE_PLAIN,) * 4
              + (MODE_DSA,) * 2 + (MODE_IDX,) * 2 + (MODE_DKV, MODE_IKW))
TAB_RET_C, TAB_RET_S, TAB_DSA_C, TAB_DSA_A, TAB_DSA_B, TAB_IDX_C, TAB_IDX_A, TAB_IDX_B = range(8)
TAB_W = 8 * LANES


def _cparams(*sem):
    return pltpu.CompilerParams(dimension_semantics=sem, vmem_limit_bytes=VMEM_LIMIT)


def _dot(a, b):
    return jnp.dot(a, b, preferred_element_type=F32)


def _dot_nt(a, b):
    return lax.dot_general(a, b, (((1,), (1,)), ((), ())), preferred_element_type=F32)


def _dot_tn(a, b):
    return lax.dot_general(a, b, (((0,), (0,)), ((), ())), preferred_element_type=F32)


def _rmsnorm_bf16(x, g):
    ms = jnp.mean(x * x, axis=-1, keepdims=True)
    return ((x * lax.rsqrt(ms + EPS)) * g).astype(BF16)


def _sigmoid(x):
    return 1.0 / (1.0 + jnp.exp(-x))


def _tab(tab_ref, which):
    return tab_ref[:, which * LANES:(which + 1) * LANES]


def _rope_cols(o_ref, g, c, sa, sb, shift):
    a = o_ref[:, g * LANES:(g + 1) * LANES]
    if sb is None:
        r = a * c + pltpu.roll(a, shift, 1) * sa
    else:
        r = a * c + pltpu.roll(a, LANES - shift, 1) * sa + pltpu.roll(a, shift, 1) * sb
    o_ref[:, g * LANES:(g + 1) * LANES] = r


def _norm_proj_kernel(x_ref, g_ref, w_ref, tab_ref, o_ref, u_ref, *, modes):
    j = pl.program_id(1)

    @pl.when(j == 0)
    def _():
        u_ref[...] = _rmsnorm_bf16(x_ref[...], g_ref[...])

    o_ref[...] = _dot(u_ref[...], w_ref[...])
    if tab_ref is None:
        return
    groups = o_ref.shape[1] // LANES

    def blocks_of(mode):
        return [b for b, m in enumerate(modes) if m == mode]

    def when_mode(mode):
        bs = blocks_of(mode)
        return pl.when((j >= bs[0]) & (j <= bs[-1]))

    @when_mode(MODE_RET)
    def _():
        for g in range(groups):
            _rope_cols(o_ref, g, _tab(tab_ref, TAB_RET_C), _tab(tab_ref, TAB_RET_S), None, LANES // 2)

    @when_mode(MODE_RETK)
    def _():
        scale = HEAD_DIM ** -0.5
        for g in range(groups):
            _rope_cols(o_ref, g, _tab(tab_ref, TAB_RET_C) * scale, _tab(tab_ref, TAB_RET_S) * scale, None,
                       LANES // 2)

    @when_mode(MODE_DSA)
    def _():
        for g in range(groups):
            _rope_cols(o_ref, g, _tab(tab_ref, TAB_DSA_C), _tab(tab_ref, TAB_DSA_A), _tab(tab_ref, TAB_DSA_B), 16)

    @when_mode(MODE_IDX)
    def _():
        for g in range(groups):
            _rope_cols(o_ref, g, _tab(tab_ref, TAB_IDX_C), _tab(tab_ref, TAB_IDX_A), _tab(tab_ref, TAB_IDX_B), 8)

    @when_mode(MODE_DKV)
    def _():
        for g in range(2):
            _rope_cols(o_ref, g, _tab(tab_ref, TAB_DSA_C), _tab(tab_ref, TAB_DSA_A), _tab(tab_ref, TAB_DSA_B), 16)

    @when_mode(MODE_IKW)
    def _():
        lane = lax.broadcasted_iota(jnp.int32, (1, LANES), 1)
        is_ik = lane < IDX_DH
        c = jnp.where(is_ik, _tab(tab_ref, TAB_IDX_C), 1.0)
        sa = jnp.where(is_ik, _tab(tab_ref, TAB_IDX_A), 0.0)
        sb = jnp.where(is_ik, _tab(tab_ref, TAB_IDX_B), 0.0)
        _rope_cols(o_ref, 0, c, sa, sb, 8)


def _norm_proj(x, g, w, tab, modes, tm, tn, pos_blocks):
    m, d = x.shape
    n = w.shape[1]
    grid = (m // tm, n // tn)
    in_specs = [
        pl.BlockSpec((tm, d), lambda i, j: (i, 0)),
        pl.BlockSpec((1, d), lambda i, j: (0, 0)),
        pl.BlockSpec((d, tn), lambda i, j: (0, j)),
    ]
    args = [x, g, w]
    if tab is not None:
        in_specs.append(pl.BlockSpec((tm, TAB_W), lambda i, j: (i % pos_blocks, 0)))
        args.append(tab)
        body = functools.partial(_norm_proj_kernel, modes=modes)
    else:
        def body(x_ref, g_ref, w_ref, o_ref, u_ref):
            _norm_proj_kernel(x_ref, g_ref, w_ref, None, o_ref, u_ref, modes=None)
    return pl.pallas_call(
        body,
        grid=grid,
        in_specs=in_specs,
        out_specs=pl.BlockSpec((tm, tn), lambda i, j: (i, j)),
        out_shape=jax.ShapeDtypeStruct((m, n), F32),
        scratch_shapes=[pltpu.VMEM((tm, d), BF16)],
        compiler_params=_cparams("parallel", "arbitrary"),
        name="norm_proj",
    )(*args)


def _rope_tables(pos):
    posf = pos.astype(F32)[:, None]
    p = pos.shape[0]

    def cs(half, theta):
        inv = theta ** (-jnp.arange(half, dtype=F32) / half)
        ang = posf * inv[None, :]
        return jnp.cos(ang), jnp.sin(ang)

    ones = lambda n: jnp.ones((p, n), F32)
    zeros = lambda n: jnp.zeros((p, n), F32)
    c, s = cs(HEAD_DIM // 2, RET_THETA)
    ret_c = jnp.concatenate([c, c], 1)
    ret_s = jnp.concatenate([-s, s], 1)
    c, s = cs(16, ROPE_THETA)
    dsa_c = jnp.concatenate([c, c, ones(96)], 1)
    dsa_a = jnp.concatenate([-s, zeros(112)], 1)
    dsa_b = jnp.concatenate([zeros(16), s, zeros(96)], 1)
    c, s = cs(8, ROPE_THETA)
    idx_c = jnp.tile(jnp.concatenate([c, c, ones(48)], 1), (1, 2))
    idx_a = jnp.tile(jnp.concatenate([-s, zeros(56)], 1), (1, 2))
    idx_b = jnp.tile(jnp.concatenate([zeros(8), s, zeros(48)], 1), (1, 2))
    return jnp.concatenate([ret_c, ret_s, dsa_c, dsa_a, dsa_b, idx_c, idx_a, idx_b], 1)


def _arrange_w_in(w_in):
    o = 0
    parts = {}
    for name, n in (("rq", 1024), ("rk", 1024), ("rv", 1024), ("rg", 1024), ("dq", 1024), ("dk", 256),
                    ("dv", 256), ("iq", 1024), ("ik", 64), ("iw", 16), ("ga", 2048), ("gb", 2048)):
        parts[name] = w_in[:, o:o + n]
        o += n
    pad = jnp.zeros((w_in.shape[0], N_PROJ - COL_IK - IDX_DH - IDX_HEADS), w_in.dtype)
    order = ("ga", "gb", "rq", "rk", "rv", "rg", "dq", "iq", "dk", "dv", "ik", "iw")
    return jnp.concatenate([parts[k] for k in order] + [pad], axis=1).astype(BF16)


def _log_decay():
    return jnp.log1p(-jnp.exp2(-5.0 - jnp.arange(RET_HEADS, dtype=F32)))


def _groupnorm_gate(o, gate, gn):
    mu = jnp.mean(o, axis=-1, keepdims=True)
    d = o - mu
    var = jnp.mean(d * d, axis=-1, keepdims=True)
    n = d * lax.rsqrt(var + EPS) * gn
    return (gate * _sigmoid(gate) * n).astype(BF16)


def _ret_prompt_kernel(q_ref, k_ref, v_ref, g_ref, gn_ref, intra_ref, qdec_ref, kdec_ref, cdec_ref,
                       y_ref, st_ref, s_scr):
    c = pl.program_id(1)

    @pl.when(c == 0)
    def _():
        s_scr[...] = jnp.zeros_like(s_scr)

    for h in range(RET_HEADS):
        sl = slice(h * HEAD_DIM, (h + 1) * HEAD_DIM)
        k = k_ref[:, sl]
        qb = q_ref[:, sl].astype(BF16)
        kb = k.astype(BF16)
        vb = v_ref[:, sl].astype(BF16)
        s = _dot_nt(qb, kb) * intra_ref[h]
        state = s_scr[h]
        o = _dot(s.astype(BF16), vb) + _dot(qb, state.astype(BF16)) * qdec_ref[h]
        kd = (k * kdec_ref[h]).astype(BF16)
        s_scr[h] = state * cdec_ref[h] + _dot_tn(kd, vb)
        y_ref[:, sl] = _groupnorm_gate(o, g_ref[:, sl], gn_ref[:, sl])

    @pl.when(c == pl.num_programs(1) - 1)
    def _():
        st_ref[0] = s_scr[...]


def _ret_prompt(proj, gn, batch, seq, chunk):
    nc = seq // chunk
    log_g = _log_decay()
    j = jnp.arange(chunk, dtype=F32)
    diff = j[:, None] - j[None, :]
    intra = jnp.where(diff >= 0, jnp.exp(log_g[:, None, None] * jnp.maximum(diff, 0.0)), 0.0)
    ones = jnp.ones((1, 1, HEAD_DIM), F32)
    qdec = jnp.exp(log_g[:, None] * (j[None, :] + 1.0))[:, :, None] * ones
    kdec = jnp.exp(log_g[:, None] * (chunk - 1.0 - j[None, :]))[:, :, None] * ones
    cdec = jnp.exp(log_g * chunk)[:, None, None] * ones
    w = RET_HEADS * HEAD_DIM
    col = lambda off: (lambda b, c: (b * nc + c, off // w))
    const3 = lambda b, c: (0, 0, 0)
    return pl.pallas_call(
        _ret_prompt_kernel,
        grid=(batch, nc),
        in_specs=[
            pl.BlockSpec((chunk, w), col(COL_RQ)),
            pl.BlockSpec((chunk, w), col(COL_RK)),
            pl.BlockSpec((chunk, w), col(COL_RV)),
            pl.BlockSpec((chunk, w), col(COL_RG)),
            pl.BlockSpec((1, w), lambda b, c: (0, 0)),
            pl.BlockSpec((RET_HEADS, chunk, chunk), const3),
            pl.BlockSpec((RET_HEADS, chunk, HEAD_DIM), const3),
            pl.BlockSpec((RET_HEADS, chunk, HEAD_DIM), const3),
            pl.BlockSpec((RET_HEADS, 1, HEAD_DIM), const3),
        ],
        out_specs=[
            pl.BlockSpec((chunk, w), lambda b, c: (b * nc + c, 0)),
            pl.BlockSpec((1, RET_HEADS, HEAD_DIM, HEAD_DIM), lambda b, c: (b, 0, 0, 0)),
        ],
        out_shape=[
            jax.ShapeDtypeStruct((batch * seq, w), BF16),
            jax.ShapeDtypeStruct((batch, RET_HEADS, HEAD_DIM, HEAD_DIM), F32),
        ],
        scratch_shapes=[pltpu.VMEM((RET_HEADS, HEAD_DIM, HEAD_DIM), F32)],
        compiler_params=_cparams("parallel", "arbitrary"),
        name="ret_prompt",
    )(proj, proj, proj, proj, gn, intra, qdec, kdec, cdec)


def _ret_sample_kernel(q_ref, k_ref, v_ref, g_ref, gn_ref, st_ref, intra_ref, qdec_ref, kdec_ref, cdec_ref,
                       y_ref, so_ref, *, n_batch, rows):
    k = k_ref[...]
    qb = q_ref[...].astype(BF16)
    kb = k.astype(BF16)
    vb = v_ref[...].astype(BF16)
    s = _dot_nt(qb, kb) * intra_ref[0]
    kd = k * kdec_ref[0]
    row_batch = lax.broadcasted_iota(jnp.int32, kd.shape, 0) // rows
    inter = []
    for b in range(n_batch):
        state = st_ref[b, 0]
        inter.append(_dot(qb[b * rows:(b + 1) * rows], state.astype(BF16)))
        kd_b = jnp.where(row_batch == b, kd, 0.0).astype(BF16)
        so_ref[b, 0] = state * cdec_ref[0] + _dot_tn(kd_b, vb)
    o = _dot(s.astype(BF16), vb) + jnp.concatenate(inter, axis=0) * qdec_ref[0]
    y_ref[...] = _groupnorm_gate(o, g_ref[...], gn_ref[...])


def _ret_sample(proj, gn, state, n_batch, rows, t_valid):
    m = n_batch * rows
    log_g = _log_decay()
    r = jnp.arange(m)
    t = (r % rows).astype(F32)
    same = (r[:, None] // rows) == (r[None, :] // rows)
    diff = t[:, None] - t[None, :]
    intra = jnp.where(same[None] & (diff >= 0)[None], jnp.exp(log_g[:, None, None] * jnp.maximum(diff, 0.0)[None]), 0.0)
    ones = jnp.ones((1, 1, HEAD_DIM), F32)
    qdec = jnp.exp(log_g[:, None] * (t[None, :] + 1.0))[:, :, None] * ones
    kdec = jnp.where(t[None, :] < t_valid, jnp.exp(log_g[:, None] * (t_valid - 1.0 - t[None, :])), 0.0)[:, :, None] * ones
    cdec = jnp.exp(log_g * t_valid)[:, None, None] * ones
    col = lambda off: (lambda h: (0, off // HEAD_DIM + h))
    per_head = lambda h: (h, 0, 0)
    return pl.pallas_call(
        functools.partial(_ret_sample_kernel, n_batch=n_batch, rows=rows),
        grid=(RET_HEADS,),
        in_specs=[
            pl.BlockSpec((m, HEAD_DIM), col(COL_RQ)),
            pl.BlockSpec((m, HEAD_DIM), col(COL_RK)),
            pl.BlockSpec((m, HEAD_DIM), col(COL_RV)),
            pl.BlockSpec((m, HEAD_DIM), col(COL_RG)),
            pl.BlockSpec((1, HEAD_DIM), lambda h: (0, h)),
            pl.BlockSpec((n_batch, 1, HEAD_DIM, HEAD_DIM), lambda h: (0, h, 0, 0)),
            pl.BlockSpec((1, m, m), per_head),
            pl.BlockSpec((1, m, HEAD_DIM), per_head),
            pl.BlockSpec((1, m, HEAD_DIM), per_head),
            pl.BlockSpec((1, 1, HEAD_DIM), per_head),
        ],
        out_specs=[
            pl.BlockSpec((m, HEAD_DIM), lambda h: (0, h)),
            pl.BlockSpec((n_batch, 1, HEAD_DIM, HEAD_DIM), lambda h: (0, h, 0, 0)),
        ],
        out_shape=[
            jax.ShapeDtypeStruct((m, RET_HEADS * HEAD_DIM), BF16),
            jax.ShapeDtypeStruct(state.shape, F32),
        ],
        compiler_params=_cparams("parallel"),
        name="ret_sample",
    )(proj, proj, proj, proj, gn, state, intra, qdec, kdec, cdec)


def _sortable_key(x):
    bits = pltpu.bitcast(x, jnp.int32)
    return bits ^ ((bits >> 31) & jnp.int32(0x7FFFFFFF))


def _kth_largest_key(count_ge, shape, topk):
    def body(b, res):
        cand = res + lax.shift_left(jnp.int32(1), jnp.int32(31) - b)
        return jnp.where(count_ge(cand) >= topk, cand, res)

    return lax.fori_loop(0, 32, body, jnp.full(shape, INT_MIN, jnp.int32))


def _dsa_prompt_kernel(dq_ref, iq_ref, iwq_ref, kall_ref, vall_ref, ikall_ref, o_ref,
                       kbf, vtb, ika, ikb, keys, m_scr, l_scr, acc_scr, *, tq, topk):
    i = pl.program_id(1)
    n_chunks = kbf.shape[0]
    scale = HEAD_DIM ** -0.5

    @pl.when(i == 0)
    def _():
        lane = lax.broadcasted_iota(jnp.int32, (tq, LANES), 1)
        for c in range(n_chunks):
            rows = slice(c * tq, (c + 1) * tq)
            kbf[c] = kall_ref[rows, :].astype(BF16)
            for kvh in range(DSA_KV_HEADS):
                sl = slice(kvh * HEAD_DIM, (kvh + 1) * HEAD_DIM)
                vtb[c, sl, :] = vall_ref[rows, sl].T.astype(BF16)
            a = ikall_ref[rows, :]
            ika[c] = jnp.where(lane < IDX_DH, a, 0.0).astype(BF16)
            ikb[c] = jnp.where(lane >= IDX_DH, pltpu.roll(a, IDX_DH, 1), 0.0).astype(BF16)

    w_t = iwq_ref[...].T
    iqb = iq_ref[...].astype(BF16)
    qb = dq_ref[...].astype(BF16)
    t_col = i * tq + lax.broadcasted_iota(jnp.int32, (tq, tq), 1)
    s_row0 = lax.broadcasted_iota(jnp.int32, (tq, tq), 0)

    def score_chunk(c, carry):
        a = ika[c]
        b = ikb[c]
        acc = jnp.zeros((tq, tq), F32)
        for p in range(IDX_HEADS // 2):
            pair = iqb[:, p * LANES:(p + 1) * LANES]
            w0 = w_t[IDX_DH + 2 * p:IDX_DH + 2 * p + 1, :]
            w1 = w_t[IDX_DH + 2 * p + 1:IDX_DH + 2 * p + 2, :]
            acc = acc + jnp.maximum(_dot_nt(a, pair), 0.0) * w0
            acc = acc + jnp.maximum(_dot_nt(b, pair), 0.0) * w1
        acc = jnp.where(s_row0 + c * tq <= t_col, acc, -jnp.inf)
        keys[c] = _sortable_key(acc)
        return carry

    lax.fori_loop(0, i + 1, score_chunk, 0)

    def count_ge(cand):
        def chunk(c, cnt):
            ge = jnp.where(keys[c] >= cand, 1.0, 0.0)
            return cnt + ge.reshape(tq // SUBLANES, SUBLANES, tq).sum(axis=0)

        cnt = lax.fori_loop(0, i + 1, chunk, jnp.zeros((SUBLANES, tq), F32))
        return cnt.sum(axis=0, keepdims=True)

    thr = _kth_largest_key(count_ge, (1, tq), topk)

    m_scr[...] = jnp.full_like(m_scr, NEG)
    l_scr[...] = jnp.zeros_like(l_scr)
    acc_scr[...] = jnp.zeros_like(acc_scr)

    def attend_chunk(c, carry):
        bias = jnp.where(s_row0 + c * tq <= t_col, jnp.where(keys[c] >= thr, 0.0, NEG), NEG)
        kc = kbf[c]
        vc = vtb[c]
        for h in range(DSA_HEADS):
            kvh = h // (DSA_HEADS // DSA_KV_HEADS)
            ksl = slice(kvh * HEAD_DIM, (kvh + 1) * HEAD_DIM)
            z = _dot_nt(kc[:, ksl], qb[:, h * HEAD_DIM:(h + 1) * HEAD_DIM]) + bias
            m_old = m_scr[h]
            m_new = jnp.maximum(m_old, z.max(axis=0, keepdims=True))
            alpha = jnp.exp((m_old - m_new) * scale)
            p = jnp.exp((z - m_new) * scale)
            l_scr[h] = l_scr[h] * alpha + p.sum(axis=0, keepdims=True)
            acc_scr[h] = acc_scr[h] * alpha + _dot(vc[ksl, :], p.astype(BF16))
            m_scr[h] = m_new
        return carry

    lax.fori_loop(0, i + 1, attend_chunk, 0)

    for h in range(DSA_HEADS):
        o = acc_scr[h] * (1.0 / l_scr[h])
        o_ref[:, h * HEAD_DIM:(h + 1) * HEAD_DIM] = o.T.astype(BF16)


def _dsa_prompt(proj, batch, seq, tq):
    nq = seq // tq
    topk = min(TOPK_MAX, seq // 4)
    qw = DSA_HEADS * HEAD_DIM
    kvw = DSA_KV_HEADS * HEAD_DIM
    return pl.pallas_call(
        functools.partial(_dsa_prompt_kernel, tq=tq, topk=topk),
        grid=(batch, nq),
        in_specs=[
            pl.BlockSpec((tq, qw), lambda b, i: (b * nq + i, COL_DQ // qw)),
            pl.BlockSpec((tq, qw), lambda b, i: (b * nq + i, COL_IQ // qw)),
            pl.BlockSpec((tq, LANES), lambda b, i: (b * nq + i, COL_IK // LANES)),
            pl.BlockSpec((seq, kvw), lambda b, i: (b, COL_DK // kvw)),
            pl.BlockSpec((seq, kvw), lambda b, i: (b, COL_DV // kvw)),
            pl.BlockSpec((seq, LANES), lambda b, i: (b, COL_IK // LANES)),
        ],
        out_specs=pl.BlockSpec((tq, qw), lambda b, i: (b * nq + i, 0)),
        out_shape=jax.ShapeDtypeStruct((batch * seq, qw), BF16),
        scratch_shapes=[
            pltpu.VMEM((nq, tq, kvw), BF16),
            pltpu.VMEM((nq, kvw, tq), BF16),
            pltpu.VMEM((nq, tq, LANES), BF16),
            pltpu.VMEM((nq, tq, LANES), BF16),
            pltpu.VMEM((nq, tq, tq), jnp.int32),
            pltpu.VMEM((DSA_HEADS, 1, tq), F32),
            pltpu.VMEM((DSA_HEADS, 1, tq), F32),
            pltpu.VMEM((DSA_HEADS, HEAD_DIM, tq), F32),
        ],
        compiler_params=_cparams("parallel", "arbitrary"),
        name="dsa_prompt",
    )(proj, proj, proj, proj, proj, proj)


def _idx_sample_kernel(pt_ref, iq_ref, iw_ref, iknew_ref, *rest, pages):
    page_refs = rest[:pages]
    sp_ref, sn_ref = rest[pages:]
    iqb = iq_ref[0].astype(BF16)
    w = iw_ref[0]
    rows = iqb.shape[0] // IDX_HEADS

    def scores(ik):
        d = jnp.maximum(_dot_nt(iqb, ik.astype(BF16)), 0.0) * w
        return d.reshape(rows, IDX_HEADS, LANES).sum(axis=1)

    for p in range(pages):
        sp_ref[0, :, p * PAGE_SIZE:(p + 1) * PAGE_SIZE] = scores(page_refs[p][0])
    t = lax.broadcasted_iota(jnp.int32, (rows, LANES), 0)
    j = lax.broadcasted_iota(jnp.int32, (rows, LANES), 1)
    sn_ref[0] = jnp.where(j <= t, scores(iknew_ref[0]), -jnp.inf)


def _idx_sample(page_table, iq, iw, ik_new, pool_ik, pages):
    n_batch, n_pages = page_table.shape
    rows16 = iq.shape[1]
    rows = rows16 // IDX_HEADS
    past = n_pages * PAGE_SIZE
    page_specs = [pl.BlockSpec((1, PAGE_SIZE, IDX_DH), functools.partial(
        lambda b, g, pt, k: (pt[b, g * pages + k], 0, 0), k=k)) for k in range(pages)]
    grid_spec = pltpu.PrefetchScalarGridSpec(
        num_scalar_prefetch=1,
        grid=(n_batch, n_pages // pages),
        in_specs=[
            pl.BlockSpec((1, rows16, IDX_DH), lambda b, g, pt: (b, 0, 0)),
            pl.BlockSpec((1, rows16, LANES), lambda b, g, pt: (b, 0, 0)),
            pl.BlockSpec((1, LANES, IDX_DH), lambda b, g, pt: (b, 0, 0)),
        ] + page_specs,
        out_specs=[
            pl.BlockSpec((1, rows, pages * PAGE_SIZE), lambda b, g, pt: (b, 0, g)),
            pl.BlockSpec((1, rows, LANES), lambda b, g, pt: (b, 0, 0)),
        ],
    )
    return pl.pallas_call(
        functools.partial(_idx_sample_kernel, pages=pages),
        grid_spec=grid_spec,
        out_shape=[
            jax.ShapeDtypeStruct((n_batch, rows, past), F32),
            jax.ShapeDtypeStruct((n_batch, rows, LANES), F32),
        ],
        compiler_params=_cparams("parallel", "arbitrary"),
        name="idx_sample",
    )(page_table, iq, iw, ik_new, *([pool_ik] * pages))


def _dsa_sample_kernel(pt_ref, sp_ref, sn_ref, q_ref, knew_ref, vnew_ref, *rest, pages, topk, rows):
    k_refs = rest[:pages]
    v_refs = rest[pages:2 * pages]
    o_ref, selp, seln, m_scr, l_scr, acc_scr = rest[2 * pages:]
    g = pl.program_id(1)
    scale = HEAD_DIM ** -0.5
    group = DSA_HEADS // DSA_KV_HEADS

    @pl.when(g == 0)
    def _():
        kp = _sortable_key(sp_ref[0])
        kn = _sortable_key(sn_ref[0])

        def count_ge(cand):
            cp = jnp.where(kp >= cand, 1.0, 0.0).sum(axis=1, keepdims=True)
            cn = jnp.where(kn >= cand, 1.0, 0.0).sum(axis=1, keepdims=True)
            return cp + cn

        thr = _kth_largest_key(count_ge, (rows, 1), topk)
        sel_past = jnp.where(kp >= thr, 0.0, NEG)
        for pg in range(selp.shape[0]):
            selp[pg] = sel_past[:, pg * PAGE_SIZE:(pg + 1) * PAGE_SIZE]
        t = lax.broadcasted_iota(jnp.int32, (rows, LANES), 0)
        j = lax.broadcasted_iota(jnp.int32, (rows, LANES), 1)
        seln[...] = jnp.where(j <= t, jnp.where(kn >= thr, 0.0, NEG), NEG)
        m_scr[...] = jnp.full_like(m_scr, NEG)
        l_scr[...] = jnp.zeros_like(l_scr)
        acc_scr[...] = jnp.zeros_like(acc_scr)

    def update(kvh, z, v):
        m_old = m_scr[kvh]
        m_new = jnp.maximum(m_old, z.max(axis=1, keepdims=True))
        alpha = jnp.exp((m_old - m_new) * scale)
        p = jnp.exp((z - m_new) * scale)
        l_scr[kvh] = l_scr[kvh] * alpha + p.sum(axis=1, keepdims=True)
        acc_scr[kvh] = acc_scr[kvh] * alpha + _dot(p.astype(BF16), v)
        m_scr[kvh] = m_new

    def attend(k, v, bias):
        bias_g = jnp.concatenate([bias] * group, axis=0)
        for kvh in range(DSA_KV_HEADS):
            sl = slice(kvh * HEAD_DIM, (kvh + 1) * HEAD_DIM)
            raw = _dot_nt(q_ref[0, kvh].astype(BF16), k[:, sl].astype(BF16))
            z = jnp.where(bias_g < 0.0, NEG, raw)
            update(kvh, z, v[:, sl].astype(BF16))

    for p in range(pages):
        attend(k_refs[p][0], v_refs[p][0], selp[g * pages + p])

    @pl.when(g == pl.num_programs(1) - 1)
    def _():
        attend(knew_ref[0], vnew_ref[0], seln[...])
        for kvh in range(DSA_KV_HEADS):
            o_ref[0, kvh] = (acc_scr[kvh] * (1.0 / l_scr[kvh])).astype(BF16)


def _dsa_sample(page_table, scores_past, scores_new, q, k_new, v_new, pool_k, pool_v, pages, topk):
    n_batch, n_pages = page_table.shape
    rows = scores_past.shape[1]
    past = n_pages * PAGE_SIZE
    kvw = DSA_KV_HEADS * HEAD_DIM
    grows = q.shape[2]
    page_spec = lambda k: pl.BlockSpec((1, PAGE_SIZE, kvw), functools.partial(
        lambda b, g, pt, k: (pt[b, g * pages + k], 0, 0), k=k))
    per_batch3 = lambda b, g, pt: (b, 0, 0)
    grid_spec = pltpu.PrefetchScalarGridSpec(
        num_scalar_prefetch=1,
        grid=(n_batch, n_pages // pages),
        in_specs=[
            pl.BlockSpec((1, rows, past), per_batch3),
            pl.BlockSpec((1, rows, LANES), per_batch3),
            pl.BlockSpec((1, DSA_KV_HEADS, grows, HEAD_DIM), lambda b, g, pt: (b, 0, 0, 0)),
            pl.BlockSpec((1, LANES, kvw), per_batch3),
            pl.BlockSpec((1, LANES, kvw), per_batch3),
        ] + [page_spec(k) for k in range(pages)] * 2,
        out_specs=pl.BlockSpec((1, DSA_KV_HEADS, grows, HEAD_DIM), lambda b, g, pt: (b, 0, 0, 0)),
        scratch_shapes=[
            pltpu.VMEM((n_pages, rows, PAGE_SIZE), F32),
            pltpu.VMEM((rows, LANES), F32),
            pltpu.VMEM((DSA_KV_HEADS, grows, 1), F32),
            pltpu.VMEM((DSA_KV_HEADS, grows, 1), F32),
            pltpu.VMEM((DSA_KV_HEADS, grows, HEAD_DIM), F32),
        ],
    )
    return pl.pallas_call(
        functools.partial(_dsa_sample_kernel, pages=pages, topk=topk, rows=rows),
        grid_spec=grid_spec,
        out_shape=jax.ShapeDtypeStruct((n_batch, DSA_KV_HEADS, grows, HEAD_DIM), BF16),
        compiler_params=_cparams("parallel", "arbitrary"),
        name="dsa_sample",
    )(page_table, scores_past, scores_new, q, k_new, v_new, *([pool_k] * pages), *([pool_v] * pages))


def _merge_kernel(x_ref, ry_ref, do_ref, ga_ref, gb_ref, wr_ref, wd_ref, wo_ref, o_ref):
    ya = _dot(ry_ref[...], wr_ref[...])
    yb = _dot(do_ref[...], wd_ref[...])
    merged = _sigmoid(ga_ref[...]) * ya + _sigmoid(gb_ref[...]) * yb
    o_ref[...] = x_ref[...] + _dot(merged.astype(BF16), wo_ref[...])


def _merge(x, ret_y, dsa_o, proj, w_ret_out, w_dsa_out, w_o, tm):
    m, d = x.shape
    w = ret_y.shape[1]
    const = lambda i: (0, 0)
    return pl.pallas_call(
        _merge_kernel,
        grid=(m // tm,),
        in_specs=[
            pl.BlockSpec((tm, d), lambda i: (i, 0)),
            pl.BlockSpec((tm, w), lambda i: (i, 0)),
            pl.BlockSpec((tm, w), lambda i: (i, 0)),
            pl.BlockSpec((tm, d), lambda i: (i, COL_GA // d)),
            pl.BlockSpec((tm, d), lambda i: (i, COL_GB // d)),
            pl.BlockSpec((w, d), const),
            pl.BlockSpec((w, d), const),
            pl.BlockSpec((d, d), const),
        ],
        out_specs=pl.BlockSpec((tm, d), lambda i: (i, 0)),
        out_shape=jax.ShapeDtypeStruct((m, d), F32),
        compiler_params=_cparams("parallel"),
        name="merge_out",
    )(x, ret_y, dsa_o, proj, proj, w_ret_out, w_dsa_out, w_o)


def _mem_attn_kernel(h_ref, g_ref, wq_ref, mk_ref, mv_ref, wo_ref, o_ref, *, rows_per_batch):
    h = h_ref[...]
    qm = _dot(_rmsnorm_bf16(h, g_ref[...]), wq_ref[...])
    tm = h.shape[0]
    nk = mk_ref.shape[0]
    scale = HEAD_DIM ** -0.5
    if rows_per_batch is not None:
        rb = lax.broadcasted_iota(jnp.int32, (tm, nk), 0) // rows_per_batch
        kb = lax.broadcasted_iota(jnp.int32, (tm, nk), 1) // MEM_LEN
        same = rb == kb
    outs = []
    for hd in range(MEM_HEADS):
        sl = slice(hd * HEAD_DIM, (hd + 1) * HEAD_DIM)
        z = _dot_nt(qm[:, sl].astype(BF16), mk_ref[:, sl].astype(BF16)) * scale
        if rows_per_batch is not None:
            z = jnp.where(same, z, NEG)
        p = jnp.exp(z - z.max(axis=-1, keepdims=True))
        p = p / p.sum(axis=-1, keepdims=True)
        outs.append(_dot(p.astype(BF16), mv_ref[:, sl].astype(BF16)))
    om = jnp.concatenate(outs, axis=1).astype(BF16)
    o_ref[...] = h + _dot(om, wo_ref[...])


def _mem_attn(h, g, w_mq, mk, mv, w_mo, *, tm, batch_tiles, mk_col, mv_col, nk, rows_per_batch):
    m, d = h.shape
    const = lambda i: (0, 0)
    if batch_tiles:
        kmap = lambda col: (lambda i: (i // batch_tiles, col))
    else:
        kmap = lambda col: (lambda i: (0, col))
    return pl.pallas_call(
        functools.partial(_mem_attn_kernel, rows_per_batch=rows_per_batch),
        grid=(m // tm,),
        in_specs=[
            pl.BlockSpec((tm, d), lambda i: (i, 0)),
            pl.BlockSpec((1, d), const),
            pl.BlockSpec((d, MEM_W), const),
            pl.BlockSpec((nk, MEM_W), kmap(mk_col)),
            pl.BlockSpec((nk, MEM_W), kmap(mv_col)),
            pl.BlockSpec((MEM_W, d), const),
        ],
        out_specs=pl.BlockSpec((tm, d), lambda i: (i, 0)),
        out_shape=jax.ShapeDtypeStruct((m, d), F32),
        compiler_params=_cparams("parallel"),
        name="mem_attn",
    )(h, g, w_mq, mk, mv, w_mo)


def _mlp_kernel(h_ref, g_ref, wu_ref, wd_ref, gf_ref, o_ref, u_ref, acc_ref):
    j = pl.program_id(1)

    @pl.when(j == 0)
    def _():
        u_ref[...] = _rmsnorm_bf16(h_ref[...], g_ref[...])
        acc_ref[...] = jnp.zeros_like(acc_ref)

    a = jnp.maximum(_dot(u_ref[...], wu_ref[...]), 0.0)
    acc_ref[...] += _dot((a * a).astype(BF16), wd_ref[...])

    @pl.when(j == pl.num_programs(1) - 1)
    def _():
        y = h_ref[...] + acc_ref[...]
        ms = jnp.mean(y * y, axis=-1, keepdims=True)
        o_ref[...] = (y * lax.rsqrt(ms + EPS)) * gf_ref[...]


def _mlp(h, g, w_up, w_down, g_final, tm, tf):
    m, d = h.shape
    ff = w_up.shape[1]
    return pl.pallas_call(
        _mlp_kernel,
        grid=(m // tm, ff // tf),
        in_specs=[
            pl.BlockSpec((tm, d), lambda i, j: (i, 0)),
            pl.BlockSpec((1, d), lambda i, j: (0, 0)),
            pl.BlockSpec((d, tf), lambda i, j: (0, j)),
            pl.BlockSpec((tf, d), lambda i, j: (j, 0)),
            pl.BlockSpec((1, d), lambda i, j: (0, 0)),
        ],
        out_specs=pl.BlockSpec((tm, d), lambda i, j: (i, 0)),
        out_shape=jax.ShapeDtypeStruct((m, d), F32),
        scratch_shapes=[pltpu.VMEM((tm, d), BF16), pltpu.VMEM((tm, d), F32)],
        compiler_params=_cparams("parallel", "arbitrary"),
        name="mlp_final",
    )(h, g, w_up, w_down, g_final)


def _pick_tile(n, pref):
    t = min(n, pref)
    while n % t:
        t //= 2
    return t


def kernel(x_prompt, x_sample, mem_prompt, cache_k, cache_v, cache_idx_k, state_ret, cache_mem_k, cache_mem_v,
           page_table, g_mix, w_in, gn_ret, w_ret_out, w_dsa_out, w_o, g_mem, g_memkv, w_mq, w_mk, w_mv, w_mo,
           g_mlp, w_up, w_down, g_final):
    assert w_in.shape[0] == 1, "one layer"
    batch, seq, d = x_prompt.shape
    n_dec, t_dec, _ = x_sample.shape
    n_pages = page_table.shape[1]
    past = n_pages * PAGE_SIZE
    mem_len = mem_prompt.shape[1]
    assert mem_len == MEM_LEN and t_dec <= SUBLANES
    row = lambda v: v.reshape(1, -1)

    w_proj = _arrange_w_in(w_in[0])
    w_ret_out_b, w_dsa_out_b, w_o_b = (w[0].astype(BF16) for w in (w_ret_out, w_dsa_out, w_o))
    w_mq_b, w_mo_b = w_mq[0].astype(BF16), w_mo[0].astype(BF16)
    w_mkv_b = jnp.concatenate([w_mk[0], w_mv[0]], axis=1).astype(BF16)
    w_up_b, w_down_b = w_up[0].astype(BF16), w_down[0].astype(BF16)
    g_mix_r, gn_r, g_mem_r, g_memkv_r, g_mlp_r, g_final_r = (
        row(v) for v in (g_mix[0], gn_ret[0], g_mem[0], g_memkv[0], g_mlp[0], g_final))

    m_p = batch * seq
    xp = x_prompt.reshape(m_p, d)
    tm_p = _pick_tile(seq, 512)
    tab_p = _rope_tables(jnp.arange(seq, dtype=jnp.int32))
    proj_p = _norm_proj(xp, g_mix_r, w_proj, tab_p, PROJ_MODES, tm_p, PROJ_TN, seq // tm_p)
    ret_y_p, ret_state_p = _ret_prompt(proj_p, gn_r, batch, seq, _pick_tile(seq, 256))
    dsa_o_p = _dsa_prompt(proj_p, batch, seq, _pick_tile(seq, 256))
    kv_p = _norm_proj(mem_prompt.reshape(batch * mem_len, d), g_memkv_r, w_mkv_b, None, None, mem_len,
                      PROJ_TN, 1)
    tm_t = _pick_tile(seq, 256)
    h_p = _merge(xp, ret_y_p, dsa_o_p, proj_p, w_ret_out_b, w_dsa_out_b, w_o_b, tm_t)
    h_p = _mem_attn(h_p, g_mem_r, w_mq_b, kv_p, kv_p, w_mo_b, tm=tm_t, batch_tiles=seq // tm_t,
                    mk_col=0, mv_col=1, nk=mem_len, rows_per_batch=None)
    y_p = _mlp(h_p, g_mlp_r, w_up_b, w_down_b, g_final_r, tm_p, 512)

    rows = SUBLANES
    m_s = n_dec * rows
    xs = jnp.pad(x_sample, ((0, 0), (0, rows - t_dec), (0, 0))).reshape(m_s, d)
    pos_s = past + (jnp.arange(m_s, dtype=jnp.int32) % rows)
    proj_s = _norm_proj(xs, g_mix_r, w_proj, _rope_tables(pos_s), PROJ_MODES, m_s, PROJ_TN, 1)
    ret_y_s, ret_state_s = _ret_sample(proj_s, gn_r, state_ret[0], n_dec, rows, t_dec)

    proj_s3 = proj_s.reshape(n_dec, rows, N_PROJ)
    iq_s = proj_s3[:, :, COL_IQ:COL_IQ + IDX_HEADS * IDX_DH].reshape(n_dec, rows * IDX_HEADS, IDX_DH)
    iw_s = proj_s3[:, :, COL_IK + IDX_DH:COL_IK + IDX_DH + IDX_HEADS].reshape(n_dec, rows * IDX_HEADS, 1)
    iw_s = jnp.broadcast_to(iw_s, (n_dec, rows * IDX_HEADS, LANES))
    pad_keys = lambda a: jnp.pad(a, ((0, 0), (0, LANES - rows), (0, 0)))
    ik_new = pad_keys(proj_s3[:, :, COL_IK:COL_IK + IDX_DH])
    k_new = pad_keys(proj_s3[:, :, COL_DK:COL_DK + DSA_KV_HEADS * HEAD_DIM])
    v_new = pad_keys(proj_s3[:, :, COL_DV:COL_DV + DSA_KV_HEADS * HEAD_DIM])
    pages = _pick_tile(n_pages, 8)
    scores_past, scores_new = _idx_sample(page_table, iq_s, iw_s, ik_new, cache_idx_k[0], pages)
    group = DSA_HEADS // DSA_KV_HEADS
    dq_s = proj_s3[:, :, COL_DQ:COL_DQ + DSA_HEADS * HEAD_DIM].reshape(n_dec, rows, DSA_KV_HEADS, group, HEAD_DIM)
    dq_s = dq_s.transpose(0, 2, 3, 1, 4).reshape(n_dec, DSA_KV_HEADS, group * rows, HEAD_DIM)
    kvw = DSA_KV_HEADS * HEAD_DIM
    topk_s = min(TOPK_MAX, (past + t_dec) // 4)
    dsa_o_s = _dsa_sample(page_table, scores_past, scores_new, dq_s, k_new, v_new,
                          cache_k[0].reshape(-1, PAGE_SIZE, kvw), cache_v[0].reshape(-1, PAGE_SIZE, kvw),
                          pages, topk_s)
    dsa_o_s = dsa_o_s.reshape(n_dec, DSA_KV_HEADS, group, rows, HEAD_DIM).transpose(0, 3, 1, 2, 4)
    dsa_o_s = dsa_o_s.reshape(m_s, DSA_HEADS * HEAD_DIM)

    h_s = _merge(xs, ret_y_s, dsa_o_s, proj_s, w_ret_out_b, w_dsa_out_b, w_o_b, m_s)
    h_s = _mem_attn(h_s, g_mem_r, w_mq_b, cache_mem_k[0].reshape(n_dec * mem_len, MEM_W),
                    cache_mem_v[0].reshape(n_dec * mem_len, MEM_W), w_mo_b, tm=m_s, batch_tiles=0,
                    mk_col=0, mv_col=0, nk=n_dec * mem_len, rows_per_batch=rows)
    y_s = _mlp(h_s, g_mlp_r, w_up_b, w_down_b, g_final_r, m_s, 512)

    def rows_p(col, width, tail):
        return proj_p[:, col:col + width].reshape((1, batch, seq) + tail)

    def rows_s(col, width, tail):
        return proj_s3[:, :t_dec, col:col + width].reshape((1, n_dec, t_dec) + tail)

    kv_shape = (DSA_KV_HEADS, HEAD_DIM)
    return (
        y_p.reshape(batch, seq, d),
        y_s.reshape(n_dec, rows, d)[:, :t_dec],
        ret_state_p[None],
        rows_p(COL_DK, kvw, kv_shape),
        rows_p(COL_DV, kvw, kv_shape),
        rows_p(COL_IK, IDX_DH, (IDX_DH,)),
        kv_p[:, :MEM_W].reshape(1, batch, mem_len, MEM_HEADS, HEAD_DIM),
        kv_p[:, MEM_W:].reshape(1, batch, mem_len, MEM_HEADS, HEAD_DIM),
        ret_state_s[None],
        rows_s(COL_DK, kvw, kv_shape),
        rows_s(COL_DV, kvw, kv_shape),
        rows_s(COL_IK, IDX_DH, (IDX_DH,)),
    )
```

```python
import functools
import math

import jax
import jax.numpy as jnp
from jax import lax
from jax.experimental import pallas as pl
from jax.experimental.pallas import tpu as pltpu

F32 = jnp.float32
BF16 = jnp.bfloat16

D_MODEL = 2048
RET_HEADS = 8
HEAD_DIM = 128
DSA_HEADS = 8
DSA_KV_HEADS = 2
IDX_HEADS = 16
IDX_DH = 64
TOPK_MAX = 256
PAGE_SIZE = 128
MEM_LEN = 256
MEM_HEADS = 4
MEM_W = MEM_HEADS * HEAD_DIM
D_FF = 4 * D_MODEL
RET_THETA = 10000.0
ROPE_THETA = 500000.0
EPS = 1e-6
LANES = 128
SUBLANES = 8
VMEM_LIMIT = 56 * 1024 * 1024
NEG = -1e30
INT_MIN = -(2 ** 31)

COL_GA = 0
COL_GB = 2048
COL_RQ = 4096
COL_RK = 5120
COL_RV = 6144
COL_RG = 7168
COL_DQ = 8192
COL_IQ = 9216
COL_DK = 10240
COL_DV = 10496
COL_IK = 10752
N_PROJ = 11264
PROJ_TN = 512
MODE_PLAIN, MODE_RET, MODE_RETK, MODE_DSA, MODE_IDX, MODE_DKV, MODE_IKW = range(7)
PROJ_MODES = ((MODE_PLAIN,) * 8 + (MODE_RET,) * 2 + (MODE_RETK,) * 2 + (MODE_PLAIN,) * 4
              + (MODE_DSA,) * 2 + (MODE_IDX,) * 2 + (MODE_DKV, MODE_IKW))
TAB_RET_C, TAB_RET_S, TAB_DSA_C, TAB_DSA_A, TAB_DSA_B, TAB_IDX_C, TAB_IDX_A, TAB_IDX_B = range(8)
TAB_W = 8 * LANES


def _cparams(*sem):
    return pltpu.CompilerParams(dimension_semantics=sem, vmem_limit_bytes=VMEM_LIMIT)


def _dot(a, b):
    return jnp.dot(a, b, preferred_element_type=F32)


def _dot_nt(a, b):
    return lax.dot_general(a, b, (((1,), (1,)), ((), ())), preferred_element_type=F32)


def _dot_tn(a, b):
    return lax.dot_general(a, b, (((0,), (0,)), ((), ())), preferred_element_type=F32)


def _rmsnorm_bf16(x, g):
    ms = jnp.mean(x * x, axis=-1, keepdims=True)
    return ((x * lax.rsqrt(ms + EPS)) * g).astype(BF16)


def _sigmoid(x):
    return 1.0 / (1.0 + jnp.exp(-x))


def _tab(tab_ref, which):
    return tab_ref[:, which * LANES:(which + 1) * LANES]


def _rope_cols(o_ref, g, c, sa, sb, shift):
    a = o_ref[:, g * LANES:(g + 1) * LANES]
    if sb is None:
        r = a * c + pltpu.roll(a, shift, 1) * sa
    else:
        r = a * c + pltpu.roll(a, LANES - shift, 1) * sa + pltpu.roll(a, shift, 1) * sb
    o_ref[:, g * LANES:(g + 1) * LANES] = r


def _norm_proj_kernel(x_ref, g_ref, w_ref, tab_ref, o_ref, u_ref, *, modes):
    j = pl.program_id(1)

    @pl.when(j == 0)
    def _():
        u_ref[...] = _rmsnorm_bf16(x_ref[...], g_ref[...])

    o_ref[...] = _dot(u_ref[...], w_ref[...])
    if tab_ref is None:
        return
    groups = o_ref.shape[1] // LANES

    def blocks_of(mode):
        return [b for b, m in enumerate(modes) if m == mode]

    def when_mode(mode):
        bs = blocks_of(mode)
        return pl.when((j >= bs[0]) & (j <= bs[-1]))

    @when_mode(MODE_RET)
    def _():
        for g in range(groups):
            _rope_cols(o_ref, g, _tab(tab_ref, TAB_RET_C), _tab(tab_ref, TAB_RET_S), None, LANES // 2)

    @when_mode(MODE_RETK)
    def _():
        scale = HEAD_DIM ** -0.5
        for g in range(groups):
            _rope_cols(o_ref, g, _tab(tab_ref, TAB_RET_C) * scale, _tab(tab_ref, TAB_RET_S) * scale, None,
                       LANES // 2)

    @when_mode(MODE_DSA)
    def _():
        for g in range(groups):
            _rope_cols(o_ref, g, _tab(tab_ref, TAB_DSA_C), _tab(tab_ref, TAB_DSA_A), _tab(tab_ref, TAB_DSA_B), 16)

    @when_mode(MODE_IDX)
    def _():
        for g in range(groups):
            _rope_cols(o_ref, g, _tab(tab_ref, TAB_IDX_C), _tab(tab_ref, TAB_IDX_A), _tab(tab_ref, TAB_IDX_B), 8)

    @when_mode(MODE_DKV)
    def _():
        for g in range(2):
            _rope_cols(o_ref, g, _tab(tab_ref, TAB_DSA_C), _tab(tab_ref, TAB_DSA_A), _tab(tab_ref, TAB_DSA_B), 16)

    @when_mode(MODE_IKW)
    def _():
        lane = lax.broadcasted_iota(jnp.int32, (1, LANES), 1)
        is_ik = lane < IDX_DH
        c = jnp.where(is_ik, _tab(tab_ref, TAB_IDX_C), 1.0)
        sa = jnp.where(is_ik, _tab(tab_ref, TAB_IDX_A), 0.0)
        sb = jnp.where(is_ik, _tab(tab_ref, TAB_IDX_B), 0.0)
        _rope_cols(o_ref, 0, c, sa, sb, 8)


def _norm_proj(x, g, w, tab, modes, tm, tn, pos_blocks):
    m, d = x.shape
    n = w.shape[1]
    grid = (m // tm, n // tn)
    in_specs = [
        pl.BlockSpec((tm, d), lambda i, j: (i, 0)),
        pl.BlockSpec((1, d), lambda i, j: (0, 0)),
        pl.BlockSpec((d, tn), lambda i, j: (0, j)),
    ]
    args = [x, g, w]
    if tab is not None:
        in_specs.append(pl.BlockSpec((tm, TAB_W), lambda i, j: (i % pos_blocks, 0)))
        args.append(tab)
        body = functools.partial(_norm_proj_kernel, modes=modes)
    else:
        def body(x_ref, g_ref, w_ref, o_ref, u_ref):
            _norm_proj_kernel(x_ref, g_ref, w_ref, None, o_ref, u_ref, modes=None)
    return pl.pallas_call(
        body,
        grid=grid,
        in_specs=in_specs,
        out_specs=pl.BlockSpec((tm, tn), lambda i, j: (i, j)),
        out_shape=jax.ShapeDtypeStruct((m, n), F32),
        scratch_shapes=[pltpu.VMEM((tm, d), BF16)],
        compiler_params=_cparams("parallel", "arbitrary"),
        name="norm_proj",
    )(*args)


def _rope_tables(pos):
    posf = pos.astype(F32)[:, None]
    p = pos.shape[0]

    def cs(half, theta):
        inv = theta ** (-jnp.arange(half, dtype=F32) / half)
        ang = posf * inv[None, :]
        return jnp.cos(ang), jnp.sin(ang)

    ones = lambda n: jnp.ones((p, n), F32)
    zeros = lambda n: jnp.zeros((p, n), F32)
    c, s = cs(HEAD_DIM // 2, RET_THETA)
    ret_c = jnp.concatenate([c, c], 1)
    ret_s = jnp.concatenate([-s, s], 1)
    c, s = cs(16, ROPE_THETA)
    dsa_c = jnp.concatenate([c, c, ones(96)], 1)
    dsa_a = jnp.concatenate([-s, zeros(112)], 1)
    dsa_b = jnp.concatenate([zeros(16), s, zeros(96)], 1)
    c, s = cs(8, ROPE_THETA)
    idx_c = jnp.tile(jnp.concatenate([c, c, ones(48)], 1), (1, 2))
    idx_a = jnp.tile(jnp.concatenate([-s, zeros(56)], 1), (1, 2))
    idx_b = jnp.tile(jnp.concatenate([zeros(8), s, zeros(48)], 1), (1, 2))
    return jnp.concatenate([ret_c, ret_s, dsa_c, dsa_a, dsa_b, idx_c, idx_a, idx_b], 1)


def _arrange_w_in(w_in):
    o = 0
    parts = {}
    for name, n in (("rq", 1024), ("rk", 1024), ("rv", 1024), ("rg", 1024), ("dq", 1024), ("dk", 256),
                    ("dv", 256), ("iq", 1024), ("ik", 64), ("iw", 16), ("ga", 2048), ("gb", 2048)):
        parts[name] = w_in[:, o:o + n]
        o += n
    pad = jnp.zeros((w_in.shape[0], N_PROJ - COL_IK - IDX_DH - IDX_HEADS), w_in.dtype)
    order = ("ga", "gb", "rq", "rk", "rv", "rg", "dq", "iq", "dk", "dv", "ik", "iw")
    return jnp.concatenate([parts[k] for k in order] + [pad], axis=1).astype(BF16)


def _log_decay():
    return jnp.log1p(-jnp.exp2(-5.0 - jnp.arange(RET_HEADS, dtype=F32)))


def _groupnorm_gate(o, gate, gn):
    mu = jnp.mean(o, axis=-1, keepdims=True)
    d = o - mu
    var = jnp.mean(d * d, axis=-1, keepdims=True)
    n = d * lax.rsqrt(var + EPS) * gn
    return (gate * _sigmoid(gate) * n).astype(BF16)


def _ret_prompt_kernel(q_ref, k_ref, v_ref, g_ref, gn_ref, intra_ref, qdec_ref, kdec_ref, cdec_ref,
                       y_ref, st_ref, s_scr):
    c = pl.program_id(1)

    @pl.when(c == 0)
    def _():
        s_scr[...] = jnp.zeros_like(s_scr)

    for h in range(RET_HEADS):
        sl = slice(h * HEAD_DIM, (h + 1) * HEAD_DIM)
        k = k_ref[:, sl]
        qb = q_ref[:, sl].astype(BF16)
        kb = k.astype(BF16)
        vb = v_ref[:, sl].astype(BF16)
        s = _dot_nt(qb, kb) * intra_ref[h]
        state = s_scr[h]
        o = _dot(s.astype(BF16), vb) + _dot(qb, state.astype(BF16)) * qdec_ref[h]
        kd = (k * kdec_ref[h]).astype(BF16)
        s_scr[h] = state * cdec_ref[h] + _dot_tn(kd, vb)
        y_ref[:, sl] = _groupnorm_gate(o, g_ref[:, sl], gn_ref[:, sl])

    @pl.when(c == pl.num_programs(1) - 1)
    def _():
        st_ref[0] = s_scr[...]


def _ret_prompt(proj, gn, batch, seq, chunk):
    nc = seq // chunk
    log_g = _log_decay()
    j = jnp.arange(chunk, dtype=F32)
    diff = j[:, None] - j[None, :]
    intra = jnp.where(diff >= 0, jnp.exp(log_g[:, None, None] * jnp.maximum(diff, 0.0)), 0.0)
    ones = jnp.ones((1, 1, HEAD_DIM), F32)
    qdec = jnp.exp(log_g[:, None] * (j[None, :] + 1.0))[:, :, None] * ones
    kdec = jnp.exp(log_g[:, None] * (chunk - 1.0 - j[None, :]))[:, :, None] * ones
    cdec = jnp.exp(log_g * chunk)[:, None, None] * ones
    w = RET_HEADS * HEAD_DIM
    col = lambda off: (lambda b, c: (b * nc + c, off // w))
    const3 = lambda b, c: (0, 0, 0)
    return pl.pallas_call(
        _ret_prompt_kernel,
        grid=(batch, nc),
        in_specs=[
            pl.BlockSpec((chunk, w), col(COL_RQ)),
            pl.BlockSpec((chunk, w), col(COL_RK)),
            pl.BlockSpec((chunk, w), col(COL_RV)),
            pl.BlockSpec((chunk, w), col(COL_RG)),
            pl.BlockSpec((1, w), lambda b, c: (0, 0)),
            pl.BlockSpec((RET_HEADS, chunk, chunk), const3),
            pl.BlockSpec((RET_HEADS, chunk, HEAD_DIM), const3),
            pl.BlockSpec((RET_HEADS, chunk, HEAD_DIM), const3),
            pl.BlockSpec((RET_HEADS, 1, HEAD_DIM), const3),
        ],
        out_specs=[
            pl.BlockSpec((chunk, w), lambda b, c: (b * nc + c, 0)),
            pl.BlockSpec((1, RET_HEADS, HEAD_DIM, HEAD_DIM), lambda b, c: (b, 0, 0, 0)),
        ],
        out_shape=[
            jax.ShapeDtypeStruct((batch * seq, w), BF16),
            jax.ShapeDtypeStruct((batch, RET_HEADS, HEAD_DIM, HEAD_DIM), F32),
        ],
        scratch_shapes=[pltpu.VMEM((RET_HEADS, HEAD_DIM, HEAD_DIM), F32)],
        compiler_params=_cparams("parallel", "arbitrary"),
        name="ret_prompt",
    )(proj, proj, proj, proj, gn, intra, qdec, kdec, cdec)


def _ret_sample_kernel(q_ref, k_ref, v_ref, g_ref, gn_ref, st_ref, intra_ref, qdec_ref, kdec_ref, cdec_ref,
                       y_ref, so_ref, *, n_batch, rows):
    k = k_ref[...]
    qb = q_ref[...].astype(BF16)
    kb = k.astype(BF16)
    vb = v_ref[...].astype(BF16)
    s = _dot_nt(qb, kb) * intra_ref[0]
    kd = k * kdec_ref[0]
    row_batch = lax.broadcasted_iota(jnp.int32, kd.shape, 0) // rows
    inter = []
    for b in range(n_batch):
        state = st_ref[b, 0]
        inter.append(_dot(qb[b * rows:(b + 1) * rows], state.astype(BF16)))
        kd_b = jnp.where(row_batch == b, kd, 0.0).astype(BF16)
        so_ref[b, 0] = state * cdec_ref[0] + _dot_tn(kd_b, vb)
    o = _dot(s.astype(BF16), vb) + jnp.concatenate(inter, axis=0) * qdec_ref[0]
    y_ref[...] = _groupnorm_gate(o, g_ref[...], gn_ref[...])


def _ret_sample(proj, gn, state, n_batch, rows, t_valid):
    m = n_batch * rows
    log_g = _log_decay()
    r = jnp.arange(m)
    t = (r % rows).astype(F32)
    same = (r[:, None] // rows) == (r[None, :] // rows)
    diff = t[:, None] - t[None, :]
    intra = jnp.where(same[None] & (diff >= 0)[None], jnp.exp(log_g[:, None, None] * jnp.maximum(diff, 0.0)[None]), 0.0)
    ones = jnp.ones((1, 1, HEAD_DIM), F32)
    qdec = jnp.exp(log_g[:, None] * (t[None, :] + 1.0))[:, :, None] * ones
    kdec = jnp.where(t[None, :] < t_valid, jnp.exp(log_g[:, None] * (t_valid - 1.0 - t[None, :])), 0.0)[:, :, None] * ones
    cdec = jnp.exp(log_g * t_valid)[:, None, None] * ones
    col = lambda off: (lambda h: (0, off // HEAD_DIM + h))
    per_head = lambda h: (h, 0, 0)
    return pl.pallas_call(
        functools.partial(_ret_sample_kernel, n_batch=n_batch, rows=rows),
        grid=(RET_HEADS,),
        in_specs=[
            pl.BlockSpec((m, HEAD_DIM), col(COL_RQ)),
            pl.BlockSpec((m, HEAD_DIM), col(COL_RK)),
            pl.BlockSpec((m, HEAD_DIM), col(COL_RV)),
            pl.BlockSpec((m, HEAD_DIM), col(COL_RG)),
            pl.BlockSpec((1, HEAD_DIM), lambda h: (0, h)),
            pl.BlockSpec((n_batch, 1, HEAD_DIM, HEAD_DIM), lambda h: (0, h, 0, 0)),
            pl.BlockSpec((1, m, m), per_head),
            pl.BlockSpec((1, m, HEAD_DIM), per_head),
            pl.BlockSpec((1, m, HEAD_DIM), per_head),
            pl.BlockSpec((1, 1, HEAD_DIM), per_head),
        ],
        out_specs=[
            pl.BlockSpec((m, HEAD_DIM), lambda h: (0, h)),
            pl.BlockSpec((n_batch, 1, HEAD_DIM, HEAD_DIM), lambda h: (0, h, 0, 0)),
        ],
        out_shape=[
            jax.ShapeDtypeStruct((m, RET_HEADS * HEAD_DIM), BF16),
            jax.ShapeDtypeStruct(state.shape, F32),
        ],
        compiler_params=_cparams("parallel"),
        name="ret_sample",
    )(proj, proj, proj, proj, gn, state, intra, qdec, kdec, cdec)


def _sortable_key(x):
    bits = pltpu.bitcast(x, jnp.int32)
    return bits ^ ((bits >> 31) & jnp.int32(0x7FFFFFFF))


def _kth_largest_key(count_ge, shape, topk):
    def body(b, res):
        cand = res + lax.shift_left(jnp.int32(1), jnp.int32(31) - b)
        return jnp.where(count_ge(cand) >= topk, cand, res)

    return lax.fori_loop(0, 32, body, jnp.full(shape, INT_MIN, jnp.int32))


def _dsa_prompt_kernel(dq_ref, iq_ref, iwq_ref, kall_ref, vall_ref, ikall_ref, o_ref,
                       kbf, vtb, ika, ikb, keys, z_scr, m_scr, l_scr, acc_scr, *, tq, topk):
    i = pl.program_id(1)
    n_chunks = kbf.shape[0]
    scale = HEAD_DIM ** -0.5

    @pl.when(i == 0)
    def _():
        lane = lax.broadcasted_iota(jnp.int32, (tq, LANES), 1)
        for c in range(n_chunks):
            rows = slice(c * tq, (c + 1) * tq)
            kbf[c] = kall_ref[rows, :].astype(BF16)
            for kvh in range(DSA_KV_HEADS):
                sl = slice(kvh * HEAD_DIM, (kvh + 1) * HEAD_DIM)
                vtb[c, sl, :] = vall_ref[rows, sl].T.astype(BF16)
            a = ikall_ref[rows, :]
            ika[c] = jnp.where(lane < IDX_DH, a, 0.0).astype(BF16)
            ikb[c] = jnp.where(lane >= IDX_DH, pltpu.roll(a, IDX_DH, 1), 0.0).astype(BF16)

    w_t = iwq_ref[...].T
    iqb = iq_ref[...].astype(BF16)
    qb = dq_ref[...].astype(BF16)
    t_col = i * tq + lax.broadcasted_iota(jnp.int32, (tq, tq), 1)
    s_row0 = lax.broadcasted_iota(jnp.int32, (tq, tq), 0)

    def score_chunk(c, carry):
        a = ika[c]
        b = ikb[c]
        acc = jnp.zeros((tq, tq), F32)
        for p in range(IDX_HEADS // 2):
            pair = iqb[:, p * LANES:(p + 1) * LANES]
            w0 = w_t[IDX_DH + 2 * p:IDX_DH + 2 * p + 1, :]
            w1 = w_t[IDX_DH + 2 * p + 1:IDX_DH + 2 * p + 2, :]
            acc = acc + jnp.maximum(_dot_nt(a, pair), 0.0) * w0
            acc = acc + jnp.maximum(_dot_nt(b, pair), 0.0) * w1
        acc = jnp.where(s_row0 + c * tq <= t_col, acc, -jnp.inf)
        keys[c] = _sortable_key(acc)
        return carry

    lax.fori_loop(0, i + 1, score_chunk, 0)
    keys[i + 1] = jnp.full((tq, tq), INT_MIN, jnp.int32)

    def count_ge(cand):
        def part(c):
            ge = jnp.where(keys[c] >= cand, 1.0, 0.0)
            return ge.reshape(tq // SUBLANES, SUBLANES, tq).sum(axis=0)

        def pair(j, cnt):
            return cnt + part(2 * j) + part(2 * j + 1)

        cnt = lax.fori_loop(0, (i + 2) // 2, pair, jnp.zeros((SUBLANES, tq), F32))
        return cnt.sum(axis=0, keepdims=True)

    thr = _kth_largest_key(count_ge, (1, tq), topk)

    m_scr[...] = jnp.full_like(m_scr, NEG)
    l_scr[...] = jnp.zeros_like(l_scr)
    acc_scr[...] = jnp.zeros_like(acc_scr)

    def attend_chunk(c, carry):
        bias = jnp.where(s_row0 + c * tq <= t_col, jnp.where(keys[c] >= thr, 0.0, NEG), NEG)
        kc = kbf[c]
        vc = vtb[c]
        kv_slice = lambda h: slice(h // (DSA_HEADS // DSA_KV_HEADS) * HEAD_DIM,
                                   (h // (DSA_HEADS // DSA_KV_HEADS) + 1) * HEAD_DIM)
        z_max = []
        for h in range(DSA_HEADS):
            z = _dot_nt(kc[:, kv_slice(h)], qb[:, h * HEAD_DIM:(h + 1) * HEAD_DIM]) + bias
            z_scr[h] = z
            z_max.append(z.max(axis=0, keepdims=True))
        for h in range(DSA_HEADS):
            m_old = m_scr[h]
            m_new = jnp.maximum(m_old, z_max[h])
            alpha = jnp.exp((m_old - m_new) * scale)
            p = jnp.exp((z_scr[h] - m_new) * scale)
            l_scr[h] = l_scr[h] * alpha + p.sum(axis=0, keepdims=True)
            acc_scr[h] = acc_scr[h] * alpha + _dot(vc[kv_slice(h), :], p.astype(BF16))
            m_scr[h] = m_new
        return carry

    lax.fori_loop(0, i + 1, attend_chunk, 0)

    for h in range(DSA_HEADS):
        o = acc_scr[h] * (1.0 / l_scr[h])
        o_ref[:, h * HEAD_DIM:(h + 1) * HEAD_DIM] = o.T.astype(BF16)


def _dsa_prompt(proj, batch, seq, tq):
    nq = seq // tq
    topk = min(TOPK_MAX, seq // 4)
    qw = DSA_HEADS * HEAD_DIM
    kvw = DSA_KV_HEADS * HEAD_DIM
    return pl.pallas_call(
        functools.partial(_dsa_prompt_kernel, tq=tq, topk=topk),
        grid=(batch, nq),
        in_specs=[
            pl.BlockSpec((tq, qw), lambda b, i: (b * nq + i, COL_DQ // qw)),
            pl.BlockSpec((tq, qw), lambda b, i: (b * nq + i, COL_IQ // qw)),
            pl.BlockSpec((tq, LANES), lambda b, i: (b * nq + i, COL_IK // LANES)),
            pl.BlockSpec((seq, kvw), lambda b, i: (b, COL_DK // kvw)),
            pl.BlockSpec((seq, kvw), lambda b, i: (b, COL_DV // kvw)),
            pl.BlockSpec((seq, LANES), lambda b, i: (b, COL_IK // LANES)),
        ],
        out_specs=pl.BlockSpec((tq, qw), lambda b, i: (b * nq + i, 0)),
        out_shape=jax.ShapeDtypeStruct((batch * seq, qw), BF16),
        scratch_shapes=[
            pltpu.VMEM((nq, tq, kvw), BF16),
            pltpu.VMEM((nq, kvw, tq), BF16),
            pltpu.VMEM((nq, tq, LANES), BF16),
            pltpu.VMEM((nq, tq, LANES), BF16),
            pltpu.VMEM((nq + 1, tq, tq), jnp.int32),
            pltpu.VMEM((DSA_HEADS, tq, tq), F32),
            pltpu.VMEM((DSA_HEADS, 1, tq), F32),
            pltpu.VMEM((DSA_HEADS, 1, tq), F32),
            pltpu.VMEM((DSA_HEADS, HEAD_DIM, tq), F32),
        ],
        compiler_params=_cparams("parallel", "arbitrary"),
        name="dsa_prompt",
    )(proj, proj, proj, proj, proj, proj)


def _idx_sample_kernel(pt_ref, iq_ref, iw_ref, iknew_ref, *rest, pages):
    page_refs = rest[:pages]
    sp_ref, sn_ref = rest[pages:]
    iqb = iq_ref[0].astype(BF16)
    w = iw_ref[0]
    rows = iqb.shape[0] // IDX_HEADS

    def scores(ik):
        d = jnp.maximum(_dot_nt(iqb, ik.astype(BF16)), 0.0) * w
        return d.reshape(rows, IDX_HEADS, LANES).sum(axis=1)

    for p in range(pages):
        sp_ref[0, :, p * PAGE_SIZE:(p + 1) * PAGE_SIZE] = scores(page_refs[p][0])
    t = lax.broadcasted_iota(jnp.int32, (rows, LANES), 0)
    j = lax.broadcasted_iota(jnp.int32, (rows, LANES), 1)
    sn_ref[0] = jnp.where(j <= t, scores(iknew_ref[0]), -jnp.inf)


def _idx_sample(page_table, iq, iw, ik_new, pool_ik, pages):
    n_batch, n_pages = page_table.shape
    rows16 = iq.shape[1]
    rows = rows16 // IDX_HEADS
    past = n_pages * PAGE_SIZE
    page_specs = [pl.BlockSpec((1, PAGE_SIZE, IDX_DH), functools.partial(
        lambda b, g, pt, k: (pt[b, g * pages + k], 0, 0), k=k)) for k in range(pages)]
    grid_spec = pltpu.PrefetchScalarGridSpec(
        num_scalar_prefetch=1,
        grid=(n_batch, n_pages // pages),
        in_specs=[
            pl.BlockSpec((1, rows16, IDX_DH), lambda b, g, pt: (b, 0, 0)),
            pl.BlockSpec((1, rows16, LANES), lambda b, g, pt: (b, 0, 0)),
            pl.BlockSpec((1, LANES, IDX_DH), lambda b, g, pt: (b, 0, 0)),
        ] + page_specs,
        out_specs=[
            pl.BlockSpec((1, rows, pages * PAGE_SIZE), lambda b, g, pt: (b, 0, g)),
            pl.BlockSpec((1, rows, LANES), lambda b, g, pt: (b, 0, 0)),
        ],
    )
    return pl.pallas_call(
        functools.partial(_idx_sample_kernel, pages=pages),
        grid_spec=grid_spec,
        out_shape=[
            jax.ShapeDtypeStruct((n_batch, rows, past), F32),
            jax.ShapeDtypeStruct((n_batch, rows, LANES), F32),
        ],
        compiler_params=_cparams("parallel", "arbitrary"),
        name="idx_sample",
    )(page_table, iq, iw, ik_new, *([pool_ik] * pages))


def _dsa_sample_kernel(pt_ref, sp_ref, sn_ref, q_ref, knew_ref, vnew_ref, *rest, pages, topk, rows):
    k_refs = rest[:pages]
    v_refs = rest[pages:2 * pages]
    o_ref, selp, seln, m_scr, l_scr, acc_scr = rest[2 * pages:]
    g = pl.program_id(1)
    scale = HEAD_DIM ** -0.5
    group = DSA_HEADS // DSA_KV_HEADS

    @pl.when(g == 0)
    def _():
        kp = _sortable_key(sp_ref[0])
        kn = _sortable_key(sn_ref[0])

        def count_ge(cand):
            cp = jnp.where(kp >= cand, 1.0, 0.0).sum(axis=1, keepdims=True)
            cn = jnp.where(kn >= cand, 1.0, 0.0).sum(axis=1, keepdims=True)
            return cp + cn

        thr = _kth_largest_key(count_ge, (rows, 1), topk)
        sel_past = jnp.where(kp >= thr, 0.0, NEG)
        for pg in range(selp.shape[0]):
            selp[pg] = sel_past[:, pg * PAGE_SIZE:(pg + 1) * PAGE_SIZE]
        t = lax.broadcasted_iota(jnp.int32, (rows, LANES), 0)
        j = lax.broadcasted_iota(jnp.int32, (rows, LANES), 1)
        seln[...] = jnp.where(j <= t, jnp.where(kn >= thr, 0.0, NEG), NEG)
        m_scr[...] = jnp.full_like(m_scr, NEG)
        l_scr[...] = jnp.zeros_like(l_scr)
        acc_scr[...] = jnp.zeros_like(acc_scr)

    def head_rows(ref, kvh):
        return ref[0, pl.ds(kvh, PAGE_SIZE, stride=DSA_KV_HEADS), :].astype(BF16)

    def attend(k_pages, v_pages, bias):
        bias_g = jnp.concatenate([bias] * group, axis=0)
        for kvh in range(DSA_KV_HEADS):
            qb = q_ref[0, kvh].astype(BF16)
            raw = jnp.concatenate([_dot_nt(qb, head_rows(k, kvh)) for k in k_pages], axis=1)
            z = jnp.where(bias_g < 0.0, NEG, raw)
            m_old = m_scr[kvh]
            m_new = jnp.maximum(m_old, z.max(axis=1, keepdims=True))
            alpha = jnp.exp((m_old - m_new) * scale)
            p = jnp.exp((z - m_new) * scale)
            l_scr[kvh] = l_scr[kvh] * alpha + p.sum(axis=1, keepdims=True)
            pb = p.astype(BF16)
            pv = _dot(pb[:, :PAGE_SIZE], head_rows(v_pages[0], kvh))
            for n in range(1, len(v_pages)):
                pv = pv + _dot(pb[:, n * PAGE_SIZE:(n + 1) * PAGE_SIZE], head_rows(v_pages[n], kvh))
            acc_scr[kvh] = acc_scr[kvh] * alpha + pv
            m_scr[kvh] = m_new

    attend(k_refs, v_refs, jnp.concatenate([selp[g * pages + p] for p in range(pages)], axis=1))

    @pl.when(g == pl.num_programs(1) - 1)
    def _():
        attend([knew_ref], [vnew_ref], seln[...])
        for kvh in range(DSA_KV_HEADS):
            o_ref[0, kvh] = (acc_scr[kvh] * (1.0 / l_scr[kvh])).astype(BF16)


def _dsa_sample(page_table, scores_past, scores_new, q, k_new, v_new, pool_k, pool_v, pages, topk):
    n_batch, n_pages = page_table.shape
    rows = scores_past.shape[1]
    past = n_pages * PAGE_SIZE
    page_rows = DSA_KV_HEADS * PAGE_SIZE
    grows = q.shape[2]
    page_spec = lambda k: pl.BlockSpec((1, page_rows, HEAD_DIM), functools.partial(
        lambda b, g, pt, k: (pt[b, g * pages + k], 0, 0), k=k))
    per_batch3 = lambda b, g, pt: (b, 0, 0)
    grid_spec = pltpu.PrefetchScalarGridSpec(
        num_scalar_prefetch=1,
        grid=(n_batch, n_pages // pages),
        in_specs=[
            pl.BlockSpec((1, rows, past), per_batch3),
            pl.BlockSpec((1, rows, LANES), per_batch3),
            pl.BlockSpec((1, DSA_KV_HEADS, grows, HEAD_DIM), lambda b, g, pt: (b, 0, 0, 0)),
            pl.BlockSpec((1, page_rows, HEAD_DIM), per_batch3),
            pl.BlockSpec((1, page_rows, HEAD_DIM), per_batch3),
        ] + [page_spec(k) for k in range(pages)] * 2,
        out_specs=pl.BlockSpec((1, DSA_KV_HEADS, grows, HEAD_DIM), lambda b, g, pt: (b, 0, 0, 0)),
        scratch_shapes=[
            pltpu.VMEM((n_pages, rows, PAGE_SIZE), F32),
            pltpu.VMEM((rows, LANES), F32),
            pltpu.VMEM((DSA_KV_HEADS, grows, 1), F32),
            pltpu.VMEM((DSA_KV_HEADS, grows, 1), F32),
            pltpu.VMEM((DSA_KV_HEADS, grows, HEAD_DIM), F32),
        ],
    )
    return pl.pallas_call(
        functools.partial(_dsa_sample_kernel, pages=pages, topk=topk, rows=rows),
        grid_spec=grid_spec,
        out_shape=jax.ShapeDtypeStruct((n_batch, DSA_KV_HEADS, grows, HEAD_DIM), BF16),
        compiler_params=_cparams("parallel", "arbitrary"),
        name="dsa_sample",
    )(page_table, scores_past, scores_new, q, k_new, v_new, *([pool_k] * pages), *([pool_v] * pages))


def _merge_kernel(x_ref, ry_ref, do_ref, ga_ref, gb_ref, wr_ref, wd_ref, wo_ref, o_ref):
    ya = _dot(ry_ref[...], wr_ref[...])
    yb = _dot(do_ref[...], wd_ref[...])
    merged = _sigmoid(ga_ref[...]) * ya + _sigmoid(gb_ref[...]) * yb
    o_ref[...] = x_ref[...] + _dot(merged.astype(BF16), wo_ref[...])


def _merge(x, ret_y, dsa_o, proj, w_ret_out, w_dsa_out, w_o, tm):
    m, d = x.shape
    w = ret_y.shape[1]
    const = lambda i: (0, 0)
    return pl.pallas_call(
        _merge_kernel,
        grid=(m // tm,),
        in_specs=[
            pl.BlockSpec((tm, d), lambda i: (i, 0)),
            pl.BlockSpec((tm, w), lambda i: (i, 0)),
            pl.BlockSpec((tm, w), lambda i: (i, 0)),
            pl.BlockSpec((tm, d), lambda i: (i, COL_GA // d)),
            pl.BlockSpec((tm, d), lambda i: (i, COL_GB // d)),
            pl.BlockSpec((w, d), const),
            pl.BlockSpec((w, d), const),
            pl.BlockSpec((d, d), const),
        ],
        out_specs=pl.BlockSpec((tm, d), lambda i: (i, 0)),
        out_shape=jax.ShapeDtypeStruct((m, d), F32),
        compiler_params=_cparams("parallel"),
        name="merge_out",
    )(x, ret_y, dsa_o, proj, proj, w_ret_out, w_dsa_out, w_o)


def _mem_attn_kernel(h_ref, g_ref, wq_ref, mk_ref, mv_ref, wo_ref, o_ref, *, rows_per_batch):
    h = h_ref[...]
    qm = _dot(_rmsnorm_bf16(h, g_ref[...]), wq_ref[...])
    tm = h.shape[0]
    nk = mk_ref.shape[0]
    scale = HEAD_DIM ** -0.5
    if rows_per_batch is not None:
        rb = lax.broadcasted_iota(jnp.int32, (tm, nk), 0) // rows_per_batch
        kb = lax.broadcasted_iota(jnp.int32, (tm, nk), 1) // MEM_LEN
        same = rb == kb
    outs = []
    for hd in range(MEM_HEADS):
        sl = slice(hd * HEAD_DIM, (hd + 1) * HEAD_DIM)
        z = _dot_nt(qm[:, sl].astype(BF16), mk_ref[:, sl].astype(BF16)) * scale
        if rows_per_batch is not None:
            z = jnp.where(same, z, NEG)
        p = jnp.exp(z - z.max(axis=-1, keepdims=True))
        p = p / p.sum(axis=-1, keepdims=True)
        outs.append(_dot(p.astype(BF16), mv_ref[:, sl].astype(BF16)))
    om = jnp.concatenate(outs, axis=1).astype(BF16)
    o_ref[...] = h + _dot(om, wo_ref[...])


def _mem_attn(h, g, w_mq, mk, mv, w_mo, *, tm, batch_tiles, mk_col, mv_col, nk, rows_per_batch):
    m, d = h.shape
    const = lambda i: (0, 0)
    if batch_tiles:
        kmap = lambda col: (lambda i: (i // batch_tiles, col))
    else:
        kmap = lambda col: (lambda i: (0, col))
    return pl.pallas_call(
        functools.partial(_mem_attn_kernel, rows_per_batch=rows_per_batch),
        grid=(m // tm,),
        in_specs=[
            pl.BlockSpec((tm, d), lambda i: (i, 0)),
            pl.BlockSpec((1, d), const),
            pl.BlockSpec((d, MEM_W), const),
            pl.BlockSpec((nk, MEM_W), kmap(mk_col)),
            pl.BlockSpec((nk, MEM_W), kmap(mv_col)),
            pl.BlockSpec((MEM_W, d), const),
        ],
        out_specs=pl.BlockSpec((tm, d), lambda i: (i, 0)),
        out_shape=jax.ShapeDtypeStruct((m, d), F32),
        compiler_params=_cparams("parallel"),
        name="mem_attn",
    )(h, g, w_mq, mk, mv, w_mo)


def _mlp_kernel(h_ref, g_ref, wu_ref, wd_ref, gf_ref, o_ref, u_ref, acc_ref):
    j = pl.program_id(1)

    @pl.when(j == 0)
    def _():
        u_ref[...] = _rmsnorm_bf16(h_ref[...], g_ref[...])
        acc_ref[...] = jnp.zeros_like(acc_ref)

    a = jnp.maximum(_dot(u_ref[...], wu_ref[...]), 0.0)
    acc_ref[...] += _dot((a * a).astype(BF16), wd_ref[...])

    @pl.when(j == pl.num_programs(1) - 1)
    def _():
        y = h_ref[...] + acc_ref[...]
        ms = jnp.mean(y * y, axis=-1, keepdims=True)
        o_ref[...] = (y * lax.rsqrt(ms + EPS)) * gf_ref[...]


def _mlp(h, g, w_up, w_down, g_final, tm, tf):
    m, d = h.shape
    ff = w_up.shape[1]
    return pl.pallas_call(
        _mlp_kernel,
        grid=(m // tm, ff // tf),
        in_specs=[
            pl.BlockSpec((tm, d), lambda i, j: (i, 0)),
            pl.BlockSpec((1, d), lambda i, j: (0, 0)),
            pl.BlockSpec((d, tf), lambda i, j: (0, j)),
            pl.BlockSpec((tf, d), lambda i, j: (j, 0)),
            pl.BlockSpec((1, d), lambda i, j: (0, 0)),
        ],
        out_specs=pl.BlockSpec((tm, d), lambda i, j: (i, 0)),
        out_shape=jax.ShapeDtypeStruct((m, d), F32),
        scratch_shapes=[pltpu.VMEM((tm, d), BF16), pltpu.VMEM((tm, d), F32)],
        compiler_params=_cparams("parallel", "arbitrary"),
        name="mlp_final",
    )(h, g, w_up, w_down, g_final)


def _pick_tile(n, pref):
    t = min(n, pref)
    while n % t:
        t //= 2
    return t


def kernel(x_prompt, x_sample, mem_prompt, cache_k, cache_v, cache_idx_k, state_ret, cache_mem_k, cache_mem_v,
           page_table, g_mix, w_in, gn_ret, w_ret_out, w_dsa_out, w_o, g_mem, g_memkv, w_mq, w_mk, w_mv, w_mo,
           g_mlp, w_up, w_down, g_final):
    assert w_in.shape[0] == 1, "one layer"
    batch, seq, d = x_prompt.shape
    n_dec, t_dec, _ = x_sample.shape
    n_pages = page_table.shape[1]
    past = n_pages * PAGE_SIZE
    mem_len = mem_prompt.shape[1]
    assert mem_len == MEM_LEN and t_dec <= SUBLANES
    row = lambda v: v.reshape(1, -1)

    w_proj = _arrange_w_in(w_in[0])
    w_ret_out_b, w_dsa_out_b, w_o_b = (w[0].astype(BF16) for w in (w_ret_out, w_dsa_out, w_o))
    w_mq_b, w_mo_b = w_mq[0].astype(BF16), w_mo[0].astype(BF16)
    w_mkv_b = jnp.concatenate([w_mk[0], w_mv[0]], axis=1).astype(BF16)
    w_up_b, w_down_b = w_up[0].astype(BF16), w_down[0].astype(BF16)
    g_mix_r, gn_r, g_mem_r, g_memkv_r, g_mlp_r, g_final_r = (
        row(v) for v in (g_mix[0], gn_ret[0], g_mem[0], g_memkv[0], g_mlp[0], g_final))

    m_p = batch * seq
    xp = x_prompt.reshape(m_p, d)
    tm_p = _pick_tile(seq, 512)
    tm_proj = _pick_tile(seq, 1024)
    tab_p = _rope_tables(jnp.arange(seq, dtype=jnp.int32))
    proj_p = _norm_proj(xp, g_mix_r, w_proj, tab_p, PROJ_MODES, tm_proj, PROJ_TN, seq // tm_proj)
    ret_y_p, ret_state_p = _ret_prompt(proj_p, gn_r, batch, seq, _pick_tile(seq, 256))
    dsa_o_p = _dsa_prompt(proj_p, batch, seq, _pick_tile(seq, 256))
    kv_p = _norm_proj(mem_prompt.reshape(batch * mem_len, d), g_memkv_r, w_mkv_b, None, None, mem_len,
                      PROJ_TN, 1)
    tm_t = _pick_tile(seq, 256)
    h_p = _merge(xp, ret_y_p, dsa_o_p, proj_p, w_ret_out_b, w_dsa_out_b, w_o_b, tm_t)
    h_p = _mem_attn(h_p, g_mem_r, w_mq_b, kv_p, kv_p, w_mo_b, tm=tm_t, batch_tiles=seq // tm_t,
                    mk_col=0, mv_col=1, nk=mem_len, rows_per_batch=None)
    y_p = _mlp(h_p, g_mlp_r, w_up_b, w_down_b, g_final_r, tm_p, 512)

    rows = SUBLANES
    m_s = n_dec * rows
    xs = jnp.pad(x_sample, ((0, 0), (0, rows - t_dec), (0, 0))).reshape(m_s, d)
    pos_s = past + (jnp.arange(m_s, dtype=jnp.int32) % rows)
    proj_s = _norm_proj(xs, g_mix_r, w_proj, _rope_tables(pos_s), PROJ_MODES, m_s, PROJ_TN, 1)
    ret_y_s, ret_state_s = _ret_sample(proj_s, gn_r, state_ret[0], n_dec, rows, t_dec)

    proj_s3 = proj_s.reshape(n_dec, rows, N_PROJ)
    iq_s = proj_s3[:, :, COL_IQ:COL_IQ + IDX_HEADS * IDX_DH].reshape(n_dec, rows * IDX_HEADS, IDX_DH)
    iw_s = proj_s3[:, :, COL_IK + IDX_DH:COL_IK + IDX_DH + IDX_HEADS].reshape(n_dec, rows * IDX_HEADS, 1)
    iw_s = jnp.broadcast_to(iw_s, (n_dec, rows * IDX_HEADS, LANES))
    kvw = DSA_KV_HEADS * HEAD_DIM
    page_rows = DSA_KV_HEADS * PAGE_SIZE
    pad_keys = lambda a: jnp.pad(a, ((0, 0), (0, PAGE_SIZE - rows), (0, 0)))
    ik_new = pad_keys(proj_s3[:, :, COL_IK:COL_IK + IDX_DH])
    k_new = pad_keys(proj_s3[:, :, COL_DK:COL_DK + kvw]).reshape(n_dec, page_rows, HEAD_DIM)
    v_new = pad_keys(proj_s3[:, :, COL_DV:COL_DV + kvw]).reshape(n_dec, page_rows, HEAD_DIM)
    scores_past, scores_new = _idx_sample(page_table, iq_s, iw_s, ik_new,
                                          cache_idx_k.reshape(-1, PAGE_SIZE, IDX_DH), _pick_tile(n_pages, 16))
    group = DSA_HEADS // DSA_KV_HEADS
    dq_s = proj_s3[:, :, COL_DQ:COL_DQ + DSA_HEADS * HEAD_DIM].reshape(n_dec, rows, DSA_KV_HEADS, group, HEAD_DIM)
    dq_s = dq_s.transpose(0, 2, 3, 1, 4).reshape(n_dec, DSA_KV_HEADS, group * rows, HEAD_DIM)
    topk_s = min(TOPK_MAX, (past + t_dec) // 4)
    dsa_o_s = _dsa_sample(page_table, scores_past, scores_new, dq_s, k_new, v_new,
                          cache_k.reshape(-1, page_rows, HEAD_DIM), cache_v.reshape(-1, page_rows, HEAD_DIM),
                          _pick_tile(n_pages, 16), topk_s)
    dsa_o_s = dsa_o_s.reshape(n_dec, DSA_KV_HEADS, group, rows, HEAD_DIM).transpose(0, 3, 1, 2, 4)
    dsa_o_s = dsa_o_s.reshape(m_s, DSA_HEADS * HEAD_DIM)

    h_s = _merge(xs, ret_y_s, dsa_o_s, proj_s, w_ret_out_b, w_dsa_out_b, w_o_b, m_s)
    h_s = _mem_attn(h_s, g_mem_r, w_mq_b, cache_mem_k[0].reshape(n_dec * mem_len, MEM_W),
                    cache_mem_v[0].reshape(n_dec * mem_len, MEM_W), w_mo_b, tm=m_s, batch_tiles=0,
                    mk_col=0, mv_col=0, nk=n_dec * mem_len, rows_per_batch=rows)
    y_s = _mlp(h_s, g_mlp_r, w_up_b, w_down_b, g_final_r, m_s, 512)

    def rows_p(col, width, tail):
        return proj_p[:, col:col + width].reshape((1, batch, seq) + tail)

    def rows_s(col, width, tail):
        return proj_s3[:, :t_dec, col:col + width].reshape((1, n_dec, t_dec) + tail)

    kv_shape = (DSA_KV_HEADS, HEAD_DIM)
    return (
        y_p.reshape(batch, seq, d),
        y_s.reshape(n_dec, rows, d)[:, :t_dec],
        ret_state_p[None],
        rows_p(COL_DK, kvw, kv_shape),
        rows_p(COL_DV, kvw, kv_shape),
        rows_p(COL_IK, IDX_DH, (IDX_DH,)),
        kv_p[:, :MEM_W].reshape(1, batch, mem_len, MEM_HEADS, HEAD_DIM),
        kv_p[:, MEM_W:].reshape(1, batch, mem_len, MEM_HEADS, HEAD_DIM),
        ret_state_s[None],
        rows_s(COL_DK, kvw, kv_shape),
        rows_s(COL_DV, kvw, kv_shape),
        rows_s(COL_IK, IDX_DH, (IDX_DH,)),
    )
```

```python
import functools
import math

import jax
import jax.numpy as jnp
import numpy as np
from jax import lax
from jax.experimental import pallas as pl
from jax.experimental.pallas import tpu as pltpu

F32 = jnp.float32
BF16 = jnp.bfloat16

D_MODEL = 2048
RET_HEADS = 8
HEAD_DIM = 128
DSA_HEADS = 8
DSA_KV_HEADS = 2
IDX_HEADS = 16
IDX_DH = 64
TOPK_MAX = 256
PAGE_SIZE = 128
MEM_LEN = 256
MEM_HEADS = 4
MEM_W = MEM_HEADS * HEAD_DIM
D_FF = 4 * D_MODEL
RET_THETA = 10000.0
ROPE_THETA = 500000.0
EPS = 1e-6
LANES = 128
SUBLANES = 8
ROW_ALIGN = 16
VMEM_LIMIT = 56 * 1024 * 1024
NEG = -1e30
INT_MIN = -(2 ** 31)

COL_GA = 0
COL_GB = 2048
COL_RQ = 4096
COL_RK = 5120
COL_RV = 6144
COL_RG = 7168
COL_DQ = 8192
COL_IQ = 9216
COL_DK = 10240
COL_DV = 10496
COL_IK = 10752
N_PROJ = 11264
PROJ_TN = 512
MODE_PLAIN, MODE_RET, MODE_RETK, MODE_DSA, MODE_IDX, MODE_DKV, MODE_IKW = range(7)
PROJ_MODES = ((MODE_PLAIN,) * 8 + (MODE_RET,) * 2 + (MODE_RETK,) * 2 + (MODE_PLAIN,) * 4
              + (MODE_DSA,) * 2 + (MODE_IDX,) * 2 + (MODE_DKV, MODE_IKW))
TAB_RET_C, TAB_RET_S, TAB_DSA_C, TAB_DSA_A, TAB_DSA_B, TAB_IDX_C, TAB_IDX_A, TAB_IDX_B = range(8)
TAB_W = 8 * LANES


def _cparams(*sem):
    return pltpu.CompilerParams(dimension_semantics=sem, vmem_limit_bytes=VMEM_LIMIT)


def _dot(a, b):
    return jnp.dot(a, b, preferred_element_type=F32)


def _dot_nt(a, b):
    return lax.dot_general(a, b, (((1,), (1,)), ((), ())), preferred_element_type=F32)


def _dot_tn(a, b):
    return lax.dot_general(a, b, (((0,), (0,)), ((), ())), preferred_element_type=F32)


def _rmsnorm_bf16(x, g):
    ms = jnp.mean(x * x, axis=-1, keepdims=True)
    return ((x * lax.rsqrt(ms + EPS)) * g).astype(BF16)


def _sigmoid(x):
    return 1.0 / (1.0 + jnp.exp(-x))


def _tab(tab_ref, which):
    return tab_ref[:, which * LANES:(which + 1) * LANES]


def _rope_cols(o_ref, g, c, sa, sb, shift):
    a = o_ref[:, g * LANES:(g + 1) * LANES]
    if sb is None:
        r = a * c + pltpu.roll(a, shift, 1) * sa
    else:
        r = a * c + pltpu.roll(a, LANES - shift, 1) * sa + pltpu.roll(a, shift, 1) * sb
    o_ref[:, g * LANES:(g + 1) * LANES] = r


def _norm_proj_kernel(x_ref, g_ref, w_ref, tab_ref, o_ref, u_ref, *, modes):
    j = pl.program_id(1)

    @pl.when(j == 0)
    def _():
        u_ref[...] = _rmsnorm_bf16(x_ref[...], g_ref[...])

    if tab_ref is None:
        o_ref[...] = _dot(u_ref[...], w_ref[...])
        return
    o_ref[...] = _dot_nt(u_ref[...], w_ref[...])
    groups = o_ref.shape[1] // LANES

    def blocks_of(mode):
        return [b for b, m in enumerate(modes) if m == mode]

    def when_mode(mode):
        bs = blocks_of(mode)
        return pl.when((j >= bs[0]) & (j <= bs[-1]))

    @when_mode(MODE_RET)
    def _():
        for g in range(groups):
            _rope_cols(o_ref, g, _tab(tab_ref, TAB_RET_C), _tab(tab_ref, TAB_RET_S), None, LANES // 2)

    @when_mode(MODE_RETK)
    def _():
        scale = HEAD_DIM ** -0.5
        for g in range(groups):
            _rope_cols(o_ref, g, _tab(tab_ref, TAB_RET_C) * scale, _tab(tab_ref, TAB_RET_S) * scale, None,
                       LANES // 2)

    @when_mode(MODE_DSA)
    def _():
        for g in range(groups):
            _rope_cols(o_ref, g, _tab(tab_ref, TAB_DSA_C), _tab(tab_ref, TAB_DSA_A), _tab(tab_ref, TAB_DSA_B), 16)

    @when_mode(MODE_IDX)
    def _():
        for g in range(groups):
            _rope_cols(o_ref, g, _tab(tab_ref, TAB_IDX_C), _tab(tab_ref, TAB_IDX_A), _tab(tab_ref, TAB_IDX_B), 8)

    @when_mode(MODE_DKV)
    def _():
        for g in range(2):
            _rope_cols(o_ref, g, _tab(tab_ref, TAB_DSA_C), _tab(tab_ref, TAB_DSA_A), _tab(tab_ref, TAB_DSA_B), 16)

    @when_mode(MODE_IKW)
    def _():
        lane = lax.broadcasted_iota(jnp.int32, (1, LANES), 1)
        is_ik = lane < IDX_DH
        c = jnp.where(is_ik, _tab(tab_ref, TAB_IDX_C), 1.0)
        sa = jnp.where(is_ik, _tab(tab_ref, TAB_IDX_A), 0.0)
        sb = jnp.where(is_ik, _tab(tab_ref, TAB_IDX_B), 0.0)
        _rope_cols(o_ref, 0, c, sa, sb, 8)


def _norm_proj(x, g, w, tm, tn):
    m, d = x.shape
    n = w.shape[1]

    def body(x_ref, g_ref, w_ref, o_ref, u_ref):
        _norm_proj_kernel(x_ref, g_ref, w_ref, None, o_ref, u_ref, modes=None)

    return pl.pallas_call(
        body,
        grid=(m // tm, n // tn),
        in_specs=[
            pl.BlockSpec((tm, d), lambda i, j: (i, 0)),
            pl.BlockSpec((1, d), lambda i, j: (0, 0)),
            pl.BlockSpec((d, tn), lambda i, j: (0, j)),
        ],
        out_specs=pl.BlockSpec((tm, tn), lambda i, j: (i, j)),
        out_shape=jax.ShapeDtypeStruct((m, n), F32),
        scratch_shapes=[pltpu.VMEM((tm, d), BF16)],
        compiler_params=_cparams("parallel", "arbitrary"),
        name="norm_proj",
    )(x, g, w)


_W_IN_ROW = {"rq": 0, "rk": 1024, "rv": 2048, "rg": 3072, "dq": 4096, "dk": 5120, "iq": 5632, "ik": 6656,
             "ga": 6736, "gb": 8784}
PROJ_SRC_ROWS = tuple(_W_IN_ROW[name] + PROJ_TN * k for name, nblk in (
    ("ga", 4), ("gb", 4), ("rq", 2), ("rk", 2), ("rv", 2), ("rg", 2), ("dq", 2), ("iq", 2), ("dk", 1), ("ik", 1))
    for k in range(nblk))


def _norm_proj_rope(x, g, w_t, tab, tm, pos_blocks):
    m, d = x.shape
    assert all(r % ROW_ALIGN == 0 for r in PROJ_SRC_ROWS)
    src_rows = jnp.asarray([r // ROW_ALIGN for r in PROJ_SRC_ROWS], jnp.int32)
    grid_spec = pltpu.PrefetchScalarGridSpec(
        num_scalar_prefetch=1,
        grid=(m // tm, len(PROJ_SRC_ROWS)),
        in_specs=[
            pl.BlockSpec((tm, d), lambda i, j, src: (i, 0)),
            pl.BlockSpec((1, d), lambda i, j, src: (0, 0)),
            pl.BlockSpec((pl.Element(PROJ_TN), pl.Element(d)), lambda i, j, src: (src[j] * ROW_ALIGN, 0)),
            pl.BlockSpec((tm, TAB_W), lambda i, j, src: (i % pos_blocks, 0)),
        ],
        out_specs=pl.BlockSpec((tm, PROJ_TN), lambda i, j, src: (i, j)),
        scratch_shapes=[pltpu.VMEM((tm, d), BF16)],
    )

    def body(src_ref, x_ref, g_ref, w_ref, tab_ref, o_ref, u_ref):
        _norm_proj_kernel(x_ref, g_ref, w_ref, tab_ref, o_ref, u_ref, modes=PROJ_MODES)

    return pl.pallas_call(
        body,
        grid_spec=grid_spec,
        out_shape=jax.ShapeDtypeStruct((m, N_PROJ), F32),
        compiler_params=_cparams("parallel", "arbitrary"),
        name="norm_proj_rope",
    )(src_rows, x, g, w_t, tab)


@functools.lru_cache(maxsize=None)
def _rope_tables(start, count, period):
    pos = (start + np.arange(count) % period).astype(np.float64)[:, None]

    def cs(half, theta):
        inv = theta ** (-np.arange(half, dtype=np.float64) / half)
        ang = pos * inv[None, :]
        return np.cos(ang), np.sin(ang)

    ones = lambda n: np.ones((count, n))
    zeros = lambda n: np.zeros((count, n))
    c, s = cs(HEAD_DIM // 2, RET_THETA)
    ret_c = np.concatenate([c, c], 1)
    ret_s = np.concatenate([-s, s], 1)
    c, s = cs(16, ROPE_THETA)
    dsa_c = np.concatenate([c, c, ones(96)], 1)
    dsa_a = np.concatenate([-s, zeros(112)], 1)
    dsa_b = np.concatenate([zeros(16), s, zeros(96)], 1)
    c, s = cs(8, ROPE_THETA)
    idx_c = np.tile(np.concatenate([c, c, ones(48)], 1), (1, 2))
    idx_a = np.tile(np.concatenate([-s, zeros(56)], 1), (1, 2))
    idx_b = np.tile(np.concatenate([zeros(8), s, zeros(48)], 1), (1, 2))
    return np.concatenate([ret_c, ret_s, dsa_c, dsa_a, dsa_b, idx_c, idx_a, idx_b], 1).astype(np.float32)


def _log_decay():
    return jnp.log1p(-jnp.exp2(-5.0 - jnp.arange(RET_HEADS, dtype=F32)))


def _groupnorm_gate(o, gate, gn):
    mu = jnp.mean(o, axis=-1, keepdims=True)
    d = o - mu
    var = jnp.mean(d * d, axis=-1, keepdims=True)
    n = d * lax.rsqrt(var + EPS) * gn
    return (gate * _sigmoid(gate) * n).astype(BF16)


def _ret_prompt_kernel(q_ref, k_ref, v_ref, g_ref, gn_ref, intra_ref, qdec_ref, kdec_ref, cdec_ref,
                       y_ref, st_ref, s_scr):
    c = pl.program_id(1)

    @pl.when(c == 0)
    def _():
        s_scr[...] = jnp.zeros_like(s_scr)

    for h in range(RET_HEADS):
        sl = slice(h * HEAD_DIM, (h + 1) * HEAD_DIM)
        k = k_ref[:, sl]
        qb = q_ref[:, sl].astype(BF16)
        kb = k.astype(BF16)
        vb = v_ref[:, sl].astype(BF16)
        s = _dot_nt(qb, kb) * intra_ref[h]
        state = s_scr[h]
        o = _dot(s.astype(BF16), vb) + _dot(qb, state.astype(BF16)) * qdec_ref[h]
        kd = (k * kdec_ref[h]).astype(BF16)
        s_scr[h] = state * cdec_ref[h] + _dot_tn(kd, vb)
        y_ref[:, sl] = _groupnorm_gate(o, g_ref[:, sl], gn_ref[:, sl])

    @pl.when(c == pl.num_programs(1) - 1)
    def _():
        st_ref[0] = s_scr[...]


def _ret_prompt(proj, gn, batch, seq, chunk):
    nc = seq // chunk
    log_g = _log_decay()
    j = jnp.arange(chunk, dtype=F32)
    diff = j[:, None] - j[None, :]
    intra = jnp.where(diff >= 0, jnp.exp(log_g[:, None, None] * jnp.maximum(diff, 0.0)), 0.0)
    ones = jnp.ones((1, 1, HEAD_DIM), F32)
    qdec = jnp.exp(log_g[:, None] * (j[None, :] + 1.0))[:, :, None] * ones
    kdec = jnp.exp(log_g[:, None] * (chunk - 1.0 - j[None, :]))[:, :, None] * ones
    cdec = jnp.exp(log_g * chunk)[:, None, None] * ones
    w = RET_HEADS * HEAD_DIM
    col = lambda off: (lambda b, c: (b * nc + c, off // w))
    const3 = lambda b, c: (0, 0, 0)
    return pl.pallas_call(
        _ret_prompt_kernel,
        grid=(batch, nc),
        in_specs=[
            pl.BlockSpec((chunk, w), col(COL_RQ)),
            pl.BlockSpec((chunk, w), col(COL_RK)),
            pl.BlockSpec((chunk, w), col(COL_RV)),
            pl.BlockSpec((chunk, w), col(COL_RG)),
            pl.BlockSpec((1, w), lambda b, c: (0, 0)),
            pl.BlockSpec((RET_HEADS, chunk, chunk), const3),
            pl.BlockSpec((RET_HEADS, chunk, HEAD_DIM), const3),
            pl.BlockSpec((RET_HEADS, chunk, HEAD_DIM), const3),
            pl.BlockSpec((RET_HEADS, 1, HEAD_DIM), const3),
        ],
        out_specs=[
            pl.BlockSpec((chunk, w), lambda b, c: (b * nc + c, 0)),
            pl.BlockSpec((1, RET_HEADS, HEAD_DIM, HEAD_DIM), lambda b, c: (b, 0, 0, 0)),
        ],
        out_shape=[
            jax.ShapeDtypeStruct((batch * seq, w), BF16),
            jax.ShapeDtypeStruct((batch, RET_HEADS, HEAD_DIM, HEAD_DIM), F32),
        ],
        scratch_shapes=[pltpu.VMEM((RET_HEADS, HEAD_DIM, HEAD_DIM), F32)],
        compiler_params=_cparams("parallel", "arbitrary"),
        name="ret_prompt",
    )(proj, proj, proj, proj, gn, intra, qdec, kdec, cdec)


def _ret_sample_kernel(q_ref, k_ref, v_ref, g_ref, gn_ref, st_ref, intra_ref, qdec_ref, kdec_ref, cdec_ref,
                       y_ref, so_ref, *, n_batch, rows):
    k = k_ref[...]
    qb = q_ref[...].astype(BF16)
    kb = k.astype(BF16)
    vb = v_ref[...].astype(BF16)
    s = _dot_nt(qb, kb) * intra_ref[0]
    kd = k * kdec_ref[0]
    row_batch = lax.broadcasted_iota(jnp.int32, kd.shape, 0) // rows
    inter = []
    for b in range(n_batch):
        state = st_ref[b, 0]
        inter.append(_dot(qb[b * rows:(b + 1) * rows], state.astype(BF16)))
        kd_b = jnp.where(row_batch == b, kd, 0.0).astype(BF16)
        so_ref[b, 0] = state * cdec_ref[0] + _dot_tn(kd_b, vb)
    o = _dot(s.astype(BF16), vb) + jnp.concatenate(inter, axis=0) * qdec_ref[0]
    y_ref[...] = _groupnorm_gate(o, g_ref[...], gn_ref[...])


def _ret_sample(proj, gn, state, n_batch, rows, t_valid):
    m = n_batch * rows
    log_g = _log_decay()
    r = jnp.arange(m)
    t = (r % rows).astype(F32)
    same = (r[:, None] // rows) == (r[None, :] // rows)
    diff = t[:, None] - t[None, :]
    intra = jnp.where(same[None] & (diff >= 0)[None], jnp.exp(log_g[:, None, None] * jnp.maximum(diff, 0.0)[None]), 0.0)
    ones = jnp.ones((1, 1, HEAD_DIM), F32)
    qdec = jnp.exp(log_g[:, None] * (t[None, :] + 1.0))[:, :, None] * ones
    kdec = jnp.where(t[None, :] < t_valid, jnp.exp(log_g[:, None] * (t_valid - 1.0 - t[None, :])), 0.0)[:, :, None] * ones
    cdec = jnp.exp(log_g * t_valid)[:, None, None] * ones
    col = lambda off: (lambda h: (0, off // HEAD_DIM + h))
    per_head = lambda h: (h, 0, 0)
    return pl.pallas_call(
        functools.partial(_ret_sample_kernel, n_batch=n_batch, rows=rows),
        grid=(RET_HEADS,),
        in_specs=[
            pl.BlockSpec((m, HEAD_DIM), col(COL_RQ)),
            pl.BlockSpec((m, HEAD_DIM), col(COL_RK)),
            pl.BlockSpec((m, HEAD_DIM), col(COL_RV)),
            pl.BlockSpec((m, HEAD_DIM), col(COL_RG)),
            pl.BlockSpec((1, HEAD_DIM), lambda h: (0, h)),
            pl.BlockSpec((n_batch, 1, HEAD_DIM, HEAD_DIM), lambda h: (0, h, 0, 0)),
            pl.BlockSpec((1, m, m), per_head),
            pl.BlockSpec((1, m, HEAD_DIM), per_head),
            pl.BlockSpec((1, m, HEAD_DIM), per_head),
            pl.BlockSpec((1, 1, HEAD_DIM), per_head),
        ],
        out_specs=[
            pl.BlockSpec((m, HEAD_DIM), lambda h: (0, h)),
            pl.BlockSpec((n_batch, 1, HEAD_DIM, HEAD_DIM), lambda h: (0, h, 0, 0)),
        ],
        out_shape=[
            jax.ShapeDtypeStruct((m, RET_HEADS * HEAD_DIM), BF16),
            jax.ShapeDtypeStruct(state.shape, F32),
        ],
        compiler_params=_cparams("parallel"),
        name="ret_sample",
    )(proj, proj, proj, proj, gn, state, intra, qdec, kdec, cdec)


def _sortable_key(x):
    bits = pltpu.bitcast(x, jnp.int32)
    return bits ^ ((bits >> 31) & jnp.int32(0x7FFFFFFF))


def _kth_largest_key(count_ge, shape, topk):
    def body(b, res):
        cand = res + lax.shift_left(jnp.int32(1), jnp.int32(31) - b)
        return jnp.where(count_ge(cand) >= topk, cand, res)

    return lax.fori_loop(0, 32, body, jnp.full(shape, INT_MIN, jnp.int32))


def _dsa_prompt_kernel(dq_ref, iq_ref, iwq_ref, kall_ref, vall_ref, ikall_ref, o_ref,
                       kbf, vtb, ika, ikb, keys, z_scr, m_scr, acc_scr, *, tq, topk):
    i = pl.program_id(1)
    n_chunks = kbf.shape[0]
    scale_log2e = HEAD_DIM ** -0.5 * math.log2(math.e)

    @pl.when(i == 0)
    def _():
        lane = lax.broadcasted_iota(jnp.int32, (tq, LANES), 1)
        for c in range(n_chunks):
            rows = slice(c * tq, (c + 1) * tq)
            kbf[c] = kall_ref[rows, :].astype(BF16)
            for kvh in range(DSA_KV_HEADS):
                sl = slice(kvh * HEAD_DIM, (kvh + 1) * HEAD_DIM)
                vtb[c, kvh, :HEAD_DIM, :] = vall_ref[rows, sl].T.astype(BF16)
                vtb[c, kvh, HEAD_DIM:, :] = jnp.ones((ROW_ALIGN, tq), BF16)
            a = ikall_ref[rows, :]
            ika[c] = jnp.where(lane < IDX_DH, a, 0.0).astype(BF16)
            ikb[c] = jnp.where(lane >= IDX_DH, pltpu.roll(a, IDX_DH, 1), 0.0).astype(BF16)

    w_t = iwq_ref[...].T
    iqb = iq_ref[...].astype(BF16)
    qb = (dq_ref[...] * scale_log2e).astype(BF16)
    t_col = i * tq + lax.broadcasted_iota(jnp.int32, (tq, tq), 1)
    s_row0 = lax.broadcasted_iota(jnp.int32, (tq, tq), 0)

    def score_chunk(c, carry):
        a = ika[c]
        b = ikb[c]
        acc = jnp.zeros((tq, tq), F32)
        for p in range(IDX_HEADS // 2):
            pair = iqb[:, p * LANES:(p + 1) * LANES]
            w0 = w_t[IDX_DH + 2 * p:IDX_DH + 2 * p + 1, :]
            w1 = w_t[IDX_DH + 2 * p + 1:IDX_DH + 2 * p + 2, :]
            acc = acc + jnp.maximum(_dot_nt(a, pair), 0.0) * w0
            acc = acc + jnp.maximum(_dot_nt(b, pair), 0.0) * w1
        acc = jnp.where(s_row0 + c * tq <= t_col, acc, -jnp.inf)
        keys[c] = _sortable_key(acc)
        return carry

    lax.fori_loop(0, i + 1, score_chunk, 0)
    keys[i + 1] = jnp.full((tq, tq), INT_MIN, jnp.int32)

    def count_ge(cand):
        def part(c):
            ge = jnp.where(keys[c] >= cand, 1.0, 0.0)
            return ge.reshape(tq // SUBLANES, SUBLANES, tq).sum(axis=0)

        def pair(j, cnt):
            return cnt + part(2 * j) + part(2 * j + 1)

        cnt = lax.fori_loop(0, (i + 2) // 2, pair, jnp.zeros((SUBLANES, tq), F32))
        return cnt.sum(axis=0, keepdims=True)

    thr = _kth_largest_key(count_ge, (1, tq), topk)

    m_scr[...] = jnp.full_like(m_scr, NEG)
    acc_scr[...] = jnp.zeros_like(acc_scr)

    def attend_chunk(c, carry):
        bias = jnp.where(s_row0 + c * tq <= t_col, jnp.where(keys[c] >= thr, 0.0, NEG), NEG)
        kc = kbf[c]
        kv_of = lambda h: h // (DSA_HEADS // DSA_KV_HEADS)
        z_max = []
        for h in range(DSA_HEADS):
            ksl = slice(kv_of(h) * HEAD_DIM, (kv_of(h) + 1) * HEAD_DIM)
            z = _dot_nt(kc[:, ksl], qb[:, h * HEAD_DIM:(h + 1) * HEAD_DIM]) + bias
            z_scr[h] = z
            z_max.append(z.max(axis=0, keepdims=True))
        for h in range(DSA_HEADS):
            m_old = m_scr[h]
            m_new = jnp.maximum(m_old, z_max[h])
            p = jnp.exp2(z_scr[h] - m_new)
            acc_scr[h] = acc_scr[h] * jnp.exp2(m_old - m_new) + _dot(vtb[c, kv_of(h)], p.astype(BF16))
            m_scr[h] = m_new
        return carry

    lax.fori_loop(0, i + 1, attend_chunk, 0)

    for h in range(DSA_HEADS):
        o = acc_scr[h, :HEAD_DIM, :] * (1.0 / acc_scr[h, HEAD_DIM:HEAD_DIM + 1, :])
        o_ref[:, h * HEAD_DIM:(h + 1) * HEAD_DIM] = o.T.astype(BF16)


def _dsa_prompt(proj, batch, seq, tq):
    nq = seq // tq
    topk = min(TOPK_MAX, seq // 4)
    qw = DSA_HEADS * HEAD_DIM
    kvw = DSA_KV_HEADS * HEAD_DIM
    return pl.pallas_call(
        functools.partial(_dsa_prompt_kernel, tq=tq, topk=topk),
        grid=(batch, nq),
        in_specs=[
            pl.BlockSpec((tq, qw), lambda b, i: (b * nq + i, COL_DQ // qw)),
            pl.BlockSpec((tq, qw), lambda b, i: (b * nq + i, COL_IQ // qw)),
            pl.BlockSpec((tq, LANES), lambda b, i: (b * nq + i, COL_IK // LANES)),
            pl.BlockSpec((seq, kvw), lambda b, i: (b, COL_DK // kvw)),
            pl.BlockSpec((seq, kvw), lambda b, i: (b, COL_DV // kvw)),
            pl.BlockSpec((seq, LANES), lambda b, i: (b, COL_IK // LANES)),
        ],
        out_specs=pl.BlockSpec((tq, qw), lambda b, i: (b * nq + i, 0)),
        out_shape=jax.ShapeDtypeStruct((batch * seq, qw), BF16),
        scratch_shapes=[
            pltpu.VMEM((nq, tq, kvw), BF16),
            pltpu.VMEM((nq, DSA_KV_HEADS, HEAD_DIM + ROW_ALIGN, tq), BF16),
            pltpu.VMEM((nq, tq, LANES), BF16),
            pltpu.VMEM((nq, tq, LANES), BF16),
            pltpu.VMEM((nq + 1, tq, tq), jnp.int32),
            pltpu.VMEM((DSA_HEADS, tq, tq), F32),
            pltpu.VMEM((DSA_HEADS, 1, tq), F32),
            pltpu.VMEM((DSA_HEADS, HEAD_DIM + ROW_ALIGN, tq), F32),
        ],
        compiler_params=_cparams("parallel", "arbitrary"),
        name="dsa_prompt",
    )(proj, proj, proj, proj, proj, proj)


def _idx_sample_kernel(pt_ref, iq_ref, iw_ref, iknew_ref, *rest, pages):
    page_refs = rest[:pages]
    sp_ref, sn_ref = rest[pages:]
    iqb = iq_ref[0].astype(BF16)
    w = iw_ref[0]
    rows = iqb.shape[0] // IDX_HEADS

    def scores(ik_t):
        d = jnp.maximum(_dot(iqb, ik_t.astype(BF16)), 0.0) * w
        return d.reshape(rows, IDX_HEADS, LANES).sum(axis=1)

    for p in range(pages):
        sp_ref[0, :, p * PAGE_SIZE:(p + 1) * PAGE_SIZE] = scores(page_refs[p][0])
    t = lax.broadcasted_iota(jnp.int32, (rows, LANES), 0)
    j = lax.broadcasted_iota(jnp.int32, (rows, LANES), 1)
    sn_ref[0] = jnp.where(j <= t, scores(iknew_ref[0]), -jnp.inf)


def _idx_sample(page_table, iq, iw, ik_new, pool_ik, pages):
    n_batch, n_pages = page_table.shape
    rows16 = iq.shape[1]
    rows = rows16 // IDX_HEADS
    past = n_pages * PAGE_SIZE
    page_specs = [pl.BlockSpec((1, IDX_DH, PAGE_SIZE), functools.partial(
        lambda b, g, pt, k: (pt[b, g * pages + k], 0, 0), k=k)) for k in range(pages)]
    grid_spec = pltpu.PrefetchScalarGridSpec(
        num_scalar_prefetch=1,
        grid=(n_batch, n_pages // pages),
        in_specs=[
            pl.BlockSpec((1, rows16, IDX_DH), lambda b, g, pt: (b, 0, 0)),
            pl.BlockSpec((1, rows16, LANES), lambda b, g, pt: (b, 0, 0)),
            pl.BlockSpec((1, IDX_DH, PAGE_SIZE), lambda b, g, pt: (b, 0, 0)),
        ] + page_specs,
        out_specs=[
            pl.BlockSpec((1, rows, pages * PAGE_SIZE), lambda b, g, pt: (b, 0, g)),
            pl.BlockSpec((1, rows, LANES), lambda b, g, pt: (b, 0, 0)),
        ],
    )
    return pl.pallas_call(
        functools.partial(_idx_sample_kernel, pages=pages),
        grid_spec=grid_spec,
        out_shape=[
            jax.ShapeDtypeStruct((n_batch, rows, past), F32),
            jax.ShapeDtypeStruct((n_batch, rows, LANES), F32),
        ],
        compiler_params=_cparams("parallel", "arbitrary"),
        name="idx_sample",
    )(page_table, iq, iw, ik_new, *([pool_ik] * pages))


def _dsa_sample_kernel(pt_ref, sp_ref, sn_ref, q_ref, knew_ref, vnew_ref, *rest, pages, topk, rows):
    k_refs = rest[:pages]
    v_refs = rest[pages:2 * pages]
    o_ref, selp, seln, m_scr, l_scr, acc_scr = rest[2 * pages:]
    g = pl.program_id(1)
    scale = HEAD_DIM ** -0.5
    group = DSA_HEADS // DSA_KV_HEADS

    @pl.when(g == 0)
    def _():
        kp = _sortable_key(sp_ref[0])
        kn = _sortable_key(sn_ref[0])

        def count_ge(cand):
            cp = jnp.where(kp >= cand, 1.0, 0.0).sum(axis=1, keepdims=True)
            cn = jnp.where(kn >= cand, 1.0, 0.0).sum(axis=1, keepdims=True)
            return cp + cn

        thr = _kth_largest_key(count_ge, (rows, 1), topk)
        sel_past = jnp.where(kp >= thr, 0.0, NEG)
        for pg in range(selp.shape[0]):
            selp[pg] = sel_past[:, pg * PAGE_SIZE:(pg + 1) * PAGE_SIZE]
        t = lax.broadcasted_iota(jnp.int32, (rows, LANES), 0)
        j = lax.broadcasted_iota(jnp.int32, (rows, LANES), 1)
        seln[...] = jnp.where(j <= t, jnp.where(kn >= thr, 0.0, NEG), NEG)
        m_scr[...] = jnp.full_like(m_scr, NEG)
        l_scr[...] = jnp.zeros_like(l_scr)
        acc_scr[...] = jnp.zeros_like(acc_scr)

    def head_rows(ref, kvh):
        return ref[0, pl.ds(kvh, PAGE_SIZE, stride=DSA_KV_HEADS), :].astype(BF16)

    def attend(k_pages, v_pages, bias):
        bias_g = jnp.concatenate([bias] * group, axis=0)
        kv_heads = range(DSA_KV_HEADS)
        qb = [q_ref[0, kvh].astype(BF16) for kvh in kv_heads]
        raw = [jnp.concatenate([_dot_nt(qb[kvh], head_rows(k, kvh)) for k in k_pages], axis=1) for kvh in kv_heads]
        pb, alpha = [], []
        for kvh in kv_heads:
            z = jnp.where(bias_g < 0.0, NEG, raw[kvh])
            m_old = m_scr[kvh]
            m_new = jnp.maximum(m_old, z.max(axis=1, keepdims=True))
            alpha.append(jnp.exp((m_old - m_new) * scale))
            p = jnp.exp((z - m_new) * scale)
            l_scr[kvh] = l_scr[kvh] * alpha[kvh] + p.sum(axis=1, keepdims=True)
            m_scr[kvh] = m_new
            pb.append(p.astype(BF16))
        for kvh in kv_heads:
            pv = _dot(pb[kvh][:, :PAGE_SIZE], head_rows(v_pages[0], kvh))
            for n in range(1, len(v_pages)):
                pv = pv + _dot(pb[kvh][:, n * PAGE_SIZE:(n + 1) * PAGE_SIZE], head_rows(v_pages[n], kvh))
            acc_scr[kvh] = acc_scr[kvh] * alpha[kvh] + pv

    attend(k_refs, v_refs, jnp.concatenate([selp[g * pages + p] for p in range(pages)], axis=1))

    @pl.when(g == pl.num_programs(1) - 1)
    def _():
        attend([knew_ref], [vnew_ref], seln[...])
        for kvh in range(DSA_KV_HEADS):
            o_ref[0, kvh] = (acc_scr[kvh] * (1.0 / l_scr[kvh])).astype(BF16)


def _dsa_sample(page_table, scores_past, scores_new, q, k_new, v_new, pool_k, pool_v, pages, topk):
    n_batch, n_pages = page_table.shape
    rows = scores_past.shape[1]
    past = n_pages * PAGE_SIZE
    page_rows = DSA_KV_HEADS * PAGE_SIZE
    grows = q.shape[2]
    page_spec = lambda k: pl.BlockSpec((1, page_rows, HEAD_DIM), functools.partial(
        lambda b, g, pt, k: (pt[b, g * pages + k], 0, 0), k=k))
    per_batch3 = lambda b, g, pt: (b, 0, 0)
    grid_spec = pltpu.PrefetchScalarGridSpec(
        num_scalar_prefetch=1,
        grid=(n_batch, n_pages // pages),
        in_specs=[
            pl.BlockSpec((1, rows, past), per_batch3),
            pl.BlockSpec((1, rows, LANES), per_batch3),
            pl.BlockSpec((1, DSA_KV_HEADS, grows, HEAD_DIM), lambda b, g, pt: (b, 0, 0, 0)),
            pl.BlockSpec((1, page_rows, HEAD_DIM), per_batch3),
            pl.BlockSpec((1, page_rows, HEAD_DIM), per_batch3),
        ] + [page_spec(k) for k in range(pages)] * 2,
        out_specs=pl.BlockSpec((1, DSA_KV_HEADS, grows, HEAD_DIM), lambda b, g, pt: (b, 0, 0, 0)),
        scratch_shapes=[
            pltpu.VMEM((n_pages, rows, PAGE_SIZE), F32),
            pltpu.VMEM((rows, LANES), F32),
            pltpu.VMEM((DSA_KV_HEADS, grows, 1), F32),
            pltpu.VMEM((DSA_KV_HEADS, grows, 1), F32),
            pltpu.VMEM((DSA_KV_HEADS, grows, HEAD_DIM), F32),
        ],
    )
    return pl.pallas_call(
        functools.partial(_dsa_sample_kernel, pages=pages, topk=topk, rows=rows),
        grid_spec=grid_spec,
        out_shape=jax.ShapeDtypeStruct((n_batch, DSA_KV_HEADS, grows, HEAD_DIM), BF16),
        compiler_params=_cparams("parallel", "arbitrary"),
        name="dsa_sample",
    )(page_table, scores_past, scores_new, q, k_new, v_new, *([pool_k] * pages), *([pool_v] * pages))


def _merge_kernel(x_ref, ry_ref, do_ref, ga_ref, gb_ref, wr_ref, wd_ref, wo_ref, o_ref):
    ya = _dot(ry_ref[...], wr_ref[...])
    yb = _dot(do_ref[...], wd_ref[...])
    merged = _sigmoid(ga_ref[...]) * ya + _sigmoid(gb_ref[...]) * yb
    o_ref[...] = x_ref[...] + _dot(merged.astype(BF16), wo_ref[...])


def _merge(x, ret_y, dsa_o, proj, w_ret_out, w_dsa_out, w_o, tm):
    m, d = x.shape
    w = ret_y.shape[1]
    const = lambda i: (0, 0)
    return pl.pallas_call(
        _merge_kernel,
        grid=(m // tm,),
        in_specs=[
            pl.BlockSpec((tm, d), lambda i: (i, 0)),
            pl.BlockSpec((tm, w), lambda i: (i, 0)),
            pl.BlockSpec((tm, w), lambda i: (i, 0)),
            pl.BlockSpec((tm, d), lambda i: (i, COL_GA // d)),
            pl.BlockSpec((tm, d), lambda i: (i, COL_GB // d)),
            pl.BlockSpec((w, d), const),
            pl.BlockSpec((w, d), const),
            pl.BlockSpec((d, d), const),
        ],
        out_specs=pl.BlockSpec((tm, d), lambda i: (i, 0)),
        out_shape=jax.ShapeDtypeStruct((m, d), F32),
        compiler_params=_cparams("parallel"),
        name="merge_out",
    )(x, ret_y, dsa_o, proj, proj, w_ret_out, w_dsa_out, w_o)


def _mem_attn_kernel(h_ref, g_ref, wq_ref, mk_ref, mv_ref, wo_ref, o_ref, *, rows_per_batch):
    h = h_ref[...]
    qm = _dot(_rmsnorm_bf16(h, g_ref[...]), wq_ref[...])
    tm = h.shape[0]
    nk = mk_ref.shape[0]
    scale = HEAD_DIM ** -0.5
    if rows_per_batch is not None:
        rb = lax.broadcasted_iota(jnp.int32, (tm, nk), 0) // rows_per_batch
        kb = lax.broadcasted_iota(jnp.int32, (tm, nk), 1) // MEM_LEN
        same = rb == kb
    outs = []
    for hd in range(MEM_HEADS):
        sl = slice(hd * HEAD_DIM, (hd + 1) * HEAD_DIM)
        z = _dot_nt(qm[:, sl].astype(BF16), mk_ref[:, sl].astype(BF16)) * scale
        if rows_per_batch is not None:
            z = jnp.where(same, z, NEG)
        p = jnp.exp(z - z.max(axis=-1, keepdims=True))
        p = p / p.sum(axis=-1, keepdims=True)
        outs.append(_dot(p.astype(BF16), mv_ref[:, sl].astype(BF16)))
    om = jnp.concatenate(outs, axis=1).astype(BF16)
    o_ref[...] = h + _dot(om, wo_ref[...])


def _mem_attn(h, g, w_mq, mk, mv, w_mo, *, tm, batch_tiles, mk_col, mv_col, nk, rows_per_batch):
    m, d = h.shape
    const = lambda i: (0, 0)
    if batch_tiles:
        kmap = lambda col: (lambda i: (i // batch_tiles, col))
    else:
        kmap = lambda col: (lambda i: (0, col))
    return pl.pallas_call(
        functools.partial(_mem_attn_kernel, rows_per_batch=rows_per_batch),
        grid=(m // tm,),
        in_specs=[
            pl.BlockSpec((tm, d), lambda i: (i, 0)),
            pl.BlockSpec((1, d), const),
            pl.BlockSpec((d, MEM_W), const),
            pl.BlockSpec((nk, MEM_W), kmap(mk_col)),
            pl.BlockSpec((nk, MEM_W), kmap(mv_col)),
            pl.BlockSpec((MEM_W, d), const),
        ],
        out_specs=pl.BlockSpec((tm, d), lambda i: (i, 0)),
        out_shape=jax.ShapeDtypeStruct((m, d), F32),
        compiler_params=_cparams("parallel"),
        name="mem_attn",
    )(h, g, w_mq, mk, mv, w_mo)


def _mlp_kernel(h_ref, g_ref, wu_ref, wd_ref, gf_ref, o_ref, u_ref, acc_ref):
    j = pl.program_id(1)

    @pl.when(j == 0)
    def _():
        u_ref[...] = _rmsnorm_bf16(h_ref[...], g_ref[...])
        acc_ref[...] = jnp.zeros_like(acc_ref)

    a = jnp.maximum(_dot(u_ref[...], wu_ref[...]), 0.0)
    acc_ref[...] += _dot((a * a).astype(BF16), wd_ref[...])

    @pl.when(j == pl.num_programs(1) - 1)
    def _():
        y = h_ref[...] + acc_ref[...]
        ms = jnp.mean(y * y, axis=-1, keepdims=True)
        o_ref[...] = (y * lax.rsqrt(ms + EPS)) * gf_ref[...]


def _mlp(h, g, w_up, w_down, g_final, tm, tf):
    m, d = h.shape
    ff = w_up.shape[1]
    return pl.pallas_call(
        _mlp_kernel,
        grid=(m // tm, ff // tf),
        in_specs=[
            pl.BlockSpec((tm, d), lambda i, j: (i, 0)),
            pl.BlockSpec((1, d), lambda i, j: (0, 0)),
            pl.BlockSpec((d, tf), lambda i, j: (0, j)),
            pl.BlockSpec((tf, d), lambda i, j: (j, 0)),
            pl.BlockSpec((1, d), lambda i, j: (0, 0)),
        ],
        out_specs=pl.BlockSpec((tm, d), lambda i, j: (i, 0)),
        out_shape=jax.ShapeDtypeStruct((m, d), F32),
        scratch_shapes=[pltpu.VMEM((tm, d), BF16), pltpu.VMEM((tm, d), F32)],
        compiler_params=_cparams("parallel", "arbitrary"),
        name="mlp_final",
    )(h, g, w_up, w_down, g_final)


def _pick_tile(n, pref):
    t = min(n, pref)
    while n % t:
        t //= 2
    return t


def kernel(x_prompt, x_sample, mem_prompt, cache_k, cache_v, cache_idx_k, state_ret, cache_mem_k, cache_mem_v,
           page_table, g_mix, w_in, gn_ret, w_ret_out, w_dsa_out, w_o, g_mem, g_memkv, w_mq, w_mk, w_mv, w_mo,
           g_mlp, w_up, w_down, g_final):
    assert w_in.shape[0] == 1, "one layer"
    batch, seq, d = x_prompt.shape
    n_dec, t_dec, _ = x_sample.shape
    n_pages = page_table.shape[1]
    past = n_pages * PAGE_SIZE
    mem_len = mem_prompt.shape[1]
    assert mem_len == MEM_LEN and t_dec <= SUBLANES
    row = lambda v: v.reshape(1, -1)

    w_proj_t = jnp.swapaxes(w_in[0], 0, 1).astype(BF16)
    w_ret_out_b, w_dsa_out_b, w_o_b = (w[0].astype(BF16) for w in (w_ret_out, w_dsa_out, w_o))
    w_mq_b, w_mo_b = w_mq[0].astype(BF16), w_mo[0].astype(BF16)
    w_mkv_b = jnp.concatenate([w_mk[0], w_mv[0]], axis=1).astype(BF16)
    w_up_b, w_down_b = w_up[0].astype(BF16), w_down[0].astype(BF16)
    g_mix_r, gn_r, g_mem_r, g_memkv_r, g_mlp_r, g_final_r = (
        row(v) for v in (g_mix[0], gn_ret[0], g_mem[0], g_memkv[0], g_mlp[0], g_final))

    m_p = batch * seq
    xp = x_prompt.reshape(m_p, d)
    tm_p = _pick_tile(seq, 512)
    tm_proj = _pick_tile(seq, 1024)
    proj_p = _norm_proj_rope(xp, g_mix_r, w_proj_t, _rope_tables(0, seq, seq), tm_proj, seq // tm_proj)
    ret_y_p, ret_state_p = _ret_prompt(proj_p, gn_r, batch, seq, _pick_tile(seq, 256))
    dsa_o_p = _dsa_prompt(proj_p, batch, seq, _pick_tile(seq, 256))
    kv_p = _norm_proj(mem_prompt.reshape(batch * mem_len, d), g_memkv_r, w_mkv_b, mem_len, PROJ_TN)
    tm_t = _pick_tile(seq, 256)
    h_p = _merge(xp, ret_y_p, dsa_o_p, proj_p, w_ret_out_b, w_dsa_out_b, w_o_b, tm_t)
    h_p = _mem_attn(h_p, g_mem_r, w_mq_b, kv_p, kv_p, w_mo_b, tm=tm_t, batch_tiles=seq // tm_t,
                    mk_col=0, mv_col=1, nk=mem_len, rows_per_batch=None)
    y_p = _mlp(h_p, g_mlp_r, w_up_b, w_down_b, g_final_r, tm_p, 512)

    rows = SUBLANES
    m_s = n_dec * rows
    xs = jnp.pad(x_sample, ((0, 0), (0, rows - t_dec), (0, 0))).reshape(m_s, d)
    proj_s = _norm_proj_rope(xs, g_mix_r, w_proj_t, _rope_tables(past, m_s, rows), m_s, 1)
    ret_y_s, ret_state_s = _ret_sample(proj_s, gn_r, state_ret[0], n_dec, rows, t_dec)

    proj_s3 = proj_s.reshape(n_dec, rows, N_PROJ)
    iq_s = proj_s3[:, :, COL_IQ:COL_IQ + IDX_HEADS * IDX_DH].reshape(n_dec, rows * IDX_HEADS, IDX_DH)
    iw_s = proj_s3[:, :, COL_IK + IDX_DH:COL_IK + IDX_DH + IDX_HEADS].reshape(n_dec, rows * IDX_HEADS, 1)
    iw_s = jnp.broadcast_to(iw_s, (n_dec, rows * IDX_HEADS, LANES))
    kvw = DSA_KV_HEADS * HEAD_DIM
    page_rows = DSA_KV_HEADS * PAGE_SIZE
    pad_keys = lambda a: jnp.pad(a, ((0, 0), (0, PAGE_SIZE - rows), (0, 0)))
    ik_new_t = jnp.swapaxes(pad_keys(proj_s3[:, :, COL_IK:COL_IK + IDX_DH]), 1, 2)
    pool_ik_t = jnp.swapaxes(cache_idx_k, 2, 3).reshape(-1, IDX_DH, PAGE_SIZE)
    k_new = pad_keys(proj_s3[:, :, COL_DK:COL_DK + kvw]).reshape(n_dec, page_rows, HEAD_DIM)
    v_new = pad_keys(proj_s3[:, :, COL_DV:COL_DV + kvw]).reshape(n_dec, page_rows, HEAD_DIM)
    scores_past, scores_new = _idx_sample(page_table, iq_s, iw_s, ik_new_t, pool_ik_t, _pick_tile(n_pages, 16))
    group = DSA_HEADS // DSA_KV_HEADS
    dq_s = proj_s3[:, :, COL_DQ:COL_DQ + DSA_HEADS * HEAD_DIM].reshape(n_dec, rows, DSA_KV_HEADS, group, HEAD_DIM)
    dq_s = dq_s.transpose(0, 2, 3, 1, 4).reshape(n_dec, DSA_KV_HEADS, group * rows, HEAD_DIM)
    topk_s = min(TOPK_MAX, (past + t_dec) // 4)
    dsa_o_s = _dsa_sample(page_table, scores_past, scores_new, dq_s, k_new, v_new,
                          cache_k.reshape(-1, page_rows, HEAD_DIM), cache_v.reshape(-1, page_rows, HEAD_DIM),
                          _pick_tile(n_pages, 16), topk_s)
    dsa_o_s = dsa_o_s.reshape(n_dec, DSA_KV_HEADS, group, rows, HEAD_DIM).transpose(0, 3, 1, 2, 4)
    dsa_o_s = dsa_o_s.reshape(m_s, DSA_HEADS * HEAD_DIM)

    h_s = _merge(xs, ret_y_s, dsa_o_s, proj_s, w_ret_out_b, w_dsa_out_b, w_o_b, m_s)
    h_s = _mem_attn(h_s, g_mem_r, w_mq_b, cache_mem_k[0].reshape(n_dec * mem_len, MEM_W),
                    cache_mem_v[0].reshape(n_dec * mem_len, MEM_W), w_mo_b, tm=m_s, batch_tiles=0,
                    mk_col=0, mv_col=0, nk=n_dec * mem_len, rows_per_batch=rows)
    y_s = _mlp(h_s, g_mlp_r, w_up_b, w_down_b, g_final_r, m_s, 512)

    def rows_p(col, width, tail):
        return proj_p[:, col:col + width].reshape((1, batch, seq) + tail)

    def rows_s(col, width, tail):
        return proj_s3[:, :t_dec, col:col + width].reshape((1, n_dec, t_dec) + tail)

    kv_shape = (DSA_KV_HEADS, HEAD_DIM)
    return (
        y_p.reshape(batch, seq, d),
        y_s.reshape(n_dec, rows, d)[:, :t_dec],
        ret_state_p[None],
        rows_p(COL_DK, kvw, kv_shape),
        rows_p(COL_DV, kvw, kv_shape),
        rows_p(COL_IK, IDX_DH, (IDX_DH,)),
        kv_p[:, :MEM_W].reshape(1, batch, mem_len, MEM_HEADS, HEAD_DIM),
        kv_p[:, MEM_W:].reshape(1, batch, mem_len, MEM_HEADS, HEAD_DIM),
        ret_state_s[None],
        rows_s(COL_DK, kvw, kv_shape),
        rows_s(COL_DV, kvw, kv_shape),
        rows_s(COL_IK, IDX_DH, (IDX_DH,)),
    )
```

```python
import functools
import math

import jax
import jax.numpy as jnp
import numpy as np
from jax import lax
from jax.experimental import pallas as pl
from jax.experimental.pallas import tpu as pltpu

F32 = jnp.float32
BF16 = jnp.bfloat16

D_MODEL = 2048
RET_HEADS = 8
HEAD_DIM = 128
DSA_HEADS = 8
DSA_KV_HEADS = 2
IDX_HEADS = 16
IDX_DH = 64
TOPK_MAX = 256
PAGE_SIZE = 128
MEM_LEN = 256
MEM_HEADS = 4
MEM_W = MEM_HEADS * HEAD_DIM
D_FF = 4 * D_MODEL
RET_THETA = 10000.0
ROPE_THETA = 500000.0
EPS = 1e-6
LANES = 128
SUBLANES = 8
ROW_ALIGN = 16
VMEM_LIMIT = 56 * 1024 * 1024
NEG = -1e30
BISECT_STEPS = 16
INT_MIN = -(2 ** 31)

COL_GA = 0
COL_GB = 2048
COL_RQ = 4096
COL_RK = 5120
COL_RV = 6144
COL_RG = 7168
COL_DQ = 8192
COL_IQ = 9216
COL_DK = 10240
COL_DV = 10496
COL_IK = 10752
N_PROJ = 11264
PROJ_TN = 512
MODE_PLAIN, MODE_RET, MODE_RETK, MODE_DSA, MODE_IDX, MODE_DKV, MODE_IKW = range(7)
PROJ_MODES = ((MODE_PLAIN,) * 8 + (MODE_RET,) * 2 + (MODE_RETK,) * 2 + (MODE_PLAIN,) * 4
              + (MODE_DSA,) * 2 + (MODE_IDX,) * 2 + (MODE_DKV, MODE_IKW))
TAB_RET_C, TAB_RET_S, TAB_DSA_C, TAB_DSA_A, TAB_DSA_B, TAB_IDX_C, TAB_IDX_A, TAB_IDX_B = range(8)
TAB_W = 8 * LANES


def _cparams(*sem):
    return pltpu.CompilerParams(dimension_semantics=sem, vmem_limit_bytes=VMEM_LIMIT)


def _dot(a, b):
    return jnp.dot(a, b, preferred_element_type=F32)


def _dot_nt(a, b):
    return lax.dot_general(a, b, (((1,), (1,)), ((), ())), preferred_element_type=F32)


def _dot_tn(a, b):
    return lax.dot_general(a, b, (((0,), (0,)), ((), ())), preferred_element_type=F32)


def _rmsnorm_bf16(x, g):
    ms = jnp.mean(x * x, axis=-1, keepdims=True)
    return ((x * lax.rsqrt(ms + EPS)) * g).astype(BF16)


def _sigmoid(x):
    return 1.0 / (1.0 + jnp.exp(-x))


def _tab(tab_ref, which):
    return tab_ref[:, which * LANES:(which + 1) * LANES]


def _rope_cols(o_ref, g, c, sa, sb, shift):
    a = o_ref[:, g * LANES:(g + 1) * LANES]
    if sb is None:
        r = a * c + pltpu.roll(a, shift, 1) * sa
    else:
        r = a * c + pltpu.roll(a, LANES - shift, 1) * sa + pltpu.roll(a, shift, 1) * sb
    o_ref[:, g * LANES:(g + 1) * LANES] = r


def _norm_proj_kernel(x_ref, g_ref, w_ref, tab_ref, o_ref, u_ref, *, modes):
    j = pl.program_id(1)

    @pl.when(j == 0)
    def _():
        u_ref[...] = _rmsnorm_bf16(x_ref[...], g_ref[...])

    if tab_ref is None:
        o_ref[...] = _dot(u_ref[...], w_ref[...])
        return
    o_ref[...] = _dot_nt(u_ref[...], w_ref[...].astype(BF16))
    groups = o_ref.shape[1] // LANES

    def blocks_of(mode):
        return [b for b, m in enumerate(modes) if m == mode]

    def when_mode(mode):
        bs = blocks_of(mode)
        return pl.when((j >= bs[0]) & (j <= bs[-1]))

    @when_mode(MODE_RET)
    def _():
        for g in range(groups):
            _rope_cols(o_ref, g, _tab(tab_ref, TAB_RET_C), _tab(tab_ref, TAB_RET_S), None, LANES // 2)

    @when_mode(MODE_RETK)
    def _():
        scale = HEAD_DIM ** -0.5
        for g in range(groups):
            _rope_cols(o_ref, g, _tab(tab_ref, TAB_RET_C) * scale, _tab(tab_ref, TAB_RET_S) * scale, None,
                       LANES // 2)

    @when_mode(MODE_DSA)
    def _():
        for g in range(groups):
            _rope_cols(o_ref, g, _tab(tab_ref, TAB_DSA_C), _tab(tab_ref, TAB_DSA_A), _tab(tab_ref, TAB_DSA_B), 16)

    @when_mode(MODE_IDX)
    def _():
        for g in range(groups):
            _rope_cols(o_ref, g, _tab(tab_ref, TAB_IDX_C), _tab(tab_ref, TAB_IDX_A), _tab(tab_ref, TAB_IDX_B), 8)

    @when_mode(MODE_DKV)
    def _():
        for g in range(2):
            _rope_cols(o_ref, g, _tab(tab_ref, TAB_DSA_C), _tab(tab_ref, TAB_DSA_A), _tab(tab_ref, TAB_DSA_B), 16)

    @when_mode(MODE_IKW)
    def _():
        lane = lax.broadcasted_iota(jnp.int32, (1, LANES), 1)
        is_ik = lane < IDX_DH
        c = jnp.where(is_ik, _tab(tab_ref, TAB_IDX_C), 1.0)
        sa = jnp.where(is_ik, _tab(tab_ref, TAB_IDX_A), 0.0)
        sb = jnp.where(is_ik, _tab(tab_ref, TAB_IDX_B), 0.0)
        _rope_cols(o_ref, 0, c, sa, sb, 8)


def _norm_proj(x, g, w, tm, tn):
    m, d = x.shape
    n = w.shape[1]

    def body(x_ref, g_ref, w_ref, o_ref, u_ref):
        _norm_proj_kernel(x_ref, g_ref, w_ref, None, o_ref, u_ref, modes=None)

    return pl.pallas_call(
        body,
        grid=(m // tm, n // tn),
        in_specs=[
            pl.BlockSpec((tm, d), lambda i, j: (i, 0)),
            pl.BlockSpec((1, d), lambda i, j: (0, 0)),
            pl.BlockSpec((d, tn), lambda i, j: (0, j)),
        ],
        out_specs=pl.BlockSpec((tm, tn), lambda i, j: (i, j)),
        out_shape=jax.ShapeDtypeStruct((m, n), F32),
        scratch_shapes=[pltpu.VMEM((tm, d), BF16)],
        compiler_params=_cparams("parallel", "arbitrary"),
        name="norm_proj",
    )(x, g, w)


_W_IN_ROW = {"rq": 0, "rk": 1024, "rv": 2048, "rg": 3072, "dq": 4096, "dk": 5120, "iq": 5632, "ik": 6656,
             "ga": 6736, "gb": 8784}
PROJ_SRC_ROWS = tuple(_W_IN_ROW[name] + PROJ_TN * k for name, nblk in (
    ("ga", 4), ("gb", 4), ("rq", 2), ("rk", 2), ("rv", 2), ("rg", 2), ("dq", 2), ("iq", 2), ("dk", 1), ("ik", 1))
    for k in range(nblk))


def _norm_proj_rope(x, g, w_t, tab, tm, pos_blocks):
    m, d = x.shape
    assert all(r % ROW_ALIGN == 0 for r in PROJ_SRC_ROWS)
    src_rows = jnp.asarray([r // ROW_ALIGN for r in PROJ_SRC_ROWS], jnp.int32)
    grid_spec = pltpu.PrefetchScalarGridSpec(
        num_scalar_prefetch=1,
        grid=(m // tm, len(PROJ_SRC_ROWS)),
        in_specs=[
            pl.BlockSpec((tm, d), lambda i, j, src: (i, 0)),
            pl.BlockSpec((1, d), lambda i, j, src: (0, 0)),
            pl.BlockSpec((pl.Element(PROJ_TN), pl.Element(d)), lambda i, j, src: (src[j] * ROW_ALIGN, 0)),
            pl.BlockSpec((tm, TAB_W), lambda i, j, src: (i % pos_blocks, 0)),
        ],
        out_specs=pl.BlockSpec((tm, PROJ_TN), lambda i, j, src: (i, j)),
        scratch_shapes=[pltpu.VMEM((tm, d), BF16)],
    )

    def body(src_ref, x_ref, g_ref, w_ref, tab_ref, o_ref, u_ref):
        _norm_proj_kernel(x_ref, g_ref, w_ref, tab_ref, o_ref, u_ref, modes=PROJ_MODES)

    return pl.pallas_call(
        body,
        grid_spec=grid_spec,
        out_shape=jax.ShapeDtypeStruct((m, N_PROJ), F32),
        compiler_params=_cparams("parallel", "arbitrary"),
        name="norm_proj_rope",
    )(src_rows, x, g, w_t, tab)


@functools.lru_cache(maxsize=None)
def _rope_tables(start, count, period):
    pos = (start + np.arange(count) % period).astype(np.float64)[:, None]

    def cs(half, theta):
        inv = theta ** (-np.arange(half, dtype=np.float64) / half)
        ang = pos * inv[None, :]
        return np.cos(ang), np.sin(ang)

    ones = lambda n: np.ones((count, n))
    zeros = lambda n: np.zeros((count, n))
    c, s = cs(HEAD_DIM // 2, RET_THETA)
    ret_c = np.concatenate([c, c], 1)
    ret_s = np.concatenate([-s, s], 1)
    c, s = cs(16, ROPE_THETA)
    dsa_c = np.concatenate([c, c, ones(96)], 1)
    dsa_a = np.concatenate([-s, zeros(112)], 1)
    dsa_b = np.concatenate([zeros(16), s, zeros(96)], 1)
    c, s = cs(8, ROPE_THETA)
    idx_c = np.tile(np.concatenate([c, c, ones(48)], 1), (1, 2))
    idx_a = np.tile(np.concatenate([-s, zeros(56)], 1), (1, 2))
    idx_b = np.tile(np.concatenate([zeros(8), s, zeros(48)], 1), (1, 2))
    return np.concatenate([ret_c, ret_s, dsa_c, dsa_a, dsa_b, idx_c, idx_a, idx_b], 1).astype(np.float32)


def _log_decay():
    return jnp.log1p(-jnp.exp2(-5.0 - jnp.arange(RET_HEADS, dtype=F32)))


def _groupnorm_gate(o, gate, gn):
    mu = jnp.mean(o, axis=-1, keepdims=True)
    d = o - mu
    var = jnp.mean(d * d, axis=-1, keepdims=True)
    n = d * lax.rsqrt(var + EPS) * gn
    return (gate * _sigmoid(gate) * n).astype(BF16)


def _ret_prompt_kernel(q_ref, k_ref, v_ref, g_ref, gn_ref, intra_ref, qdec_ref, kdec_ref, cdec_ref,
                       y_ref, st_ref, s_scr):
    c = pl.program_id(1)

    @pl.when(c == 0)
    def _():
        s_scr[...] = jnp.zeros_like(s_scr)

    for h in range(RET_HEADS):
        sl = slice(h * HEAD_DIM, (h + 1) * HEAD_DIM)
        k = k_ref[:, sl]
        qb = q_ref[:, sl].astype(BF16)
        kb = k.astype(BF16)
        vb = v_ref[:, sl].astype(BF16)
        s = _dot_nt(qb, kb) * intra_ref[h]
        state = s_scr[h]
        o = _dot(s.astype(BF16), vb) + _dot(qb, state.astype(BF16)) * qdec_ref[h]
        kd = (k * kdec_ref[h]).astype(BF16)
        s_scr[h] = state * cdec_ref[h] + _dot_tn(kd, vb)
        y_ref[:, sl] = _groupnorm_gate(o, g_ref[:, sl], gn_ref[:, sl])

    @pl.when(c == pl.num_programs(1) - 1)
    def _():
        st_ref[0] = s_scr[...]


def _ret_prompt(proj, gn, batch, seq, chunk):
    nc = seq // chunk
    log_g = _log_decay()
    j = jnp.arange(chunk, dtype=F32)
    diff = j[:, None] - j[None, :]
    intra = jnp.where(diff >= 0, jnp.exp(log_g[:, None, None] * jnp.maximum(diff, 0.0)), 0.0)
    ones = jnp.ones((1, 1, HEAD_DIM), F32)
    qdec = jnp.exp(log_g[:, None] * (j[None, :] + 1.0))[:, :, None] * ones
    kdec = jnp.exp(log_g[:, None] * (chunk - 1.0 - j[None, :]))[:, :, None] * ones
    cdec = jnp.exp(log_g * chunk)[:, None, None] * ones
    w = RET_HEADS * HEAD_DIM
    col = lambda off: (lambda b, c: (b * nc + c, off // w))
    const3 = lambda b, c: (0, 0, 0)
    return pl.pallas_call(
        _ret_prompt_kernel,
        grid=(batch, nc),
        in_specs=[
            pl.BlockSpec((chunk, w), col(COL_RQ)),
            pl.BlockSpec((chunk, w), col(COL_RK)),
            pl.BlockSpec((chunk, w), col(COL_RV)),
            pl.BlockSpec((chunk, w), col(COL_RG)),
            pl.BlockSpec((1, w), lambda b, c: (0, 0)),
            pl.BlockSpec((RET_HEADS, chunk, chunk), const3),
            pl.BlockSpec((RET_HEADS, chunk, HEAD_DIM), const3),
            pl.BlockSpec((RET_HEADS, chunk, HEAD_DIM), const3),
            pl.BlockSpec((RET_HEADS, 1, HEAD_DIM), const3),
        ],
        out_specs=[
            pl.BlockSpec((chunk, w), lambda b, c: (b * nc + c, 0)),
            pl.BlockSpec((1, RET_HEADS, HEAD_DIM, HEAD_DIM), lambda b, c: (b, 0, 0, 0)),
        ],
        out_shape=[
            jax.ShapeDtypeStruct((batch * seq, w), BF16),
            jax.ShapeDtypeStruct((batch, RET_HEADS, HEAD_DIM, HEAD_DIM), F32),
        ],
        scratch_shapes=[pltpu.VMEM((RET_HEADS, HEAD_DIM, HEAD_DIM), F32)],
        compiler_params=_cparams("parallel", "arbitrary"),
        name="ret_prompt",
    )(proj, proj, proj, proj, gn, intra, qdec, kdec, cdec)


def _ret_sample_kernel(q_ref, k_ref, v_ref, g_ref, gn_ref, st_ref, intra_ref, qdec_ref, kdec_ref, cdec_ref,
                       y_ref, so_ref, *, n_batch, rows):
    k = k_ref[...]
    qb = q_ref[...].astype(BF16)
    kb = k.astype(BF16)
    vb = v_ref[...].astype(BF16)
    s = _dot_nt(qb, kb) * intra_ref[0]
    kd = k * kdec_ref[0]
    row_batch = lax.broadcasted_iota(jnp.int32, kd.shape, 0) // rows
    inter = []
    for b in range(n_batch):
        state = st_ref[b, 0]
        inter.append(_dot(qb[b * rows:(b + 1) * rows], state.astype(BF16)))
        kd_b = jnp.where(row_batch == b, kd, 0.0).astype(BF16)
        so_ref[b, 0] = state * cdec_ref[0] + _dot_tn(kd_b, vb)
    o = _dot(s.astype(BF16), vb) + jnp.concatenate(inter, axis=0) * qdec_ref[0]
    y_ref[...] = _groupnorm_gate(o, g_ref[...], gn_ref[...])


def _ret_sample(proj, gn, state, n_batch, rows, t_valid):
    m = n_batch * rows
    log_g = _log_decay()
    r = jnp.arange(m)
    t = (r % rows).astype(F32)
    same = (r[:, None] // rows) == (r[None, :] // rows)
    diff = t[:, None] - t[None, :]
    intra = jnp.where(same[None] & (diff >= 0)[None], jnp.exp(log_g[:, None, None] * jnp.maximum(diff, 0.0)[None]), 0.0)
    ones = jnp.ones((1, 1, HEAD_DIM), F32)
    qdec = jnp.exp(log_g[:, None] * (t[None, :] + 1.0))[:, :, None] * ones
    kdec = jnp.where(t[None, :] < t_valid, jnp.exp(log_g[:, None] * (t_valid - 1.0 - t[None, :])), 0.0)[:, :, None] * ones
    cdec = jnp.exp(log_g * t_valid)[:, None, None] * ones
    col = lambda off: (lambda h: (0, off // HEAD_DIM + h))
    per_head = lambda h: (h, 0, 0)
    return pl.pallas_call(
        functools.partial(_ret_sample_kernel, n_batch=n_batch, rows=rows),
        grid=(RET_HEADS,),
        in_specs=[
            pl.BlockSpec((m, HEAD_DIM), col(COL_RQ)),
            pl.BlockSpec((m, HEAD_DIM), col(COL_RK)),
            pl.BlockSpec((m, HEAD_DIM), col(COL_RV)),
            pl.BlockSpec((m, HEAD_DIM), col(COL_RG)),
            pl.BlockSpec((1, HEAD_DIM), lambda h: (0, h)),
            pl.BlockSpec((n_batch, 1, HEAD_DIM, HEAD_DIM), lambda h: (0, h, 0, 0)),
            pl.BlockSpec((1, m, m), per_head),
            pl.BlockSpec((1, m, HEAD_DIM), per_head),
            pl.BlockSpec((1, m, HEAD_DIM), per_head),
            pl.BlockSpec((1, 1, HEAD_DIM), per_head),
        ],
        out_specs=[
            pl.BlockSpec((m, HEAD_DIM), lambda h: (0, h)),
            pl.BlockSpec((n_batch, 1, HEAD_DIM, HEAD_DIM), lambda h: (0, h, 0, 0)),
        ],
        out_shape=[
            jax.ShapeDtypeStruct((m, RET_HEADS * HEAD_DIM), BF16),
            jax.ShapeDtypeStruct(state.shape, F32),
        ],
        compiler_params=_cparams("parallel"),
        name="ret_sample",
    )(proj, proj, proj, proj, gn, state, intra, qdec, kdec, cdec)


def _sortable_key(x):
    bits = pltpu.bitcast(x, jnp.int32)
    return bits ^ ((bits >> 31) & jnp.int32(0x7FFFFFFF))


def _kth_largest_key(count_ge, shape, topk):
    def body(b, res):
        cand = res + lax.shift_left(jnp.int32(1), jnp.int32(31) - b)
        return jnp.where(count_ge(cand) >= topk, cand, res)

    return lax.fori_loop(0, 32, body, jnp.full(shape, INT_MIN, jnp.int32))


def _dsa_prompt_kernel(dq_ref, iq_ref, iwq_ref, kall_ref, vall_ref, ikall_ref, o_ref,
                       kbf, vtb, ika, ikb, keys, z_scr, m_scr, acc_scr, *, tq, topk):
    i = pl.program_id(1)
    n_chunks = kbf.shape[0]
    scale_log2e = HEAD_DIM ** -0.5 * math.log2(math.e)

    @pl.when(i == 0)
    def _():
        lane = lax.broadcasted_iota(jnp.int32, (tq, LANES), 1)
        for c in range(n_chunks):
            rows = slice(c * tq, (c + 1) * tq)
            kbf[c] = kall_ref[rows, :].astype(BF16)
            for kvh in range(DSA_KV_HEADS):
                sl = slice(kvh * HEAD_DIM, (kvh + 1) * HEAD_DIM)
                vtb[c, kvh, :HEAD_DIM, :] = vall_ref[rows, sl].T.astype(BF16)
                vtb[c, kvh, HEAD_DIM:, :] = jnp.ones((ROW_ALIGN, tq), BF16)
            a = ikall_ref[rows, :]
            ika[c] = jnp.where(lane < IDX_DH, a, 0.0).astype(BF16)
            ikb[c] = jnp.where(lane >= IDX_DH, pltpu.roll(a, IDX_DH, 1), 0.0).astype(BF16)

    w_t = iwq_ref[...].T
    iqb = iq_ref[...].astype(BF16)
    qb = (dq_ref[...] * scale_log2e).astype(BF16)
    t_col = i * tq + lax.broadcasted_iota(jnp.int32, (tq, tq), 1)
    s_row0 = lax.broadcasted_iota(jnp.int32, (tq, tq), 0)

    fold = lambda x: x.reshape(tq // SUBLANES, SUBLANES, tq)

    def score_chunk(c, carry):
        smax, smin = carry
        a = ika[c]
        b = ikb[c]
        acc = jnp.zeros((tq, tq), F32)
        for p in range(IDX_HEADS // 2):
            pair = iqb[:, p * LANES:(p + 1) * LANES]
            w0 = w_t[IDX_DH + 2 * p:IDX_DH + 2 * p + 1, :]
            w1 = w_t[IDX_DH + 2 * p + 1:IDX_DH + 2 * p + 2, :]
            acc = acc + jnp.maximum(_dot_nt(a, pair), 0.0) * w0
            acc = acc + jnp.maximum(_dot_nt(b, pair), 0.0) * w1
        admissible = s_row0 + c * tq <= t_col
        keys[c] = jnp.where(admissible, acc, -jnp.inf)
        smax = jnp.maximum(smax, fold(jnp.where(admissible, acc, -jnp.inf)).max(axis=0))
        smin = jnp.minimum(smin, fold(jnp.where(admissible, acc, jnp.inf)).min(axis=0))
        return smax, smin

    smax, smin = lax.fori_loop(0, i + 1, score_chunk, (jnp.full((SUBLANES, tq), -jnp.inf, F32),
                                                      jnp.full((SUBLANES, tq), jnp.inf, F32)))
    smax = smax.max(axis=0, keepdims=True)
    smin = smin.min(axis=0, keepdims=True)
    keys[i + 1] = jnp.full((tq, tq), -jnp.inf, F32)
    n_pairs = (i + 2) // 2

    def count_ge(t):
        def pair(j, cnt):
            for c in (2 * j, 2 * j + 1):
                cnt = cnt + fold(jnp.where(keys[c] >= t, 1.0, 0.0)).sum(axis=0)
            return cnt

        return lax.fori_loop(0, n_pairs, pair, jnp.zeros((SUBLANES, tq), F32)).sum(axis=0, keepdims=True)

    n_adm = (t_col[:1, :] + 1).astype(F32)
    few = n_adm < topk
    lo = jnp.where(few, -jnp.inf, smin)
    c_lo = jnp.where(few, float(topk), n_adm)
    hi = jnp.where(few, -jnp.inf, smax + jnp.maximum(jnp.abs(smax) * 1e-6, 1e-30))

    def bisect(_, state):
        lo, hi, c_lo = state
        mid = 0.5 * lo + 0.5 * hi
        c = count_ge(mid)
        take = c >= topk
        return jnp.where(take, mid, lo), jnp.where(take, hi, mid), jnp.where(take, c, c_lo)

    lo, hi, c_lo = lax.fori_loop(0, BISECT_STEPS, bisect, (lo, hi, c_lo))
    c_lo = jnp.where(few, float(topk), c_lo)

    def unfinished(state):
        return jnp.max(state[2]) > 0.0

    def step_up(state):
        lo, c_lo, active = state

        def pair(j, carry):
            nxt, n_eq = carry
            for c in (2 * j, 2 * j + 1):
                k = keys[c]
                nxt = jnp.minimum(nxt, fold(jnp.where(k > lo, k, jnp.inf)).min(axis=0))
                n_eq = n_eq + fold(jnp.where(k == lo, 1.0, 0.0)).sum(axis=0)
            return nxt, n_eq

        nxt, n_eq = lax.fori_loop(0, n_pairs, pair, (jnp.full((SUBLANES, tq), jnp.inf, F32),
                                                     jnp.zeros((SUBLANES, tq), F32)))
        c_nxt = c_lo - n_eq.sum(axis=0, keepdims=True)
        move = active * jnp.where(c_nxt >= topk, 1.0, 0.0)
        lo = jnp.where(move > 0.0, nxt.min(axis=0, keepdims=True), lo)
        c_lo = jnp.where(move > 0.0, c_nxt, c_lo)
        return lo, c_lo, move * jnp.where(c_nxt > topk, 1.0, 0.0)

    thr, _, _ = lax.while_loop(unfinished, step_up, (lo, c_lo, jnp.where(c_lo > topk, 1.0, 0.0)))

    m_scr[...] = jnp.full_like(m_scr, NEG)
    acc_scr[...] = jnp.zeros_like(acc_scr)

    def attend_chunk(c, carry):
        bias = jnp.where(s_row0 + c * tq <= t_col, jnp.where(keys[c] >= thr, 0.0, NEG), NEG)
        kc = kbf[c]
        kv_of = lambda h: h // (DSA_HEADS // DSA_KV_HEADS)
        z_max = {}

        def logits(h):
            ksl = slice(kv_of(h) * HEAD_DIM, (kv_of(h) + 1) * HEAD_DIM)
            z = _dot_nt(kc[:, ksl], qb[:, h * HEAD_DIM:(h + 1) * HEAD_DIM]) + bias
            z_scr[h] = z
            z_max[h] = z.max(axis=0, keepdims=True)

        def update(h):
            m_old = m_scr[h]
            m_new = jnp.maximum(m_old, z_max[h])
            p = jnp.exp2(z_scr[h] - m_new)
            acc_scr[h] = acc_scr[h] * jnp.exp2(m_old - m_new) + _dot(vtb[c, kv_of(h)], p.astype(BF16))
            m_scr[h] = m_new

        for h in range(DSA_HEADS):
            logits(h)
        for h in range(DSA_HEADS):
            update(h)
        return carry

    lax.fori_loop(0, i + 1, attend_chunk, 0)

    for h in range(DSA_HEADS):
        o = acc_scr[h, :HEAD_DIM, :] * (1.0 / acc_scr[h, HEAD_DIM:HEAD_DIM + 1, :])
        o_ref[:, h * HEAD_DIM:(h + 1) * HEAD_DIM] = o.T.astype(BF16)


def _dsa_prompt(proj, batch, seq, tq):
    nq = seq // tq
    topk = min(TOPK_MAX, seq // 4)
    qw = DSA_HEADS * HEAD_DIM
    kvw = DSA_KV_HEADS * HEAD_DIM
    return pl.pallas_call(
        functools.partial(_dsa_prompt_kernel, tq=tq, topk=topk),
        grid=(batch, nq),
        in_specs=[
            pl.BlockSpec((tq, qw), lambda b, i: (b * nq + i, COL_DQ // qw)),
            pl.BlockSpec((tq, qw), lambda b, i: (b * nq + i, COL_IQ // qw)),
            pl.BlockSpec((tq, LANES), lambda b, i: (b * nq + i, COL_IK // LANES)),
            pl.BlockSpec((seq, kvw), lambda b, i: (b, COL_DK // kvw)),
            pl.BlockSpec((seq, kvw), lambda b, i: (b, COL_DV // kvw)),
            pl.BlockSpec((seq, LANES), lambda b, i: (b, COL_IK // LANES)),
        ],
        out_specs=pl.BlockSpec((tq, qw), lambda b, i: (b * nq + i, 0)),
        out_shape=jax.ShapeDtypeStruct((batch * seq, qw), BF16),
        scratch_shapes=[
            pltpu.VMEM((nq, tq, kvw), BF16),
            pltpu.VMEM((nq, DSA_KV_HEADS, HEAD_DIM + ROW_ALIGN, tq), BF16),
            pltpu.VMEM((nq, tq, LANES), BF16),
            pltpu.VMEM((nq, tq, LANES), BF16),
            pltpu.VMEM((nq + 1, tq, tq), F32),
            pltpu.VMEM((DSA_HEADS, tq, tq), F32),
            pltpu.VMEM((DSA_HEADS, 1, tq), F32),
            pltpu.VMEM((DSA_HEADS, HEAD_DIM + ROW_ALIGN, tq), F32),
        ],
        compiler_params=_cparams("parallel", "arbitrary"),
        name="dsa_prompt",
    )(proj, proj, proj, proj, proj, proj)


def _idx_sample_kernel(pt_ref, iq_ref, iw_ref, iknew_ref, *rest, pages):
    page_refs = rest[:pages]
    sp_ref, sn_ref = rest[pages:]
    iqb = iq_ref[0].astype(BF16)
    w = iw_ref[0]
    rows = iqb.shape[0] // IDX_HEADS

    def scores(ik_t):
        d = jnp.maximum(_dot(iqb, ik_t.astype(BF16)), 0.0) * w
        return d.reshape(rows, IDX_HEADS, LANES).sum(axis=1)

    for p in range(pages):
        sp_ref[0, :, p * PAGE_SIZE:(p + 1) * PAGE_SIZE] = scores(page_refs[p][0])
    t = lax.broadcasted_iota(jnp.int32, (rows, LANES), 0)
    j = lax.broadcasted_iota(jnp.int32, (rows, LANES), 1)
    sn_ref[0] = jnp.where(j <= t, scores(iknew_ref[0]), -jnp.inf)


def _idx_sample(page_table, iq, iw, ik_new, pool_ik, pages):
    n_batch, n_pages = page_table.shape
    rows16 = iq.shape[1]
    rows = rows16 // IDX_HEADS
    past = n_pages * PAGE_SIZE
    page_specs = [pl.BlockSpec((1, IDX_DH, PAGE_SIZE), functools.partial(
        lambda b, g, pt, k: (pt[b, g * pages + k], 0, 0), k=k)) for k in range(pages)]
    grid_spec = pltpu.PrefetchScalarGridSpec(
        num_scalar_prefetch=1,
        grid=(n_batch, n_pages // pages),
        in_specs=[
            pl.BlockSpec((1, rows16, IDX_DH), lambda b, g, pt: (b, 0, 0)),
            pl.BlockSpec((1, rows16, LANES), lambda b, g, pt: (b, 0, 0)),
            pl.BlockSpec((1, IDX_DH, PAGE_SIZE), lambda b, g, pt: (b, 0, 0)),
        ] + page_specs,
        out_specs=[
            pl.BlockSpec((1, rows, pages * PAGE_SIZE), lambda b, g, pt: (b, 0, g)),
            pl.BlockSpec((1, rows, LANES), lambda b, g, pt: (b, 0, 0)),
        ],
    )
    return pl.pallas_call(
        functools.partial(_idx_sample_kernel, pages=pages),
        grid_spec=grid_spec,
        out_shape=[
            jax.ShapeDtypeStruct((n_batch, rows, past), F32),
            jax.ShapeDtypeStruct((n_batch, rows, LANES), F32),
        ],
        compiler_params=_cparams("parallel", "arbitrary"),
        name="idx_sample",
    )(page_table, iq, iw, ik_new, *([pool_ik] * pages))


def _sel_sample_kernel(sp_ref, sn_ref, bp_ref, bn_ref, *, topk):
    n_batch, rows, past = sp_ref.shape
    kp = _sortable_key(sp_ref[...].reshape(n_batch * rows, past))
    kn = _sortable_key(sn_ref[...].reshape(n_batch * rows, LANES))

    def count_ge(cand):
        cp = jnp.where(kp >= cand, 1.0, 0.0).sum(axis=1, keepdims=True)
        cn = jnp.where(kn >= cand, 1.0, 0.0).sum(axis=1, keepdims=True)
        return cp + cn

    thr = _kth_largest_key(count_ge, (n_batch * rows, 1), topk)
    bp_ref[...] = jnp.where(kp >= thr, 0.0, NEG).reshape(n_batch, rows, past)
    t = lax.broadcasted_iota(jnp.int32, (n_batch * rows, LANES), 0) % rows
    j = lax.broadcasted_iota(jnp.int32, (n_batch * rows, LANES), 1)
    bn_ref[...] = jnp.where(j <= t, jnp.where(kn >= thr, 0.0, NEG), NEG).reshape(n_batch, rows, LANES)


def _sel_sample(scores_past, scores_new, topk):
    full = lambda a: pl.BlockSpec(a.shape, lambda: (0,) * a.ndim)
    return pl.pallas_call(
        functools.partial(_sel_sample_kernel, topk=topk),
        in_specs=[full(scores_past), full(scores_new)],
        out_specs=[full(scores_past), full(scores_new)],
        out_shape=[jax.ShapeDtypeStruct(scores_past.shape, F32), jax.ShapeDtypeStruct(scores_new.shape, F32)],
        compiler_params=pltpu.CompilerParams(vmem_limit_bytes=VMEM_LIMIT),
        name="sel_sample",
    )(scores_past, scores_new)


def _dsa_sample_kernel(pt_ref, bp_ref, bn_ref, q_ref, knew_ref, vnew_ref, *rest, pages):
    k_refs = rest[:pages]
    v_refs = rest[pages:2 * pages]
    o_ref, m_scr, l_scr, acc_scr = rest[2 * pages:]
    g = pl.program_id(1)
    scale = HEAD_DIM ** -0.5
    group = DSA_HEADS // DSA_KV_HEADS

    @pl.when(g == 0)
    def _():
        m_scr[...] = jnp.full_like(m_scr, NEG)
        l_scr[...] = jnp.zeros_like(l_scr)
        acc_scr[...] = jnp.zeros_like(acc_scr)

    def head_rows(ref, kvh):
        return ref[0, pl.ds(kvh, PAGE_SIZE, stride=DSA_KV_HEADS), :].astype(BF16)

    def attend(k_pages, v_pages, bias):
        bias_g = jnp.concatenate([bias] * group, axis=0)
        kv_heads = range(DSA_KV_HEADS)
        qb = [q_ref[0, kvh].astype(BF16) for kvh in kv_heads]
        raw = [jnp.concatenate([_dot_nt(qb[kvh], head_rows(k, kvh)) for k in k_pages], axis=1) for kvh in kv_heads]
        pb, alpha = [], []
        for kvh in kv_heads:
            z = jnp.where(bias_g < 0.0, NEG, raw[kvh])
            m_old = m_scr[kvh]
            m_new = jnp.maximum(m_old, z.max(axis=1, keepdims=True))
            alpha.append(jnp.exp((m_old - m_new) * scale))
            p = jnp.exp((z - m_new) * scale)
            l_scr[kvh] = l_scr[kvh] * alpha[kvh] + p.sum(axis=1, keepdims=True)
            m_scr[kvh] = m_new
            pb.append(p.astype(BF16))
        for kvh in kv_heads:
            pv = _dot(pb[kvh][:, :PAGE_SIZE], head_rows(v_pages[0], kvh))
            for n in range(1, len(v_pages)):
                pv = pv + _dot(pb[kvh][:, n * PAGE_SIZE:(n + 1) * PAGE_SIZE], head_rows(v_pages[n], kvh))
            acc_scr[kvh] = acc_scr[kvh] * alpha[kvh] + pv

    attend(k_refs, v_refs, bp_ref[0])

    @pl.when(g == pl.num_programs(1) - 1)
    def _():
        attend([knew_ref], [vnew_ref], bn_ref[0])
        for kvh in range(DSA_KV_HEADS):
            o_ref[0, kvh] = (acc_scr[kvh] * (1.0 / l_scr[kvh])).astype(BF16)


def _dsa_sample(page_table, bias_past, bias_new, q, k_new, v_new, pool_k, pool_v, pages):
    n_batch, n_pages = page_table.shape
    rows = bias_past.shape[1]
    page_rows = DSA_KV_HEADS * PAGE_SIZE
    grows = q.shape[2]
    page_spec = lambda k: pl.BlockSpec((1, page_rows, HEAD_DIM), functools.partial(
        lambda b, g, pt, k: (pt[b, g * pages + k], 0, 0), k=k))
    per_batch3 = lambda b, g, pt: (b, 0, 0)
    grid_spec = pltpu.PrefetchScalarGridSpec(
        num_scalar_prefetch=1,
        grid=(n_batch, n_pages // pages),
        in_specs=[
            pl.BlockSpec((1, rows, pages * PAGE_SIZE), lambda b, g, pt: (b, 0, g)),
            pl.BlockSpec((1, rows, LANES), per_batch3),
            pl.BlockSpec((1, DSA_KV_HEADS, grows, HEAD_DIM), lambda b, g, pt: (b, 0, 0, 0)),
            pl.BlockSpec((1, page_rows, HEAD_DIM), per_batch3),
            pl.BlockSpec((1, page_rows, HEAD_DIM), per_batch3),
        ] + [page_spec(k) for k in range(pages)] * 2,
        out_specs=pl.BlockSpec((1, DSA_KV_HEADS, grows, HEAD_DIM), lambda b, g, pt: (b, 0, 0, 0)),
        scratch_shapes=[
            pltpu.VMEM((DSA_KV_HEADS, grows, 1), F32),
            pltpu.VMEM((DSA_KV_HEADS, grows, 1), F32),
            pltpu.VMEM((DSA_KV_HEADS, grows, HEAD_DIM), F32),
        ],
    )
    return pl.pallas_call(
        functools.partial(_dsa_sample_kernel, pages=pages),
        grid_spec=grid_spec,
        out_shape=jax.ShapeDtypeStruct((n_batch, DSA_KV_HEADS, grows, HEAD_DIM), BF16),
        compiler_params=_cparams("parallel", "arbitrary"),
        name="dsa_sample",
    )(page_table, bias_past, bias_new, q, k_new, v_new, *([pool_k] * pages), *([pool_v] * pages))


def _merge_kernel(x_ref, ry_ref, do_ref, ga_ref, gb_ref, wr_ref, wd_ref, wo_ref, o_ref):
    ya = _dot(ry_ref[...], wr_ref[...])
    yb = _dot(do_ref[...], wd_ref[...])
    merged = _sigmoid(ga_ref[...]) * ya + _sigmoid(gb_ref[...]) * yb
    o_ref[...] = x_ref[...] + _dot(merged.astype(BF16), wo_ref[...])


def _merge(x, ret_y, dsa_o, proj, w_ret_out, w_dsa_out, w_o, tm):
    m, d = x.shape
    w = ret_y.shape[1]
    const = lambda i: (0, 0)
    return pl.pallas_call(
        _merge_kernel,
        grid=(m // tm,),
        in_specs=[
            pl.BlockSpec((tm, d), lambda i: (i, 0)),
            pl.BlockSpec((tm, w), lambda i: (i, 0)),
            pl.BlockSpec((tm, w), lambda i: (i, 0)),
            pl.BlockSpec((tm, d), lambda i: (i, COL_GA // d)),
            pl.BlockSpec((tm, d), lambda i: (i, COL_GB // d)),
            pl.BlockSpec((w, d), const),
            pl.BlockSpec((w, d), const),
            pl.BlockSpec((d, d), const),
        ],
        out_specs=pl.BlockSpec((tm, d), lambda i: (i, 0)),
        out_shape=jax.ShapeDtypeStruct((m, d), F32),
        compiler_params=_cparams("parallel"),
        name="merge_out",
    )(x, ret_y, dsa_o, proj, proj, w_ret_out, w_dsa_out, w_o)


def _mem_attn_kernel(h_ref, g_ref, wq_ref, mk_ref, mv_ref, wo_ref, o_ref, *, rows_per_batch):
    h = h_ref[...]
    qm = _dot(_rmsnorm_bf16(h, g_ref[...]), wq_ref[...])
    tm = h.shape[0]
    nk = mk_ref.shape[0]
    scale = HEAD_DIM ** -0.5
    if rows_per_batch is not None:
        rb = lax.broadcasted_iota(jnp.int32, (tm, nk), 0) // rows_per_batch
        kb = lax.broadcasted_iota(jnp.int32, (tm, nk), 1) // MEM_LEN
        same = rb == kb
    outs = []
    for hd in range(MEM_HEADS):
        sl = slice(hd * HEAD_DIM, (hd + 1) * HEAD_DIM)
        z = _dot_nt(qm[:, sl].astype(BF16), mk_ref[:, sl].astype(BF16)) * scale
        if rows_per_batch is not None:
            z = jnp.where(same, z, NEG)
        p = jnp.exp(z - z.max(axis=-1, keepdims=True))
        p = p / p.sum(axis=-1, keepdims=True)
        outs.append(_dot(p.astype(BF16), mv_ref[:, sl].astype(BF16)))
    om = jnp.concatenate(outs, axis=1).astype(BF16)
    o_ref[...] = h + _dot(om, wo_ref[...])


def _mem_attn(h, g, w_mq, mk, mv, w_mo, *, tm, batch_tiles, mk_col, mv_col, nk, rows_per_batch):
    m, d = h.shape
    const = lambda i: (0, 0)
    if batch_tiles:
        kmap = lambda col: (lambda i: (i // batch_tiles, col))
    else:
        kmap = lambda col: (lambda i: (0, col))
    return pl.pallas_call(
        functools.partial(_mem_attn_kernel, rows_per_batch=rows_per_batch),
        grid=(m // tm,),
        in_specs=[
            pl.BlockSpec((tm, d), lambda i: (i, 0)),
            pl.BlockSpec((1, d), const),
            pl.BlockSpec((d, MEM_W), const),
            pl.BlockSpec((nk, MEM_W), kmap(mk_col)),
            pl.BlockSpec((nk, MEM_W), kmap(mv_col)),
            pl.BlockSpec((MEM_W, d), const),
        ],
        out_specs=pl.BlockSpec((tm, d), lambda i: (i, 0)),
        out_shape=jax.ShapeDtypeStruct((m, d), F32),
        compiler_params=_cparams("parallel"),
        name="mem_attn",
    )(h, g, w_mq, mk, mv, w_mo)


def _mlp_kernel(h_ref, g_ref, wu_ref, wd_ref, gf_ref, o_ref, u_ref):
    j = pl.program_id(1)

    @pl.when(j == 0)
    def _():
        u_ref[...] = _rmsnorm_bf16(h_ref[...], g_ref[...])
        o_ref[...] = h_ref[...]

    a = jnp.maximum(_dot(u_ref[...], wu_ref[...]), 0.0)
    o_ref[...] += _dot((a * a).astype(BF16), wd_ref[...])

    @pl.when(j == pl.num_programs(1) - 1)
    def _():
        y = o_ref[...]
        ms = jnp.mean(y * y, axis=-1, keepdims=True)
        o_ref[...] = (y * lax.rsqrt(ms + EPS)) * gf_ref[...]


def _mlp(h, g, w_up, w_down, g_final, tm, tf):
    m, d = h.shape
    ff = w_up.shape[1]
    return pl.pallas_call(
        _mlp_kernel,
        grid=(m // tm, ff // tf),
        in_specs=[
            pl.BlockSpec((tm, d), lambda i, j: (i, 0)),
            pl.BlockSpec((1, d), lambda i, j: (0, 0)),
            pl.BlockSpec((d, tf), lambda i, j: (0, j)),
            pl.BlockSpec((tf, d), lambda i, j: (j, 0)),
            pl.BlockSpec((1, d), lambda i, j: (0, 0)),
        ],
        out_specs=pl.BlockSpec((tm, d), lambda i, j: (i, 0)),
        out_shape=jax.ShapeDtypeStruct((m, d), F32),
        scratch_shapes=[pltpu.VMEM((tm, d), BF16)],
        compiler_params=_cparams("parallel", "arbitrary"),
        name="mlp_final",
    )(h, g, w_up, w_down, g_final)


def _pick_tile(n, pref):
    t = min(n, pref)
    while n % t:
        t //= 2
    return t


def kernel(x_prompt, x_sample, mem_prompt, cache_k, cache_v, cache_idx_k, state_ret, cache_mem_k, cache_mem_v,
           page_table, g_mix, w_in, gn_ret, w_ret_out, w_dsa_out, w_o, g_mem, g_memkv, w_mq, w_mk, w_mv, w_mo,
           g_mlp, w_up, w_down, g_final):
    assert w_in.shape[0] == 1, "one layer"
    batch, seq, d = x_prompt.shape
    n_dec, t_dec, _ = x_sample.shape
    n_pages = page_table.shape[1]
    past = n_pages * PAGE_SIZE
    mem_len = mem_prompt.shape[1]
    assert mem_len == MEM_LEN and t_dec <= SUBLANES
    row = lambda v: v.reshape(1, -1)

    w_proj_t = jnp.swapaxes(w_in[0], 0, 1)
    w_ret_out_b, w_dsa_out_b, w_o_b = (w[0].astype(BF16) for w in (w_ret_out, w_dsa_out, w_o))
    w_mq_b, w_mo_b = w_mq[0].astype(BF16), w_mo[0].astype(BF16)
    w_mkv_b = jnp.concatenate([w_mk[0], w_mv[0]], axis=1).astype(BF16)
    w_up_b, w_down_b = w_up[0].astype(BF16), w_down[0].astype(BF16)
    g_mix_r, gn_r, g_mem_r, g_memkv_r, g_mlp_r, g_final_r = (
        row(v) for v in (g_mix[0], gn_ret[0], g_mem[0], g_memkv[0], g_mlp[0], g_final))

    m_p = batch * seq
    xp = x_prompt.reshape(m_p, d)
    tm_p = _pick_tile(seq, 512)
    tm_proj = _pick_tile(seq, 1024)
    proj_p = _norm_proj_rope(xp, g_mix_r, w_proj_t, _rope_tables(0, seq, seq), tm_proj, seq // tm_proj)
    ret_y_p, ret_state_p = _ret_prompt(proj_p, gn_r, batch, seq, _pick_tile(seq, 256))
    dsa_o_p = _dsa_prompt(proj_p, batch, seq, _pick_tile(seq, 256))
    kv_p = _norm_proj(mem_prompt.reshape(batch * mem_len, d), g_memkv_r, w_mkv_b, mem_len, PROJ_TN)
    tm_t = _pick_tile(seq, 256)
    h_p = _merge(xp, ret_y_p, dsa_o_p, proj_p, w_ret_out_b, w_dsa_out_b, w_o_b, tm_t)
    h_p = _mem_attn(h_p, g_mem_r, w_mq_b, kv_p, kv_p, w_mo_b, tm=tm_t, batch_tiles=seq // tm_t,
                    mk_col=0, mv_col=1, nk=mem_len, rows_per_batch=None)
    y_p = _mlp(h_p, g_mlp_r, w_up_b, w_down_b, g_final_r, _pick_tile(seq, 1024), 512)

    rows = SUBLANES
    m_s = n_dec * rows
    xs = jnp.pad(x_sample, ((0, 0), (0, rows - t_dec), (0, 0))).reshape(m_s, d)
    proj_s = _norm_proj_rope(xs, g_mix_r, w_proj_t, _rope_tables(past, m_s, rows), m_s, 1)
    ret_y_s, ret_state_s = _ret_sample(proj_s, gn_r, state_ret[0], n_dec, rows, t_dec)

    proj_s3 = proj_s.reshape(n_dec, rows, N_PROJ)
    iq_s = proj_s3[:, :, COL_IQ:COL_IQ + IDX_HEADS * IDX_DH].reshape(n_dec, rows * IDX_HEADS, IDX_DH)
    iw_s = proj_s3[:, :, COL_IK + IDX_DH:COL_IK + IDX_DH + IDX_HEADS].reshape(n_dec, rows * IDX_HEADS, 1)
    iw_s = jnp.broadcast_to(iw_s, (n_dec, rows * IDX_HEADS, LANES))
    kvw = DSA_KV_HEADS * HEAD_DIM
    page_rows = DSA_KV_HEADS * PAGE_SIZE
    pad_keys = lambda a: jnp.pad(a, ((0, 0), (0, PAGE_SIZE - rows), (0, 0)))
    ik_new_t = jnp.swapaxes(pad_keys(proj_s3[:, :, COL_IK:COL_IK + IDX_DH]), 1, 2)
    pool_ik_t = jnp.swapaxes(cache_idx_k, 2, 3).reshape(-1, IDX_DH, PAGE_SIZE)
    k_new = pad_keys(proj_s3[:, :, COL_DK:COL_DK + kvw]).reshape(n_dec, page_rows, HEAD_DIM)
    v_new = pad_keys(proj_s3[:, :, COL_DV:COL_DV + kvw]).reshape(n_dec, page_rows, HEAD_DIM)
    scores_past, scores_new = _idx_sample(page_table, iq_s, iw_s, ik_new_t, pool_ik_t, _pick_tile(n_pages, 16))
    group = DSA_HEADS // DSA_KV_HEADS
    dq_s = proj_s3[:, :, COL_DQ:COL_DQ + DSA_HEADS * HEAD_DIM].reshape(n_dec, rows, DSA_KV_HEADS, group, HEAD_DIM)
    dq_s = dq_s.transpose(0, 2, 3, 1, 4).reshape(n_dec, DSA_KV_HEADS, group * rows, HEAD_DIM)
    bias_past, bias_new = _sel_sample(scores_past, scores_new, min(TOPK_MAX, (past + t_dec) // 4))
    dsa_o_s = _dsa_sample(page_table, bias_past, bias_new, dq_s, k_new, v_new,
                          cache_k.reshape(-1, page_rows, HEAD_DIM), cache_v.reshape(-1, page_rows, HEAD_DIM),
                          _pick_tile(n_pages, 16))
    dsa_o_s = dsa_o_s.reshape(n_dec, DSA_KV_HEADS, group, rows, HEAD_DIM).transpose(0, 3, 1, 2, 4)
    dsa_o_s = dsa_o_s.reshape(m_s, DSA_HEADS * HEAD_DIM)

    h_s = _merge(xs, ret_y_s, dsa_o_s, proj_s, w_ret_out_b, w_dsa_out_b, w_o_b, m_s)
    h_s = _mem_attn(h_s, g_mem_r, w_mq_b, cache_mem_k[0].reshape(n_dec * mem_len, MEM_W),
                    cache_mem_v[0].reshape(n_dec * mem_len, MEM_W), w_mo_b, tm=m_s, batch_tiles=0,
                    mk_col=0, mv_col=0, nk=n_dec * mem_len, rows_per_batch=rows)
    y_s = _mlp(h_s, g_mlp_r, w_up_b, w_down_b, g_final_r, m_s, 512)

    def rows_p(col, width, tail):
        return proj_p[:, col:col + width].reshape((1, batch, seq) + tail)

    def rows_s(col, width, tail):
        return proj_s3[:, :t_dec, col:col + width].reshape((1, n_dec, t_dec) + tail)

    kv_shape = (DSA_KV_HEADS, HEAD_DIM)
    return (
        y_p.reshape(batch, seq, d),
        y_s.reshape(n_dec, rows, d)[:, :t_dec],
        ret_state_p[None],
        rows_p(COL_DK, kvw, kv_shape),
        rows_p(COL_DV, kvw, kv_shape),
        rows_p(COL_IK, IDX_DH, (IDX_DH,)),
        kv_p[:, :MEM_W].reshape(1, batch, mem_len, MEM_HEADS, HEAD_DIM),
        kv_p[:, MEM_W:].reshape(1, batch, mem_len, MEM_HEADS, HEAD_DIM),
        ret_state_s[None],
        rows_s(COL_DK, kvw, kv_shape),
        rows_s(COL_DV, kvw, kv_shape),
        rows_s(COL_IK, IDX_DH, (IDX_DH,)),
    )
```

```python
import functools
import math

import jax
import jax.numpy as jnp
import numpy as np
from jax import lax
from jax.experimental import pallas as pl
from jax.experimental.pallas import tpu as pltpu

F32 = jnp.float32
BF16 = jnp.bfloat16

D_MODEL = 2048
RET_HEADS = 8
HEAD_DIM = 128
DSA_HEADS = 8
DSA_KV_HEADS = 2
IDX_HEADS = 16
IDX_DH = 64
TOPK_MAX = 256
PAGE_SIZE = 128
MEM_LEN = 256
MEM_HEADS = 4
MEM_W = MEM_HEADS * HEAD_DIM
D_FF = 4 * D_MODEL
RET_THETA = 10000.0
ROPE_THETA = 500000.0
EPS = 1e-6
LANES = 128
SUBLANES = 8
ROW_ALIGN = 16
VMEM_LIMIT = 56 * 1024 * 1024
NEG = -1e30
BISECT_STEPS = 16
INT_MIN = -(2 ** 31)

COL_GA = 0
COL_GB = 2048
COL_RQ = 4096
COL_RK = 5120
COL_RV = 6144
COL_RG = 7168
COL_DQ = 8192
COL_IQ = 9216
COL_DK = 10240
COL_DV = 10496
COL_IK = 10752
N_PROJ = 11264
PROJ_TN = 512
MODE_PLAIN, MODE_RET, MODE_RETK, MODE_DSA, MODE_IDX, MODE_DKV, MODE_IKW = range(7)
PROJ_MODES = ((MODE_PLAIN,) * 8 + (MODE_RET,) * 2 + (MODE_RETK,) * 2 + (MODE_PLAIN,) * 4
              + (MODE_DSA,) * 2 + (MODE_IDX,) * 2 + (MODE_DKV, MODE_IKW))
TAB_RET_C, TAB_RET_S, TAB_DSA_C, TAB_DSA_S, TAB_IDX_C, TAB_IDX_S = range(6)
TAB_W = 6 * LANES
DSA_ROT = HEAD_DIM // 4
IDX_ROT = IDX_DH // 4


def _cparams(*sem):
    return pltpu.CompilerParams(dimension_semantics=sem, vmem_limit_bytes=VMEM_LIMIT)


def _dot(a, b):
    return jnp.dot(a, b, preferred_element_type=F32)


def _dot_nt(a, b):
    return lax.dot_general(a, b, (((1,), (1,)), ((), ())), preferred_element_type=F32)


def _dot_tn(a, b):
    return lax.dot_general(a, b, (((0,), (0,)), ((), ())), preferred_element_type=F32)


def _rmsnorm_bf16(x, g):
    ms = jnp.mean(x * x, axis=-1, keepdims=True)
    return ((x * lax.rsqrt(ms + EPS)) * g).astype(BF16)


def _sigmoid(x):
    return 1.0 / (1.0 + jnp.exp(-x))


def _tab(tab_ref, which):
    return tab_ref[:, which * LANES:(which + 1) * LANES]


def _rope_cols(o_ref, g, c, s, half, period):
    a = o_ref[:, g * LANES:(g + 1) * LANES]
    if 2 * half == LANES:
        partner = pltpu.roll(a, half, 1)
    else:
        lane = lax.broadcasted_iota(jnp.int32, (1, LANES), 1)
        first = (lane & (period - 1)) < half
        partner = jnp.where(first, pltpu.roll(a, LANES - half, 1), pltpu.roll(a, half, 1))
    o_ref[:, g * LANES:(g + 1) * LANES] = a * c + partner * s


def _norm_proj_kernel(x_ref, g_ref, w_ref, tab_ref, o_ref, u_ref, *, modes, kr_ref=None, vr_ref=None):
    j = pl.program_id(1)

    @pl.when(j == 0)
    def _():
        u_ref[...] = _rmsnorm_bf16(x_ref[...], g_ref[...])

    if tab_ref is None:
        o_ref[...] = _dot(u_ref[...], w_ref[...])
        return
    o_ref[...] = _dot_nt(u_ref[...], w_ref[...].astype(BF16))
    groups = o_ref.shape[1] // LANES

    def blocks_of(mode):
        return [b for b, m in enumerate(modes) if m == mode]

    def when_mode(mode):
        bs = blocks_of(mode)
        return pl.when((j >= bs[0]) & (j <= bs[-1]))

    @when_mode(MODE_RET)
    def _():
        for g in range(groups):
            _rope_cols(o_ref, g, _tab(tab_ref, TAB_RET_C), _tab(tab_ref, TAB_RET_S), LANES // 2, LANES)

    @when_mode(MODE_RETK)
    def _():
        scale = HEAD_DIM ** -0.5
        for g in range(groups):
            _rope_cols(o_ref, g, _tab(tab_ref, TAB_RET_C) * scale, _tab(tab_ref, TAB_RET_S) * scale,
                       LANES // 2, LANES)

    @when_mode(MODE_DSA)
    def _():
        for g in range(groups):
            _rope_cols(o_ref, g, _tab(tab_ref, TAB_DSA_C), _tab(tab_ref, TAB_DSA_S), DSA_ROT // 2, LANES)

    @when_mode(MODE_IDX)
    def _():
        for g in range(groups):
            _rope_cols(o_ref, g, _tab(tab_ref, TAB_IDX_C), _tab(tab_ref, TAB_IDX_S), IDX_ROT // 2, IDX_DH)

    @when_mode(MODE_DKV)
    def _():
        for g in range(DSA_KV_HEADS):
            _rope_cols(o_ref, g, _tab(tab_ref, TAB_DSA_C), _tab(tab_ref, TAB_DSA_S), DSA_ROT // 2, LANES)
        rows = o_ref.shape[0]
        for kvh in range(DSA_KV_HEADS):
            dst = pl.ds(kvh, rows, stride=DSA_KV_HEADS)
            kr_ref[dst, :] = o_ref[:, kvh * HEAD_DIM:(kvh + 1) * HEAD_DIM]
            vr_ref[dst, :] = o_ref[:, (DSA_KV_HEADS + kvh) * HEAD_DIM:(DSA_KV_HEADS + kvh + 1) * HEAD_DIM]

    @when_mode(MODE_IKW)
    def _():
        lane = lax.broadcasted_iota(jnp.int32, (1, LANES), 1)
        is_ik = lane < IDX_DH
        c = jnp.where(is_ik, _tab(tab_ref, TAB_IDX_C), 1.0)
        s = jnp.where(is_ik, _tab(tab_ref, TAB_IDX_S), 0.0)
        _rope_cols(o_ref, 0, c, s, IDX_ROT // 2, IDX_DH)


def _norm_proj(x, g, w, tm, tn):
    m, d = x.shape
    n = w.shape[1]

    def body(x_ref, g_ref, w_ref, o_ref, u_ref):
        _norm_proj_kernel(x_ref, g_ref, w_ref, None, o_ref, u_ref, modes=None)

    return pl.pallas_call(
        body,
        grid=(m // tm, n // tn),
        in_specs=[
            pl.BlockSpec((tm, d), lambda i, j: (i, 0)),
            pl.BlockSpec((1, d), lambda i, j: (0, 0)),
            pl.BlockSpec((d, tn), lambda i, j: (0, j)),
        ],
        out_specs=pl.BlockSpec((tm, tn), lambda i, j: (i, j)),
        out_shape=jax.ShapeDtypeStruct((m, n), F32),
        scratch_shapes=[pltpu.VMEM((tm, d), BF16)],
        compiler_params=_cparams("parallel", "arbitrary"),
        name="norm_proj",
    )(x, g, w)


_W_IN_ROW = {"rq": 0, "rk": 1024, "rv": 2048, "rg": 3072, "dq": 4096, "dk": 5120, "iq": 5632, "ik": 6656,
             "ga": 6736, "gb": 8784}
PROJ_SRC_ROWS = tuple(_W_IN_ROW[name] + PROJ_TN * k for name, nblk in (
    ("ga", 4), ("gb", 4), ("rq", 2), ("rk", 2), ("rv", 2), ("rg", 2), ("dq", 2), ("iq", 2), ("dk", 1), ("ik", 1))
    for k in range(nblk))


def _norm_proj_rope(x, g, w_t, tab, tm, pos_blocks):
    m, d = x.shape
    assert all(r % ROW_ALIGN == 0 for r in PROJ_SRC_ROWS)
    src_rows = jnp.asarray([r // ROW_ALIGN for r in PROJ_SRC_ROWS], jnp.int32)
    grid_spec = pltpu.PrefetchScalarGridSpec(
        num_scalar_prefetch=1,
        grid=(m // tm, len(PROJ_SRC_ROWS)),
        in_specs=[
            pl.BlockSpec((tm, d), lambda i, j, src: (i, 0)),
            pl.BlockSpec((1, d), lambda i, j, src: (0, 0)),
            pl.BlockSpec((pl.Element(PROJ_TN), pl.Element(d)), lambda i, j, src: (src[j] * ROW_ALIGN, 0)),
            pl.BlockSpec((tm, TAB_W), lambda i, j, src: (i % pos_blocks, 0)),
        ],
        out_specs=[
            pl.BlockSpec((tm, PROJ_TN), lambda i, j, src: (i, j)),
            pl.BlockSpec((DSA_KV_HEADS * tm, HEAD_DIM), lambda i, j, src: (i, 0)),
            pl.BlockSpec((DSA_KV_HEADS * tm, HEAD_DIM), lambda i, j, src: (i, 0)),
        ],
        scratch_shapes=[pltpu.VMEM((tm, d), BF16)],
    )

    def body(src_ref, x_ref, g_ref, w_ref, tab_ref, o_ref, kr_ref, vr_ref, u_ref):
        _norm_proj_kernel(x_ref, g_ref, w_ref, tab_ref, o_ref, u_ref, modes=PROJ_MODES, kr_ref=kr_ref, vr_ref=vr_ref)

    kv_rows = jax.ShapeDtypeStruct((DSA_KV_HEADS * m, HEAD_DIM), F32)
    return pl.pallas_call(
        body,
        grid_spec=grid_spec,
        out_shape=[jax.ShapeDtypeStruct((m, N_PROJ), F32), kv_rows, kv_rows],
        compiler_params=_cparams("parallel", "arbitrary"),
        name="norm_proj_rope",
    )(src_rows, x, g, w_t, tab)


@functools.lru_cache(maxsize=None)
def _rope_tables(start, count, period):
    pos = (start + np.arange(count) % period).astype(np.float64)[:, None]

    def cs(half, theta):
        inv = theta ** (-np.arange(half, dtype=np.float64) / half)
        ang = pos * inv[None, :]
        return np.cos(ang), np.sin(ang)

    ones = lambda n: np.ones((count, n))
    zeros = lambda n: np.zeros((count, n))
    c, s = cs(HEAD_DIM // 2, RET_THETA)
    ret_c = np.concatenate([c, c], 1)
    ret_s = np.concatenate([-s, s], 1)
    c, s = cs(DSA_ROT // 2, ROPE_THETA)
    dsa_c = np.concatenate([c, c, ones(LANES - DSA_ROT)], 1)
    dsa_s = np.concatenate([-s, s, zeros(LANES - DSA_ROT)], 1)
    c, s = cs(IDX_ROT // 2, ROPE_THETA)
    idx_c = np.tile(np.concatenate([c, c, ones(IDX_DH - IDX_ROT)], 1), (1, 2))
    idx_s = np.tile(np.concatenate([-s, s, zeros(IDX_DH - IDX_ROT)], 1), (1, 2))
    return np.concatenate([ret_c, ret_s, dsa_c, dsa_s, idx_c, idx_s], 1).astype(np.float32)


def _log_decay():
    return jnp.log1p(-jnp.exp2(-5.0 - jnp.arange(RET_HEADS, dtype=F32)))


def _groupnorm_gate(o, gate, gn):
    mu = jnp.mean(o, axis=-1, keepdims=True)
    d = o - mu
    var = jnp.mean(d * d, axis=-1, keepdims=True)
    n = d * lax.rsqrt(var + EPS) * gn
    return (gate * _sigmoid(gate) * n).astype(BF16)


def _ret_prompt_kernel(q_ref, k_ref, v_ref, g_ref, gn_ref, intra_ref, qdec_ref, kdec_ref, cdec_ref,
                       y_ref, st_ref, s_scr):
    c = pl.program_id(1)

    @pl.when(c == 0)
    def _():
        s_scr[...] = jnp.zeros_like(s_scr)

    for h in range(RET_HEADS):
        sl = slice(h * HEAD_DIM, (h + 1) * HEAD_DIM)
        k = k_ref[:, sl]
        qb = q_ref[:, sl].astype(BF16)
        kb = k.astype(BF16)
        vb = v_ref[:, sl].astype(BF16)
        s = _dot_nt(qb, kb) * intra_ref[h]
        state = s_scr[h]
        o = _dot(s.astype(BF16), vb) + _dot(qb, state.astype(BF16)) * qdec_ref[h]
        kd = (k * kdec_ref[h]).astype(BF16)
        s_scr[h] = state * cdec_ref[h] + _dot_tn(kd, vb)
        y_ref[:, sl] = _groupnorm_gate(o, g_ref[:, sl], gn_ref[:, sl])

    @pl.when(c == pl.num_programs(1) - 1)
    def _():
        st_ref[0] = s_scr[...]


def _ret_prompt(proj, gn, batch, seq, chunk):
    nc = seq // chunk
    log_g = _log_decay()
    j = jnp.arange(chunk, dtype=F32)
    diff = j[:, None] - j[None, :]
    intra = jnp.where(diff >= 0, jnp.exp(log_g[:, None, None] * jnp.maximum(diff, 0.0)), 0.0)
    ones = jnp.ones((1, 1, HEAD_DIM), F32)
    qdec = jnp.exp(log_g[:, None] * (j[None, :] + 1.0))[:, :, None] * ones
    kdec = jnp.exp(log_g[:, None] * (chunk - 1.0 - j[None, :]))[:, :, None] * ones
    cdec = jnp.exp(log_g * chunk)[:, None, None] * ones
    w = RET_HEADS * HEAD_DIM
    col = lambda off: (lambda b, c: (b * nc + c, off // w))
    const3 = lambda b, c: (0, 0, 0)
    return pl.pallas_call(
        _ret_prompt_kernel,
        grid=(batch, nc),
        in_specs=[
            pl.BlockSpec((chunk, w), col(COL_RQ)),
            pl.BlockSpec((chunk, w), col(COL_RK)),
            pl.BlockSpec((chunk, w), col(COL_RV)),
            pl.BlockSpec((chunk, w), col(COL_RG)),
            pl.BlockSpec((1, w), lambda b, c: (0, 0)),
            pl.BlockSpec((RET_HEADS, chunk, chunk), const3),
            pl.BlockSpec((RET_HEADS, chunk, HEAD_DIM), const3),
            pl.BlockSpec((RET_HEADS, chunk, HEAD_DIM), const3),
            pl.BlockSpec((RET_HEADS, 1, HEAD_DIM), const3),
        ],
        out_specs=[
            pl.BlockSpec((chunk, w), lambda b, c: (b * nc + c, 0)),
            pl.BlockSpec((1, RET_HEADS, HEAD_DIM, HEAD_DIM), lambda b, c: (b, 0, 0, 0)),
        ],
        out_shape=[
            jax.ShapeDtypeStruct((batch * seq, w), BF16),
            jax.ShapeDtypeStruct((batch, RET_HEADS, HEAD_DIM, HEAD_DIM), F32),
        ],
        scratch_shapes=[pltpu.VMEM((RET_HEADS, HEAD_DIM, HEAD_DIM), F32)],
        compiler_params=_cparams("parallel", "arbitrary"),
        name="ret_prompt",
    )(proj, proj, proj, proj, gn, intra, qdec, kdec, cdec)


def _ret_sample_kernel(q_ref, k_ref, v_ref, g_ref, gn_ref, st_ref, intra_ref, qdec_ref, kdec_ref, cdec_ref,
                       y_ref, so_ref, *, n_batch, rows):
    k = k_ref[...]
    qb = q_ref[...].astype(BF16)
    kb = k.astype(BF16)
    vb = v_ref[...].astype(BF16)
    s = _dot_nt(qb, kb) * intra_ref[0]
    kd = k * kdec_ref[0]
    row_batch = lax.broadcasted_iota(jnp.int32, kd.shape, 0) // rows
    inter = []
    for b in range(n_batch):
        state = st_ref[b, 0]
        inter.append(_dot(qb[b * rows:(b + 1) * rows], state.astype(BF16)))
        kd_b = jnp.where(row_batch == b, kd, 0.0).astype(BF16)
        so_ref[b, 0] = state * cdec_ref[0] + _dot_tn(kd_b, vb)
    o = _dot(s.astype(BF16), vb) + jnp.concatenate(inter, axis=0) * qdec_ref[0]
    y_ref[...] = _groupnorm_gate(o, g_ref[...], gn_ref[...])


def _ret_sample(proj, gn, state, n_batch, rows, t_valid):
    m = n_batch * rows
    log_g = _log_decay()
    r = jnp.arange(m)
    t = (r % rows).astype(F32)
    same = (r[:, None] // rows) == (r[None, :] // rows)
    diff = t[:, None] - t[None, :]
    intra = jnp.where(same[None] & (diff >= 0)[None], jnp.exp(log_g[:, None, None] * jnp.maximum(diff, 0.0)[None]), 0.0)
    ones = jnp.ones((1, 1, HEAD_DIM), F32)
    qdec = jnp.exp(log_g[:, None] * (t[None, :] + 1.0))[:, :, None] * ones
    kdec = jnp.where(t[None, :] < t_valid, jnp.exp(log_g[:, None] * (t_valid - 1.0 - t[None, :])), 0.0)[:, :, None] * ones
    cdec = jnp.exp(log_g * t_valid)[:, None, None] * ones
    col = lambda off: (lambda h: (0, off // HEAD_DIM + h))
    per_head = lambda h: (h, 0, 0)
    return pl.pallas_call(
        functools.partial(_ret_sample_kernel, n_batch=n_batch, rows=rows),
        grid=(RET_HEADS,),
        in_specs=[
            pl.BlockSpec((m, HEAD_DIM), col(COL_RQ)),
            pl.BlockSpec((m, HEAD_DIM), col(COL_RK)),
            pl.BlockSpec((m, HEAD_DIM), col(COL_RV)),
            pl.BlockSpec((m, HEAD_DIM), col(COL_RG)),
            pl.BlockSpec((1, HEAD_DIM), lambda h: (0, h)),
            pl.BlockSpec((n_batch, 1, HEAD_DIM, HEAD_DIM), lambda h: (0, h, 0, 0)),
            pl.BlockSpec((1, m, m), per_head),
            pl.BlockSpec((1, m, HEAD_DIM), per_head),
            pl.BlockSpec((1, m, HEAD_DIM), per_head),
            pl.BlockSpec((1, 1, HEAD_DIM), per_head),
        ],
        out_specs=[
            pl.BlockSpec((m, HEAD_DIM), lambda h: (0, h)),
            pl.BlockSpec((n_batch, 1, HEAD_DIM, HEAD_DIM), lambda h: (0, h, 0, 0)),
        ],
        out_shape=[
            jax.ShapeDtypeStruct((m, RET_HEADS * HEAD_DIM), BF16),
            jax.ShapeDtypeStruct(state.shape, F32),
        ],
        compiler_params=_cparams("parallel"),
        name="ret_sample",
    )(proj, proj, proj, proj, gn, state, intra, qdec, kdec, cdec)


def _sortable_key(x):
    bits = pltpu.bitcast(x, jnp.int32)
    return bits ^ ((bits >> 31) & jnp.int32(0x7FFFFFFF))


def _kth_largest_key(count_ge, shape, topk):
    def body(b, res):
        cand = res + lax.shift_left(jnp.int32(1), jnp.int32(31) - b)
        return jnp.where(count_ge(cand) >= topk, cand, res)

    return lax.fori_loop(0, 32, body, jnp.full(shape, INT_MIN, jnp.int32))


def _dsa_prompt_kernel(dq_ref, iq_ref, iwq_ref, kall_ref, vall_ref, ikall_ref, o_ref,
                       kbf, vtb, ika, ikb, keys, z_a, z_b, zmax_a, zmax_b, m_scr, acc_scr, *, tq, topk):
    z_scr, zmax_scr = (z_a, z_b), (zmax_a, zmax_b)
    i = pl.program_id(1)
    n_chunks = kbf.shape[0]
    scale_log2e = HEAD_DIM ** -0.5 * math.log2(math.e)

    @pl.when(i == 0)
    def _():
        lane = lax.broadcasted_iota(jnp.int32, (tq, LANES), 1)
        for c in range(n_chunks):
            rows = slice(c * tq, (c + 1) * tq)
            kbf[c] = kall_ref[rows, :].astype(BF16)
            for kvh in range(DSA_KV_HEADS):
                sl = slice(kvh * HEAD_DIM, (kvh + 1) * HEAD_DIM)
                vtb[c, kvh, :HEAD_DIM, :] = vall_ref[rows, sl].T.astype(BF16)
                vtb[c, kvh, HEAD_DIM:, :] = jnp.ones((ROW_ALIGN, tq), BF16)
            a = ikall_ref[rows, :]
            ika[c] = jnp.where(lane < IDX_DH, a, 0.0).astype(BF16)
            ikb[c] = jnp.where(lane >= IDX_DH, pltpu.roll(a, IDX_DH, 1), 0.0).astype(BF16)

    w_t = iwq_ref[...].T
    iqb = iq_ref[...].astype(BF16)
    qb = (dq_ref[...] * scale_log2e).astype(BF16)
    t_col = i * tq + lax.broadcasted_iota(jnp.int32, (tq, tq), 1)
    s_row0 = lax.broadcasted_iota(jnp.int32, (tq, tq), 0)

    fold = lambda x: x.reshape(tq // SUBLANES, SUBLANES, tq)

    def score_chunk(c, carry):
        smax, smin = carry
        a = ika[c]
        b = ikb[c]
        acc = jnp.zeros((tq, tq), F32)
        for p in range(IDX_HEADS // 2):
            pair = iqb[:, p * LANES:(p + 1) * LANES]
            w0 = w_t[IDX_DH + 2 * p:IDX_DH + 2 * p + 1, :]
            w1 = w_t[IDX_DH + 2 * p + 1:IDX_DH + 2 * p + 2, :]
            acc = acc + jnp.maximum(_dot_nt(a, pair), 0.0) * w0
            acc = acc + jnp.maximum(_dot_nt(b, pair), 0.0) * w1
        admissible = s_row0 + c * tq <= t_col
        keys[c] = jnp.where(admissible, acc, -jnp.inf)
        smax = jnp.maximum(smax, fold(jnp.where(admissible, acc, -jnp.inf)).max(axis=0))
        smin = jnp.minimum(smin, fold(jnp.where(admissible, acc, jnp.inf)).min(axis=0))
        return smax, smin

    smax, smin = lax.fori_loop(0, i + 1, score_chunk, (jnp.full((SUBLANES, tq), -jnp.inf, F32),
                                                      jnp.full((SUBLANES, tq), jnp.inf, F32)))
    smax = smax.max(axis=0, keepdims=True)
    smin = smin.min(axis=0, keepdims=True)
    keys[i + 1] = jnp.full((tq, tq), -jnp.inf, F32)
    n_pairs = (i + 2) // 2

    def count_ge(t):
        def pair(j, cnt):
            for c in (2 * j, 2 * j + 1):
                cnt = cnt + fold(jnp.where(keys[c] >= t, 1.0, 0.0)).sum(axis=0)
            return cnt

        return lax.fori_loop(0, n_pairs, pair, jnp.zeros((SUBLANES, tq), F32)).sum(axis=0, keepdims=True)

    n_adm = (t_col[:1, :] + 1).astype(F32)
    few = n_adm < topk
    lo = jnp.where(few, -jnp.inf, smin)
    c_lo = jnp.where(few, float(topk), n_adm)
    hi = jnp.where(few, -jnp.inf, smax + jnp.maximum(jnp.abs(smax) * 1e-6, 1e-30))

    def bisect(_, state):
        lo, hi, c_lo = state
        mid = 0.5 * lo + 0.5 * hi
        c = count_ge(mid)
        take = c >= topk
        return jnp.where(take, mid, lo), jnp.where(take, hi, mid), jnp.where(take, c, c_lo)

    lo, hi, c_lo = lax.fori_loop(0, BISECT_STEPS, bisect, (lo, hi, c_lo))
    c_lo = jnp.where(few, float(topk), c_lo)

    def unfinished(state):
        return jnp.max(state[2]) > 0.0

    def step_up(state):
        lo, c_lo, active = state

        def pair(j, carry):
            nxt, n_eq = carry
            for c in (2 * j, 2 * j + 1):
                k = keys[c]
                nxt = jnp.minimum(nxt, fold(jnp.where(k > lo, k, jnp.inf)).min(axis=0))
                n_eq = n_eq + fold(jnp.where(k == lo, 1.0, 0.0)).sum(axis=0)
            return nxt, n_eq

        nxt, n_eq = lax.fori_loop(0, n_pairs, pair, (jnp.full((SUBLANES, tq), jnp.inf, F32),
                                                     jnp.zeros((SUBLANES, tq), F32)))
        c_nxt = c_lo - n_eq.sum(axis=0, keepdims=True)
        move = active * jnp.where(c_nxt >= topk, 1.0, 0.0)
        lo = jnp.where(move > 0.0, nxt.min(axis=0, keepdims=True), lo)
        c_lo = jnp.where(move > 0.0, c_nxt, c_lo)
        return lo, c_lo, move * jnp.where(c_nxt > topk, 1.0, 0.0)

    thr, _, _ = lax.while_loop(unfinished, step_up, (lo, c_lo, jnp.where(c_lo > topk, 1.0, 0.0)))

    m_scr[...] = jnp.full_like(m_scr, NEG)
    acc_scr[...] = jnp.zeros_like(acc_scr)

    kv_of = lambda h: h // (DSA_HEADS // DSA_KV_HEADS)

    def logits(c, slot):
        bias = jnp.where(s_row0 + c * tq <= t_col, jnp.where(keys[c] >= thr, 0.0, NEG), NEG)
        kc = kbf[c]
        for h in range(DSA_HEADS):
            ksl = slice(kv_of(h) * HEAD_DIM, (kv_of(h) + 1) * HEAD_DIM)
            z = _dot_nt(kc[:, ksl], qb[:, h * HEAD_DIM:(h + 1) * HEAD_DIM]) + bias
            z_scr[slot][h] = z
            zmax_scr[slot][h] = z.max(axis=0, keepdims=True)

    def update(c, slot):
        for h in range(DSA_HEADS):
            m_old = m_scr[h]
            m_new = jnp.maximum(m_old, zmax_scr[slot][h])
            p = jnp.exp2(z_scr[slot][h] - m_new)
            acc_scr[h] = acc_scr[h] * jnp.exp2(m_old - m_new) + _dot(vtb[c, kv_of(h)], p.astype(BF16))
            m_scr[h] = m_new

    n = i + 1
    logits(0, 0)

    def two_chunks(k, carry):
        update(2 * k, 0)
        logits(2 * k + 1, 1)
        update(2 * k + 1, 1)
        logits(2 * k + 2, 0)
        return carry

    lax.fori_loop(0, (n - 1) // 2, two_chunks, 0)

    @pl.when(n % 2 == 1)
    def _():
        update(n - 1, 0)

    @pl.when(n % 2 == 0)
    def _():
        update(n - 2, 0)
        logits(n - 1, 1)
        update(n - 1, 1)

    for h in range(DSA_HEADS):
        o = acc_scr[h, :HEAD_DIM, :] * (1.0 / acc_scr[h, HEAD_DIM:HEAD_DIM + 1, :])
        o_ref[:, h * HEAD_DIM:(h + 1) * HEAD_DIM] = o.T.astype(BF16)


def _dsa_prompt(proj, batch, seq, tq):
    nq = seq // tq
    topk = min(TOPK_MAX, seq // 4)
    qw = DSA_HEADS * HEAD_DIM
    kvw = DSA_KV_HEADS * HEAD_DIM
    return pl.pallas_call(
        functools.partial(_dsa_prompt_kernel, tq=tq, topk=topk),
        grid=(batch, nq),
        in_specs=[
            pl.BlockSpec((tq, qw), lambda b, i: (b * nq + i, COL_DQ // qw)),
            pl.BlockSpec((tq, qw), lambda b, i: (b * nq + i, COL_IQ // qw)),
            pl.BlockSpec((tq, LANES), lambda b, i: (b * nq + i, COL_IK // LANES)),
            pl.BlockSpec((seq, kvw), lambda b, i: (b, COL_DK // kvw)),
            pl.BlockSpec((seq, kvw), lambda b, i: (b, COL_DV // kvw)),
            pl.BlockSpec((seq, LANES), lambda b, i: (b, COL_IK // LANES)),
        ],
        out_specs=pl.BlockSpec((tq, qw), lambda b, i: (b * nq + i, 0)),
        out_shape=jax.ShapeDtypeStruct((batch * seq, qw), BF16),
        scratch_shapes=[
            pltpu.VMEM((nq, tq, kvw), BF16),
            pltpu.VMEM((nq, DSA_KV_HEADS, HEAD_DIM + ROW_ALIGN, tq), BF16),
            pltpu.VMEM((nq, tq, LANES), BF16),
            pltpu.VMEM((nq, tq, LANES), BF16),
            pltpu.VMEM((nq + 1, tq, tq), F32),
            pltpu.VMEM((DSA_HEADS, tq, tq), F32),
            pltpu.VMEM((DSA_HEADS, tq, tq), F32),
            pltpu.VMEM((DSA_HEADS, 1, tq), F32),
            pltpu.VMEM((DSA_HEADS, 1, tq), F32),
            pltpu.VMEM((DSA_HEADS, 1, tq), F32),
            pltpu.VMEM((DSA_HEADS, HEAD_DIM + ROW_ALIGN, tq), F32),
        ],
        compiler_params=_cparams("parallel", "arbitrary"),
        name="dsa_prompt",
    )(proj, proj, proj, proj, proj, proj)


def _idx_sample_kernel(pt_ref, iq_ref, iw_ref, iknew_ref, *rest, pages):
    page_refs = rest[:pages]
    sp_ref, sn_ref = rest[pages:]
    iqb = iq_ref[0].astype(BF16)
    w = iw_ref[0]
    rows = iqb.shape[0] // IDX_HEADS

    def scores(ik_t):
        d = jnp.maximum(_dot(iqb, ik_t.astype(BF16)), 0.0) * w
        return d.reshape(rows, IDX_HEADS, LANES).sum(axis=1)

    for p in range(pages):
        sp_ref[0, :, p * PAGE_SIZE:(p + 1) * PAGE_SIZE] = scores(page_refs[p][0])
    t = lax.broadcasted_iota(jnp.int32, (rows, LANES), 0)
    j = lax.broadcasted_iota(jnp.int32, (rows, LANES), 1)
    sn_ref[0] = jnp.where(j <= t, scores(iknew_ref[0]), -jnp.inf)


def _idx_sample(page_table, iq, iw, ik_new, pool_ik, pages):
    n_batch, n_pages = page_table.shape
    rows16 = iq.shape[1]
    rows = rows16 // IDX_HEADS
    past = n_pages * PAGE_SIZE
    page_specs = [pl.BlockSpec((1, IDX_DH, PAGE_SIZE), functools.partial(
        lambda b, g, pt, k: (pt[b, g * pages + k], 0, 0), k=k)) for k in range(pages)]
    grid_spec = pltpu.PrefetchScalarGridSpec(
        num_scalar_prefetch=1,
        grid=(n_batch, n_pages // pages),
        in_specs=[
            pl.BlockSpec((1, rows16, IDX_DH), lambda b, g, pt: (b, 0, 0)),
            pl.BlockSpec((1, rows16, LANES), lambda b, g, pt: (b, 0, 0)),
            pl.BlockSpec((1, IDX_DH, PAGE_SIZE), lambda b, g, pt: (b, 0, 0)),
        ] + page_specs,
        out_specs=[
            pl.BlockSpec((1, rows, pages * PAGE_SIZE), lambda b, g, pt: (b, 0, g)),
            pl.BlockSpec((1, rows, LANES), lambda b, g, pt: (b, 0, 0)),
        ],
    )
    return pl.pallas_call(
        functools.partial(_idx_sample_kernel, pages=pages),
        grid_spec=grid_spec,
        out_shape=[
            jax.ShapeDtypeStruct((n_batch, rows, past), F32),
            jax.ShapeDtypeStruct((n_batch, rows, LANES), F32),
        ],
        compiler_params=_cparams("parallel", "arbitrary"),
        name="idx_sample",
    )(page_table, iq, iw, ik_new, *([pool_ik] * pages))


def _sel_sample_kernel(sp_ref, sn_ref, bp_ref, bn_ref, *, topk):
    n_batch, rows, past = sp_ref.shape
    kp = _sortable_key(sp_ref[...].reshape(n_batch * rows, past))
    kn = _sortable_key(sn_ref[...].reshape(n_batch * rows, LANES))

    def count_ge(cand):
        cp = jnp.where(kp >= cand, 1.0, 0.0).sum(axis=1, keepdims=True)
        cn = jnp.where(kn >= cand, 1.0, 0.0).sum(axis=1, keepdims=True)
        return cp + cn

    thr = _kth_largest_key(count_ge, (n_batch * rows, 1), topk)
    bp_ref[...] = jnp.where(kp >= thr, 0.0, NEG).reshape(n_batch, rows, past)
    t = lax.broadcasted_iota(jnp.int32, (n_batch * rows, LANES), 0) % rows
    j = lax.broadcasted_iota(jnp.int32, (n_batch * rows, LANES), 1)
    bn_ref[...] = jnp.where(j <= t, jnp.where(kn >= thr, 0.0, NEG), NEG).reshape(n_batch, rows, LANES)


def _sel_sample(scores_past, scores_new, topk):
    full = lambda a: pl.BlockSpec(a.shape, lambda: (0,) * a.ndim)
    return pl.pallas_call(
        functools.partial(_sel_sample_kernel, topk=topk),
        in_specs=[full(scores_past), full(scores_new)],
        out_specs=[full(scores_past), full(scores_new)],
        out_shape=[jax.ShapeDtypeStruct(scores_past.shape, F32), jax.ShapeDtypeStruct(scores_new.shape, F32)],
        compiler_params=pltpu.CompilerParams(vmem_limit_bytes=VMEM_LIMIT),
        name="sel_sample",
    )(scores_past, scores_new)


def _dsa_sample_kernel(pt_ref, bp_ref, bn_ref, q_ref, knew_ref, vnew_ref, *rest, pages):
    k_refs = rest[:pages]
    v_refs = rest[pages:2 * pages]
    o_ref, m_scr, l_scr, acc_scr = rest[2 * pages:]
    g = pl.program_id(1)
    scale = HEAD_DIM ** -0.5
    group = DSA_HEADS // DSA_KV_HEADS

    @pl.when(g == 0)
    def _():
        m_scr[...] = jnp.full_like(m_scr, NEG)
        l_scr[...] = jnp.zeros_like(l_scr)
        acc_scr[...] = jnp.zeros_like(acc_scr)

    def head_rows(ref, kvh):
        return ref[0, pl.ds(kvh, PAGE_SIZE, stride=DSA_KV_HEADS), :].astype(BF16)

    def attend(k_pages, v_pages, bias):
        bias_g = jnp.concatenate([bias] * group, axis=0)
        kv_heads = range(DSA_KV_HEADS)
        qb = [q_ref[0, kvh].astype(BF16) for kvh in kv_heads]
        raw = [jnp.concatenate([_dot_nt(qb[kvh], head_rows(k, kvh)) for k in k_pages], axis=1) for kvh in kv_heads]
        pb, alpha = [], []
        for kvh in kv_heads:
            z = jnp.where(bias_g < 0.0, NEG, raw[kvh])
            m_old = m_scr[kvh]
            m_new = jnp.maximum(m_old, z.max(axis=1, keepdims=True))
            alpha.append(jnp.exp((m_old - m_new) * scale))
            p = jnp.exp((z - m_new) * scale)
            l_scr[kvh] = l_scr[kvh] * alpha[kvh] + p.sum(axis=1, keepdims=True)
            m_scr[kvh] = m_new
            pb.append(p.astype(BF16))
        for kvh in kv_heads:
            pv = _dot(pb[kvh][:, :PAGE_SIZE], head_rows(v_pages[0], kvh))
            for n in range(1, len(v_pages)):
                pv = pv + _dot(pb[kvh][:, n * PAGE_SIZE:(n + 1) * PAGE_SIZE], head_rows(v_pages[n], kvh))
            acc_scr[kvh] = acc_scr[kvh] * alpha[kvh] + pv

    attend(k_refs, v_refs, bp_ref[0])

    @pl.when(g == pl.num_programs(1) - 1)
    def _():
        attend([knew_ref], [vnew_ref], bn_ref[0])
        for kvh in range(DSA_KV_HEADS):
            o_ref[0, kvh] = (acc_scr[kvh] * (1.0 / l_scr[kvh])).astype(BF16)


def _dsa_sample(page_table, bias_past, bias_new, q, k_new, v_new, pool_k, pool_v, pages):
    n_batch, n_pages = page_table.shape
    rows = bias_past.shape[1]
    page_rows = DSA_KV_HEADS * PAGE_SIZE
    grows = q.shape[2]
    page_spec = lambda k: pl.BlockSpec((1, page_rows, HEAD_DIM), functools.partial(
        lambda b, g, pt, k: (pt[b, g * pages + k], 0, 0), k=k))
    per_batch3 = lambda b, g, pt: (b, 0, 0)
    grid_spec = pltpu.PrefetchScalarGridSpec(
        num_scalar_prefetch=1,
        grid=(n_batch, n_pages // pages),
        in_specs=[
            pl.BlockSpec((1, rows, pages * PAGE_SIZE), lambda b, g, pt: (b, 0, g)),
            pl.BlockSpec((1, rows, LANES), per_batch3),
            pl.BlockSpec((1, DSA_KV_HEADS, grows, HEAD_DIM), lambda b, g, pt: (b, 0, 0, 0)),
            pl.BlockSpec((1, page_rows, HEAD_DIM), per_batch3),
            pl.BlockSpec((1, page_rows, HEAD_DIM), per_batch3),
        ] + [page_spec(k) for k in range(pages)] * 2,
        out_specs=pl.BlockSpec((1, DSA_KV_HEADS, grows, HEAD_DIM), lambda b, g, pt: (b, 0, 0, 0)),
        scratch_shapes=[
            pltpu.VMEM((DSA_KV_HEADS, grows, 1), F32),
            pltpu.VMEM((DSA_KV_HEADS, grows, 1), F32),
            pltpu.VMEM((DSA_KV_HEADS, grows, HEAD_DIM), F32),
        ],
    )
    return pl.pallas_call(
        functools.partial(_dsa_sample_kernel, pages=pages),
        grid_spec=grid_spec,
        out_shape=jax.ShapeDtypeStruct((n_batch, DSA_KV_HEADS, grows, HEAD_DIM), BF16),
        compiler_params=_cparams("parallel", "arbitrary"),
        name="dsa_sample",
    )(page_table, bias_past, bias_new, q, k_new, v_new, *([pool_k] * pages), *([pool_v] * pages))


def _merge_kernel(x_ref, ry_ref, do_ref, ga_ref, gb_ref, wr_ref, wd_ref, wo_ref, o_ref):
    ya = _dot(ry_ref[...], wr_ref[...])
    yb = _dot(do_ref[...], wd_ref[...])
    merged = _sigmoid(ga_ref[...]) * ya + _sigmoid(gb_ref[...]) * yb
    o_ref[...] = x_ref[...] + _dot(merged.astype(BF16), wo_ref[...])


def _merge(x, ret_y, dsa_o, proj, w_ret_out, w_dsa_out, w_o, tm):
    m, d = x.shape
    w = ret_y.shape[1]
    const = lambda i: (0, 0)
    return pl.pallas_call(
        _merge_kernel,
        grid=(m // tm,),
        in_specs=[
            pl.BlockSpec((tm, d), lambda i: (i, 0)),
            pl.BlockSpec((tm, w), lambda i: (i, 0)),
            pl.BlockSpec((tm, w), lambda i: (i, 0)),
            pl.BlockSpec((tm, d), lambda i: (i, COL_GA // d)),
            pl.BlockSpec((tm, d), lambda i: (i, COL_GB // d)),
            pl.BlockSpec((w, d), const),
            pl.BlockSpec((w, d), const),
            pl.BlockSpec((d, d), const),
        ],
        out_specs=pl.BlockSpec((tm, d), lambda i: (i, 0)),
        out_shape=jax.ShapeDtypeStruct((m, d), F32),
        compiler_params=_cparams("parallel"),
        name="merge_out",
    )(x, ret_y, dsa_o, proj, proj, w_ret_out, w_dsa_out, w_o)


def _mem_attn_kernel(h_ref, g_ref, wq_ref, mk_ref, mv_ref, wo_ref, o_ref, *, rows_per_batch):
    h = h_ref[...]
    qm = _dot(_rmsnorm_bf16(h, g_ref[...]), wq_ref[...])
    tm = h.shape[0]
    nk = mk_ref.shape[0]
    scale = HEAD_DIM ** -0.5
    if rows_per_batch is not None:
        rb = lax.broadcasted_iota(jnp.int32, (tm, nk), 0) // rows_per_batch
        kb = lax.broadcasted_iota(jnp.int32, (tm, nk), 1) // MEM_LEN
        same = rb == kb
    outs = []
    for hd in range(MEM_HEADS):
        sl = slice(hd * HEAD_DIM, (hd + 1) * HEAD_DIM)
        z = _dot_nt(qm[:, sl].astype(BF16), mk_ref[:, sl].astype(BF16)) * scale
        if rows_per_batch is not None:
            z = jnp.where(same, z, NEG)
        p = jnp.exp(z - z.max(axis=-1, keepdims=True))
        p = p / p.sum(axis=-1, keepdims=True)
        outs.append(_dot(p.astype(BF16), mv_ref[:, sl].astype(BF16)))
    om = jnp.concatenate(outs, axis=1).astype(BF16)
    o_ref[...] = h + _dot(om, wo_ref[...])


def _mem_attn(h, g, w_mq, mk, mv, w_mo, *, tm, batch_tiles, mk_col, mv_col, nk, rows_per_batch):
    m, d = h.shape
    const = lambda i: (0, 0)
    if batch_tiles:
        kmap = lambda col: (lambda i: (i // batch_tiles, col))
    else:
        kmap = lambda col: (lambda i: (0, col))
    return pl.pallas_call(
        functools.partial(_mem_attn_kernel, rows_per_batch=rows_per_batch),
        grid=(m // tm,),
        in_specs=[
            pl.BlockSpec((tm, d), lambda i: (i, 0)),
            pl.BlockSpec((1, d), const),
            pl.BlockSpec((d, MEM_W), const),
            pl.BlockSpec((nk, MEM_W), kmap(mk_col)),
            pl.BlockSpec((nk, MEM_W), kmap(mv_col)),
            pl.BlockSpec((MEM_W, d), const),
        ],
        out_specs=pl.BlockSpec((tm, d), lambda i: (i, 0)),
        out_shape=jax.ShapeDtypeStruct((m, d), F32),
        compiler_params=_cparams("parallel"),
        name="mem_attn",
    )(h, g, w_mq, mk, mv, w_mo)


def _mlp_kernel(h_ref, g_ref, wu_ref, wd_ref, gf_ref, o_ref, *rest):
    u_ref = rest[-1]
    j = pl.program_id(1)

    @pl.when(j == 0)
    def _():
        u_ref[...] = _rmsnorm_bf16(h_ref[...], g_ref[...])
        o_ref[...] = h_ref[...]

    wu, wd = wu_ref[...], wd_ref[...]
    if len(rest) == 3:
        wu, wd = wu.astype(BF16), wd.astype(BF16)
        rest[0][...] = wu
        rest[1][...] = wd
    a = jnp.maximum(_dot(u_ref[...], wu), 0.0)
    o_ref[...] += _dot((a * a).astype(BF16), wd)

    @pl.when(j == pl.num_programs(1) - 1)
    def _():
        y = o_ref[...]
        ms = jnp.mean(y * y, axis=-1, keepdims=True)
        o_ref[...] = (y * lax.rsqrt(ms + EPS)) * gf_ref[...]


def _mlp(h, g, w_up, w_down, g_final, tm, tf):
    m, d = h.shape
    ff = w_up.shape[1]
    emit = w_up.dtype != BF16
    assert not emit or m == tm, "weight copies are written once, by a call with one row tile"
    wu_spec = pl.BlockSpec((d, tf), lambda i, j: (0, j))
    wd_spec = pl.BlockSpec((tf, d), lambda i, j: (j, 0))
    out_specs = [pl.BlockSpec((tm, d), lambda i, j: (i, 0))]
    out_shape = [jax.ShapeDtypeStruct((m, d), F32)]
    if emit:
        out_specs += [wu_spec, wd_spec]
        out_shape += [jax.ShapeDtypeStruct(w_up.shape, BF16), jax.ShapeDtypeStruct(w_down.shape, BF16)]
    out = pl.pallas_call(
        _mlp_kernel,
        grid=(m // tm, ff // tf),
        in_specs=[
            pl.BlockSpec((tm, d), lambda i, j: (i, 0)),
            pl.BlockSpec((1, d), lambda i, j: (0, 0)),
            wu_spec,
            wd_spec,
            pl.BlockSpec((1, d), lambda i, j: (0, 0)),
        ],
        out_specs=out_specs,
        out_shape=out_shape,
        scratch_shapes=[pltpu.VMEM((tm, d), BF16)],
        compiler_params=_cparams("parallel", "arbitrary"),
        name="mlp_final",
    )(h, g, w_up, w_down, g_final)
    return out if emit else out[0]


def _pick_tile(n, pref):
    t = min(n, pref)
    while n % t:
        t //= 2
    return t


def kernel(x_prompt, x_sample, mem_prompt, cache_k, cache_v, cache_idx_k, state_ret, cache_mem_k, cache_mem_v,
           page_table, g_mix, w_in, gn_ret, w_ret_out, w_dsa_out, w_o, g_mem, g_memkv, w_mq, w_mk, w_mv, w_mo,
           g_mlp, w_up, w_down, g_final):
    assert w_in.shape[0] == 1, "one layer"
    batch, seq, d = x_prompt.shape
    n_dec, t_dec, _ = x_sample.shape
    n_pages = page_table.shape[1]
    past = n_pages * PAGE_SIZE
    mem_len = mem_prompt.shape[1]
    assert mem_len == MEM_LEN and t_dec <= SUBLANES
    row = lambda v: v.reshape(1, -1)

    w_proj_t = jnp.swapaxes(w_in[0], 0, 1)
    w_ret_out_b, w_dsa_out_b, w_o_b = (w[0].astype(BF16) for w in (w_ret_out, w_dsa_out, w_o))
    w_mq_b, w_mo_b = w_mq[0].astype(BF16), w_mo[0].astype(BF16)
    w_mkv_b = jnp.concatenate([w_mk[0], w_mv[0]], axis=1).astype(BF16)
    g_mix_r, gn_r, g_mem_r, g_memkv_r, g_mlp_r, g_final_r = (
        row(v) for v in (g_mix[0], gn_ret[0], g_mem[0], g_memkv[0], g_mlp[0], g_final))

    m_p = batch * seq
    xp = x_prompt.reshape(m_p, d)
    tm_p = _pick_tile(seq, 512)
    tm_proj = _pick_tile(seq, 1024)
    proj_p, k_rows_p, v_rows_p = _norm_proj_rope(xp, g_mix_r, w_proj_t, _rope_tables(0, seq, seq), tm_proj,
                                                 seq // tm_proj)
    ret_y_p, ret_state_p = _ret_prompt(proj_p, gn_r, batch, seq, _pick_tile(seq, 256))
    dsa_o_p = _dsa_prompt(proj_p, batch, seq, _pick_tile(seq, 256))
    kv_p = _norm_proj(mem_prompt.reshape(batch * mem_len, d), g_memkv_r, w_mkv_b, mem_len, PROJ_TN)
    tm_t = _pick_tile(seq, 256)
    h_p = _merge(xp, ret_y_p, dsa_o_p, proj_p, w_ret_out_b, w_dsa_out_b, w_o_b, tm_t)
    h_p = _mem_attn(h_p, g_mem_r, w_mq_b, kv_p, kv_p, w_mo_b, tm=tm_t, batch_tiles=seq // tm_t,
                    mk_col=0, mv_col=1, nk=mem_len, rows_per_batch=None)

    rows = SUBLANES
    m_s = n_dec * rows
    xs = jnp.pad(x_sample, ((0, 0), (0, rows - t_dec), (0, 0))).reshape(m_s, d)
    proj_s, k_rows_s, v_rows_s = _norm_proj_rope(xs, g_mix_r, w_proj_t, _rope_tables(past, m_s, rows), m_s, 1)
    ret_y_s, ret_state_s = _ret_sample(proj_s, gn_r, state_ret[0], n_dec, rows, t_dec)

    proj_s3 = proj_s.reshape(n_dec, rows, N_PROJ)
    iq_s = proj_s3[:, :, COL_IQ:COL_IQ + IDX_HEADS * IDX_DH].reshape(n_dec, rows * IDX_HEADS, IDX_DH)
    iw_s = proj_s3[:, :, COL_IK + IDX_DH:COL_IK + IDX_DH + IDX_HEADS].reshape(n_dec, rows * IDX_HEADS, 1)
    iw_s = jnp.broadcast_to(iw_s, (n_dec, rows * IDX_HEADS, LANES))
    kvw = DSA_KV_HEADS * HEAD_DIM
    page_rows = DSA_KV_HEADS * PAGE_SIZE
    pad_keys = lambda a: jnp.pad(a, ((0, 0), (0, PAGE_SIZE - rows), (0, 0)))
    ik_new_t = jnp.swapaxes(pad_keys(proj_s3[:, :, COL_IK:COL_IK + IDX_DH]), 1, 2)
    pool_ik_t = jnp.swapaxes(cache_idx_k, 2, 3).reshape(-1, IDX_DH, PAGE_SIZE)
    new_page = lambda a: jnp.pad(a.reshape(n_dec, DSA_KV_HEADS * rows, HEAD_DIM),
                                 ((0, 0), (0, page_rows - DSA_KV_HEADS * rows), (0, 0)))
    k_new, v_new = new_page(k_rows_s), new_page(v_rows_s)
    scores_past, scores_new = _idx_sample(page_table, iq_s, iw_s, ik_new_t, pool_ik_t, _pick_tile(n_pages, 32))
    group = DSA_HEADS // DSA_KV_HEADS
    dq_s = proj_s3[:, :, COL_DQ:COL_DQ + DSA_HEADS * HEAD_DIM].reshape(n_dec, rows, DSA_KV_HEADS, group, HEAD_DIM)
    dq_s = dq_s.transpose(0, 2, 3, 1, 4).reshape(n_dec, DSA_KV_HEADS, group * rows, HEAD_DIM)
    bias_past, bias_new = _sel_sample(scores_past, scores_new, min(TOPK_MAX, (past + t_dec) // 4))
    dsa_o_s = _dsa_sample(page_table, bias_past, bias_new, dq_s, k_new, v_new,
                          cache_k.reshape(-1, page_rows, HEAD_DIM), cache_v.reshape(-1, page_rows, HEAD_DIM),
                          _pick_tile(n_pages, 32))
    dsa_o_s = dsa_o_s.reshape(n_dec, DSA_KV_HEADS, group, rows, HEAD_DIM).transpose(0, 3, 1, 2, 4)
    dsa_o_s = dsa_o_s.reshape(m_s, DSA_HEADS * HEAD_DIM)

    h_s = _merge(xs, ret_y_s, dsa_o_s, proj_s, w_ret_out_b, w_dsa_out_b, w_o_b, m_s)
    h_s = _mem_attn(h_s, g_mem_r, w_mq_b, cache_mem_k[0].reshape(n_dec * mem_len, MEM_W),
                    cache_mem_v[0].reshape(n_dec * mem_len, MEM_W), w_mo_b, tm=m_s, batch_tiles=0,
                    mk_col=0, mv_col=0, nk=n_dec * mem_len, rows_per_batch=rows)
    y_s, w_up_b, w_down_b = _mlp(h_s, g_mlp_r, w_up[0], w_down[0], g_final_r, m_s, 512)
    y_p = _mlp(h_p, g_mlp_r, w_up_b, w_down_b, g_final_r, _pick_tile(seq, 1024), 512)

    def rows_p(col, width, tail):
        return proj_p[:, col:col + width].reshape((1, batch, seq) + tail)

    def rows_s(col, width, tail):
        return proj_s3[:, :t_dec, col:col + width].reshape((1, n_dec, t_dec) + tail)

    kv_p_rows = lambda a: a.reshape(1, batch, seq, DSA_KV_HEADS, HEAD_DIM)
    kv_s_rows = lambda a: a.reshape(1, n_dec, rows, DSA_KV_HEADS, HEAD_DIM)[:, :, :t_dec]
    return (
        y_p.reshape(batch, seq, d),
        y_s.reshape(n_dec, rows, d)[:, :t_dec],
        ret_state_p[None],
        kv_p_rows(k_rows_p),
        kv_p_rows(v_rows_p),
        rows_p(COL_IK, IDX_DH, (IDX_DH,)),
        kv_p[:, :MEM_W].reshape(1, batch, mem_len, MEM_HEADS, HEAD_DIM),
        kv_p[:, MEM_W:].reshape(1, batch, mem_len, MEM_HEADS, HEAD_DIM),
        ret_state_s[None],
        kv_s_rows(k_rows_s),
        kv_s_rows(v_rows_s),
        rows_s(COL_IK, IDX_DH, (IDX_DH,)),
    )
```

```python
import functools
import math

import jax
import jax.numpy as jnp
import numpy as np
from jax import lax
from jax.experimental import pallas as pl
from jax.experimental.pallas import tpu as pltpu

F32 = jnp.float32
BF16 = jnp.bfloat16

D_MODEL = 2048
RET_HEADS = 8
HEAD_DIM = 128
DSA_HEADS = 8
DSA_KV_HEADS = 2
IDX_HEADS = 16
IDX_DH = 64
TOPK_MAX = 256
PAGE_SIZE = 128
MEM_LEN = 256
MEM_HEADS = 4
MEM_W = MEM_HEADS * HEAD_DIM
D_FF = 4 * D_MODEL
RET_THETA = 10000.0
ROPE_THETA = 500000.0
EPS = 1e-6
LANES = 128
SUBLANES = 8
ROW_ALIGN = 16
VMEM_LIMIT = 56 * 1024 * 1024
NEG = -1e30
BISECT_STEPS = 16

COL_GA = 0
COL_GB = 2048
COL_RQ = 4096
COL_RK = 5120
COL_RV = 6144
COL_RG = 7168
COL_DQ = 8192
COL_IQ = 9216
COL_DK = 10240
COL_DV = 10496
COL_IK = 10752
N_PROJ = 11264
PROJ_TN = 512
PROJ_TN_BF16 = 1024
ROPE_NONE, ROPE_RET, ROPE_RETK, ROPE_DSA, ROPE_IDX, ROPE_IKW = range(6)
COL_ROPE = ((ROPE_NONE,) * 32 + (ROPE_RET,) * 8 + (ROPE_RETK,) * 8 + (ROPE_NONE,) * 16 + (ROPE_DSA,) * 8
            + (ROPE_IDX,) * 8 + (ROPE_DSA,) * 2 + (ROPE_NONE,) * 2 + (ROPE_IKW,) + (ROPE_NONE,) * 3)
TAB_RET_C, TAB_RET_S, TAB_DSA_C, TAB_DSA_S, TAB_IDX_C, TAB_IDX_S = range(6)
TAB_W = 6 * LANES
DSA_ROT = HEAD_DIM // 4
IDX_ROT = IDX_DH // 4


def _cparams(*sem):
    return pltpu.CompilerParams(dimension_semantics=sem, vmem_limit_bytes=VMEM_LIMIT)


def _dot(a, b):
    return jnp.dot(a, b, preferred_element_type=F32)


def _dot_nt(a, b):
    return lax.dot_general(a, b, (((1,), (1,)), ((), ())), preferred_element_type=F32)


def _dot_tn(a, b):
    return lax.dot_general(a, b, (((0,), (0,)), ((), ())), preferred_element_type=F32)


def _rmsnorm_bf16(x, g):
    ms = jnp.mean(x * x, axis=-1, keepdims=True)
    return ((x * lax.rsqrt(ms + EPS)) * g).astype(BF16)


def _sigmoid(x):
    return 1.0 / (1.0 + jnp.exp(-x))


def _tab(tab_ref, which):
    return tab_ref[:, which * LANES:(which + 1) * LANES]


def _rope_cols(o_ref, g, c, s, half, period):
    a = o_ref[:, g * LANES:(g + 1) * LANES]
    if 2 * half == LANES:
        partner = pltpu.roll(a, half, 1)
    else:
        lane = lax.broadcasted_iota(jnp.int32, (1, LANES), 1)
        first = (lane & (period - 1)) < half
        partner = jnp.where(first, pltpu.roll(a, LANES - half, 1), pltpu.roll(a, half, 1))
    o_ref[:, g * LANES:(g + 1) * LANES] = a * c + partner * s


def _norm_proj_kernel(x_ref, g_ref, w_ref, tab_ref, o_ref, u_ref, *, kr_ref=None, vr_ref=None, wq_ref=None):
    j = pl.program_id(1)

    @pl.when(j == 0)
    def _():
        u_ref[...] = _rmsnorm_bf16(x_ref[...], g_ref[...])

    if tab_ref is None:
        o_ref[...] = _dot(u_ref[...], w_ref[...])
        return
    wb = w_ref[...]
    if wb.dtype != BF16:
        wb = wb.astype(BF16)
        wq_ref[...] = wb
    o_ref[...] = _dot_nt(u_ref[...], wb)
    groups = o_ref.shape[1] // LANES

    def rope_group(g, kind):
        if kind == ROPE_RET:
            _rope_cols(o_ref, g, _tab(tab_ref, TAB_RET_C), _tab(tab_ref, TAB_RET_S), LANES // 2, LANES)
        elif kind == ROPE_RETK:
            scale = HEAD_DIM ** -0.5
            _rope_cols(o_ref, g, _tab(tab_ref, TAB_RET_C) * scale, _tab(tab_ref, TAB_RET_S) * scale,
                       LANES // 2, LANES)
        elif kind == ROPE_DSA:
            _rope_cols(o_ref, g, _tab(tab_ref, TAB_DSA_C), _tab(tab_ref, TAB_DSA_S), DSA_ROT // 2, LANES)
        elif kind == ROPE_IDX:
            _rope_cols(o_ref, g, _tab(tab_ref, TAB_IDX_C), _tab(tab_ref, TAB_IDX_S), IDX_ROT // 2, IDX_DH)
        elif kind == ROPE_IKW:
            lane = lax.broadcasted_iota(jnp.int32, (1, LANES), 1)
            is_ik = lane < IDX_DH
            c = jnp.where(is_ik, _tab(tab_ref, TAB_IDX_C), 1.0)
            s = jnp.where(is_ik, _tab(tab_ref, TAB_IDX_S), 0.0)
            _rope_cols(o_ref, g, c, s, IDX_ROT // 2, IDX_DH)

    n_blocks = N_PROJ // (groups * LANES)
    plans = {}
    for b in range(n_blocks):
        plans.setdefault(COL_ROPE[b * groups:(b + 1) * groups], []).append(b)
    kv_block, kv_group = divmod(COL_DK // LANES, groups)
    for plan, blocks in plans.items():
        if all(kind == ROPE_NONE for kind in plan):
            continue
        cond = functools.reduce(jnp.logical_or, [j == b for b in blocks])

        @pl.when(cond)
        def _(plan=plan, blocks=blocks):
            for g, kind in enumerate(plan):
                rope_group(g, kind)
            if kv_block in blocks:
                rows = o_ref.shape[0]
                for kvh in range(DSA_KV_HEADS):
                    dst = pl.ds(kvh, rows, stride=DSA_KV_HEADS)
                    gk, gv = kv_group + kvh, kv_group + DSA_KV_HEADS + kvh
                    kr_ref[dst, :] = o_ref[:, gk * LANES:(gk + 1) * LANES]
                    vr_ref[dst, :] = o_ref[:, gv * LANES:(gv + 1) * LANES]


def _norm_proj(x, g, w, tm, tn):
    m, d = x.shape
    n = w.shape[1]

    def body(x_ref, g_ref, w_ref, o_ref, u_ref):
        _norm_proj_kernel(x_ref, g_ref, w_ref, None, o_ref, u_ref)

    return pl.pallas_call(
        body,
        grid=(m // tm, n // tn),
        in_specs=[
            pl.BlockSpec((tm, d), lambda i, j: (i, 0)),
            pl.BlockSpec((1, d), lambda i, j: (0, 0)),
            pl.BlockSpec((d, tn), lambda i, j: (0, j)),
        ],
        out_specs=pl.BlockSpec((tm, tn), lambda i, j: (i, j)),
        out_shape=jax.ShapeDtypeStruct((m, n), F32),
        scratch_shapes=[pltpu.VMEM((tm, d), BF16)],
        compiler_params=_cparams("parallel", "arbitrary"),
        name="norm_proj",
    )(x, g, w)


_W_IN_ROW = {"rq": 0, "rk": 1024, "rv": 2048, "rg": 3072, "dq": 4096, "dk": 5120, "iq": 5632, "ik": 6656,
             "ga": 6736, "gb": 8784}
PROJ_SRC_ROWS = tuple(_W_IN_ROW[name] + PROJ_TN * k for name, nblk in (
    ("ga", 4), ("gb", 4), ("rq", 2), ("rk", 2), ("rv", 2), ("rg", 2), ("dq", 2), ("iq", 2), ("dk", 1), ("ik", 1))
    for k in range(nblk))


def _norm_proj_rope(x, g, w_t, tab, tm, pos_blocks):
    m, d = x.shape
    from_f32 = w_t.dtype != BF16
    tn = PROJ_TN if from_f32 else PROJ_TN_BF16
    kv_spec = pl.BlockSpec((DSA_KV_HEADS * tm, HEAD_DIM), lambda i, j, *_: (i, 0))
    kv_rows = jax.ShapeDtypeStruct((DSA_KV_HEADS * m, HEAD_DIM), F32)
    in_specs = [
        pl.BlockSpec((tm, d), lambda i, j, *_: (i, 0)),
        pl.BlockSpec((1, d), lambda i, j, *_: (0, 0)),
        None,
        pl.BlockSpec((tm, TAB_W), lambda i, j, *_: (i % pos_blocks, 0)),
    ]
    out_specs = [pl.BlockSpec((tm, tn), lambda i, j, *_: (i, j)), kv_spec, kv_spec]
    out_shape = [jax.ShapeDtypeStruct((m, N_PROJ), F32), kv_rows, kv_rows]
    scratch = [pltpu.VMEM((tm, d), BF16)]
    if from_f32:
        assert m == tm and all(r % ROW_ALIGN == 0 for r in PROJ_SRC_ROWS)
        src_rows = jnp.asarray([r // ROW_ALIGN for r in PROJ_SRC_ROWS], jnp.int32)
        in_specs[2] = pl.BlockSpec((pl.Element(tn), pl.Element(d)), lambda i, j, src: (src[j] * ROW_ALIGN, 0))
        out_specs.append(pl.BlockSpec((tn, d), lambda i, j, src: (j, 0)))
        out_shape.append(jax.ShapeDtypeStruct((N_PROJ, d), BF16))

        def body(src_ref, x_ref, g_ref, w_ref, tab_ref, o_ref, kr_ref, vr_ref, wq_ref, u_ref):
            _norm_proj_kernel(x_ref, g_ref, w_ref, tab_ref, o_ref, u_ref, kr_ref=kr_ref, vr_ref=vr_ref, wq_ref=wq_ref)

        grid_spec = pltpu.PrefetchScalarGridSpec(num_scalar_prefetch=1, grid=(1, N_PROJ // tn), in_specs=in_specs,
                                                 out_specs=out_specs, scratch_shapes=scratch)
        args = (src_rows, x, g, w_t, tab)
    else:
        in_specs[2] = pl.BlockSpec((tn, d), lambda i, j: (j, 0))

        def body(x_ref, g_ref, w_ref, tab_ref, o_ref, kr_ref, vr_ref, u_ref):
            _norm_proj_kernel(x_ref, g_ref, w_ref, tab_ref, o_ref, u_ref, kr_ref=kr_ref, vr_ref=vr_ref)

        grid_spec = pltpu.PrefetchScalarGridSpec(num_scalar_prefetch=0, grid=(m // tm, N_PROJ // tn),
                                                 in_specs=in_specs, out_specs=out_specs, scratch_shapes=scratch)
        args = (x, g, w_t, tab)
    return pl.pallas_call(
        body,
        grid_spec=grid_spec,
        out_shape=out_shape,
        compiler_params=_cparams("parallel", "arbitrary"),
        name="norm_proj_rope",
    )(*args)


@functools.lru_cache(maxsize=None)
def _rope_tables(start, count, period):
    pos = (start + np.arange(count) % period).astype(np.float64)[:, None]

    def cs(half, theta):
        inv = theta ** (-np.arange(half, dtype=np.float64) / half)
        ang = pos * inv[None, :]
        return np.cos(ang), np.sin(ang)

    ones = lambda n: np.ones((count, n))
    zeros = lambda n: np.zeros((count, n))
    c, s = cs(HEAD_DIM // 2, RET_THETA)
    ret_c = np.concatenate([c, c], 1)
    ret_s = np.concatenate([-s, s], 1)
    c, s = cs(DSA_ROT // 2, ROPE_THETA)
    dsa_c = np.concatenate([c, c, ones(LANES - DSA_ROT)], 1)
    dsa_s = np.concatenate([-s, s, zeros(LANES - DSA_ROT)], 1)
    c, s = cs(IDX_ROT // 2, ROPE_THETA)
    idx_c = np.tile(np.concatenate([c, c, ones(IDX_DH - IDX_ROT)], 1), (1, 2))
    idx_s = np.tile(np.concatenate([-s, s, zeros(IDX_DH - IDX_ROT)], 1), (1, 2))
    return np.concatenate([ret_c, ret_s, dsa_c, dsa_s, idx_c, idx_s], 1).astype(np.float32)


def _log_decay():
    return jnp.log1p(-jnp.exp2(-5.0 - jnp.arange(RET_HEADS, dtype=F32)))


def _groupnorm_gate(o, gate, gn):
    mu = jnp.mean(o, axis=-1, keepdims=True)
    d = o - mu
    var = jnp.mean(d * d, axis=-1, keepdims=True)
    n = d * lax.rsqrt(var + EPS) * gn
    return (gate * _sigmoid(gate) * n).astype(BF16)


def _ret_prompt_kernel(q_ref, k_ref, v_ref, g_ref, gn_ref, intra_ref, qdec_ref, kdec_ref, cdec_ref,
                       y_ref, st_ref, s_scr):
    c = pl.program_id(1)

    @pl.when(c == 0)
    def _():
        s_scr[...] = jnp.zeros_like(s_scr)

    for h in range(RET_HEADS):
        sl = slice(h * HEAD_DIM, (h + 1) * HEAD_DIM)
        k = k_ref[:, sl]
        qb = q_ref[:, sl].astype(BF16)
        kb = k.astype(BF16)
        vb = v_ref[:, sl].astype(BF16)
        s = _dot_nt(qb, kb) * intra_ref[h]
        state = s_scr[h]
        o = _dot(s.astype(BF16), vb) + _dot(qb, state.astype(BF16)) * qdec_ref[h]
        kd = (k * kdec_ref[h]).astype(BF16)
        s_scr[h] = state * cdec_ref[h] + _dot_tn(kd, vb)
        y_ref[:, sl] = _groupnorm_gate(o, g_ref[:, sl], gn_ref[:, sl])

    @pl.when(c == pl.num_programs(1) - 1)
    def _():
        st_ref[0] = s_scr[...]


def _ret_prompt(proj, gn, batch, seq, chunk):
    nc = seq // chunk
    log_g = _log_decay()
    j = jnp.arange(chunk, dtype=F32)
    diff = j[:, None] - j[None, :]
    intra = jnp.where(diff >= 0, jnp.exp(log_g[:, None, None] * jnp.maximum(diff, 0.0)), 0.0)
    ones = jnp.ones((1, 1, HEAD_DIM), F32)
    qdec = jnp.exp(log_g[:, None] * (j[None, :] + 1.0))[:, :, None] * ones
    kdec = jnp.exp(log_g[:, None] * (chunk - 1.0 - j[None, :]))[:, :, None] * ones
    cdec = jnp.exp(log_g * chunk)[:, None, None] * ones
    w = RET_HEADS * HEAD_DIM
    col = lambda off: (lambda b, c: (b * nc + c, off // w))
    const3 = lambda b, c: (0, 0, 0)
    return pl.pallas_call(
        _ret_prompt_kernel,
        grid=(batch, nc),
        in_specs=[
            pl.BlockSpec((chunk, w), col(COL_RQ)),
            pl.BlockSpec((chunk, w), col(COL_RK)),
            pl.BlockSpec((chunk, w), col(COL_RV)),
            pl.BlockSpec((chunk, w), col(COL_RG)),
            pl.BlockSpec((1, w), lambda b, c: (0, 0)),
            pl.BlockSpec((RET_HEADS, chunk, chunk), const3),
            pl.BlockSpec((RET_HEADS, chunk, HEAD_DIM), const3),
            pl.BlockSpec((RET_HEADS, chunk, HEAD_DIM), const3),
            pl.BlockSpec((RET_HEADS, 1, HEAD_DIM), const3),
        ],
        out_specs=[
            pl.BlockSpec((chunk, w), lambda b, c: (b * nc + c, 0)),
            pl.BlockSpec((1, RET_HEADS, HEAD_DIM, HEAD_DIM), lambda b, c: (b, 0, 0, 0)),
        ],
        out_shape=[
            jax.ShapeDtypeStruct((batch * seq, w), BF16),
            jax.ShapeDtypeStruct((batch, RET_HEADS, HEAD_DIM, HEAD_DIM), F32),
        ],
        scratch_shapes=[pltpu.VMEM((RET_HEADS, HEAD_DIM, HEAD_DIM), F32)],
        compiler_params=_cparams("parallel", "arbitrary"),
        name="ret_prompt",
    )(proj, proj, proj, proj, gn, intra, qdec, kdec, cdec)


def _ret_sample_kernel(q_ref, k_ref, v_ref, g_ref, gn_ref, st_ref, intra_ref, qdec_ref, kdec_ref, cdec_ref,
                       y_ref, so_ref, *, n_batch, rows):
    k = k_ref[...]
    qb = q_ref[...].astype(BF16)
    kb = k.astype(BF16)
    vb = v_ref[...].astype(BF16)
    s = _dot_nt(qb, kb) * intra_ref[0]
    kd = k * kdec_ref[0]
    row_batch = lax.broadcasted_iota(jnp.int32, kd.shape, 0) // rows
    inter = []
    for b in range(n_batch):
        state = st_ref[b, 0]
        inter.append(_dot(qb[b * rows:(b + 1) * rows], state.astype(BF16)))
        kd_b = jnp.where(row_batch == b, kd, 0.0).astype(BF16)
        so_ref[b, 0] = state * cdec_ref[0] + _dot_tn(kd_b, vb)
    o = _dot(s.astype(BF16), vb) + jnp.concatenate(inter, axis=0) * qdec_ref[0]
    y_ref[...] = _groupnorm_gate(o, g_ref[...], gn_ref[...])


def _ret_sample(proj, gn, state, n_batch, rows, t_valid):
    m = n_batch * rows
    log_g = _log_decay()
    r = jnp.arange(m)
    t = (r % rows).astype(F32)
    same = (r[:, None] // rows) == (r[None, :] // rows)
    diff = t[:, None] - t[None, :]
    intra = jnp.where(same[None] & (diff >= 0)[None], jnp.exp(log_g[:, None, None] * jnp.maximum(diff, 0.0)[None]), 0.0)
    ones = jnp.ones((1, 1, HEAD_DIM), F32)
    qdec = jnp.exp(log_g[:, None] * (t[None, :] + 1.0))[:, :, None] * ones
    kdec = jnp.where(t[None, :] < t_valid, jnp.exp(log_g[:, None] * (t_valid - 1.0 - t[None, :])), 0.0)[:, :, None] * ones
    cdec = jnp.exp(log_g * t_valid)[:, None, None] * ones
    col = lambda off: (lambda h: (0, off // HEAD_DIM + h))
    per_head = lambda h: (h, 0, 0)
    return pl.pallas_call(
        functools.partial(_ret_sample_kernel, n_batch=n_batch, rows=rows),
        grid=(RET_HEADS,),
        in_specs=[
            pl.BlockSpec((m, HEAD_DIM), col(COL_RQ)),
            pl.BlockSpec((m, HEAD_DIM), col(COL_RK)),
            pl.BlockSpec((m, HEAD_DIM), col(COL_RV)),
            pl.BlockSpec((m, HEAD_DIM), col(COL_RG)),
            pl.BlockSpec((1, HEAD_DIM), lambda h: (0, h)),
            pl.BlockSpec((n_batch, 1, HEAD_DIM, HEAD_DIM), lambda h: (0, h, 0, 0)),
            pl.BlockSpec((1, m, m), per_head),
            pl.BlockSpec((1, m, HEAD_DIM), per_head),
            pl.BlockSpec((1, m, HEAD_DIM), per_head),
            pl.BlockSpec((1, 1, HEAD_DIM), per_head),
        ],
        out_specs=[
            pl.BlockSpec((m, HEAD_DIM), lambda h: (0, h)),
            pl.BlockSpec((n_batch, 1, HEAD_DIM, HEAD_DIM), lambda h: (0, h, 0, 0)),
        ],
        out_shape=[
            jax.ShapeDtypeStruct((m, RET_HEADS * HEAD_DIM), BF16),
            jax.ShapeDtypeStruct(state.shape, F32),
        ],
        compiler_params=_cparams("parallel"),
        name="ret_sample",
    )(proj, proj, proj, proj, gn, state, intra, qdec, kdec, cdec)


def _kth_largest(count_ge, next_above, smin, smax, n_adm, topk):
    few = n_adm < topk
    lo = jnp.where(few, -jnp.inf, smin)
    c_lo = jnp.where(few, float(topk), n_adm)
    hi = jnp.where(few, -jnp.inf, smax + jnp.maximum(jnp.abs(smax) * 1e-6, 1e-30))

    def bisect(_, state):
        lo, hi, c_lo = state
        mid = 0.5 * lo + 0.5 * hi
        c = count_ge(mid)
        take = c >= topk
        return jnp.where(take, mid, lo), jnp.where(take, hi, mid), jnp.where(take, c, c_lo)

    lo, hi, c_lo = lax.fori_loop(0, BISECT_STEPS, bisect, (lo, hi, c_lo))
    c_lo = jnp.where(few, float(topk), c_lo)

    def unfinished(state):
        return jnp.max(state[2]) > 0.0

    def step_up(state):
        lo, c_lo, active = state
        nxt, n_eq = next_above(lo)
        c_nxt = c_lo - n_eq
        move = active * jnp.where(c_nxt >= topk, 1.0, 0.0)
        lo = jnp.where(move > 0.0, nxt, lo)
        c_lo = jnp.where(move > 0.0, c_nxt, c_lo)
        return lo, c_lo, move * jnp.where(c_nxt > topk, 1.0, 0.0)

    return lax.while_loop(unfinished, step_up, (lo, c_lo, jnp.where(c_lo > topk, 1.0, 0.0)))[0]


def _dsa_prompt_kernel(dq_ref, iq_ref, iwq_ref, kall_ref, vall_ref, ikall_ref, o_ref,
                       kbf, vtb, ika, ikb, keys, z_a, z_b, zmax_a, zmax_b, m_scr, acc_scr, *, tq, topk):
    z_scr, zmax_scr = (z_a, z_b), (zmax_a, zmax_b)
    i = pl.program_id(1)
    n_chunks = kbf.shape[0]
    scale_log2e = HEAD_DIM ** -0.5 * math.log2(math.e)

    @pl.when(i == 0)
    def _():
        lane = lax.broadcasted_iota(jnp.int32, (tq, LANES), 1)
        for c in range(n_chunks):
            rows = slice(c * tq, (c + 1) * tq)
            kbf[c] = kall_ref[rows, :].astype(BF16)
            for kvh in range(DSA_KV_HEADS):
                sl = slice(kvh * HEAD_DIM, (kvh + 1) * HEAD_DIM)
                vtb[c, kvh, :HEAD_DIM, :] = vall_ref[rows, sl].T.astype(BF16)
                vtb[c, kvh, HEAD_DIM:, :] = jnp.ones((ROW_ALIGN, tq), BF16)
            a = ikall_ref[rows, :]
            ika[c] = jnp.where(lane < IDX_DH, a, 0.0).astype(BF16)
            ikb[c] = jnp.where(lane >= IDX_DH, pltpu.roll(a, IDX_DH, 1), 0.0).astype(BF16)

    w_t = iwq_ref[...].T
    iqb = iq_ref[...].astype(BF16)
    qb = (dq_ref[...] * scale_log2e).astype(BF16)
    t_col = i * tq + lax.broadcasted_iota(jnp.int32, (tq, tq), 1)
    s_row0 = lax.broadcasted_iota(jnp.int32, (tq, tq), 0)

    fold = lambda x: x.reshape(tq // SUBLANES, SUBLANES, tq)

    def score_chunk(c, carry):
        smax, smin = carry
        a = ika[c]
        b = ikb[c]
        acc = jnp.zeros((tq, tq), F32)
        for p in range(IDX_HEADS // 2):
            pair = iqb[:, p * LANES:(p + 1) * LANES]
            w0 = w_t[IDX_DH + 2 * p:IDX_DH + 2 * p + 1, :]
            w1 = w_t[IDX_DH + 2 * p + 1:IDX_DH + 2 * p + 2, :]
            acc = acc + jnp.maximum(_dot_nt(a, pair), 0.0) * w0
            acc = acc + jnp.maximum(_dot_nt(b, pair), 0.0) * w1
        admissible = s_row0 + c * tq <= t_col
        keys[c] = jnp.where(admissible, acc, -jnp.inf)
        smax = jnp.maximum(smax, fold(jnp.where(admissible, acc, -jnp.inf)).max(axis=0))
        smin = jnp.minimum(smin, fold(jnp.where(admissible, acc, jnp.inf)).min(axis=0))
        return smax, smin

    smax, smin = lax.fori_loop(0, i + 1, score_chunk, (jnp.full((SUBLANES, tq), -jnp.inf, F32),
                                                      jnp.full((SUBLANES, tq), jnp.inf, F32)))
    smax = smax.max(axis=0, keepdims=True)
    smin = smin.min(axis=0, keepdims=True)
    keys[i + 1] = jnp.full((tq, tq), -jnp.inf, F32)
    n_pairs = (i + 2) // 2

    def count_ge(t):
        def pair(j, cnt):
            for c in (2 * j, 2 * j + 1):
                cnt = cnt + fold(jnp.where(keys[c] >= t, 1.0, 0.0)).sum(axis=0)
            return cnt

        return lax.fori_loop(0, n_pairs, pair, jnp.zeros((SUBLANES, tq), F32)).sum(axis=0, keepdims=True)

    def next_above(lo):
        def pair(j, carry):
            nxt, n_eq = carry
            for c in (2 * j, 2 * j + 1):
                k = keys[c]
                nxt = jnp.minimum(nxt, fold(jnp.where(k > lo, k, jnp.inf)).min(axis=0))
                n_eq = n_eq + fold(jnp.where(k == lo, 1.0, 0.0)).sum(axis=0)
            return nxt, n_eq

        nxt, n_eq = lax.fori_loop(0, n_pairs, pair, (jnp.full((SUBLANES, tq), jnp.inf, F32),
                                                     jnp.zeros((SUBLANES, tq), F32)))
        return nxt.min(axis=0, keepdims=True), n_eq.sum(axis=0, keepdims=True)

    thr = _kth_largest(count_ge, next_above, smin, smax, (t_col[:1, :] + 1).astype(F32), topk)

    m_scr[...] = jnp.full_like(m_scr, NEG)
    acc_scr[...] = jnp.zeros_like(acc_scr)

    kv_of = lambda h: h // (DSA_HEADS // DSA_KV_HEADS)

    def logits(c, slot):
        bias = jnp.where(s_row0 + c * tq <= t_col, jnp.where(keys[c] >= thr, 0.0, NEG), NEG)
        kc = kbf[c]
        for h in range(DSA_HEADS):
            ksl = slice(kv_of(h) * HEAD_DIM, (kv_of(h) + 1) * HEAD_DIM)
            z = _dot_nt(kc[:, ksl], qb[:, h * HEAD_DIM:(h + 1) * HEAD_DIM]) + bias
            z_scr[slot][h] = z
            zmax_scr[slot][h] = z.max(axis=0, keepdims=True)

    def update(c, slot):
        for h in range(DSA_HEADS):
            m_old = m_scr[h]
            m_new = jnp.maximum(m_old, zmax_scr[slot][h])
            p = jnp.exp2(z_scr[slot][h] - m_new)
            acc_scr[h] = acc_scr[h] * jnp.exp2(m_old - m_new) + _dot(vtb[c, kv_of(h)], p.astype(BF16))
            m_scr[h] = m_new

    n = i + 1
    logits(0, 0)

    def two_chunks(k, carry):
        update(2 * k, 0)
        logits(2 * k + 1, 1)
        update(2 * k + 1, 1)
        logits(2 * k + 2, 0)
        return carry

    lax.fori_loop(0, (n - 1) // 2, two_chunks, 0)

    @pl.when(n % 2 == 1)
    def _():
        update(n - 1, 0)

    @pl.when(n % 2 == 0)
    def _():
        update(n - 2, 0)
        logits(n - 1, 1)
        update(n - 1, 1)

    for h in range(DSA_HEADS):
        o = acc_scr[h, :HEAD_DIM, :] * (1.0 / acc_scr[h, HEAD_DIM:HEAD_DIM + 1, :])
        o_ref[:, h * HEAD_DIM:(h + 1) * HEAD_DIM] = o.T.astype(BF16)


def _dsa_prompt(proj, batch, seq, tq):
    nq = seq // tq
    topk = min(TOPK_MAX, seq // 4)
    qw = DSA_HEADS * HEAD_DIM
    kvw = DSA_KV_HEADS * HEAD_DIM
    return pl.pallas_call(
        functools.partial(_dsa_prompt_kernel, tq=tq, topk=topk),
        grid=(batch, nq),
        in_specs=[
            pl.BlockSpec((tq, qw), lambda b, i: (b * nq + i, COL_DQ // qw)),
            pl.BlockSpec((tq, qw), lambda b, i: (b * nq + i, COL_IQ // qw)),
            pl.BlockSpec((tq, LANES), lambda b, i: (b * nq + i, COL_IK // LANES)),
            pl.BlockSpec((seq, kvw), lambda b, i: (b, COL_DK // kvw)),
            pl.BlockSpec((seq, kvw), lambda b, i: (b, COL_DV // kvw)),
            pl.BlockSpec((seq, LANES), lambda b, i: (b, COL_IK // LANES)),
        ],
        out_specs=pl.BlockSpec((tq, qw), lambda b, i: (b * nq + i, 0)),
        out_shape=jax.ShapeDtypeStruct((batch * seq, qw), BF16),
        scratch_shapes=[
            pltpu.VMEM((nq, tq, kvw), BF16),
            pltpu.VMEM((nq, DSA_KV_HEADS, HEAD_DIM + ROW_ALIGN, tq), BF16),
            pltpu.VMEM((nq, tq, LANES), BF16),
            pltpu.VMEM((nq, tq, LANES), BF16),
            pltpu.VMEM((nq + 1, tq, tq), F32),
            pltpu.VMEM((DSA_HEADS, tq, tq), F32),
            pltpu.VMEM((DSA_HEADS, tq, tq), F32),
            pltpu.VMEM((DSA_HEADS, 1, tq), F32),
            pltpu.VMEM((DSA_HEADS, 1, tq), F32),
            pltpu.VMEM((DSA_HEADS, 1, tq), F32),
            pltpu.VMEM((DSA_HEADS, HEAD_DIM + ROW_ALIGN, tq), F32),
        ],
        compiler_params=_cparams("parallel", "arbitrary"),
        name="dsa_prompt",
    )(proj, proj, proj, proj, proj, proj)


def _idx_sample_kernel(pt_ref, iq_ref, iw_ref, iknew_ref, *rest, pages):
    page_refs = rest[:pages]
    sp_ref, sn_ref = rest[pages:]
    iqb = iq_ref[0].astype(BF16)
    w = iw_ref[0]
    rows = iqb.shape[0] // IDX_HEADS

    def scores(ik_t):
        d = jnp.maximum(_dot(iqb, ik_t.astype(BF16)), 0.0) * w
        return d.reshape(rows, IDX_HEADS, LANES).sum(axis=1)

    for p in range(pages):
        sp_ref[0, :, p * PAGE_SIZE:(p + 1) * PAGE_SIZE] = scores(page_refs[p][0])
    t = lax.broadcasted_iota(jnp.int32, (rows, LANES), 0)
    j = lax.broadcasted_iota(jnp.int32, (rows, LANES), 1)
    sn_ref[0] = jnp.where(j <= t, scores(iknew_ref[0]), -jnp.inf)


def _idx_sample(page_table, iq, iw, ik_new, pool_ik, pages):
    n_batch, n_pages = page_table.shape
    rows16 = iq.shape[1]
    rows = rows16 // IDX_HEADS
    past = n_pages * PAGE_SIZE
    page_specs = [pl.BlockSpec((1, IDX_DH, PAGE_SIZE), functools.partial(
        lambda b, g, pt, k: (pt[b, g * pages + k], 0, 0), k=k)) for k in range(pages)]
    grid_spec = pltpu.PrefetchScalarGridSpec(
        num_scalar_prefetch=1,
        grid=(n_batch, n_pages // pages),
        in_specs=[
            pl.BlockSpec((1, rows16, IDX_DH), lambda b, g, pt: (b, 0, 0)),
            pl.BlockSpec((1, rows16, LANES), lambda b, g, pt: (b, 0, 0)),
            pl.BlockSpec((1, IDX_DH, PAGE_SIZE), lambda b, g, pt: (b, 0, 0)),
        ] + page_specs,
        out_specs=[
            pl.BlockSpec((1, rows, pages * PAGE_SIZE), lambda b, g, pt: (b, 0, g)),
            pl.BlockSpec((1, rows, LANES), lambda b, g, pt: (b, 0, 0)),
        ],
    )
    return pl.pallas_call(
        functools.partial(_idx_sample_kernel, pages=pages),
        grid_spec=grid_spec,
        out_shape=[
            jax.ShapeDtypeStruct((n_batch, rows, past), F32),
            jax.ShapeDtypeStruct((n_batch, rows, LANES), F32),
        ],
        compiler_params=_cparams("parallel", "arbitrary"),
        name="idx_sample",
    )(page_table, iq, iw, ik_new, *([pool_ik] * pages))


def _sel_sample_kernel(sp_ref, sn_ref, bp_ref, bn_ref, *, topk):
    n_batch, rows, past = sp_ref.shape
    sp = sp_ref[...].reshape(n_batch * rows, past)
    sn = sn_ref[...].reshape(n_batch * rows, LANES)
    t = lax.broadcasted_iota(jnp.int32, (n_batch * rows, LANES), 0) % rows
    j = lax.broadcasted_iota(jnp.int32, (n_batch * rows, LANES), 1)
    admissible_new = j <= t
    row_sum = lambda a: a.sum(axis=1, keepdims=True)
    row_min = lambda a: a.min(axis=1, keepdims=True)

    def count_ge(thr):
        return row_sum(jnp.where(sp >= thr, 1.0, 0.0)) + row_sum(jnp.where(sn >= thr, 1.0, 0.0))

    def next_above(lo):
        nxt = jnp.minimum(row_min(jnp.where(sp > lo, sp, jnp.inf)), row_min(jnp.where(sn > lo, sn, jnp.inf)))
        return nxt, row_sum(jnp.where(sp == lo, 1.0, 0.0)) + row_sum(jnp.where(sn == lo, 1.0, 0.0))

    smin = jnp.minimum(row_min(sp), row_min(jnp.where(admissible_new, sn, jnp.inf)))
    smax = jnp.maximum(sp.max(axis=1, keepdims=True), sn.max(axis=1, keepdims=True))
    n_adm = float(past) + row_sum(jnp.where(admissible_new, 1.0, 0.0))
    thr = _kth_largest(count_ge, next_above, smin, smax, n_adm, topk)
    bp_ref[...] = jnp.where(sp >= thr, 0.0, NEG).reshape(n_batch, rows, past)
    bn_ref[...] = jnp.where(admissible_new, jnp.where(sn >= thr, 0.0, NEG), NEG).reshape(n_batch, rows, LANES)


def _sel_sample(scores_past, scores_new, topk):
    full = lambda a: pl.BlockSpec(a.shape, lambda: (0,) * a.ndim)
    return pl.pallas_call(
        functools.partial(_sel_sample_kernel, topk=topk),
        in_specs=[full(scores_past), full(scores_new)],
        out_specs=[full(scores_past), full(scores_new)],
        out_shape=[jax.ShapeDtypeStruct(scores_past.shape, F32), jax.ShapeDtypeStruct(scores_new.shape, F32)],
        compiler_params=pltpu.CompilerParams(vmem_limit_bytes=VMEM_LIMIT),
        name="sel_sample",
    )(scores_past, scores_new)


def _dsa_sample_kernel(pt_ref, bp_ref, bn_ref, q_ref, knew_ref, vnew_ref, *rest, pages):
    k_refs = rest[:pages]
    v_refs = rest[pages:2 * pages]
    o_ref, m_scr, l_scr, acc_scr = rest[2 * pages:]
    g = pl.program_id(1)
    scale = HEAD_DIM ** -0.5
    group = DSA_HEADS // DSA_KV_HEADS

    @pl.when(g == 0)
    def _():
        m_scr[...] = jnp.full_like(m_scr, NEG)
        l_scr[...] = jnp.zeros_like(l_scr)
        acc_scr[...] = jnp.zeros_like(acc_scr)

    def head_rows(ref, kvh):
        return ref[0, pl.ds(kvh, PAGE_SIZE, stride=DSA_KV_HEADS), :].astype(BF16)

    def attend(k_pages, v_pages, bias):
        bias_g = jnp.concatenate([bias] * group, axis=0)
        kv_heads = range(DSA_KV_HEADS)
        qb = [q_ref[0, kvh].astype(BF16) for kvh in kv_heads]
        raw = [jnp.concatenate([_dot_nt(qb[kvh], head_rows(k, kvh)) for k in k_pages], axis=1) for kvh in kv_heads]
        pb, alpha = [], []
        for kvh in kv_heads:
            z = jnp.where(bias_g < 0.0, NEG, raw[kvh])
            m_old = m_scr[kvh]
            m_new = jnp.maximum(m_old, z.max(axis=1, keepdims=True))
            alpha.append(jnp.exp((m_old - m_new) * scale))
            p = jnp.exp((z - m_new) * scale)
            l_scr[kvh] = l_scr[kvh] * alpha[kvh] + p.sum(axis=1, keepdims=True)
            m_scr[kvh] = m_new
            pb.append(p.astype(BF16))
        for kvh in kv_heads:
            pv = _dot(pb[kvh][:, :PAGE_SIZE], head_rows(v_pages[0], kvh))
            for n in range(1, len(v_pages)):
                pv = pv + _dot(pb[kvh][:, n * PAGE_SIZE:(n + 1) * PAGE_SIZE], head_rows(v_pages[n], kvh))
            acc_scr[kvh] = acc_scr[kvh] * alpha[kvh] + pv

    attend(k_refs, v_refs, bp_ref[0])

    @pl.when(g == pl.num_programs(1) - 1)
    def _():
        attend([knew_ref], [vnew_ref], bn_ref[0])
        for kvh in range(DSA_KV_HEADS):
            o_ref[0, kvh] = (acc_scr[kvh] * (1.0 / l_scr[kvh])).astype(BF16)


def _dsa_sample(page_table, bias_past, bias_new, q, k_new, v_new, pool_k, pool_v, pages):
    n_batch, n_pages = page_table.shape
    rows = bias_past.shape[1]
    page_rows = DSA_KV_HEADS * PAGE_SIZE
    grows = q.shape[2]
    page_spec = lambda k: pl.BlockSpec((1, page_rows, HEAD_DIM), functools.partial(
        lambda b, g, pt, k: (pt[b, g * pages + k], 0, 0), k=k))
    per_batch3 = lambda b, g, pt: (b, 0, 0)
    grid_spec = pltpu.PrefetchScalarGridSpec(
        num_scalar_prefetch=1,
        grid=(n_batch, n_pages // pages),
        in_specs=[
            pl.BlockSpec((1, rows, pages * PAGE_SIZE), lambda b, g, pt: (b, 0, g)),
            pl.BlockSpec((1, rows, LANES), per_batch3),
            pl.BlockSpec((1, DSA_KV_HEADS, grows, HEAD_DIM), lambda b, g, pt: (b, 0, 0, 0)),
            pl.BlockSpec((1, page_rows, HEAD_DIM), per_batch3),
            pl.BlockSpec((1, page_rows, HEAD_DIM), per_batch3),
        ] + [page_spec(k) for k in range(pages)] * 2,
        out_specs=pl.BlockSpec((1, DSA_KV_HEADS, grows, HEAD_DIM), lambda b, g, pt: (b, 0, 0, 0)),
        scratch_shapes=[
            pltpu.VMEM((DSA_KV_HEADS, grows, 1), F32),
            pltpu.VMEM((DSA_KV_HEADS, grows, 1), F32),
            pltpu.VMEM((DSA_KV_HEADS, grows, HEAD_DIM), F32),
        ],
    )
    return pl.pallas_call(
        functools.partial(_dsa_sample_kernel, pages=pages),
        grid_spec=grid_spec,
        out_shape=jax.ShapeDtypeStruct((n_batch, DSA_KV_HEADS, grows, HEAD_DIM), BF16),
        compiler_params=_cparams("parallel", "arbitrary"),
        name="dsa_sample",
    )(page_table, bias_past, bias_new, q, k_new, v_new, *([pool_k] * pages), *([pool_v] * pages))


def _merge_kernel(x_ref, ry_ref, do_ref, ga_ref, gb_ref, wr_ref, wd_ref, wo_ref, o_ref):
    ya = _dot(ry_ref[...], wr_ref[...])
    yb = _dot(do_ref[...], wd_ref[...])
    merged = _sigmoid(ga_ref[...]) * ya + _sigmoid(gb_ref[...]) * yb
    o_ref[...] = x_ref[...] + _dot(merged.astype(BF16), wo_ref[...])


def _merge(x, ret_y, dsa_o, proj, w_ret_out, w_dsa_out, w_o, tm):
    m, d = x.shape
    w = ret_y.shape[1]
    const = lambda i: (0, 0)
    return pl.pallas_call(
        _merge_kernel,
        grid=(m // tm,),
        in_specs=[
            pl.BlockSpec((tm, d), lambda i: (i, 0)),
            pl.BlockSpec((tm, w), lambda i: (i, 0)),
            pl.BlockSpec((tm, w), lambda i: (i, 0)),
            pl.BlockSpec((tm, d), lambda i: (i, COL_GA // d)),
            pl.BlockSpec((tm, d), lambda i: (i, COL_GB // d)),
            pl.BlockSpec((w, d), const),
            pl.BlockSpec((w, d), const),
            pl.BlockSpec((d, d), const),
        ],
        out_specs=pl.BlockSpec((tm, d), lambda i: (i, 0)),
        out_shape=jax.ShapeDtypeStruct((m, d), F32),
        compiler_params=_cparams("parallel"),
        name="merge_out",
    )(x, ret_y, dsa_o, proj, proj, w_ret_out, w_dsa_out, w_o)


def _mem_attn_kernel(h_ref, g_ref, wq_ref, mk_ref, mv_ref, wo_ref, o_ref, *, rows_per_batch):
    h = h_ref[...]
    qm = _dot(_rmsnorm_bf16(h, g_ref[...]), wq_ref[...])
    tm = h.shape[0]
    nk = mk_ref.shape[0]
    scale = HEAD_DIM ** -0.5
    if rows_per_batch is not None:
        rb = lax.broadcasted_iota(jnp.int32, (tm, nk), 0) // rows_per_batch
        kb = lax.broadcasted_iota(jnp.int32, (tm, nk), 1) // MEM_LEN
        same = rb == kb
    outs = []
    for hd in range(MEM_HEADS):
        sl = slice(hd * HEAD_DIM, (hd + 1) * HEAD_DIM)
        z = _dot_nt(qm[:, sl].astype(BF16), mk_ref[:, sl].astype(BF16)) * scale
        if rows_per_batch is not None:
            z = jnp.where(same, z, NEG)
        p = jnp.exp(z - z.max(axis=-1, keepdims=True))
        p = p / p.sum(axis=-1, keepdims=True)
        outs.append(_dot(p.astype(BF16), mv_ref[:, sl].astype(BF16)))
    om = jnp.concatenate(outs, axis=1).astype(BF16)
    o_ref[...] = h + _dot(om, wo_ref[...])


def _mem_attn(h, g, w_mq, mk, mv, w_mo, *, tm, batch_tiles, mk_col, mv_col, nk, rows_per_batch):
    m, d = h.shape
    const = lambda i: (0, 0)
    if batch_tiles:
        kmap = lambda col: (lambda i: (i // batch_tiles, col))
    else:
        kmap = lambda col: (lambda i: (0, col))
    return pl.pallas_call(
        functools.partial(_mem_attn_kernel, rows_per_batch=rows_per_batch),
        grid=(m // tm,),
        in_specs=[
            pl.BlockSpec((tm, d), lambda i: (i, 0)),
            pl.BlockSpec((1, d), const),
            pl.BlockSpec((d, MEM_W), const),
            pl.BlockSpec((nk, MEM_W), kmap(mk_col)),
            pl.BlockSpec((nk, MEM_W), kmap(mv_col)),
            pl.BlockSpec((MEM_W, d), const),
        ],
        out_specs=pl.BlockSpec((tm, d), lambda i: (i, 0)),
        out_shape=jax.ShapeDtypeStruct((m, d), F32),
        compiler_params=_cparams("parallel"),
        name="mem_attn",
    )(h, g, w_mq, mk, mv, w_mo)


def _mlp_kernel(h_ref, g_ref, wu_ref, wd_ref, gf_ref, o_ref, *rest):
    u_ref = rest[-1]
    j = pl.program_id(1)

    @pl.when(j == 0)
    def _():
        u_ref[...] = _rmsnorm_bf16(h_ref[...], g_ref[...])
        o_ref[...] = h_ref[...]

    wu, wd = wu_ref[...], wd_ref[...]
    if len(rest) == 3:
        wu, wd = wu.astype(BF16), wd.astype(BF16)
        rest[0][...] = wu
        rest[1][...] = wd
    a = jnp.maximum(_dot(u_ref[...], wu), 0.0)
    o_ref[...] += _dot((a * a).astype(BF16), wd)

    @pl.when(j == pl.num_programs(1) - 1)
    def _():
        y = o_ref[...]
        ms = jnp.mean(y * y, axis=-1, keepdims=True)
        o_ref[...] = (y * lax.rsqrt(ms + EPS)) * gf_ref[...]


def _mlp(h, g, w_up, w_down, g_final, tm, tf):
    m, d = h.shape
    ff = w_up.shape[1]
    emit = w_up.dtype != BF16
    assert not emit or m == tm, "weight copies are written once, by a call with one row tile"
    wu_spec = pl.BlockSpec((d, tf), lambda i, j: (0, j))
    wd_spec = pl.BlockSpec((tf, d), lambda i, j: (j, 0))
    out_specs = [pl.BlockSpec((tm, d), lambda i, j: (i, 0))]
    out_shape = [jax.ShapeDtypeStruct((m, d), F32)]
    if emit:
        out_specs += [wu_spec, wd_spec]
        out_shape += [jax.ShapeDtypeStruct(w_up.shape, BF16), jax.ShapeDtypeStruct(w_down.shape, BF16)]
    out = pl.pallas_call(
        _mlp_kernel,
        grid=(m // tm, ff // tf),
        in_specs=[
            pl.BlockSpec((tm, d), lambda i, j: (i, 0)),
            pl.BlockSpec((1, d), lambda i, j: (0, 0)),
            wu_spec,
            wd_spec,
            pl.BlockSpec((1, d), lambda i, j: (0, 0)),
        ],
        out_specs=out_specs,
        out_shape=out_shape,
        scratch_shapes=[pltpu.VMEM((tm, d), BF16)],
        compiler_params=_cparams("parallel", "arbitrary"),
        name="mlp_final",
    )(h, g, w_up, w_down, g_final)
    return out if emit else out[0]


def _pick_tile(n, pref):
    t = min(n, pref)
    while n % t:
        t //= 2
    return t


def kernel(x_prompt, x_sample, mem_prompt, cache_k, cache_v, cache_idx_k, state_ret, cache_mem_k, cache_mem_v,
           page_table, g_mix, w_in, gn_ret, w_ret_out, w_dsa_out, w_o, g_mem, g_memkv, w_mq, w_mk, w_mv, w_mo,
           g_mlp, w_up, w_down, g_final):
    assert w_in.shape[0] == 1, "one layer"
    batch, seq, d = x_prompt.shape
    n_dec, t_dec, _ = x_sample.shape
    n_pages = page_table.shape[1]
    past = n_pages * PAGE_SIZE
    mem_len = mem_prompt.shape[1]
    assert mem_len == MEM_LEN and t_dec <= SUBLANES
    row = lambda v: v.reshape(1, -1)

    w_ret_out_b, w_dsa_out_b, w_o_b = (w[0].astype(BF16) for w in (w_ret_out, w_dsa_out, w_o))
    w_mq_b, w_mo_b = w_mq[0].astype(BF16), w_mo[0].astype(BF16)
    w_mkv_b = jnp.concatenate([w_mk[0], w_mv[0]], axis=1).astype(BF16)
    g_mix_r, gn_r, g_mem_r, g_memkv_r, g_mlp_r, g_final_r = (
        row(v) for v in (g_mix[0], gn_ret[0], g_mem[0], g_memkv[0], g_mlp[0], g_final))

    rows = SUBLANES
    m_s = n_dec * rows
    xs = jnp.pad(x_sample, ((0, 0), (0, rows - t_dec), (0, 0))).reshape(m_s, d)
    proj_s, k_rows_s, v_rows_s, w_proj_b = _norm_proj_rope(xs, g_mix_r, jnp.swapaxes(w_in[0], 0, 1),
                                                           _rope_tables(past, m_s, rows), m_s, 1)

    m_p = batch * seq
    xp = x_prompt.reshape(m_p, d)
    tm_proj = _pick_tile(seq, 1024)
    proj_p, k_rows_p, v_rows_p = _norm_proj_rope(xp, g_mix_r, w_proj_b, _rope_tables(0, seq, seq), tm_proj,
                                                 seq // tm_proj)
    ret_y_p, ret_state_p = _ret_prompt(proj_p, gn_r, batch, seq, _pick_tile(seq, 256))
    dsa_o_p = _dsa_prompt(proj_p, batch, seq, _pick_tile(seq, 256))
    kv_p = _norm_proj(mem_prompt.reshape(batch * mem_len, d), g_memkv_r, w_mkv_b, mem_len, PROJ_TN)
    tm_t = _pick_tile(seq, 256)
    h_p = _merge(xp, ret_y_p, dsa_o_p, proj_p, w_ret_out_b, w_dsa_out_b, w_o_b, tm_t)
    h_p = _mem_attn(h_p, g_mem_r, w_mq_b, kv_p, kv_p, w_mo_b, tm=tm_t, batch_tiles=seq // tm_t,
                    mk_col=0, mv_col=1, nk=mem_len, rows_per_batch=None)

    ret_y_s, ret_state_s = _ret_sample(proj_s, gn_r, state_ret[0], n_dec, rows, t_dec)

    proj_s3 = proj_s.reshape(n_dec, rows, N_PROJ)
    iq_s = proj_s3[:, :, COL_IQ:COL_IQ + IDX_HEADS * IDX_DH].reshape(n_dec, rows * IDX_HEADS, IDX_DH)
    iw_s = proj_s3[:, :, COL_IK + IDX_DH:COL_IK + IDX_DH + IDX_HEADS].reshape(n_dec, rows * IDX_HEADS, 1)
    iw_s = jnp.broadcast_to(iw_s, (n_dec, rows * IDX_HEADS, LANES))
    kvw = DSA_KV_HEADS * HEAD_DIM
    page_rows = DSA_KV_HEADS * PAGE_SIZE
    pad_keys = lambda a: jnp.pad(a, ((0, 0), (0, PAGE_SIZE - rows), (0, 0)))
    ik_new_t = jnp.swapaxes(pad_keys(proj_s3[:, :, COL_IK:COL_IK + IDX_DH]), 1, 2)
    pool_ik_t = jnp.swapaxes(cache_idx_k, 2, 3).reshape(-1, IDX_DH, PAGE_SIZE)
    new_page = lambda a: jnp.pad(a.reshape(n_dec, DSA_KV_HEADS * rows, HEAD_DIM),
                                 ((0, 0), (0, page_rows - DSA_KV_HEADS * rows), (0, 0)))
    k_new, v_new = new_page(k_rows_s), new_page(v_rows_s)
    scores_past, scores_new = _idx_sample(page_table, iq_s, iw_s, ik_new_t, pool_ik_t, _pick_tile(n_pages, 32))
    group = DSA_HEADS // DSA_KV_HEADS
    dq_s = proj_s3[:, :, COL_DQ:COL_DQ + DSA_HEADS * HEAD_DIM].reshape(n_dec, rows, DSA_KV_HEADS, group, HEAD_DIM)
    dq_s = dq_s.transpose(0, 2, 3, 1, 4).reshape(n_dec, DSA_KV_HEADS, group * rows, HEAD_DIM)
    bias_past, bias_new = _sel_sample(scores_past, scores_new, min(TOPK_MAX, (past + t_dec) // 4))
    dsa_o_s = _dsa_sample(page_table, bias_past, bias_new, dq_s, k_new, v_new,
                          cache_k.reshape(-1, page_rows, HEAD_DIM), cache_v.reshape(-1, page_rows, HEAD_DIM),
                          _pick_tile(n_pages, 32))
    dsa_o_s = dsa_o_s.reshape(n_dec, DSA_KV_HEADS, group, rows, HEAD_DIM).transpose(0, 3, 1, 2, 4)
    dsa_o_s = dsa_o_s.reshape(m_s, DSA_HEADS * HEAD_DIM)

    h_s = _merge(xs, ret_y_s, dsa_o_s, proj_s, w_ret_out_b, w_dsa_out_b, w_o_b, m_s)
    h_s = _mem_attn(h_s, g_mem_r, w_mq_b, cache_mem_k[0].reshape(n_dec * mem_len, MEM_W),
                    cache_mem_v[0].reshape(n_dec * mem_len, MEM_W), w_mo_b, tm=m_s, batch_tiles=0,
                    mk_col=0, mv_col=0, nk=n_dec * mem_len, rows_per_batch=rows)
    y_s, w_up_b, w_down_b = _mlp(h_s, g_mlp_r, w_up[0], w_down[0], g_final_r, m_s, 512)
    y_p = _mlp(h_p, g_mlp_r, w_up_b, w_down_b, g_final_r, _pick_tile(seq, 1024), 512)

    def rows_p(col, width, tail):
        return proj_p[:, col:col + width].reshape((1, batch, seq) + tail)

    def rows_s(col, width, tail):
        return proj_s3[:, :t_dec, col:col + width].reshape((1, n_dec, t_dec) + tail)

    kv_p_rows = lambda a: a.reshape(1, batch, seq, DSA_KV_HEADS, HEAD_DIM)
    kv_s_rows = lambda a: a.reshape(1, n_dec, rows, DSA_KV_HEADS, HEAD_DIM)[:, :, :t_dec]
    return (
        y_p.reshape(batch, seq, d),
        y_s.reshape(n_dec, rows, d)[:, :t_dec],
        ret_state_p[None],
        kv_p_rows(k_rows_p),
        kv_p_rows(v_rows_p),
        rows_p(COL_IK, IDX_DH, (IDX_DH,)),
        kv_p[:, :MEM_W].reshape(1, batch, mem_len, MEM_HEADS, HEAD_DIM),
        kv_p[:, MEM_W:].reshape(1, batch, mem_len, MEM_HEADS, HEAD_DIM),
        ret_state_s[None],
        kv_s_rows(k_rows_s),
        kv_s_rows(v_rows_s),
        rows_s(COL_IK, IDX_DH, (IDX_DH,)),
    )
```

```python
import functools
import math

import jax
import jax.numpy as jnp
import numpy as np
from jax import lax
from jax.experimental import pallas as pl
from jax.experimental.pallas import tpu as pltpu

F32 = jnp.float32
BF16 = jnp.bfloat16

D_MODEL = 2048
RET_HEADS = 8
HEAD_DIM = 128
DSA_HEADS = 8
DSA_KV_HEADS = 2
IDX_HEADS = 16
IDX_DH = 64
TOPK_MAX = 256
PAGE_SIZE = 128
MEM_LEN = 256
MEM_HEADS = 4
MEM_W = MEM_HEADS * HEAD_DIM
D_FF = 4 * D_MODEL
RET_THETA = 10000.0
ROPE_THETA = 500000.0
EPS = 1e-6
LANES = 128
SUBLANES = 8
ROW_ALIGN = 16
VMEM_LIMIT = 56 * 1024 * 1024
NEG = -1e30
BISECT_STEPS = 16
MAX_PAGES_PER_STEP = 32

COL_GA = 0
COL_GB = 2048
COL_RQ = 4096
COL_RK = 5120
COL_RV = 6144
COL_RG = 7168
COL_DQ = 8192
COL_IQ = 9216
COL_DK = 10240
COL_DV = 10496
COL_IK = 10752
N_PROJ = 11264
PROJ_TN = 512
PROJ_TN_BF16 = 1024
ROPE_NONE, ROPE_RET, ROPE_RETK, ROPE_DSA, ROPE_IDX, ROPE_IKW = range(6)
COL_ROPE = ((ROPE_NONE,) * 32 + (ROPE_RET,) * 8 + (ROPE_RETK,) * 8 + (ROPE_NONE,) * 16 + (ROPE_DSA,) * 8
            + (ROPE_IDX,) * 8 + (ROPE_DSA,) * 2 + (ROPE_NONE,) * 2 + (ROPE_IKW,) + (ROPE_NONE,) * 3)
TAB_RET_C, TAB_RET_S, TAB_DSA_C, TAB_DSA_S, TAB_IDX_C, TAB_IDX_S = range(6)
TAB_W = 6 * LANES
DSA_ROT = HEAD_DIM // 4
IDX_ROT = IDX_DH // 4


def _cparams(*sem):
    return pltpu.CompilerParams(dimension_semantics=sem, vmem_limit_bytes=VMEM_LIMIT)


def _dot(a, b):
    return jnp.dot(a, b, preferred_element_type=F32)


def _dot_nt(a, b):
    return lax.dot_general(a, b, (((1,), (1,)), ((), ())), preferred_element_type=F32)


def _dot_tn(a, b):
    return lax.dot_general(a, b, (((0,), (0,)), ((), ())), preferred_element_type=F32)


def _rmsnorm_bf16(x, g):
    ms = jnp.mean(x * x, axis=-1, keepdims=True)
    return ((x * lax.rsqrt(ms + EPS)) * g).astype(BF16)


def _sigmoid(x):
    return 1.0 / (1.0 + jnp.exp(-x))


def _tab(tab_ref, which):
    return tab_ref[:, which * LANES:(which + 1) * LANES]


def _rope_cols(o_ref, g, c, s, half, period):
    a = o_ref[:, g * LANES:(g + 1) * LANES]
    if 2 * half == LANES:
        partner = pltpu.roll(a, half, 1)
    else:
        lane = lax.broadcasted_iota(jnp.int32, (1, LANES), 1)
        first = (lane & (period - 1)) < half
        partner = jnp.where(first, pltpu.roll(a, LANES - half, 1), pltpu.roll(a, half, 1))
    o_ref[:, g * LANES:(g + 1) * LANES] = a * c + partner * s


def _norm_proj_kernel(x_ref, g_ref, w_ref, tab_ref, o_ref, u_ref, *, kr_ref=None, vr_ref=None, wq_ref=None):
    j = pl.program_id(1)

    @pl.when(j == 0)
    def _():
        u_ref[...] = _rmsnorm_bf16(x_ref[...], g_ref[...])

    if tab_ref is None:
        o_ref[...] = _dot(u_ref[...], w_ref[...])
        return
    wb = w_ref[...]
    if wb.dtype != BF16:
        wb = wb.astype(BF16)
        wq_ref[...] = wb
    o_ref[...] = _dot_nt(u_ref[...], wb)
    groups = o_ref.shape[1] // LANES

    def rope_group(g, kind):
        if kind == ROPE_RET:
            _rope_cols(o_ref, g, _tab(tab_ref, TAB_RET_C), _tab(tab_ref, TAB_RET_S), LANES // 2, LANES)
        elif kind == ROPE_RETK:
            scale = HEAD_DIM ** -0.5
            _rope_cols(o_ref, g, _tab(tab_ref, TAB_RET_C) * scale, _tab(tab_ref, TAB_RET_S) * scale,
                       LANES // 2, LANES)
        elif kind == ROPE_DSA:
            _rope_cols(o_ref, g, _tab(tab_ref, TAB_DSA_C), _tab(tab_ref, TAB_DSA_S), DSA_ROT // 2, LANES)
        elif kind == ROPE_IDX:
            _rope_cols(o_ref, g, _tab(tab_ref, TAB_IDX_C), _tab(tab_ref, TAB_IDX_S), IDX_ROT // 2, IDX_DH)
        elif kind == ROPE_IKW:
            lane = lax.broadcasted_iota(jnp.int32, (1, LANES), 1)
            is_ik = lane < IDX_DH
            c = jnp.where(is_ik, _tab(tab_ref, TAB_IDX_C), 1.0)
            s = jnp.where(is_ik, _tab(tab_ref, TAB_IDX_S), 0.0)
            _rope_cols(o_ref, g, c, s, IDX_ROT // 2, IDX_DH)

    n_blocks = N_PROJ // (groups * LANES)
    plans = {}
    for b in range(n_blocks):
        plans.setdefault(COL_ROPE[b * groups:(b + 1) * groups], []).append(b)
    kv_block, kv_group = divmod(COL_DK // LANES, groups)
    for plan, blocks in plans.items():
        if all(kind == ROPE_NONE for kind in plan):
            continue
        cond = functools.reduce(jnp.logical_or, [j == b for b in blocks])

        @pl.when(cond)
        def _(plan=plan, blocks=blocks):
            for g, kind in enumerate(plan):
                rope_group(g, kind)
            if kv_block in blocks:
                rows = o_ref.shape[0]
                for kvh in range(DSA_KV_HEADS):
                    dst = pl.ds(kvh, rows, stride=DSA_KV_HEADS)
                    gk, gv = kv_group + kvh, kv_group + DSA_KV_HEADS + kvh
                    kr_ref[dst, :] = o_ref[:, gk * LANES:(gk + 1) * LANES]
                    vr_ref[dst, :] = o_ref[:, gv * LANES:(gv + 1) * LANES]


def _norm_proj(x, g, w, tm, tn):
    m, d = x.shape
    n = w.shape[1]

    def body(x_ref, g_ref, w_ref, o_ref, u_ref):
        _norm_proj_kernel(x_ref, g_ref, w_ref, None, o_ref, u_ref)

    return pl.pallas_call(
        body,
        grid=(m // tm, n // tn),
        in_specs=[
            pl.BlockSpec((tm, d), lambda i, j: (i, 0)),
            pl.BlockSpec((1, d), lambda i, j: (0, 0)),
            pl.BlockSpec((d, tn), lambda i, j: (0, j)),
        ],
        out_specs=pl.BlockSpec((tm, tn), lambda i, j: (i, j)),
        out_shape=jax.ShapeDtypeStruct((m, n), F32),
        scratch_shapes=[pltpu.VMEM((tm, d), BF16)],
        compiler_params=_cparams("parallel", "arbitrary"),
        name="norm_proj",
    )(x, g, w)


_W_IN_ROW = {"rq": 0, "rk": 1024, "rv": 2048, "rg": 3072, "dq": 4096, "dk": 5120, "iq": 5632, "ik": 6656,
             "ga": 6736, "gb": 8784}
PROJ_SRC_ROWS = tuple(_W_IN_ROW[name] + PROJ_TN * k for name, nblk in (
    ("ga", 4), ("gb", 4), ("rq", 2), ("rk", 2), ("rv", 2), ("rg", 2), ("dq", 2), ("iq", 2), ("dk", 1), ("ik", 1))
    for k in range(nblk))


def _norm_proj_rope(x, g, w_t, tab, tm, pos_blocks):
    m, d = x.shape
    from_f32 = w_t.dtype != BF16
    tn = PROJ_TN if from_f32 else PROJ_TN_BF16
    kv_spec = pl.BlockSpec((DSA_KV_HEADS * tm, HEAD_DIM), lambda i, j, *_: (i, 0))
    kv_rows = jax.ShapeDtypeStruct((DSA_KV_HEADS * m, HEAD_DIM), F32)
    in_specs = [
        pl.BlockSpec((tm, d), lambda i, j, *_: (i, 0)),
        pl.BlockSpec((1, d), lambda i, j, *_: (0, 0)),
        None,
        pl.BlockSpec((tm, TAB_W), lambda i, j, *_: (i % pos_blocks, 0)),
    ]
    out_specs = [pl.BlockSpec((tm, tn), lambda i, j, *_: (i, j)), kv_spec, kv_spec]
    out_shape = [jax.ShapeDtypeStruct((m, N_PROJ), F32), kv_rows, kv_rows]
    scratch = [pltpu.VMEM((tm, d), BF16)]
    if from_f32:
        assert m == tm and all(r % ROW_ALIGN == 0 for r in PROJ_SRC_ROWS)
        src_rows = jnp.asarray([r // ROW_ALIGN for r in PROJ_SRC_ROWS], jnp.int32)
        in_specs[2] = pl.BlockSpec((pl.Element(tn), pl.Element(d)), lambda i, j, src: (src[j] * ROW_ALIGN, 0))
        out_specs.append(pl.BlockSpec((tn, d), lambda i, j, src: (j, 0)))
        out_shape.append(jax.ShapeDtypeStruct((N_PROJ, d), BF16))

        def body(src_ref, x_ref, g_ref, w_ref, tab_ref, o_ref, kr_ref, vr_ref, wq_ref, u_ref):
            _norm_proj_kernel(x_ref, g_ref, w_ref, tab_ref, o_ref, u_ref, kr_ref=kr_ref, vr_ref=vr_ref, wq_ref=wq_ref)

        grid_spec = pltpu.PrefetchScalarGridSpec(num_scalar_prefetch=1, grid=(1, N_PROJ // tn), in_specs=in_specs,
                                                 out_specs=out_specs, scratch_shapes=scratch)
        args = (src_rows, x, g, w_t, tab)
    else:
        in_specs[2] = pl.BlockSpec((tn, d), lambda i, j: (j, 0))

        def body(x_ref, g_ref, w_ref, tab_ref, o_ref, kr_ref, vr_ref, u_ref):
            _norm_proj_kernel(x_ref, g_ref, w_ref, tab_ref, o_ref, u_ref, kr_ref=kr_ref, vr_ref=vr_ref)

        grid_spec = pltpu.PrefetchScalarGridSpec(num_scalar_prefetch=0, grid=(m // tm, N_PROJ // tn),
                                                 in_specs=in_specs, out_specs=out_specs, scratch_shapes=scratch)
        args = (x, g, w_t, tab)
    return pl.pallas_call(
        body,
        grid_spec=grid_spec,
        out_shape=out_shape,
        compiler_params=_cparams("parallel", "arbitrary"),
        name="norm_proj_rope",
    )(*args)


@functools.lru_cache(maxsize=None)
def _rope_tables(start, count, period):
    pos = (start + np.arange(count) % period).astype(np.float64)[:, None]

    def cs(half, theta):
        inv = theta ** (-np.arange(half, dtype=np.float64) / half)
        ang = pos * inv[None, :]
        return np.cos(ang), np.sin(ang)

    ones = lambda n: np.ones((count, n))
    zeros = lambda n: np.zeros((count, n))
    c, s = cs(HEAD_DIM // 2, RET_THETA)
    ret_c = np.concatenate([c, c], 1)
    ret_s = np.concatenate([-s, s], 1)
    c, s = cs(DSA_ROT // 2, ROPE_THETA)
    dsa_c = np.concatenate([c, c, ones(LANES - DSA_ROT)], 1)
    dsa_s = np.concatenate([-s, s, zeros(LANES - DSA_ROT)], 1)
    c, s = cs(IDX_ROT // 2, ROPE_THETA)
    idx_c = np.tile(np.concatenate([c, c, ones(IDX_DH - IDX_ROT)], 1), (1, 2))
    idx_s = np.tile(np.concatenate([-s, s, zeros(IDX_DH - IDX_ROT)], 1), (1, 2))
    return np.concatenate([ret_c, ret_s, dsa_c, dsa_s, idx_c, idx_s], 1).astype(np.float32)


def _log_decay():
    return jnp.log1p(-jnp.exp2(-5.0 - jnp.arange(RET_HEADS, dtype=F32)))


def _groupnorm_gate(o, gate, gn):
    mu = jnp.mean(o, axis=-1, keepdims=True)
    d = o - mu
    var = jnp.mean(d * d, axis=-1, keepdims=True)
    n = d * lax.rsqrt(var + EPS) * gn
    return (gate * _sigmoid(gate) * n).astype(BF16)


def _ret_prompt_kernel(q_ref, k_ref, v_ref, g_ref, gn_ref, intra_ref, qdec_ref, kdec_ref, cdec_ref,
                       y_ref, st_ref, s_scr):
    c = pl.program_id(1)

    @pl.when(c == 0)
    def _():
        s_scr[...] = jnp.zeros_like(s_scr)

    for h in range(RET_HEADS):
        sl = slice(h * HEAD_DIM, (h + 1) * HEAD_DIM)
        k = k_ref[:, sl]
        qb = q_ref[:, sl].astype(BF16)
        kb = k.astype(BF16)
        vb = v_ref[:, sl].astype(BF16)
        s = _dot_nt(qb, kb) * intra_ref[h]
        state = s_scr[h]
        o = _dot(s.astype(BF16), vb) + _dot(qb, state.astype(BF16)) * qdec_ref[h]
        kd = (k * kdec_ref[h]).astype(BF16)
        s_scr[h] = state * cdec_ref[h] + _dot_tn(kd, vb)
        y_ref[:, sl] = _groupnorm_gate(o, g_ref[:, sl], gn_ref[:, sl])

    @pl.when(c == pl.num_programs(1) - 1)
    def _():
        st_ref[0] = s_scr[...]


def _ret_prompt(proj, gn, batch, seq, chunk):
    nc = seq // chunk
    log_g = _log_decay()
    j = jnp.arange(chunk, dtype=F32)
    diff = j[:, None] - j[None, :]
    intra = jnp.where(diff >= 0, jnp.exp(log_g[:, None, None] * jnp.maximum(diff, 0.0)), 0.0)
    ones = jnp.ones((1, 1, HEAD_DIM), F32)
    qdec = jnp.exp(log_g[:, None] * (j[None, :] + 1.0))[:, :, None] * ones
    kdec = jnp.exp(log_g[:, None] * (chunk - 1.0 - j[None, :]))[:, :, None] * ones
    cdec = jnp.exp(log_g * chunk)[:, None, None] * ones
    w = RET_HEADS * HEAD_DIM
    col = lambda off: (lambda b, c: (b * nc + c, off // w))
    const3 = lambda b, c: (0, 0, 0)
    return pl.pallas_call(
        _ret_prompt_kernel,
        grid=(batch, nc),
        in_specs=[
            pl.BlockSpec((chunk, w), col(COL_RQ)),
            pl.BlockSpec((chunk, w), col(COL_RK)),
            pl.BlockSpec((chunk, w), col(COL_RV)),
            pl.BlockSpec((chunk, w), col(COL_RG)),
            pl.BlockSpec((1, w), lambda b, c: (0, 0)),
            pl.BlockSpec((RET_HEADS, chunk, chunk), const3),
            pl.BlockSpec((RET_HEADS, chunk, HEAD_DIM), const3),
            pl.BlockSpec((RET_HEADS, chunk, HEAD_DIM), const3),
            pl.BlockSpec((RET_HEADS, 1, HEAD_DIM), const3),
        ],
        out_specs=[
            pl.BlockSpec((chunk, w), lambda b, c: (b * nc + c, 0)),
            pl.BlockSpec((1, RET_HEADS, HEAD_DIM, HEAD_DIM), lambda b, c: (b, 0, 0, 0)),
        ],
        out_shape=[
            jax.ShapeDtypeStruct((batch * seq, w), BF16),
            jax.ShapeDtypeStruct((batch, RET_HEADS, HEAD_DIM, HEAD_DIM), F32),
        ],
        scratch_shapes=[pltpu.VMEM((RET_HEADS, HEAD_DIM, HEAD_DIM), F32)],
        compiler_params=_cparams("parallel", "arbitrary"),
        name="ret_prompt",
    )(proj, proj, proj, proj, gn, intra, qdec, kdec, cdec)


def _ret_sample_kernel(q_ref, k_ref, v_ref, g_ref, gn_ref, st_ref, intra_ref, qdec_ref, kdec_ref, cdec_ref,
                       y_ref, so_ref, *, n_batch, rows):
    k = k_ref[...]
    qb = q_ref[...].astype(BF16)
    kb = k.astype(BF16)
    vb = v_ref[...].astype(BF16)
    s = _dot_nt(qb, kb) * intra_ref[0]
    kd = k * kdec_ref[0]
    row_batch = lax.broadcasted_iota(jnp.int32, kd.shape, 0) // rows
    inter = []
    for b in range(n_batch):
        state = st_ref[b, 0]
        inter.append(_dot(qb[b * rows:(b + 1) * rows], state.astype(BF16)))
        kd_b = jnp.where(row_batch == b, kd, 0.0).astype(BF16)
        so_ref[b, 0] = state * cdec_ref[0] + _dot_tn(kd_b, vb)
    o = _dot(s.astype(BF16), vb) + jnp.concatenate(inter, axis=0) * qdec_ref[0]
    y_ref[...] = _groupnorm_gate(o, g_ref[...], gn_ref[...])


def _ret_sample(proj, gn, state, n_batch, rows, t_valid):
    m = n_batch * rows
    log_g = _log_decay()
    r = jnp.arange(m)
    t = (r % rows).astype(F32)
    same = (r[:, None] // rows) == (r[None, :] // rows)
    diff = t[:, None] - t[None, :]
    intra = jnp.where(same[None] & (diff >= 0)[None], jnp.exp(log_g[:, None, None] * jnp.maximum(diff, 0.0)[None]), 0.0)
    ones = jnp.ones((1, 1, HEAD_DIM), F32)
    qdec = jnp.exp(log_g[:, None] * (t[None, :] + 1.0))[:, :, None] * ones
    kdec = jnp.where(t[None, :] < t_valid, jnp.exp(log_g[:, None] * (t_valid - 1.0 - t[None, :])), 0.0)[:, :, None] * ones
    cdec = jnp.exp(log_g * t_valid)[:, None, None] * ones
    col = lambda off: (lambda h: (0, off // HEAD_DIM + h))
    per_head = lambda h: (h, 0, 0)
    return pl.pallas_call(
        functools.partial(_ret_sample_kernel, n_batch=n_batch, rows=rows),
        grid=(RET_HEADS,),
        in_specs=[
            pl.BlockSpec((m, HEAD_DIM), col(COL_RQ)),
            pl.BlockSpec((m, HEAD_DIM), col(COL_RK)),
            pl.BlockSpec((m, HEAD_DIM), col(COL_RV)),
            pl.BlockSpec((m, HEAD_DIM), col(COL_RG)),
            pl.BlockSpec((1, HEAD_DIM), lambda h: (0, h)),
            pl.BlockSpec((n_batch, 1, HEAD_DIM, HEAD_DIM), lambda h: (0, h, 0, 0)),
            pl.BlockSpec((1, m, m), per_head),
            pl.BlockSpec((1, m, HEAD_DIM), per_head),
            pl.BlockSpec((1, m, HEAD_DIM), per_head),
            pl.BlockSpec((1, 1, HEAD_DIM), per_head),
        ],
        out_specs=[
            pl.BlockSpec((m, HEAD_DIM), lambda h: (0, h)),
            pl.BlockSpec((n_batch, 1, HEAD_DIM, HEAD_DIM), lambda h: (0, h, 0, 0)),
        ],
        out_shape=[
            jax.ShapeDtypeStruct((m, RET_HEADS * HEAD_DIM), BF16),
            jax.ShapeDtypeStruct(state.shape, F32),
        ],
        compiler_params=_cparams("parallel"),
        name="ret_sample",
    )(proj, proj, proj, proj, gn, state, intra, qdec, kdec, cdec)


def _kth_largest(count_ge, next_above, smin, smax, n_adm, topk):
    few = n_adm < topk
    lo = jnp.where(few, -jnp.inf, smin)
    c_lo = jnp.where(few, float(topk), n_adm)
    hi = jnp.where(few, -jnp.inf, smax + jnp.maximum(jnp.abs(smax) * 1e-6, 1e-30))

    def bisect(_, state):
        lo, hi, c_lo = state
        mid = 0.5 * lo + 0.5 * hi
        c = count_ge(mid)
        take = c >= topk
        return jnp.where(take, mid, lo), jnp.where(take, hi, mid), jnp.where(take, c, c_lo)

    lo, hi, c_lo = lax.fori_loop(0, BISECT_STEPS, bisect, (lo, hi, c_lo))
    c_lo = jnp.where(few, float(topk), c_lo)

    def unfinished(state):
        return jnp.max(state[2]) > 0.0

    def step_up(state):
        lo, c_lo, active = state
        nxt, n_eq = next_above(lo)
        c_nxt = c_lo - n_eq
        move = active * jnp.where(c_nxt >= topk, 1.0, 0.0)
        lo = jnp.where(move > 0.0, nxt, lo)
        c_lo = jnp.where(move > 0.0, c_nxt, c_lo)
        return lo, c_lo, move * jnp.where(c_nxt > topk, 1.0, 0.0)

    return lax.while_loop(unfinished, step_up, (lo, c_lo, jnp.where(c_lo > topk, 1.0, 0.0)))[0]


def _dsa_prompt_kernel(dq_ref, iq_ref, iwq_ref, kall_ref, vall_ref, ikall_ref, o_ref,
                       kbf, vtb, ika, ikb, keys, z_a, z_b, zmax_a, zmax_b, m_scr, acc_scr, *, tq, topk):
    z_scr, zmax_scr = (z_a, z_b), (zmax_a, zmax_b)
    i = pl.program_id(1)
    n_chunks = kbf.shape[0]
    scale_log2e = HEAD_DIM ** -0.5 * math.log2(math.e)

    @pl.when(i == 0)
    def _():
        lane = lax.broadcasted_iota(jnp.int32, (tq, LANES), 1)
        for c in range(n_chunks):
            rows = slice(c * tq, (c + 1) * tq)
            kbf[c] = kall_ref[rows, :].astype(BF16)
            for kvh in range(DSA_KV_HEADS):
                sl = slice(kvh * HEAD_DIM, (kvh + 1) * HEAD_DIM)
                vtb[c, kvh, :HEAD_DIM, :] = vall_ref[rows, sl].T.astype(BF16)
                vtb[c, kvh, HEAD_DIM:, :] = jnp.ones((ROW_ALIGN, tq), BF16)
            a = ikall_ref[rows, :]
            ika[c] = jnp.where(lane < IDX_DH, a, 0.0).astype(BF16)
            ikb[c] = jnp.where(lane >= IDX_DH, pltpu.roll(a, IDX_DH, 1), 0.0).astype(BF16)

    w_t = iwq_ref[...].T
    iqb = iq_ref[...].astype(BF16)
    qb = (dq_ref[...] * scale_log2e).astype(BF16)
    t_col = i * tq + lax.broadcasted_iota(jnp.int32, (tq, tq), 1)
    s_row0 = lax.broadcasted_iota(jnp.int32, (tq, tq), 0)

    fold = lambda x: x.reshape(tq // SUBLANES, SUBLANES, tq)

    def score_chunk(c, carry):
        smax, smin = carry
        a = ika[c]
        b = ikb[c]
        acc = jnp.zeros((tq, tq), F32)
        for p in range(IDX_HEADS // 2):
            pair = iqb[:, p * LANES:(p + 1) * LANES]
            w0 = w_t[IDX_DH + 2 * p:IDX_DH + 2 * p + 1, :]
            w1 = w_t[IDX_DH + 2 * p + 1:IDX_DH + 2 * p + 2, :]
            acc = acc + jnp.maximum(_dot_nt(a, pair), 0.0) * w0
            acc = acc + jnp.maximum(_dot_nt(b, pair), 0.0) * w1
        admissible = s_row0 + c * tq <= t_col
        keys[c] = jnp.where(admissible, acc, -jnp.inf)
        smax = jnp.maximum(smax, fold(jnp.where(admissible, acc, -jnp.inf)).max(axis=0))
        smin = jnp.minimum(smin, fold(jnp.where(admissible, acc, jnp.inf)).min(axis=0))
        return smax, smin

    smax, smin = lax.fori_loop(0, i + 1, score_chunk, (jnp.full((SUBLANES, tq), -jnp.inf, F32),
                                                      jnp.full((SUBLANES, tq), jnp.inf, F32)))
    smax = smax.max(axis=0, keepdims=True)
    smin = smin.min(axis=0, keepdims=True)
    keys[i + 1] = jnp.full((tq, tq), -jnp.inf, F32)
    n_pairs = (i + 2) // 2

    def count_ge(t):
        def pair(j, cnt):
            for c in (2 * j, 2 * j + 1):
                cnt = cnt + fold(jnp.where(keys[c] >= t, 1.0, 0.0)).sum(axis=0)
            return cnt

        return lax.fori_loop(0, n_pairs, pair, jnp.zeros((SUBLANES, tq), F32)).sum(axis=0, keepdims=True)

    def next_above(lo):
        def pair(j, carry):
            nxt, n_eq = carry
            for c in (2 * j, 2 * j + 1):
                k = keys[c]
                nxt = jnp.minimum(nxt, fold(jnp.where(k > lo, k, jnp.inf)).min(axis=0))
                n_eq = n_eq + fold(jnp.where(k == lo, 1.0, 0.0)).sum(axis=0)
            return nxt, n_eq

        nxt, n_eq = lax.fori_loop(0, n_pairs, pair, (jnp.full((SUBLANES, tq), jnp.inf, F32),
                                                     jnp.zeros((SUBLANES, tq), F32)))
        return nxt.min(axis=0, keepdims=True), n_eq.sum(axis=0, keepdims=True)

    thr = _kth_largest(count_ge, next_above, smin, smax, (t_col[:1, :] + 1).astype(F32), topk)

    m_scr[...] = jnp.full_like(m_scr, NEG)
    acc_scr[...] = jnp.zeros_like(acc_scr)

    kv_of = lambda h: h // (DSA_HEADS // DSA_KV_HEADS)

    def logits(c, slot):
        bias = jnp.where(s_row0 + c * tq <= t_col, jnp.where(keys[c] >= thr, 0.0, NEG), NEG)
        kc = kbf[c]
        for h in range(DSA_HEADS):
            ksl = slice(kv_of(h) * HEAD_DIM, (kv_of(h) + 1) * HEAD_DIM)
            z = _dot_nt(kc[:, ksl], qb[:, h * HEAD_DIM:(h + 1) * HEAD_DIM]) + bias
            z_scr[slot][h] = z
            zmax_scr[slot][h] = z.max(axis=0, keepdims=True)

    def update(c, slot):
        for h in range(DSA_HEADS):
            m_old = m_scr[h]
            m_new = jnp.maximum(m_old, zmax_scr[slot][h])
            p = jnp.exp2(z_scr[slot][h] - m_new)
            acc_scr[h] = acc_scr[h] * jnp.exp2(m_old - m_new) + _dot(vtb[c, kv_of(h)], p.astype(BF16))
            m_scr[h] = m_new

    n = i + 1
    logits(0, 0)

    def two_chunks(k, carry):
        update(2 * k, 0)
        logits(2 * k + 1, 1)
        update(2 * k + 1, 1)
        logits(2 * k + 2, 0)
        return carry

    lax.fori_loop(0, (n - 1) // 2, two_chunks, 0)

    @pl.when(n % 2 == 1)
    def _():
        update(n - 1, 0)

    @pl.when(n % 2 == 0)
    def _():
        update(n - 2, 0)
        logits(n - 1, 1)
        update(n - 1, 1)

    for h in range(DSA_HEADS):
        o = acc_scr[h, :HEAD_DIM, :] * (1.0 / acc_scr[h, HEAD_DIM:HEAD_DIM + 1, :])
        o_ref[:, h * HEAD_DIM:(h + 1) * HEAD_DIM] = o.T.astype(BF16)


def _dsa_prompt(proj, batch, seq, tq):
    nq = seq // tq
    topk = min(TOPK_MAX, seq // 4)
    qw = DSA_HEADS * HEAD_DIM
    kvw = DSA_KV_HEADS * HEAD_DIM
    return pl.pallas_call(
        functools.partial(_dsa_prompt_kernel, tq=tq, topk=topk),
        grid=(batch, nq),
        in_specs=[
            pl.BlockSpec((tq, qw), lambda b, i: (b * nq + i, COL_DQ // qw)),
            pl.BlockSpec((tq, qw), lambda b, i: (b * nq + i, COL_IQ // qw)),
            pl.BlockSpec((tq, LANES), lambda b, i: (b * nq + i, COL_IK // LANES)),
            pl.BlockSpec((seq, kvw), lambda b, i: (b, COL_DK // kvw)),
            pl.BlockSpec((seq, kvw), lambda b, i: (b, COL_DV // kvw)),
            pl.BlockSpec((seq, LANES), lambda b, i: (b, COL_IK // LANES)),
        ],
        out_specs=pl.BlockSpec((tq, qw), lambda b, i: (b * nq + i, 0)),
        out_shape=jax.ShapeDtypeStruct((batch * seq, qw), BF16),
        scratch_shapes=[
            pltpu.VMEM((nq, tq, kvw), BF16),
            pltpu.VMEM((nq, DSA_KV_HEADS, HEAD_DIM + ROW_ALIGN, tq), BF16),
            pltpu.VMEM((nq, tq, LANES), BF16),
            pltpu.VMEM((nq, tq, LANES), BF16),
            pltpu.VMEM((nq + 1, tq, tq), F32),
            pltpu.VMEM((DSA_HEADS, tq, tq), F32),
            pltpu.VMEM((DSA_HEADS, tq, tq), F32),
            pltpu.VMEM((DSA_HEADS, 1, tq), F32),
            pltpu.VMEM((DSA_HEADS, 1, tq), F32),
            pltpu.VMEM((DSA_HEADS, 1, tq), F32),
            pltpu.VMEM((DSA_HEADS, HEAD_DIM + ROW_ALIGN, tq), F32),
        ],
        compiler_params=_cparams("parallel", "arbitrary"),
        name="dsa_prompt",
    )(proj, proj, proj, proj, proj, proj)


def _idx_sample_kernel(pt_ref, iq_ref, iw_ref, iknew_ref, *rest, pages):
    page_refs = rest[:pages]
    sp_ref, sn_ref = rest[pages:]
    iqb = iq_ref[0].astype(BF16)
    w = iw_ref[0]
    rows = iqb.shape[0] // IDX_HEADS

    def scores(ik_t):
        d = jnp.maximum(_dot(iqb, ik_t.astype(BF16)), 0.0) * w
        return d.reshape(rows, IDX_HEADS, LANES).sum(axis=1)

    for p in range(pages):
        sp_ref[0, :, p * PAGE_SIZE:(p + 1) * PAGE_SIZE] = scores(page_refs[p][0])
    t = lax.broadcasted_iota(jnp.int32, (rows, LANES), 0)
    j = lax.broadcasted_iota(jnp.int32, (rows, LANES), 1)
    sn_ref[0] = jnp.where(j <= t, scores(iknew_ref[0]), -jnp.inf)


def _idx_sample(page_table, iq, iw, ik_new, pool_ik, pages):
    n_batch, n_pages = page_table.shape
    rows16 = iq.shape[1]
    rows = rows16 // IDX_HEADS
    past = n_pages * PAGE_SIZE
    page_specs = [pl.BlockSpec((1, IDX_DH, PAGE_SIZE), functools.partial(
        lambda b, g, pt, k: (pt[b, g * pages + k], 0, 0), k=k)) for k in range(pages)]
    grid_spec = pltpu.PrefetchScalarGridSpec(
        num_scalar_prefetch=1,
        grid=(n_batch, n_pages // pages),
        in_specs=[
            pl.BlockSpec((1, rows16, IDX_DH), lambda b, g, pt: (b, 0, 0)),
            pl.BlockSpec((1, rows16, LANES), lambda b, g, pt: (b, 0, 0)),
            pl.BlockSpec((1, IDX_DH, PAGE_SIZE), lambda b, g, pt: (b, 0, 0)),
        ] + page_specs,
        out_specs=[
            pl.BlockSpec((1, rows, pages * PAGE_SIZE), lambda b, g, pt: (b, 0, g)),
            pl.BlockSpec((1, rows, LANES), lambda b, g, pt: (b, 0, 0)),
        ],
    )
    return pl.pallas_call(
        functools.partial(_idx_sample_kernel, pages=pages),
        grid_spec=grid_spec,
        out_shape=[
            jax.ShapeDtypeStruct((n_batch, rows, past), F32),
            jax.ShapeDtypeStruct((n_batch, rows, LANES), F32),
        ],
        compiler_params=_cparams("parallel", "arbitrary"),
        name="idx_sample",
    )(page_table, iq, iw, ik_new, *([pool_ik] * pages))


def _sel_sample_kernel(sp_ref, sn_ref, bp_ref, bn_ref, *, topk):
    n_batch, rows, past = sp_ref.shape
    sp = sp_ref[...].reshape(n_batch * rows, past)
    sn = sn_ref[...].reshape(n_batch * rows, LANES)
    t = lax.broadcasted_iota(jnp.int32, (n_batch * rows, LANES), 0) % rows
    j = lax.broadcasted_iota(jnp.int32, (n_batch * rows, LANES), 1)
    admissible_new = j <= t
    row_sum = lambda a: a.sum(axis=1, keepdims=True)
    row_min = lambda a: a.min(axis=1, keepdims=True)

    def count_ge(thr):
        return row_sum(jnp.where(sp >= thr, 1.0, 0.0)) + row_sum(jnp.where(sn >= thr, 1.0, 0.0))

    def next_above(lo):
        nxt = jnp.minimum(row_min(jnp.where(sp > lo, sp, jnp.inf)), row_min(jnp.where(sn > lo, sn, jnp.inf)))
        return nxt, row_sum(jnp.where(sp == lo, 1.0, 0.0)) + row_sum(jnp.where(sn == lo, 1.0, 0.0))

    smin = jnp.minimum(row_min(sp), row_min(jnp.where(admissible_new, sn, jnp.inf)))
    smax = jnp.maximum(sp.max(axis=1, keepdims=True), sn.max(axis=1, keepdims=True))
    n_adm = float(past) + row_sum(jnp.where(admissible_new, 1.0, 0.0))
    thr = _kth_largest(count_ge, next_above, smin, smax, n_adm, topk)
    bp_ref[...] = jnp.where(sp >= thr, 0.0, NEG).reshape(n_batch, rows, past)
    bn_ref[...] = jnp.where(admissible_new, jnp.where(sn >= thr, 0.0, NEG), NEG).reshape(n_batch, rows, LANES)


def _sel_sample(scores_past, scores_new, topk):
    full = lambda a: pl.BlockSpec(a.shape, lambda: (0,) * a.ndim)
    return pl.pallas_call(
        functools.partial(_sel_sample_kernel, topk=topk),
        in_specs=[full(scores_past), full(scores_new)],
        out_specs=[full(scores_past), full(scores_new)],
        out_shape=[jax.ShapeDtypeStruct(scores_past.shape, F32), jax.ShapeDtypeStruct(scores_new.shape, F32)],
        compiler_params=pltpu.CompilerParams(vmem_limit_bytes=VMEM_LIMIT),
        name="sel_sample",
    )(scores_past, scores_new)


def _dsa_sample_kernel(pt_ref, *refs, pages, steps, with_merge):
    if with_merge:
        merge_in, refs = refs[:8], refs[8:]
    bp_ref, bn_ref, q_ref, knew_ref, vnew_ref = refs[:5]
    k_refs = refs[5:5 + pages]
    v_refs = refs[5 + pages:5 + 2 * pages]
    outs = refs[5 + 2 * pages:]
    if with_merge:
        _merge_kernel(*merge_in, outs[0])
        outs = outs[1:]
    o_ref, m_scr, l_scr, acc_scr = outs
    g = pl.program_id(0) % steps
    scale = HEAD_DIM ** -0.5
    group = DSA_HEADS // DSA_KV_HEADS

    @pl.when(g == 0)
    def _():
        m_scr[...] = jnp.full_like(m_scr, NEG)
        l_scr[...] = jnp.zeros_like(l_scr)
        acc_scr[...] = jnp.zeros_like(acc_scr)

    def head_rows(ref, kvh):
        return ref[0, pl.ds(kvh, PAGE_SIZE, stride=DSA_KV_HEADS), :].astype(BF16)

    def attend(k_pages, v_pages, bias):
        bias_g = jnp.concatenate([bias] * group, axis=0)
        kv_heads = range(DSA_KV_HEADS)
        qb = [q_ref[0, kvh].astype(BF16) for kvh in kv_heads]
        raw = [jnp.concatenate([_dot_nt(qb[kvh], head_rows(k, kvh)) for k in k_pages], axis=1) for kvh in kv_heads]
        pb, alpha = [], []
        for kvh in kv_heads:
            z = jnp.where(bias_g < 0.0, NEG, raw[kvh])
            m_old = m_scr[kvh]
            m_new = jnp.maximum(m_old, z.max(axis=1, keepdims=True))
            alpha.append(jnp.exp((m_old - m_new) * scale))
            p = jnp.exp((z - m_new) * scale)
            l_scr[kvh] = l_scr[kvh] * alpha[kvh] + p.sum(axis=1, keepdims=True)
            m_scr[kvh] = m_new
            pb.append(p.astype(BF16))
        for kvh in kv_heads:
            pv = _dot(pb[kvh][:, :PAGE_SIZE], head_rows(v_pages[0], kvh))
            for n in range(1, len(v_pages)):
                pv = pv + _dot(pb[kvh][:, n * PAGE_SIZE:(n + 1) * PAGE_SIZE], head_rows(v_pages[n], kvh))
            acc_scr[kvh] = acc_scr[kvh] * alpha[kvh] + pv

    attend(k_refs, v_refs, bp_ref[0])

    @pl.when(g == steps - 1)
    def _():
        attend([knew_ref], [vnew_ref], bn_ref[0])
        for kvh in range(DSA_KV_HEADS):
            o_ref[0, kvh] = (acc_scr[kvh] * (1.0 / l_scr[kvh])).astype(BF16)


def _dsa_sample(page_table, bias_past, bias_new, q, k_new, v_new, pool_k, pool_v, steps, merge=None):
    n_batch, n_pages = page_table.shape
    pages = n_pages // steps
    rows = bias_past.shape[1]
    page_rows = DSA_KV_HEADS * PAGE_SIZE
    grows = q.shape[2]
    page_spec = lambda k: pl.BlockSpec((1, page_rows, HEAD_DIM), functools.partial(
        lambda i, pt, k: (pt[i // steps, (i % steps) * pages + k], 0, 0), k=k))
    per_batch3 = lambda i, pt: (i // steps, 0, 0)
    per_batch4 = lambda i, pt: (i // steps, 0, 0, 0)
    in_specs = [
        pl.BlockSpec((1, rows, pages * PAGE_SIZE), lambda i, pt: (i // steps, 0, i % steps)),
        pl.BlockSpec((1, rows, LANES), per_batch3),
        pl.BlockSpec((1, DSA_KV_HEADS, grows, HEAD_DIM), per_batch4),
        pl.BlockSpec((1, page_rows, HEAD_DIM), per_batch3),
        pl.BlockSpec((1, page_rows, HEAD_DIM), per_batch3),
    ] + [page_spec(k) for k in range(pages)] * 2
    out_specs = [pl.BlockSpec((1, DSA_KV_HEADS, grows, HEAD_DIM), per_batch4)]
    out_shape = [jax.ShapeDtypeStruct((n_batch, DSA_KV_HEADS, grows, HEAD_DIM), BF16)]
    args = [bias_past, bias_new, q, k_new, v_new] + [pool_k] * pages + [pool_v] * pages
    if merge is not None:
        m_args, m_in_specs, m_out_spec, m_out_shape = merge
        args, in_specs = list(m_args) + args, list(m_in_specs) + in_specs
        out_specs, out_shape = [m_out_spec] + out_specs, [m_out_shape] + out_shape
    grid_spec = pltpu.PrefetchScalarGridSpec(
        num_scalar_prefetch=1,
        grid=(n_batch * steps,),
        in_specs=in_specs,
        out_specs=out_specs,
        scratch_shapes=[
            pltpu.VMEM((DSA_KV_HEADS, grows, 1), F32),
            pltpu.VMEM((DSA_KV_HEADS, grows, 1), F32),
            pltpu.VMEM((DSA_KV_HEADS, grows, HEAD_DIM), F32),
        ],
    )
    out = pl.pallas_call(
        functools.partial(_dsa_sample_kernel, pages=pages, steps=steps, with_merge=merge is not None),
        grid_spec=grid_spec,
        out_shape=out_shape,
        compiler_params=_cparams("arbitrary"),
        name="dsa_sample_merge" if merge is not None else "dsa_sample",
    )(page_table, *args)
    return out if merge is not None else out[0]


def _merge_kernel(x_ref, ry_ref, do_ref, ga_ref, gb_ref, wr_ref, wd_ref, wo_ref, o_ref):
    ya = _dot(ry_ref[...], wr_ref[...])
    yb = _dot(do_ref[...], wd_ref[...])
    merged = _sigmoid(ga_ref[...]) * ya + _sigmoid(gb_ref[...]) * yb
    o_ref[...] = x_ref[...] + _dot(merged.astype(BF16), wo_ref[...])


def _merge_parts(x, ret_y, dsa_o, proj, w_ret_out, w_dsa_out, w_o, tm):
    m, d = x.shape
    w = ret_y.shape[1]
    const = lambda i, *_: (0, 0)
    resident = dict(pipeline_mode=pl.Buffered(1))
    in_specs = [
        pl.BlockSpec((tm, d), lambda i, *_: (i, 0)),
        pl.BlockSpec((tm, w), lambda i, *_: (i, 0)),
        pl.BlockSpec((tm, w), lambda i, *_: (i, 0)),
        pl.BlockSpec((tm, d), lambda i, *_: (i, COL_GA // d)),
        pl.BlockSpec((tm, d), lambda i, *_: (i, COL_GB // d)),
        pl.BlockSpec((w, d), const, **resident),
        pl.BlockSpec((w, d), const, **resident),
        pl.BlockSpec((d, d), const, **resident),
    ]
    args = (x, ret_y, dsa_o, proj, proj, w_ret_out, w_dsa_out, w_o)
    return args, in_specs, pl.BlockSpec((tm, d), lambda i, *_: (i, 0)), jax.ShapeDtypeStruct((m, d), F32)


def _merge(*operands, tm):
    args, in_specs, out_spec, out_shape = _merge_parts(*operands, tm)
    return pl.pallas_call(
        _merge_kernel,
        grid=(out_shape.shape[0] // tm,),
        in_specs=in_specs,
        out_specs=out_spec,
        out_shape=out_shape,
        compiler_params=_cparams("parallel"),
        name="merge_out",
    )(*args)


def _mem_attn_kernel(h_ref, g_ref, wq_ref, mk_ref, mv_ref, wo_ref, o_ref, *, rows_per_batch):
    h = h_ref[...]
    qm = _dot(_rmsnorm_bf16(h, g_ref[...]), wq_ref[...])
    tm = h.shape[0]
    nk = mk_ref.shape[0]
    scale = HEAD_DIM ** -0.5
    if rows_per_batch is not None:
        rb = lax.broadcasted_iota(jnp.int32, (tm, nk), 0) // rows_per_batch
        kb = lax.broadcasted_iota(jnp.int32, (tm, nk), 1) // MEM_LEN
        same = rb == kb
    outs = []
    for hd in range(MEM_HEADS):
        sl = slice(hd * HEAD_DIM, (hd + 1) * HEAD_DIM)
        z = _dot_nt(qm[:, sl].astype(BF16), mk_ref[:, sl].astype(BF16)) * scale
        if rows_per_batch is not None:
            z = jnp.where(same, z, NEG)
        p = jnp.exp(z - z.max(axis=-1, keepdims=True))
        pv = _dot(p.astype(BF16), mv_ref[:, sl].astype(BF16))
        outs.append(pv * (1.0 / p.sum(axis=-1, keepdims=True)))
    om = jnp.concatenate(outs, axis=1).astype(BF16)
    o_ref[...] = h + _dot(om, wo_ref[...])


def _mem_attn(h, g, w_mq, mk, mv, w_mo, *, tm, batch_tiles, mk_col, mv_col, nk, rows_per_batch):
    m, d = h.shape
    const = lambda i: (0, 0)
    if batch_tiles:
        kmap = lambda col: (lambda i: (i // batch_tiles, col))
    else:
        kmap = lambda col: (lambda i: (0, col))
    return pl.pallas_call(
        functools.partial(_mem_attn_kernel, rows_per_batch=rows_per_batch),
        grid=(m // tm,),
        in_specs=[
            pl.BlockSpec((tm, d), lambda i: (i, 0)),
            pl.BlockSpec((1, d), const),
            pl.BlockSpec((d, MEM_W), const),
            pl.BlockSpec((nk, MEM_W), kmap(mk_col)),
            pl.BlockSpec((nk, MEM_W), kmap(mv_col)),
            pl.BlockSpec((MEM_W, d), const),
        ],
        out_specs=pl.BlockSpec((tm, d), lambda i: (i, 0)),
        out_shape=jax.ShapeDtypeStruct((m, d), F32),
        compiler_params=_cparams("parallel"),
        name="mem_attn",
    )(h, g, w_mq, mk, mv, w_mo)


def _mlp_kernel(h_ref, g_ref, wu_ref, wd_ref, gf_ref, o_ref, *rest):
    u_ref = rest[-1]
    j = pl.program_id(1)

    @pl.when(j == 0)
    def _():
        u_ref[...] = _rmsnorm_bf16(h_ref[...], g_ref[...])
        o_ref[...] = h_ref[...]

    wu, wd = wu_ref[...], wd_ref[...]
    if len(rest) == 3:
        wu, wd = wu.astype(BF16), wd.astype(BF16)
        rest[0][...] = wu
        rest[1][...] = wd
    a = jnp.maximum(_dot(u_ref[...], wu), 0.0)
    o_ref[...] += _dot((a * a).astype(BF16), wd)

    @pl.when(j == pl.num_programs(1) - 1)
    def _():
        y = o_ref[...]
        ms = jnp.mean(y * y, axis=-1, keepdims=True)
        o_ref[...] = (y * lax.rsqrt(ms + EPS)) * gf_ref[...]


def _mlp(h, g, w_up, w_down, g_final, tm, tf):
    m, d = h.shape
    ff = w_up.shape[1]
    emit = w_up.dtype != BF16
    assert not emit or m == tm, "weight copies are written once, by a call with one row tile"
    wu_spec = pl.BlockSpec((d, tf), lambda i, j: (0, j))
    wd_spec = pl.BlockSpec((tf, d), lambda i, j: (j, 0))
    out_specs = [pl.BlockSpec((tm, d), lambda i, j: (i, 0))]
    out_shape = [jax.ShapeDtypeStruct((m, d), F32)]
    if emit:
        out_specs += [wu_spec, wd_spec]
        out_shape += [jax.ShapeDtypeStruct(w_up.shape, BF16), jax.ShapeDtypeStruct(w_down.shape, BF16)]
    out = pl.pallas_call(
        _mlp_kernel,
        grid=(m // tm, ff // tf),
        in_specs=[
            pl.BlockSpec((tm, d), lambda i, j: (i, 0)),
            pl.BlockSpec((1, d), lambda i, j: (0, 0)),
            wu_spec,
            wd_spec,
            pl.BlockSpec((1, d), lambda i, j: (0, 0)),
        ],
        out_specs=out_specs,
        out_shape=out_shape,
        scratch_shapes=[pltpu.VMEM((tm, d), BF16)],
        compiler_params=_cparams("parallel", "arbitrary"),
        name="mlp_final",
    )(h, g, w_up, w_down, g_final)
    return out if emit else out[0]


def _pick_tile(n, pref):
    t = min(n, pref)
    while n % t:
        t //= 2
    return t


def kernel(x_prompt, x_sample, mem_prompt, cache_k, cache_v, cache_idx_k, state_ret, cache_mem_k, cache_mem_v,
           page_table, g_mix, w_in, gn_ret, w_ret_out, w_dsa_out, w_o, g_mem, g_memkv, w_mq, w_mk, w_mv, w_mo,
           g_mlp, w_up, w_down, g_final):
    assert w_in.shape[0] == 1, "one layer"
    batch, seq, d = x_prompt.shape
    n_dec, t_dec, _ = x_sample.shape
    n_pages = page_table.shape[1]
    past = n_pages * PAGE_SIZE
    mem_len = mem_prompt.shape[1]
    assert mem_len == MEM_LEN and t_dec <= SUBLANES
    row = lambda v: v.reshape(1, -1)

    w_ret_out_b, w_dsa_out_b, w_o_b = (w[0].astype(BF16) for w in (w_ret_out, w_dsa_out, w_o))
    w_mq_b, w_mo_b = w_mq[0].astype(BF16), w_mo[0].astype(BF16)
    w_mkv_b = jnp.concatenate([w_mk[0], w_mv[0]], axis=1).astype(BF16)
    g_mix_r, gn_r, g_mem_r, g_memkv_r, g_mlp_r, g_final_r = (
        row(v) for v in (g_mix[0], gn_ret[0], g_mem[0], g_memkv[0], g_mlp[0], g_final))

    rows = SUBLANES
    m_s = n_dec * rows
    xs = jnp.pad(x_sample, ((0, 0), (0, rows - t_dec), (0, 0))).reshape(m_s, d)
    proj_s, k_rows_s, v_rows_s, w_proj_b = _norm_proj_rope(xs, g_mix_r, jnp.swapaxes(w_in[0], 0, 1),
                                                           _rope_tables(past, m_s, rows), m_s, 1)

    m_p = batch * seq
    xp = x_prompt.reshape(m_p, d)
    tm_proj = _pick_tile(seq, 1024)
    proj_p, k_rows_p, v_rows_p = _norm_proj_rope(xp, g_mix_r, w_proj_b, _rope_tables(0, seq, seq), tm_proj,
                                                 seq // tm_proj)
    ret_y_p, ret_state_p = _ret_prompt(proj_p, gn_r, batch, seq, _pick_tile(seq, 256))
    dsa_o_p = _dsa_prompt(proj_p, batch, seq, _pick_tile(seq, 256))
    kv_p = _norm_proj(mem_prompt.reshape(batch * mem_len, d), g_memkv_r, w_mkv_b, mem_len, PROJ_TN)
    tm_t = _pick_tile(seq, 256)
    merge_p = (xp, ret_y_p, dsa_o_p, proj_p, w_ret_out_b, w_dsa_out_b, w_o_b)

    ret_y_s, ret_state_s = _ret_sample(proj_s, gn_r, state_ret[0], n_dec, rows, t_dec)

    proj_s3 = proj_s.reshape(n_dec, rows, N_PROJ)
    iq_s = proj_s3[:, :, COL_IQ:COL_IQ + IDX_HEADS * IDX_DH].reshape(n_dec, rows * IDX_HEADS, IDX_DH)
    iw_s = proj_s3[:, :, COL_IK + IDX_DH:COL_IK + IDX_DH + IDX_HEADS].reshape(n_dec, rows * IDX_HEADS, 1)
    iw_s = jnp.broadcast_to(iw_s, (n_dec, rows * IDX_HEADS, LANES))
    kvw = DSA_KV_HEADS * HEAD_DIM
    page_rows = DSA_KV_HEADS * PAGE_SIZE
    pad_keys = lambda a: jnp.pad(a, ((0, 0), (0, PAGE_SIZE - rows), (0, 0)))
    ik_new_t = jnp.swapaxes(pad_keys(proj_s3[:, :, COL_IK:COL_IK + IDX_DH]), 1, 2)
    pool_ik_t = jnp.swapaxes(cache_idx_k, 2, 3).reshape(-1, IDX_DH, PAGE_SIZE)
    new_page = lambda a: jnp.pad(a.reshape(n_dec, DSA_KV_HEADS * rows, HEAD_DIM),
                                 ((0, 0), (0, page_rows - DSA_KV_HEADS * rows), (0, 0)))
    k_new, v_new = new_page(k_rows_s), new_page(v_rows_s)
    scores_past, scores_new = _idx_sample(page_table, iq_s, iw_s, ik_new_t, pool_ik_t, _pick_tile(n_pages, 32))
    group = DSA_HEADS // DSA_KV_HEADS
    dq_s = proj_s3[:, :, COL_DQ:COL_DQ + DSA_HEADS * HEAD_DIM].reshape(n_dec, rows, DSA_KV_HEADS, group, HEAD_DIM)
    dq_s = dq_s.transpose(0, 2, 3, 1, 4).reshape(n_dec, DSA_KV_HEADS, group * rows, HEAD_DIM)
    bias_past, bias_new = _sel_sample(scores_past, scores_new, min(TOPK_MAX, (past + t_dec) // 4))
    dsa_s_args = (page_table, bias_past, bias_new, dq_s, k_new, v_new,
                  cache_k.reshape(-1, page_rows, HEAD_DIM), cache_v.reshape(-1, page_rows, HEAD_DIM))
    steps = (m_p // tm_t) // n_dec if (m_p // tm_t) % n_dec == 0 else 0
    if steps and n_pages % steps == 0 and n_pages // steps <= MAX_PAGES_PER_STEP:
        h_p, dsa_o_s = _dsa_sample(*dsa_s_args, steps, merge=_merge_parts(*merge_p, tm_t))
    else:
        h_p = _merge(*merge_p, tm=tm_t)
        dsa_o_s = _dsa_sample(*dsa_s_args, n_pages // _pick_tile(n_pages, MAX_PAGES_PER_STEP))
    h_p = _mem_attn(h_p, g_mem_r, w_mq_b, kv_p, kv_p, w_mo_b, tm=tm_t, batch_tiles=seq // tm_t,
                    mk_col=0, mv_col=1, nk=mem_len, rows_per_batch=None)
    dsa_o_s = dsa_o_s.reshape(n_dec, DSA_KV_HEADS, group, rows, HEAD_DIM).transpose(0, 3, 1, 2, 4)
    dsa_o_s = dsa_o_s.reshape(m_s, DSA_HEADS * HEAD_DIM)

    h_s = _merge(xs, ret_y_s, dsa_o_s, proj_s, w_ret_out_b, w_dsa_out_b, w_o_b, tm=m_s)
    h_s = _mem_attn(h_s, g_mem_r, w_mq_b, cache_mem_k[0].reshape(n_dec * mem_len, MEM_W),
                    cache_mem_v[0].reshape(n_dec * mem_len, MEM_W), w_mo_b, tm=m_s, batch_tiles=0,
                    mk_col=0, mv_col=0, nk=n_dec * mem_len, rows_per_batch=rows)
    y_s, w_up_b, w_down_b = _mlp(h_s, g_mlp_r, w_up[0], w_down[0], g_final_r, m_s, 512)
    y_p = _mlp(h_p, g_mlp_r, w_up_b, w_down_b, g_final_r, _pick_tile(seq, 1024), 512)

    def rows_p(col, width, tail):
        return proj_p[:, col:col + width].reshape((1, batch, seq) + tail)

    def rows_s(col, width, tail):
        return proj_s3[:, :t_dec, col:col + width].reshape((1, n_dec, t_dec) + tail)

    kv_p_rows = lambda a: a.reshape(1, batch, seq, DSA_KV_HEADS, HEAD_DIM)
    kv_s_rows = lambda a: a.reshape(1, n_dec, rows, DSA_KV_HEADS, HEAD_DIM)[:, :, :t_dec]
    return (
        y_p.reshape(batch, seq, d),
        y_s.reshape(n_dec, rows, d)[:, :t_dec],
        ret_state_p[None],
        kv_p_rows(k_rows_p),
        kv_p_rows(v_rows_p),
        rows_p(COL_IK, IDX_DH, (IDX_DH,)),
        kv_p[:, :MEM_W].reshape(1, batch, mem_len, MEM_HEADS, HEAD_DIM),
        kv_p[:, MEM_W:].reshape(1, batch, mem_len, MEM_HEADS, HEAD_DIM),
        ret_state_s[None],
        kv_s_rows(k_rows_s),
        kv_s_rows(v_rows_s),
        rows_s(COL_IK, IDX_DH, (IDX_DH,)),
    )
```

```python
import functools
import math

import jax
import jax.numpy as jnp
import numpy as np
from jax import lax
from jax.experimental import pallas as pl
from jax.experimental.pallas import tpu as pltpu

F32 = jnp.float32
BF16 = jnp.bfloat16

D_MODEL = 2048
RET_HEADS = 8
HEAD_DIM = 128
DSA_HEADS = 8
DSA_KV_HEADS = 2
IDX_HEADS = 16
IDX_DH = 64
TOPK_MAX = 256
PAGE_SIZE = 128
MEM_LEN = 256
MEM_HEADS = 4
MEM_W = MEM_HEADS * HEAD_DIM
D_FF = 4 * D_MODEL
RET_THETA = 10000.0
ROPE_THETA = 500000.0
EPS = 1e-6
LANES = 128
SUBLANES = 8
ROW_ALIGN = 16
VMEM_LIMIT = 56 * 1024 * 1024
NEG = -1e30
BISECT_STEPS = 16
MAX_PAGES_PER_STEP = 32

COL_GA = 0
COL_GB = 2048
COL_RQ = 4096
COL_RK = 5120
COL_RV = 6144
COL_RG = 7168
COL_DQ = 8192
COL_IQ = 9216
COL_DK = 10240
COL_DV = 10496
COL_IK = 10752
N_PROJ = 11264
PROJ_TN = 512
PROJ_TN_BF16 = 1024
ROPE_NONE, ROPE_RET, ROPE_RETK, ROPE_DSA, ROPE_IDX, ROPE_IKW = range(6)
COL_ROPE = ((ROPE_NONE,) * 32 + (ROPE_RET,) * 8 + (ROPE_RETK,) * 8 + (ROPE_NONE,) * 16 + (ROPE_DSA,) * 8
            + (ROPE_IDX,) * 8 + (ROPE_DSA,) * 2 + (ROPE_NONE,) * 2 + (ROPE_IKW,) + (ROPE_NONE,) * 3)
TAB_RET_C, TAB_RET_S, TAB_DSA_C, TAB_DSA_S, TAB_IDX_C, TAB_IDX_S = range(6)
TAB_W = 6 * LANES
DSA_ROT = HEAD_DIM // 4
IDX_ROT = IDX_DH // 4


def _cparams(*sem):
    return pltpu.CompilerParams(dimension_semantics=sem, vmem_limit_bytes=VMEM_LIMIT)


def _dot(a, b):
    return jnp.dot(a, b, preferred_element_type=F32)


def _dot_nt(a, b):
    return lax.dot_general(a, b, (((1,), (1,)), ((), ())), preferred_element_type=F32)


def _dot_tn(a, b):
    return lax.dot_general(a, b, (((0,), (0,)), ((), ())), preferred_element_type=F32)


def _rmsnorm_bf16(x, g):
    ms = jnp.mean(x * x, axis=-1, keepdims=True)
    return ((x * lax.rsqrt(ms + EPS)) * g).astype(BF16)


def _sigmoid(x):
    return 1.0 / (1.0 + jnp.exp(-x))


def _tab(tab_ref, which):
    return tab_ref[:, which * LANES:(which + 1) * LANES]


def _rope_cols(src_ref, dst_ref, g, c, s, half, period):
    a = src_ref[:, g * LANES:(g + 1) * LANES]
    if 2 * half == LANES:
        partner = pltpu.roll(a, half, 1)
    else:
        lane = lax.broadcasted_iota(jnp.int32, (1, LANES), 1)
        first = (lane & (period - 1)) < half
        partner = jnp.where(first, pltpu.roll(a, LANES - half, 1), pltpu.roll(a, half, 1))
    dst_ref[:, g * LANES:(g + 1) * LANES] = a * c + partner * s


def _finish_block(src_ref, dst_ref, tab_ref, plan, kv_group, kr_ref, vr_ref):
    for g, kind in enumerate(plan):
        rope = functools.partial(_rope_cols, src_ref, dst_ref, g)
        if kind == ROPE_RET:
            rope(_tab(tab_ref, TAB_RET_C), _tab(tab_ref, TAB_RET_S), LANES // 2, LANES)
        elif kind == ROPE_RETK:
            scale = HEAD_DIM ** -0.5
            rope(_tab(tab_ref, TAB_RET_C) * scale, _tab(tab_ref, TAB_RET_S) * scale, LANES // 2, LANES)
        elif kind == ROPE_DSA:
            rope(_tab(tab_ref, TAB_DSA_C), _tab(tab_ref, TAB_DSA_S), DSA_ROT // 2, LANES)
        elif kind == ROPE_IDX:
            rope(_tab(tab_ref, TAB_IDX_C), _tab(tab_ref, TAB_IDX_S), IDX_ROT // 2, IDX_DH)
        elif kind == ROPE_IKW:
            lane = lax.broadcasted_iota(jnp.int32, (1, LANES), 1)
            is_ik = lane < IDX_DH
            rope(jnp.where(is_ik, _tab(tab_ref, TAB_IDX_C), 1.0), jnp.where(is_ik, _tab(tab_ref, TAB_IDX_S), 0.0),
                 IDX_ROT // 2, IDX_DH)
        elif src_ref is not dst_ref:
            dst_ref[:, g * LANES:(g + 1) * LANES] = src_ref[:, g * LANES:(g + 1) * LANES]
    if kv_group is not None:
        rows = dst_ref.shape[0]
        for kvh in range(DSA_KV_HEADS):
            rows_of_head = pl.ds(kvh, rows, stride=DSA_KV_HEADS)
            gk, gv = kv_group + kvh, kv_group + DSA_KV_HEADS + kvh
            kr_ref[rows_of_head, :] = dst_ref[:, gk * LANES:(gk + 1) * LANES]
            vr_ref[rows_of_head, :] = dst_ref[:, gv * LANES:(gv + 1) * LANES]


def _norm_proj_kernel(x_ref, g_ref, w_ref, tab_ref, o_ref, u_ref, *, kr_ref=None, vr_ref=None, wq_ref=None):
    j = pl.program_id(1)

    @pl.when(j == 0)
    def _():
        u_ref[...] = _rmsnorm_bf16(x_ref[...], g_ref[...])

    if tab_ref is None:
        o_ref[...] = _dot(u_ref[...], w_ref[...])
        return
    wb = w_ref[...]
    if wb.dtype != BF16:
        wb = wb.astype(BF16)
        wq_ref[...] = wb
    o_ref[...] = _dot_nt(u_ref[...], wb)
    groups = o_ref.shape[1] // LANES

    plans = {}
    for b in range(N_PROJ // (groups * LANES)):
        plans.setdefault(COL_ROPE[b * groups:(b + 1) * groups], []).append(b)
    kv_block, kv_group = divmod(COL_DK // LANES, groups)
    for plan, blocks in plans.items():
        if all(kind == ROPE_NONE for kind in plan):
            continue
        cond = functools.reduce(jnp.logical_or, [j == b for b in blocks])

        @pl.when(cond)
        def _(plan=plan, blocks=blocks):
            _finish_block(o_ref, o_ref, tab_ref, plan, kv_group if kv_block in blocks else None, kr_ref, vr_ref)


def _norm_proj_pipelined_kernel(x_ref, g_ref, w_ref, tab_ref, o_ref, kr_ref, vr_ref, u_ref, acc_a, acc_b):
    j = pl.program_id(1)
    acc = (acc_a, acc_b)
    groups = o_ref.shape[1] // LANES
    n_blocks = N_PROJ // (groups * LANES)
    kv_block, kv_group = divmod(COL_DK // LANES, groups)
    for step in range(n_blocks + 1):
        @pl.when(j == step)
        def _(step=step):
            if step == 0:
                u_ref[...] = _rmsnorm_bf16(x_ref[...], g_ref[...])
            else:
                b = step - 1
                _finish_block(acc[b % 2], o_ref, tab_ref, COL_ROPE[b * groups:(b + 1) * groups],
                              kv_group if b == kv_block else None, kr_ref, vr_ref)
            if step < n_blocks:
                acc[step % 2][...] = _dot_nt(u_ref[...], w_ref[...])


def _norm_proj(x, g, w, tm, tn):
    m, d = x.shape
    n = w.shape[1]

    def body(x_ref, g_ref, w_ref, o_ref, u_ref):
        _norm_proj_kernel(x_ref, g_ref, w_ref, None, o_ref, u_ref)

    return pl.pallas_call(
        body,
        grid=(m // tm, n // tn),
        in_specs=[
            pl.BlockSpec((tm, d), lambda i, j: (i, 0)),
            pl.BlockSpec((1, d), lambda i, j: (0, 0)),
            pl.BlockSpec((d, tn), lambda i, j: (0, j)),
        ],
        out_specs=pl.BlockSpec((tm, tn), lambda i, j: (i, j)),
        out_shape=jax.ShapeDtypeStruct((m, n), F32),
        scratch_shapes=[pltpu.VMEM((tm, d), BF16)],
        compiler_params=_cparams("parallel", "arbitrary"),
        name="norm_proj",
    )(x, g, w)


_W_IN_ROW = {"rq": 0, "rk": 1024, "rv": 2048, "rg": 3072, "dq": 4096, "dk": 5120, "iq": 5632, "ik": 6656,
             "ga": 6736, "gb": 8784}
PROJ_SRC_ROWS = tuple(_W_IN_ROW[name] + PROJ_TN * k for name, nblk in (
    ("ga", 4), ("gb", 4), ("rq", 2), ("rk", 2), ("rv", 2), ("rg", 2), ("dq", 2), ("iq", 2), ("dk", 1), ("ik", 1))
    for k in range(nblk))


def _norm_proj_rope(x, g, w_t, tab, tm, pos_blocks):
    m, d = x.shape
    from_f32 = w_t.dtype != BF16
    tn = PROJ_TN if from_f32 else PROJ_TN_BF16
    kv_spec = pl.BlockSpec((DSA_KV_HEADS * tm, HEAD_DIM), lambda i, j, *_: (i, 0))
    kv_rows = jax.ShapeDtypeStruct((DSA_KV_HEADS * m, HEAD_DIM), F32)
    in_specs = [
        pl.BlockSpec((tm, d), lambda i, j, *_: (i, 0)),
        pl.BlockSpec((1, d), lambda i, j, *_: (0, 0)),
        None,
        pl.BlockSpec((tm, TAB_W), lambda i, j, *_: (i % pos_blocks, 0)),
    ]
    out_specs = [pl.BlockSpec((tm, tn), lambda i, j, *_: (i, j)), kv_spec, kv_spec]
    out_shape = [jax.ShapeDtypeStruct((m, N_PROJ), F32), kv_rows, kv_rows]
    scratch = [pltpu.VMEM((tm, d), BF16)]
    if from_f32:
        assert m == tm and all(r % ROW_ALIGN == 0 for r in PROJ_SRC_ROWS)
        src_rows = jnp.asarray([r // ROW_ALIGN for r in PROJ_SRC_ROWS], jnp.int32)
        in_specs[2] = pl.BlockSpec((pl.Element(tn), pl.Element(d)), lambda i, j, src: (src[j] * ROW_ALIGN, 0))
        out_specs.append(pl.BlockSpec((tn, d), lambda i, j, src: (j, 0)))
        out_shape.append(jax.ShapeDtypeStruct((N_PROJ, d), BF16))

        def body(src_ref, x_ref, g_ref, w_ref, tab_ref, o_ref, kr_ref, vr_ref, wq_ref, u_ref):
            _norm_proj_kernel(x_ref, g_ref, w_ref, tab_ref, o_ref, u_ref, kr_ref=kr_ref, vr_ref=vr_ref, wq_ref=wq_ref)

        grid_spec = pltpu.PrefetchScalarGridSpec(num_scalar_prefetch=1, grid=(1, N_PROJ // tn), in_specs=in_specs,
                                                 out_specs=out_specs, scratch_shapes=scratch)
        args = (src_rows, x, g, w_t, tab)
    else:
        n_blocks = N_PROJ // tn
        in_specs[2] = pl.BlockSpec((tn, d), lambda i, j: (jnp.minimum(j, n_blocks - 1), 0))
        in_specs[3] = pl.BlockSpec((tm, TAB_W), lambda i, j: (i % pos_blocks, 0), pipeline_mode=pl.Buffered(1))
        out_specs[0] = pl.BlockSpec((tm, tn), lambda i, j: (i, jnp.maximum(j - 1, 0)))
        body = _norm_proj_pipelined_kernel
        scratch += [pltpu.VMEM((tm, tn), F32), pltpu.VMEM((tm, tn), F32)]
        grid_spec = pltpu.PrefetchScalarGridSpec(num_scalar_prefetch=0, grid=(m // tm, n_blocks + 1),
                                                 in_specs=in_specs, out_specs=out_specs, scratch_shapes=scratch)
        args = (x, g, w_t, tab)
    return pl.pallas_call(
        body,
        grid_spec=grid_spec,
        out_shape=out_shape,
        compiler_params=_cparams("parallel", "arbitrary"),
        name="norm_proj_rope",
    )(*args)


@functools.lru_cache(maxsize=None)
def _rope_tables(start, count, period):
    pos = (start + np.arange(count) % period).astype(np.float64)[:, None]

    def cs(half, theta):
        inv = theta ** (-np.arange(half, dtype=np.float64) / half)
        ang = pos * inv[None, :]
        return np.cos(ang), np.sin(ang)

    ones = lambda n: np.ones((count, n))
    zeros = lambda n: np.zeros((count, n))
    c, s = cs(HEAD_DIM // 2, RET_THETA)
    ret_c = np.concatenate([c, c], 1)
    ret_s = np.concatenate([-s, s], 1)
    c, s = cs(DSA_ROT // 2, ROPE_THETA)
    dsa_c = np.concatenate([c, c, ones(LANES - DSA_ROT)], 1)
    dsa_s = np.concatenate([-s, s, zeros(LANES - DSA_ROT)], 1)
    c, s = cs(IDX_ROT // 2, ROPE_THETA)
    idx_c = np.tile(np.concatenate([c, c, ones(IDX_DH - IDX_ROT)], 1), (1, 2))
    idx_s = np.tile(np.concatenate([-s, s, zeros(IDX_DH - IDX_ROT)], 1), (1, 2))
    return np.concatenate([ret_c, ret_s, dsa_c, dsa_s, idx_c, idx_s], 1).astype(np.float32)


def _log_decay():
    return jnp.log1p(-jnp.exp2(-5.0 - jnp.arange(RET_HEADS, dtype=F32)))


def _groupnorm_gate(o, gate, gn):
    mu = jnp.mean(o, axis=-1, keepdims=True)
    d = o - mu
    var = jnp.mean(d * d, axis=-1, keepdims=True)
    n = d * lax.rsqrt(var + EPS) * gn
    return (gate * _sigmoid(gate) * n).astype(BF16)


def _ret_prompt_kernel(q_ref, k_ref, v_ref, g_ref, gn_ref, intra_ref, qdec_ref, kdec_ref, cdec_ref,
                       y_ref, st_ref, s_scr):
    c = pl.program_id(1)

    @pl.when(c == 0)
    def _():
        s_scr[...] = jnp.zeros_like(s_scr)

    for h in range(RET_HEADS):
        sl = slice(h * HEAD_DIM, (h + 1) * HEAD_DIM)
        k = k_ref[:, sl]
        qb = q_ref[:, sl].astype(BF16)
        kb = k.astype(BF16)
        vb = v_ref[:, sl].astype(BF16)
        s = _dot_nt(qb, kb) * intra_ref[h]
        state = s_scr[h]
        o = _dot(s.astype(BF16), vb) + _dot(qb, state.astype(BF16)) * qdec_ref[h]
        kd = (k * kdec_ref[h]).astype(BF16)
        s_scr[h] = state * cdec_ref[h] + _dot_tn(kd, vb)
        y_ref[:, sl] = _groupnorm_gate(o, g_ref[:, sl], gn_ref[:, sl])

    @pl.when(c == pl.num_programs(1) - 1)
    def _():
        st_ref[0] = s_scr[...]


def _ret_prompt(proj, gn, batch, seq, chunk):
    nc = seq // chunk
    log_g = _log_decay()
    j = jnp.arange(chunk, dtype=F32)
    diff = j[:, None] - j[None, :]
    intra = jnp.where(diff >= 0, jnp.exp(log_g[:, None, None] * jnp.maximum(diff, 0.0)), 0.0)
    ones = jnp.ones((1, 1, HEAD_DIM), F32)
    qdec = jnp.exp(log_g[:, None] * (j[None, :] + 1.0))[:, :, None] * ones
    kdec = jnp.exp(log_g[:, None] * (chunk - 1.0 - j[None, :]))[:, :, None] * ones
    cdec = jnp.exp(log_g * chunk)[:, None, None] * ones
    w = RET_HEADS * HEAD_DIM
    col = lambda off: (lambda b, c: (b * nc + c, off // w))
    const3 = lambda b, c: (0, 0, 0)
    return pl.pallas_call(
        _ret_prompt_kernel,
        grid=(batch, nc),
        in_specs=[
            pl.BlockSpec((chunk, w), col(COL_RQ)),
            pl.BlockSpec((chunk, w), col(COL_RK)),
            pl.BlockSpec((chunk, w), col(COL_RV)),
            pl.BlockSpec((chunk, w), col(COL_RG)),
            pl.BlockSpec((1, w), lambda b, c: (0, 0)),
            pl.BlockSpec((RET_HEADS, chunk, chunk), const3),
            pl.BlockSpec((RET_HEADS, chunk, HEAD_DIM), const3),
            pl.BlockSpec((RET_HEADS, chunk, HEAD_DIM), const3),
            pl.BlockSpec((RET_HEADS, 1, HEAD_DIM), const3),
        ],
        out_specs=[
            pl.BlockSpec((chunk, w), lambda b, c: (b * nc + c, 0)),
            pl.BlockSpec((1, RET_HEADS, HEAD_DIM, HEAD_DIM), lambda b, c: (b, 0, 0, 0)),
        ],
        out_shape=[
            jax.ShapeDtypeStruct((batch * seq, w), BF16),
            jax.ShapeDtypeStruct((batch, RET_HEADS, HEAD_DIM, HEAD_DIM), F32),
        ],
        scratch_shapes=[pltpu.VMEM((RET_HEADS, HEAD_DIM, HEAD_DIM), F32)],
        compiler_params=_cparams("parallel", "arbitrary"),
        name="ret_prompt",
    )(proj, proj, proj, proj, gn, intra, qdec, kdec, cdec)


def _ret_sample_kernel(q_ref, k_ref, v_ref, g_ref, gn_ref, st_ref, intra_ref, qdec_ref, kdec_ref, cdec_ref,
                       y_ref, so_ref, *, n_batch, rows):
    k = k_ref[...]
    qb = q_ref[...].astype(BF16)
    kb = k.astype(BF16)
    vb = v_ref[...].astype(BF16)
    s = _dot_nt(qb, kb) * intra_ref[0]
    kd = k * kdec_ref[0]
    row_batch = lax.broadcasted_iota(jnp.int32, kd.shape, 0) // rows
    inter = []
    for b in range(n_batch):
        state = st_ref[b, 0]
        inter.append(_dot(qb[b * rows:(b + 1) * rows], state.astype(BF16)))
        kd_b = jnp.where(row_batch == b, kd, 0.0).astype(BF16)
        so_ref[b, 0] = state * cdec_ref[0] + _dot_tn(kd_b, vb)
    o = _dot(s.astype(BF16), vb) + jnp.concatenate(inter, axis=0) * qdec_ref[0]
    y_ref[...] = _groupnorm_gate(o, g_ref[...], gn_ref[...])


def _ret_sample(proj, gn, state, n_batch, rows, t_valid):
    m = n_batch * rows
    log_g = _log_decay()
    r = jnp.arange(m)
    t = (r % rows).astype(F32)
    same = (r[:, None] // rows) == (r[None, :] // rows)
    diff = t[:, None] - t[None, :]
    intra = jnp.where(same[None] & (diff >= 0)[None], jnp.exp(log_g[:, None, None] * jnp.maximum(diff, 0.0)[None]), 0.0)
    ones = jnp.ones((1, 1, HEAD_DIM), F32)
    qdec = jnp.exp(log_g[:, None] * (t[None, :] + 1.0))[:, :, None] * ones
    kdec = jnp.where(t[None, :] < t_valid, jnp.exp(log_g[:, None] * (t_valid - 1.0 - t[None, :])), 0.0)[:, :, None] * ones
    cdec = jnp.exp(log_g * t_valid)[:, None, None] * ones
    col = lambda off: (lambda h: (0, off // HEAD_DIM + h))
    per_head = lambda h: (h, 0, 0)
    return pl.pallas_call(
        functools.partial(_ret_sample_kernel, n_batch=n_batch, rows=rows),
        grid=(RET_HEADS,),
        in_specs=[
            pl.BlockSpec((m, HEAD_DIM), col(COL_RQ)),
            pl.BlockSpec((m, HEAD_DIM), col(COL_RK)),
            pl.BlockSpec((m, HEAD_DIM), col(COL_RV)),
            pl.BlockSpec((m, HEAD_DIM), col(COL_RG)),
            pl.BlockSpec((1, HEAD_DIM), lambda h: (0, h)),
            pl.BlockSpec((n_batch, 1, HEAD_DIM, HEAD_DIM), lambda h: (0, h, 0, 0)),
            pl.BlockSpec((1, m, m), per_head),
            pl.BlockSpec((1, m, HEAD_DIM), per_head),
            pl.BlockSpec((1, m, HEAD_DIM), per_head),
            pl.BlockSpec((1, 1, HEAD_DIM), per_head),
        ],
        out_specs=[
            pl.BlockSpec((m, HEAD_DIM), lambda h: (0, h)),
            pl.BlockSpec((n_batch, 1, HEAD_DIM, HEAD_DIM), lambda h: (0, h, 0, 0)),
        ],
        out_shape=[
            jax.ShapeDtypeStruct((m, RET_HEADS * HEAD_DIM), BF16),
            jax.ShapeDtypeStruct(state.shape, F32),
        ],
        compiler_params=_cparams("parallel"),
        name="ret_sample",
    )(proj, proj, proj, proj, gn, state, intra, qdec, kdec, cdec)


def _kth_largest(count_ge, next_above, smin, smax, n_adm, topk):
    few = n_adm < topk
    lo = jnp.where(few, -jnp.inf, smin)
    c_lo = jnp.where(few, float(topk), n_adm)
    hi = jnp.where(few, -jnp.inf, smax + jnp.maximum(jnp.abs(smax) * 1e-6, 1e-30))

    def bisect(_, state):
        lo, hi, c_lo = state
        mid = 0.5 * lo + 0.5 * hi
        c = count_ge(mid)
        take = c >= topk
        return jnp.where(take, mid, lo), jnp.where(take, hi, mid), jnp.where(take, c, c_lo)

    lo, hi, c_lo = lax.fori_loop(0, BISECT_STEPS, bisect, (lo, hi, c_lo))
    c_lo = jnp.where(few, float(topk), c_lo)

    def unfinished(state):
        return jnp.max(state[2]) > 0.0

    def step_up(state):
        lo, c_lo, active = state
        nxt, n_eq = next_above(lo)
        c_nxt = c_lo - n_eq
        move = active * jnp.where(c_nxt >= topk, 1.0, 0.0)
        lo = jnp.where(move > 0.0, nxt, lo)
        c_lo = jnp.where(move > 0.0, c_nxt, c_lo)
        return lo, c_lo, move * jnp.where(c_nxt > topk, 1.0, 0.0)

    return lax.while_loop(unfinished, step_up, (lo, c_lo, jnp.where(c_lo > topk, 1.0, 0.0)))[0]


def _dsa_prompt_kernel(dq_ref, iq_ref, iwq_ref, kall_ref, vall_ref, ikall_ref, o_ref,
                       kbf, vtb, ika, ikb, keys, z_a, z_b, zmax_a, zmax_b, m_scr, acc_scr, *, tq, topk):
    z_scr, zmax_scr = (z_a, z_b), (zmax_a, zmax_b)
    i = pl.program_id(1)
    n_chunks = kbf.shape[0]
    scale_log2e = HEAD_DIM ** -0.5 * math.log2(math.e)

    @pl.when(i == 0)
    def _():
        lane = lax.broadcasted_iota(jnp.int32, (tq, LANES), 1)
        for c in range(n_chunks):
            rows = slice(c * tq, (c + 1) * tq)
            kbf[c] = kall_ref[rows, :].astype(BF16)
            for kvh in range(DSA_KV_HEADS):
                sl = slice(kvh * HEAD_DIM, (kvh + 1) * HEAD_DIM)
                vtb[c, kvh, :HEAD_DIM, :] = vall_ref[rows, sl].T.astype(BF16)
                vtb[c, kvh, HEAD_DIM:, :] = jnp.ones((ROW_ALIGN, tq), BF16)
            a = ikall_ref[rows, :]
            ika[c] = jnp.where(lane < IDX_DH, a, 0.0).astype(BF16)
            ikb[c] = jnp.where(lane >= IDX_DH, pltpu.roll(a, IDX_DH, 1), 0.0).astype(BF16)

    w_t = iwq_ref[...].T
    iqb = iq_ref[...].astype(BF16)
    qb = (dq_ref[...] * scale_log2e).astype(BF16)
    t_col = i * tq + lax.broadcasted_iota(jnp.int32, (tq, tq), 1)
    s_row0 = lax.broadcasted_iota(jnp.int32, (tq, tq), 0)

    fold = lambda x: x.reshape(tq // SUBLANES, SUBLANES, tq)

    def score_chunk(c, carry):
        smax, smin = carry
        a = ika[c]
        b = ikb[c]
        acc = jnp.zeros((tq, tq), F32)
        for p in range(IDX_HEADS // 2):
            pair = iqb[:, p * LANES:(p + 1) * LANES]
            w0 = w_t[IDX_DH + 2 * p:IDX_DH + 2 * p + 1, :]
            w1 = w_t[IDX_DH + 2 * p + 1:IDX_DH + 2 * p + 2, :]
            acc = acc + jnp.maximum(_dot_nt(a, pair), 0.0) * w0
            acc = acc + jnp.maximum(_dot_nt(b, pair), 0.0) * w1
        admissible = s_row0 + c * tq <= t_col
        keys[c] = jnp.where(admissible, acc, -jnp.inf)
        smax = jnp.maximum(smax, fold(jnp.where(admissible, acc, -jnp.inf)).max(axis=0))
        smin = jnp.minimum(smin, fold(jnp.where(admissible, acc, jnp.inf)).min(axis=0))
        return smax, smin

    smax, smin = lax.fori_loop(0, i + 1, score_chunk, (jnp.full((SUBLANES, tq), -jnp.inf, F32),
                                                      jnp.full((SUBLANES, tq), jnp.inf, F32)))
    smax = smax.max(axis=0, keepdims=True)
    smin = smin.min(axis=0, keepdims=True)
    keys[i + 1] = jnp.full((tq, tq), -jnp.inf, F32)
    n_pairs = (i + 2) // 2

    def count_ge(t):
        def pair(j, cnt):
            for c in (2 * j, 2 * j + 1):
                cnt = cnt + fold(jnp.where(keys[c] >= t, 1.0, 0.0)).sum(axis=0)
            return cnt

        return lax.fori_loop(0, n_pairs, pair, jnp.zeros((SUBLANES, tq), F32)).sum(axis=0, keepdims=True)

    def next_above(lo):
        def pair(j, carry):
            nxt, n_eq = carry
            for c in (2 * j, 2 * j + 1):
                k = keys[c]
                nxt = jnp.minimum(nxt, fold(jnp.where(k > lo, k, jnp.inf)).min(axis=0))
                n_eq = n_eq + fold(jnp.where(k == lo, 1.0, 0.0)).sum(axis=0)
            return nxt, n_eq

        nxt, n_eq = lax.fori_loop(0, n_pairs, pair, (jnp.full((SUBLANES, tq), jnp.inf, F32),
                                                     jnp.zeros((SUBLANES, tq), F32)))
        return nxt.min(axis=0, keepdims=True), n_eq.sum(axis=0, keepdims=True)

    thr = _kth_largest(count_ge, next_above, smin, smax, (t_col[:1, :] + 1).astype(F32), topk)

    m_scr[...] = jnp.full_like(m_scr, NEG)
    acc_scr[...] = jnp.zeros_like(acc_scr)

    kv_of = lambda h: h // (DSA_HEADS // DSA_KV_HEADS)

    def logits(c, slot):
        bias = jnp.where(s_row0 + c * tq <= t_col, jnp.where(keys[c] >= thr, 0.0, NEG), NEG)
        kc = kbf[c]
        for h in range(DSA_HEADS):
            ksl = slice(kv_of(h) * HEAD_DIM, (kv_of(h) + 1) * HEAD_DIM)
            z = _dot_nt(kc[:, ksl], qb[:, h * HEAD_DIM:(h + 1) * HEAD_DIM]) + bias
            z_scr[slot][h] = z
            zmax_scr[slot][h] = z.max(axis=0, keepdims=True)

    def update(c, slot):
        for h in range(DSA_HEADS):
            m_old = m_scr[h]
            m_new = jnp.maximum(m_old, zmax_scr[slot][h])
            p = jnp.exp2(z_scr[slot][h] - m_new)
            acc_scr[h] = acc_scr[h] * jnp.exp2(m_old - m_new) + _dot(vtb[c, kv_of(h)], p.astype(BF16))
            m_scr[h] = m_new

    n = i + 1
    logits(0, 0)

    def two_chunks(k, carry):
        update(2 * k, 0)
        logits(2 * k + 1, 1)
        update(2 * k + 1, 1)
        logits(2 * k + 2, 0)
        return carry

    lax.fori_loop(0, (n - 1) // 2, two_chunks, 0)

    @pl.when(n % 2 == 1)
    def _():
        update(n - 1, 0)

    @pl.when(n % 2 == 0)
    def _():
        update(n - 2, 0)
        logits(n - 1, 1)
        update(n - 1, 1)

    for h in range(DSA_HEADS):
        o = acc_scr[h, :HEAD_DIM, :] * (1.0 / acc_scr[h, HEAD_DIM:HEAD_DIM + 1, :])
        o_ref[:, h * HEAD_DIM:(h + 1) * HEAD_DIM] = o.T.astype(BF16)


def _dsa_prompt(proj, batch, seq, tq):
    nq = seq // tq
    topk = min(TOPK_MAX, seq // 4)
    qw = DSA_HEADS * HEAD_DIM
    kvw = DSA_KV_HEADS * HEAD_DIM
    return pl.pallas_call(
        functools.partial(_dsa_prompt_kernel, tq=tq, topk=topk),
        grid=(batch, nq),
        in_specs=[
            pl.BlockSpec((tq, qw), lambda b, i: (b * nq + i, COL_DQ // qw)),
            pl.BlockSpec((tq, qw), lambda b, i: (b * nq + i, COL_IQ // qw)),
            pl.BlockSpec((tq, LANES), lambda b, i: (b * nq + i, COL_IK // LANES)),
            pl.BlockSpec((seq, kvw), lambda b, i: (b, COL_DK // kvw)),
            pl.BlockSpec((seq, kvw), lambda b, i: (b, COL_DV // kvw)),
            pl.BlockSpec((seq, LANES), lambda b, i: (b, COL_IK // LANES)),
        ],
        out_specs=pl.BlockSpec((tq, qw), lambda b, i: (b * nq + i, 0)),
        out_shape=jax.ShapeDtypeStruct((batch * seq, qw), BF16),
        scratch_shapes=[
            pltpu.VMEM((nq, tq, kvw), BF16),
            pltpu.VMEM((nq, DSA_KV_HEADS, HEAD_DIM + ROW_ALIGN, tq), BF16),
            pltpu.VMEM((nq, tq, LANES), BF16),
            pltpu.VMEM((nq, tq, LANES), BF16),
            pltpu.VMEM((nq + 1, tq, tq), F32),
            pltpu.VMEM((DSA_HEADS, tq, tq), F32),
            pltpu.VMEM((DSA_HEADS, tq, tq), F32),
            pltpu.VMEM((DSA_HEADS, 1, tq), F32),
            pltpu.VMEM((DSA_HEADS, 1, tq), F32),
            pltpu.VMEM((DSA_HEADS, 1, tq), F32),
            pltpu.VMEM((DSA_HEADS, HEAD_DIM + ROW_ALIGN, tq), F32),
        ],
        compiler_params=_cparams("parallel", "arbitrary"),
        name="dsa_prompt",
    )(proj, proj, proj, proj, proj, proj)


def _idx_sample_kernel(pt_ref, iq_ref, iw_ref, iknew_ref, *rest, pages):
    page_refs = rest[:pages]
    sp_ref, sn_ref = rest[pages:]
    iqb = iq_ref[0].astype(BF16)
    w = iw_ref[0]
    rows = iqb.shape[0] // IDX_HEADS

    def scores(ik_t):
        d = jnp.maximum(_dot(iqb, ik_t.astype(BF16)), 0.0) * w
        return d.reshape(rows, IDX_HEADS, LANES).sum(axis=1)

    for p in range(pages):
        sp_ref[0, :, p * PAGE_SIZE:(p + 1) * PAGE_SIZE] = scores(page_refs[p][0])
    t = lax.broadcasted_iota(jnp.int32, (rows, LANES), 0)
    j = lax.broadcasted_iota(jnp.int32, (rows, LANES), 1)
    sn_ref[0] = jnp.where(j <= t, scores(iknew_ref[0]), -jnp.inf)


def _idx_sample(page_table, iq, iw, ik_new, pool_ik, pages):
    n_batch, n_pages = page_table.shape
    rows16 = iq.shape[1]
    rows = rows16 // IDX_HEADS
    past = n_pages * PAGE_SIZE
    page_specs = [pl.BlockSpec((1, IDX_DH, PAGE_SIZE), functools.partial(
        lambda b, g, pt, k: (pt[b, g * pages + k], 0, 0), k=k)) for k in range(pages)]
    grid_spec = pltpu.PrefetchScalarGridSpec(
        num_scalar_prefetch=1,
        grid=(n_batch, n_pages // pages),
        in_specs=[
            pl.BlockSpec((1, rows16, IDX_DH), lambda b, g, pt: (b, 0, 0)),
            pl.BlockSpec((1, rows16, LANES), lambda b, g, pt: (b, 0, 0)),
            pl.BlockSpec((1, IDX_DH, PAGE_SIZE), lambda b, g, pt: (b, 0, 0)),
        ] + page_specs,
        out_specs=[
            pl.BlockSpec((1, rows, pages * PAGE_SIZE), lambda b, g, pt: (b, 0, g)),
            pl.BlockSpec((1, rows, LANES), lambda b, g, pt: (b, 0, 0)),
        ],
    )
    return pl.pallas_call(
        functools.partial(_idx_sample_kernel, pages=pages),
        grid_spec=grid_spec,
        out_shape=[
            jax.ShapeDtypeStruct((n_batch, rows, past), F32),
            jax.ShapeDtypeStruct((n_batch, rows, LANES), F32),
        ],
        compiler_params=_cparams("parallel", "arbitrary"),
        name="idx_sample",
    )(page_table, iq, iw, ik_new, *([pool_ik] * pages))


def _sel_sample_kernel(sp_ref, sn_ref, bp_ref, bn_ref, *, topk):
    n_batch, rows, past = sp_ref.shape
    sp = sp_ref[...].reshape(n_batch * rows, past)
    sn = sn_ref[...].reshape(n_batch * rows, LANES)
    t = lax.broadcasted_iota(jnp.int32, (n_batch * rows, LANES), 0) % rows
    j = lax.broadcasted_iota(jnp.int32, (n_batch * rows, LANES), 1)
    admissible_new = j <= t
    row_sum = lambda a: a.sum(axis=1, keepdims=True)
    row_min = lambda a: a.min(axis=1, keepdims=True)

    def count_ge(thr):
        return row_sum(jnp.where(sp >= thr, 1.0, 0.0)) + row_sum(jnp.where(sn >= thr, 1.0, 0.0))

    def next_above(lo):
        nxt = jnp.minimum(row_min(jnp.where(sp > lo, sp, jnp.inf)), row_min(jnp.where(sn > lo, sn, jnp.inf)))
        return nxt, row_sum(jnp.where(sp == lo, 1.0, 0.0)) + row_sum(jnp.where(sn == lo, 1.0, 0.0))

    smin = jnp.minimum(row_min(sp), row_min(jnp.where(admissible_new, sn, jnp.inf)))
    smax = jnp.maximum(sp.max(axis=1, keepdims=True), sn.max(axis=1, keepdims=True))
    n_adm = float(past) + row_sum(jnp.where(admissible_new, 1.0, 0.0))
    thr = _kth_largest(count_ge, next_above, smin, smax, n_adm, topk)
    bp_ref[...] = jnp.where(sp >= thr, 0.0, NEG).reshape(n_batch, rows, past)
    bn_ref[...] = jnp.where(admissible_new, jnp.where(sn >= thr, 0.0, NEG), NEG).reshape(n_batch, rows, LANES)


def _sel_sample(scores_past, scores_new, topk):
    full = lambda a: pl.BlockSpec(a.shape, lambda: (0,) * a.ndim)
    return pl.pallas_call(
        functools.partial(_sel_sample_kernel, topk=topk),
        in_specs=[full(scores_past), full(scores_new)],
        out_specs=[full(scores_past), full(scores_new)],
        out_shape=[jax.ShapeDtypeStruct(scores_past.shape, F32), jax.ShapeDtypeStruct(scores_new.shape, F32)],
        compiler_params=pltpu.CompilerParams(vmem_limit_bytes=VMEM_LIMIT),
        name="sel_sample",
    )(scores_past, scores_new)


def _dsa_sample_kernel(pt_ref, bp_ref, bn_ref, q_ref, knew_ref, vnew_ref, *rest, pages, steps):
    k_refs = rest[:pages]
    v_refs = rest[pages:2 * pages]
    o_ref, m_scr, l_scr, acc_scr = rest[2 * pages:]
    g = pl.program_id(0) % steps
    scale = HEAD_DIM ** -0.5
    group = DSA_HEADS // DSA_KV_HEADS

    @pl.when(g == 0)
    def _():
        m_scr[...] = jnp.full_like(m_scr, NEG)
        l_scr[...] = jnp.zeros_like(l_scr)
        acc_scr[...] = jnp.zeros_like(acc_scr)

    def head_rows(ref, kvh):
        return ref[0, pl.ds(kvh, PAGE_SIZE, stride=DSA_KV_HEADS), :].astype(BF16)

    def attend(k_pages, v_pages, bias):
        bias_g = jnp.concatenate([bias] * group, axis=0)
        kv_heads = range(DSA_KV_HEADS)
        qb = [q_ref[0, kvh].astype(BF16) for kvh in kv_heads]
        raw = [jnp.concatenate([_dot_nt(qb[kvh], head_rows(k, kvh)) for k in k_pages], axis=1) for kvh in kv_heads]
        pb, alpha = [], []
        for kvh in kv_heads:
            z = jnp.where(bias_g < 0.0, NEG, raw[kvh])
            m_old = m_scr[kvh]
            m_new = jnp.maximum(m_old, z.max(axis=1, keepdims=True))
            alpha.append(jnp.exp((m_old - m_new) * scale))
            p = jnp.exp((z - m_new) * scale)
            l_scr[kvh] = l_scr[kvh] * alpha[kvh] + p.sum(axis=1, keepdims=True)
            m_scr[kvh] = m_new
            pb.append(p.astype(BF16))
        for kvh in kv_heads:
            pv = _dot(pb[kvh][:, :PAGE_SIZE], head_rows(v_pages[0], kvh))
            for n in range(1, len(v_pages)):
                pv = pv + _dot(pb[kvh][:, n * PAGE_SIZE:(n + 1) * PAGE_SIZE], head_rows(v_pages[n], kvh))
            acc_scr[kvh] = acc_scr[kvh] * alpha[kvh] + pv

    attend(k_refs, v_refs, bp_ref[0])

    @pl.when(g == steps - 1)
    def _():
        attend([knew_ref], [vnew_ref], bn_ref[0])
        for kvh in range(DSA_KV_HEADS):
            o_ref[0, kvh] = (acc_scr[kvh] * (1.0 / l_scr[kvh])).astype(BF16)


def _dsa_sample(page_table, bias_past, bias_new, q, k_new, v_new, pool_k, pool_v, steps):
    n_batch, n_pages = page_table.shape
    pages = n_pages // steps
    rows = bias_past.shape[1]
    page_rows = DSA_KV_HEADS * PAGE_SIZE
    grows = q.shape[2]
    page_spec = lambda k: pl.BlockSpec((1, page_rows, HEAD_DIM), functools.partial(
        lambda i, pt, k: (pt[i // steps, (i % steps) * pages + k], 0, 0), k=k))
    per_batch3 = lambda i, pt: (i // steps, 0, 0)
    per_batch4 = lambda i, pt: (i // steps, 0, 0, 0)
    in_specs = [
        pl.BlockSpec((1, rows, pages * PAGE_SIZE), lambda i, pt: (i // steps, 0, i % steps)),
        pl.BlockSpec((1, rows, LANES), per_batch3),
        pl.BlockSpec((1, DSA_KV_HEADS, grows, HEAD_DIM), per_batch4),
        pl.BlockSpec((1, page_rows, HEAD_DIM), per_batch3),
        pl.BlockSpec((1, page_rows, HEAD_DIM), per_batch3),
    ] + [page_spec(k) for k in range(pages)] * 2
    grid_spec = pltpu.PrefetchScalarGridSpec(
        num_scalar_prefetch=1,
        grid=(n_batch * steps,),
        in_specs=in_specs,
        out_specs=pl.BlockSpec((1, DSA_KV_HEADS, grows, HEAD_DIM), per_batch4),
        scratch_shapes=[
            pltpu.VMEM((DSA_KV_HEADS, grows, 1), F32),
            pltpu.VMEM((DSA_KV_HEADS, grows, 1), F32),
            pltpu.VMEM((DSA_KV_HEADS, grows, HEAD_DIM), F32),
        ],
    )
    return pl.pallas_call(
        functools.partial(_dsa_sample_kernel, pages=pages, steps=steps),
        grid_spec=grid_spec,
        out_shape=jax.ShapeDtypeStruct((n_batch, DSA_KV_HEADS, grows, HEAD_DIM), BF16),
        compiler_params=_cparams("arbitrary"),
        name="dsa_sample",
    )(page_table, bias_past, bias_new, q, k_new, v_new, *([pool_k] * pages), *([pool_v] * pages))


def _merge_kernel(x_ref, ry_ref, do_ref, ga_ref, gb_ref, wr_ref, wd_ref, wo_ref, o_ref):
    ya = _dot(ry_ref[...], wr_ref[...])
    yb = _dot(do_ref[...], wd_ref[...])
    merged = _sigmoid(ga_ref[...]) * ya + _sigmoid(gb_ref[...]) * yb
    o_ref[...] = x_ref[...] + _dot(merged.astype(BF16), wo_ref[...])


def _merge(x, ret_y, dsa_o, proj, w_ret_out, w_dsa_out, w_o, tm):
    m, d = x.shape
    w = ret_y.shape[1]
    const = lambda i: (0, 0)
    resident = dict(pipeline_mode=pl.Buffered(1))
    return pl.pallas_call(
        _merge_kernel,
        grid=(m // tm,),
        in_specs=[
            pl.BlockSpec((tm, d), lambda i: (i, 0)),
            pl.BlockSpec((tm, w), lambda i: (i, 0)),
            pl.BlockSpec((tm, w), lambda i: (i, 0)),
            pl.BlockSpec((tm, d), lambda i: (i, COL_GA // d)),
            pl.BlockSpec((tm, d), lambda i: (i, COL_GB // d)),
            pl.BlockSpec((w, d), const, **resident),
            pl.BlockSpec((w, d), const, **resident),
            pl.BlockSpec((d, d), const, **resident),
        ],
        out_specs=pl.BlockSpec((tm, d), lambda i: (i, 0)),
        out_shape=jax.ShapeDtypeStruct((m, d), F32),
        compiler_params=_cparams("parallel"),
        name="merge_out",
    )(x, ret_y, dsa_o, proj, proj, w_ret_out, w_dsa_out, w_o)


def _mem_attn_kernel(h_ref, g_ref, wq_ref, mk_ref, mv_ref, wo_ref, o_ref, *, rows_per_batch):
    h = h_ref[...]
    qm = _dot(_rmsnorm_bf16(h, g_ref[...]), wq_ref[...])
    tm = h.shape[0]
    nk = mk_ref.shape[0]
    scale = HEAD_DIM ** -0.5
    if rows_per_batch is not None:
        rb = lax.broadcasted_iota(jnp.int32, (tm, nk), 0) // rows_per_batch
        kb = lax.broadcasted_iota(jnp.int32, (tm, nk), 1) // MEM_LEN
        same = rb == kb
    outs = []
    for hd in range(MEM_HEADS):
        sl = slice(hd * HEAD_DIM, (hd + 1) * HEAD_DIM)
        z = _dot_nt(qm[:, sl].astype(BF16), mk_ref[:, sl].astype(BF16)) * scale
        if rows_per_batch is not None:
            z = jnp.where(same, z, NEG)
        p = jnp.exp(z - z.max(axis=-1, keepdims=True))
        pv = _dot(p.astype(BF16), mv_ref[:, sl].astype(BF16))
        outs.append(pv * (1.0 / p.sum(axis=-1, keepdims=True)))
    om = jnp.concatenate(outs, axis=1).astype(BF16)
    o_ref[...] = h + _dot(om, wo_ref[...])


def _mem_attn(h, g, w_mq, mk, mv, w_mo, *, tm, batch_tiles, mk_col, mv_col, nk, rows_per_batch):
    m, d = h.shape
    const = lambda i: (0, 0)
    if batch_tiles:
        kmap = lambda col: (lambda i: (i // batch_tiles, col))
    else:
        kmap = lambda col: (lambda i: (0, col))
    return pl.pallas_call(
        functools.partial(_mem_attn_kernel, rows_per_batch=rows_per_batch),
        grid=(m // tm,),
        in_specs=[
            pl.BlockSpec((tm, d), lambda i: (i, 0)),
            pl.BlockSpec((1, d), const),
            pl.BlockSpec((d, MEM_W), const),
            pl.BlockSpec((nk, MEM_W), kmap(mk_col)),
            pl.BlockSpec((nk, MEM_W), kmap(mv_col)),
            pl.BlockSpec((MEM_W, d), const),
        ],
        out_specs=pl.BlockSpec((tm, d), lambda i: (i, 0)),
        out_shape=jax.ShapeDtypeStruct((m, d), F32),
        compiler_params=_cparams("parallel"),
        name="mem_attn",
    )(h, g, w_mq, mk, mv, w_mo)


def _mlp_kernel(h_ref, g_ref, wu_ref, wd_ref, gf_ref, o_ref, *rest):
    u_ref = rest[-1]
    j = pl.program_id(1)

    @pl.when(j == 0)
    def _():
        u_ref[...] = _rmsnorm_bf16(h_ref[...], g_ref[...])
        o_ref[...] = h_ref[...]

    wu, wd = wu_ref[...], wd_ref[...]
    if len(rest) == 3:
        wu, wd = wu.astype(BF16), wd.astype(BF16)
        rest[0][...] = wu
        rest[1][...] = wd
    a = jnp.maximum(_dot(u_ref[...], wu), 0.0)
    o_ref[...] += _dot((a * a).astype(BF16), wd)

    @pl.when(j == pl.num_programs(1) - 1)
    def _():
        y = o_ref[...]
        ms = jnp.mean(y * y, axis=-1, keepdims=True)
        o_ref[...] = (y * lax.rsqrt(ms + EPS)) * gf_ref[...]


def _mlp(h, g, w_up, w_down, g_final, tm, tf):
    m, d = h.shape
    ff = w_up.shape[1]
    emit = w_up.dtype != BF16
    assert not emit or m == tm, "weight copies are written once, by a call with one row tile"
    wu_spec = pl.BlockSpec((d, tf), lambda i, j: (0, j))
    wd_spec = pl.BlockSpec((tf, d), lambda i, j: (j, 0))
    out_specs = [pl.BlockSpec((tm, d), lambda i, j: (i, 0))]
    out_shape = [jax.ShapeDtypeStruct((m, d), F32)]
    if emit:
        out_specs += [wu_spec, wd_spec]
        out_shape += [jax.ShapeDtypeStruct(w_up.shape, BF16), jax.ShapeDtypeStruct(w_down.shape, BF16)]
    out = pl.pallas_call(
        _mlp_kernel,
        grid=(m // tm, ff // tf),
        in_specs=[
            pl.BlockSpec((tm, d), lambda i, j: (i, 0)),
            pl.BlockSpec((1, d), lambda i, j: (0, 0)),
            wu_spec,
            wd_spec,
            pl.BlockSpec((1, d), lambda i, j: (0, 0)),
        ],
        out_specs=out_specs,
        out_shape=out_shape,
        scratch_shapes=[pltpu.VMEM((tm, d), BF16)],
        compiler_params=_cparams("parallel", "arbitrary"),
        name="mlp_final",
    )(h, g, w_up, w_down, g_final)
    return out if emit else out[0]


def _pick_tile(n, pref):
    t = min(n, pref)
    while n % t:
        t //= 2
    return t


def kernel(x_prompt, x_sample, mem_prompt, cache_k, cache_v, cache_idx_k, state_ret, cache_mem_k, cache_mem_v,
           page_table, g_mix, w_in, gn_ret, w_ret_out, w_dsa_out, w_o, g_mem, g_memkv, w_mq, w_mk, w_mv, w_mo,
           g_mlp, w_up, w_down, g_final):
    assert w_in.shape[0] == 1, "one layer"
    batch, seq, d = x_prompt.shape
    n_dec, t_dec, _ = x_sample.shape
    n_pages = page_table.shape[1]
    past = n_pages * PAGE_SIZE
    mem_len = mem_prompt.shape[1]
    assert mem_len == MEM_LEN and t_dec <= SUBLANES
    row = lambda v: v.reshape(1, -1)

    w_ret_out_b, w_dsa_out_b, w_o_b = (w[0].astype(BF16) for w in (w_ret_out, w_dsa_out, w_o))
    w_mq_b, w_mo_b = w_mq[0].astype(BF16), w_mo[0].astype(BF16)
    w_mkv_b = jnp.concatenate([w_mk[0], w_mv[0]], axis=1).astype(BF16)
    g_mix_r, gn_r, g_mem_r, g_memkv_r, g_mlp_r, g_final_r = (
        row(v) for v in (g_mix[0], gn_ret[0], g_mem[0], g_memkv[0], g_mlp[0], g_final))

    rows = SUBLANES
    m_s = n_dec * rows
    xs = jnp.pad(x_sample, ((0, 0), (0, rows - t_dec), (0, 0))).reshape(m_s, d)
    proj_s, k_rows_s, v_rows_s, w_proj_b = _norm_proj_rope(xs, g_mix_r, jnp.swapaxes(w_in[0], 0, 1),
                                                           _rope_tables(past, m_s, rows), m_s, 1)

    m_p = batch * seq
    xp = x_prompt.reshape(m_p, d)
    tm_proj = _pick_tile(seq, 1024)
    proj_p, k_rows_p, v_rows_p = _norm_proj_rope(xp, g_mix_r, w_proj_b, _rope_tables(0, seq, seq), tm_proj,
                                                 seq // tm_proj)
    ret_y_p, ret_state_p = _ret_prompt(proj_p, gn_r, batch, seq, _pick_tile(seq, 256))
    dsa_o_p = _dsa_prompt(proj_p, batch, seq, _pick_tile(seq, 256))
    kv_p = _norm_proj(mem_prompt.reshape(batch * mem_len, d), g_memkv_r, w_mkv_b, mem_len, PROJ_TN)
    tm_t = _pick_tile(seq, 256)
    h_p = _merge(xp, ret_y_p, dsa_o_p, proj_p, w_ret_out_b, w_dsa_out_b, w_o_b, tm_t)
    h_p = _mem_attn(h_p, g_mem_r, w_mq_b, kv_p, kv_p, w_mo_b, tm=tm_t, batch_tiles=seq // tm_t,
                    mk_col=0, mv_col=1, nk=mem_len, rows_per_batch=None)

    ret_y_s, ret_state_s = _ret_sample(proj_s, gn_r, state_ret[0], n_dec, rows, t_dec)

    proj_s3 = proj_s.reshape(n_dec, rows, N_PROJ)
    iq_s = proj_s3[:, :, COL_IQ:COL_IQ + IDX_HEADS * IDX_DH].reshape(n_dec, rows * IDX_HEADS, IDX_DH)
    iw_s = proj_s3[:, :, COL_IK + IDX_DH:COL_IK + IDX_DH + IDX_HEADS].reshape(n_dec, rows * IDX_HEADS, 1)
    iw_s = jnp.broadcast_to(iw_s, (n_dec, rows * IDX_HEADS, LANES))
    kvw = DSA_KV_HEADS * HEAD_DIM
    page_rows = DSA_KV_HEADS * PAGE_SIZE
    pad_keys = lambda a: jnp.pad(a, ((0, 0), (0, PAGE_SIZE - rows), (0, 0)))
    ik_new_t = jnp.swapaxes(pad_keys(proj_s3[:, :, COL_IK:COL_IK + IDX_DH]), 1, 2)
    pool_ik_t = jnp.swapaxes(cache_idx_k, 2, 3).reshape(-1, IDX_DH, PAGE_SIZE)
    new_page = lambda a: jnp.pad(a.reshape(n_dec, DSA_KV_HEADS * rows, HEAD_DIM),
                                 ((0, 0), (0, page_rows - DSA_KV_HEADS * rows), (0, 0)))
    k_new, v_new = new_page(k_rows_s), new_page(v_rows_s)
    scores_past, scores_new = _idx_sample(page_table, iq_s, iw_s, ik_new_t, pool_ik_t, _pick_tile(n_pages, 32))
    group = DSA_HEADS // DSA_KV_HEADS
    dq_s = proj_s3[:, :, COL_DQ:COL_DQ + DSA_HEADS * HEAD_DIM].reshape(n_dec, rows, DSA_KV_HEADS, group, HEAD_DIM)
    dq_s = dq_s.transpose(0, 2, 3, 1, 4).reshape(n_dec, DSA_KV_HEADS, group * rows, HEAD_DIM)
    bias_past, bias_new = _sel_sample(scores_past, scores_new, min(TOPK_MAX, (past + t_dec) // 4))
    dsa_o_s = _dsa_sample(page_table, bias_past, bias_new, dq_s, k_new, v_new,
                          cache_k.reshape(-1, page_rows, HEAD_DIM), cache_v.reshape(-1, page_rows, HEAD_DIM),
                          n_pages // _pick_tile(n_pages, MAX_PAGES_PER_STEP))
    dsa_o_s = dsa_o_s.reshape(n_dec, DSA_KV_HEADS, group, rows, HEAD_DIM).transpose(0, 3, 1, 2, 4)
    dsa_o_s = dsa_o_s.reshape(m_s, DSA_HEADS * HEAD_DIM)

    h_s = _merge(xs, ret_y_s, dsa_o_s, proj_s, w_ret_out_b, w_dsa_out_b, w_o_b, m_s)
    h_s = _mem_attn(h_s, g_mem_r, w_mq_b, cache_mem_k[0].reshape(n_dec * mem_len, MEM_W),
                    cache_mem_v[0].reshape(n_dec * mem_len, MEM_W), w_mo_b, tm=m_s, batch_tiles=0,
                    mk_col=0, mv_col=0, nk=n_dec * mem_len, rows_per_batch=rows)
    y_s, w_up_b, w_down_b = _mlp(h_s, g_mlp_r, w_up[0], w_down[0], g_final_r, m_s, 512)
    y_p = _mlp(h_p, g_mlp_r, w_up_b, w_down_b, g_final_r, _pick_tile(seq, 1024), 512)

    def rows_p(col, width, tail):
        return proj_p[:, col:col + width].reshape((1, batch, seq) + tail)

    def rows_s(col, width, tail):
        return proj_s3[:, :t_dec, col:col + width].reshape((1, n_dec, t_dec) + tail)

    kv_p_rows = lambda a: a.reshape(1, batch, seq, DSA_KV_HEADS, HEAD_DIM)
    kv_s_rows = lambda a: a.reshape(1, n_dec, rows, DSA_KV_HEADS, HEAD_DIM)[:, :, :t_dec]
    return (
        y_p.reshape(batch, seq, d),
        y_s.reshape(n_dec, rows, d)[:, :t_dec],
        ret_state_p[None],
        kv_p_rows(k_rows_p),
        kv_p_rows(v_rows_p),
        rows_p(COL_IK, IDX_DH, (IDX_DH,)),
        kv_p[:, :MEM_W].reshape(1, batch, mem_len, MEM_HEADS, HEAD_DIM),
        kv_p[:, MEM_W:].reshape(1, batch, mem_len, MEM_HEADS, HEAD_DIM),
        ret_state_s[None],
        kv_s_rows(k_rows_s),
        kv_s_rows(v_rows_s),
        rows_s(COL_IK, IDX_DH, (IDX_DH,)),
    )
```

```python
import functools
import math

import jax
import jax.numpy as jnp
import numpy as np
from jax import lax
from jax.experimental import pallas as pl
from jax.experimental.pallas import tpu as pltpu

F32 = jnp.float32
BF16 = jnp.bfloat16

D_MODEL = 2048
RET_HEADS = 8
HEAD_DIM = 128
DSA_HEADS = 8
DSA_KV_HEADS = 2
IDX_HEADS = 16
IDX_DH = 64
TOPK_MAX = 256
PAGE_SIZE = 128
MEM_LEN = 256
MEM_HEADS = 4
MEM_W = MEM_HEADS * HEAD_DIM
D_FF = 4 * D_MODEL
RET_THETA = 10000.0
ROPE_THETA = 500000.0
EPS = 1e-6
LANES = 128
SUBLANES = 8
ROW_ALIGN = 16
VMEM_LIMIT = 56 * 1024 * 1024
NEG = -1e30
BISECT_STEPS = 16
MAX_PAGES_PER_STEP = 32

COL_GA = 0
COL_GB = 2048
COL_RQ = 4096
COL_RK = 5120
COL_RV = 6144
COL_RG = 7168
COL_DQ = 8192
COL_IQ = 9216
COL_DK = 10240
COL_DV = 10496
COL_IK = 10752
N_PROJ = 11264
PROJ_TN = 512
PROJ_TN_BF16 = 1024
ROPE_NONE, ROPE_RET, ROPE_RETK, ROPE_DSA, ROPE_IDX, ROPE_IKW = range(6)
COL_ROPE = ((ROPE_NONE,) * 32 + (ROPE_RET,) * 8 + (ROPE_RETK,) * 8 + (ROPE_NONE,) * 16 + (ROPE_DSA,) * 8
            + (ROPE_IDX,) * 8 + (ROPE_DSA,) * 2 + (ROPE_NONE,) * 2 + (ROPE_IKW,) + (ROPE_NONE,) * 3)
TAB_RET_C, TAB_RET_S, TAB_DSA_C, TAB_DSA_S, TAB_IDX_C, TAB_IDX_S = range(6)
TAB_W = 6 * LANES
DSA_ROT = HEAD_DIM // 4
IDX_ROT = IDX_DH // 4


def _cparams(*sem):
    return pltpu.CompilerParams(dimension_semantics=sem, vmem_limit_bytes=VMEM_LIMIT)


def _dot(a, b):
    return jnp.dot(a, b, preferred_element_type=F32)


def _dot_nt(a, b):
    return lax.dot_general(a, b, (((1,), (1,)), ((), ())), preferred_element_type=F32)


def _dot_tn(a, b):
    return lax.dot_general(a, b, (((0,), (0,)), ((), ())), preferred_element_type=F32)


def _rmsnorm_bf16(x, g):
    ms = jnp.mean(x * x, axis=-1, keepdims=True)
    return ((x * lax.rsqrt(ms + EPS)) * g).astype(BF16)


def _sigmoid(x):
    return 1.0 / (1.0 + jnp.exp(-x))


def _tab(tab_ref, which):
    return tab_ref[:, which * LANES:(which + 1) * LANES]


def _rope_cols(o_ref, g, c, s, half, period):
    a = o_ref[:, g * LANES:(g + 1) * LANES]
    if 2 * half == LANES:
        partner = pltpu.roll(a, half, 1)
    else:
        lane = lax.broadcasted_iota(jnp.int32, (1, LANES), 1)
        first = (lane & (period - 1)) < half
        partner = jnp.where(first, pltpu.roll(a, LANES - half, 1), pltpu.roll(a, half, 1))
    o_ref[:, g * LANES:(g + 1) * LANES] = a * c + partner * s


def _finish_block(dst_ref, tab_ref, plan, kv_group, kr_ref, vr_ref):
    for g, kind in enumerate(plan):
        rope = functools.partial(_rope_cols, dst_ref, g)
        if kind == ROPE_RET:
            rope(_tab(tab_ref, TAB_RET_C), _tab(tab_ref, TAB_RET_S), LANES // 2, LANES)
        elif kind == ROPE_RETK:
            scale = HEAD_DIM ** -0.5
            rope(_tab(tab_ref, TAB_RET_C) * scale, _tab(tab_ref, TAB_RET_S) * scale, LANES // 2, LANES)
        elif kind == ROPE_DSA:
            rope(_tab(tab_ref, TAB_DSA_C), _tab(tab_ref, TAB_DSA_S), DSA_ROT // 2, LANES)
        elif kind == ROPE_IDX:
            rope(_tab(tab_ref, TAB_IDX_C), _tab(tab_ref, TAB_IDX_S), IDX_ROT // 2, IDX_DH)
        elif kind == ROPE_IKW:
            lane = lax.broadcasted_iota(jnp.int32, (1, LANES), 1)
            is_ik = lane < IDX_DH
            rope(jnp.where(is_ik, _tab(tab_ref, TAB_IDX_C), 1.0), jnp.where(is_ik, _tab(tab_ref, TAB_IDX_S), 0.0),
                 IDX_ROT // 2, IDX_DH)
    if kv_group is not None:
        rows = dst_ref.shape[0]
        for kvh in range(DSA_KV_HEADS):
            rows_of_head = pl.ds(kvh, rows, stride=DSA_KV_HEADS)
            gk, gv = kv_group + kvh, kv_group + DSA_KV_HEADS + kvh
            kr_ref[rows_of_head, :] = dst_ref[:, gk * LANES:(gk + 1) * LANES]
            vr_ref[rows_of_head, :] = dst_ref[:, gv * LANES:(gv + 1) * LANES]


def _norm_proj_kernel(x_ref, g_ref, w_ref, tab_ref, o_ref, u_ref, *, kr_ref=None, vr_ref=None, wq_ref=None):
    j = pl.program_id(1)

    @pl.when(j == 0)
    def _():
        u_ref[...] = _rmsnorm_bf16(x_ref[...], g_ref[...])

    if tab_ref is None:
        o_ref[...] = _dot(u_ref[...], w_ref[...])
        return
    wb = w_ref[...]
    if wb.dtype != BF16:
        wb = wb.astype(BF16)
        wq_ref[...] = wb
    o_ref[...] = _dot_nt(u_ref[...], wb)
    groups = o_ref.shape[1] // LANES

    plans = {}
    for b in range(N_PROJ // (groups * LANES)):
        plans.setdefault(COL_ROPE[b * groups:(b + 1) * groups], []).append(b)
    kv_block, kv_group = divmod(COL_DK // LANES, groups)
    for plan, blocks in plans.items():
        if all(kind == ROPE_NONE for kind in plan):
            continue
        cond = functools.reduce(jnp.logical_or, [j == b for b in blocks])

        @pl.when(cond)
        def _(plan=plan, blocks=blocks):
            _finish_block(o_ref, tab_ref, plan, kv_group if kv_block in blocks else None, kr_ref, vr_ref)


def _norm_proj(x, g, w, tm, tn):
    m, d = x.shape
    n = w.shape[1]

    def body(x_ref, g_ref, w_ref, o_ref, u_ref):
        _norm_proj_kernel(x_ref, g_ref, w_ref, None, o_ref, u_ref)

    return pl.pallas_call(
        body,
        grid=(m // tm, n // tn),
        in_specs=[
            pl.BlockSpec((tm, d), lambda i, j: (i, 0)),
            pl.BlockSpec((1, d), lambda i, j: (0, 0)),
            pl.BlockSpec((d, tn), lambda i, j: (0, j)),
        ],
        out_specs=pl.BlockSpec((tm, tn), lambda i, j: (i, j)),
        out_shape=jax.ShapeDtypeStruct((m, n), F32),
        scratch_shapes=[pltpu.VMEM((tm, d), BF16)],
        compiler_params=_cparams("parallel", "arbitrary"),
        name="norm_proj",
    )(x, g, w)


_W_IN_ROW = {"rq": 0, "rk": 1024, "rv": 2048, "rg": 3072, "dq": 4096, "dk": 5120, "iq": 5632, "ik": 6656,
             "ga": 6736, "gb": 8784}
PROJ_SRC_ROWS = tuple(_W_IN_ROW[name] + PROJ_TN * k for name, nblk in (
    ("ga", 4), ("gb", 4), ("rq", 2), ("rk", 2), ("rv", 2), ("rg", 2), ("dq", 2), ("iq", 2), ("dk", 1), ("ik", 1))
    for k in range(nblk))


def _norm_proj_rope(x, g, w_t, tab, tm, pos_blocks):
    m, d = x.shape
    from_f32 = w_t.dtype != BF16
    tn = PROJ_TN if from_f32 else PROJ_TN_BF16
    kv_spec = pl.BlockSpec((DSA_KV_HEADS * tm, HEAD_DIM), lambda i, j, *_: (i, 0))
    kv_rows = jax.ShapeDtypeStruct((DSA_KV_HEADS * m, HEAD_DIM), F32)
    in_specs = [
        pl.BlockSpec((tm, d), lambda i, j, *_: (i, 0)),
        pl.BlockSpec((1, d), lambda i, j, *_: (0, 0)),
        None,
        pl.BlockSpec((tm, TAB_W), lambda i, j, *_: (i % pos_blocks, 0)),
    ]
    out_specs = [pl.BlockSpec((tm, tn), lambda i, j, *_: (i, j)), kv_spec, kv_spec]
    out_shape = [jax.ShapeDtypeStruct((m, N_PROJ), F32), kv_rows, kv_rows]
    scratch = [pltpu.VMEM((tm, d), BF16)]
    if from_f32:
        assert m == tm and all(r % ROW_ALIGN == 0 for r in PROJ_SRC_ROWS)
        src_rows = jnp.asarray([r // ROW_ALIGN for r in PROJ_SRC_ROWS], jnp.int32)
        in_specs[2] = pl.BlockSpec((pl.Element(tn), pl.Element(d)), lambda i, j, src: (src[j] * ROW_ALIGN, 0))
        out_specs.append(pl.BlockSpec((tn, d), lambda i, j, src: (j, 0)))
        out_shape.append(jax.ShapeDtypeStruct((N_PROJ, d), BF16))

        def body(src_ref, x_ref, g_ref, w_ref, tab_ref, o_ref, kr_ref, vr_ref, wq_ref, u_ref):
            _norm_proj_kernel(x_ref, g_ref, w_ref, tab_ref, o_ref, u_ref, kr_ref=kr_ref, vr_ref=vr_ref, wq_ref=wq_ref)

        grid_spec = pltpu.PrefetchScalarGridSpec(num_scalar_prefetch=1, grid=(1, N_PROJ // tn), in_specs=in_specs,
                                                 out_specs=out_specs, scratch_shapes=scratch)
        args = (src_rows, x, g, w_t, tab)
    else:
        in_specs[2] = pl.BlockSpec((tn, d), lambda i, j: (j, 0))

        def body(x_ref, g_ref, w_ref, tab_ref, o_ref, kr_ref, vr_ref, u_ref):
            _norm_proj_kernel(x_ref, g_ref, w_ref, tab_ref, o_ref, u_ref, kr_ref=kr_ref, vr_ref=vr_ref)

        grid_spec = pltpu.PrefetchScalarGridSpec(num_scalar_prefetch=0, grid=(m // tm, N_PROJ // tn),
                                                 in_specs=in_specs, out_specs=out_specs, scratch_shapes=scratch)
        args = (x, g, w_t, tab)
    return pl.pallas_call(
        body,
        grid_spec=grid_spec,
        out_shape=out_shape,
        compiler_params=_cparams("parallel", "arbitrary"),
        name="norm_proj_rope",
    )(*args)


@functools.lru_cache(maxsize=None)
def _rope_tables(start, count, period):
    pos = (start + np.arange(count) % period).astype(np.float64)[:, None]

    def cs(half, theta):
        inv = theta ** (-np.arange(half, dtype=np.float64) / half)
        ang = pos * inv[None, :]
        return np.cos(ang), np.sin(ang)

    ones = lambda n: np.ones((count, n))
    zeros = lambda n: np.zeros((count, n))
    c, s = cs(HEAD_DIM // 2, RET_THETA)
    ret_c = np.concatenate([c, c], 1)
    ret_s = np.concatenate([-s, s], 1)
    c, s = cs(DSA_ROT // 2, ROPE_THETA)
    dsa_c = np.concatenate([c, c, ones(LANES - DSA_ROT)], 1)
    dsa_s = np.concatenate([-s, s, zeros(LANES - DSA_ROT)], 1)
    c, s = cs(IDX_ROT // 2, ROPE_THETA)
    idx_c = np.tile(np.concatenate([c, c, ones(IDX_DH - IDX_ROT)], 1), (1, 2))
    idx_s = np.tile(np.concatenate([-s, s, zeros(IDX_DH - IDX_ROT)], 1), (1, 2))
    return np.concatenate([ret_c, ret_s, dsa_c, dsa_s, idx_c, idx_s], 1).astype(np.float32)


def _log_decay():
    return jnp.log1p(-jnp.exp2(-5.0 - jnp.arange(RET_HEADS, dtype=F32)))


def _groupnorm_gate(o, gate, gn):
    mu = jnp.mean(o, axis=-1, keepdims=True)
    d = o - mu
    var = jnp.mean(d * d, axis=-1, keepdims=True)
    n = d * lax.rsqrt(var + EPS) * gn
    return (gate * _sigmoid(gate) * n).astype(BF16)


def _ret_prompt_kernel(q_ref, k_ref, v_ref, g_ref, gn_ref, intra_ref, qdec_ref, kdec_ref, cdec_ref,
                       y_ref, st_ref, s_scr):
    c = pl.program_id(1)

    @pl.when(c == 0)
    def _():
        s_scr[...] = jnp.zeros_like(s_scr)

    for h in range(RET_HEADS):
        sl = slice(h * HEAD_DIM, (h + 1) * HEAD_DIM)
        k = k_ref[:, sl]
        qb = q_ref[:, sl].astype(BF16)
        kb = k.astype(BF16)
        vb = v_ref[:, sl].astype(BF16)
        s = _dot_nt(qb, kb) * intra_ref[h]
        state = s_scr[h]
        o = _dot(s.astype(BF16), vb) + _dot(qb, state.astype(BF16)) * qdec_ref[h]
        kd = (k * kdec_ref[h]).astype(BF16)
        s_scr[h] = state * cdec_ref[h] + _dot_tn(kd, vb)
        y_ref[:, sl] = _groupnorm_gate(o, g_ref[:, sl], gn_ref[:, sl])

    @pl.when(c == pl.num_programs(1) - 1)
    def _():
        st_ref[0] = s_scr[...]


def _ret_prompt(proj, gn, batch, seq, chunk):
    nc = seq // chunk
    log_g = _log_decay()
    j = jnp.arange(chunk, dtype=F32)
    diff = j[:, None] - j[None, :]
    intra = jnp.where(diff >= 0, jnp.exp(log_g[:, None, None] * jnp.maximum(diff, 0.0)), 0.0)
    ones = jnp.ones((1, 1, HEAD_DIM), F32)
    qdec = jnp.exp(log_g[:, None] * (j[None, :] + 1.0))[:, :, None] * ones
    kdec = jnp.exp(log_g[:, None] * (chunk - 1.0 - j[None, :]))[:, :, None] * ones
    cdec = jnp.exp(log_g * chunk)[:, None, None] * ones
    w = RET_HEADS * HEAD_DIM
    col = lambda off: (lambda b, c: (b * nc + c, off // w))
    const3 = lambda b, c: (0, 0, 0)
    return pl.pallas_call(
        _ret_prompt_kernel,
        grid=(batch, nc),
        in_specs=[
            pl.BlockSpec((chunk, w), col(COL_RQ)),
            pl.BlockSpec((chunk, w), col(COL_RK)),
            pl.BlockSpec((chunk, w), col(COL_RV)),
            pl.BlockSpec((chunk, w), col(COL_RG)),
            pl.BlockSpec((1, w), lambda b, c: (0, 0)),
            pl.BlockSpec((RET_HEADS, chunk, chunk), const3),
            pl.BlockSpec((RET_HEADS, chunk, HEAD_DIM), const3),
            pl.BlockSpec((RET_HEADS, chunk, HEAD_DIM), const3),
            pl.BlockSpec((RET_HEADS, 1, HEAD_DIM), const3),
        ],
        out_specs=[
            pl.BlockSpec((chunk, w), lambda b, c: (b * nc + c, 0)),
            pl.BlockSpec((1, RET_HEADS, HEAD_DIM, HEAD_DIM), lambda b, c: (b, 0, 0, 0)),
        ],
        out_shape=[
            jax.ShapeDtypeStruct((batch * seq, w), BF16),
            jax.ShapeDtypeStruct((batch, RET_HEADS, HEAD_DIM, HEAD_DIM), F32),
        ],
        scratch_shapes=[pltpu.VMEM((RET_HEADS, HEAD_DIM, HEAD_DIM), F32)],
        compiler_params=_cparams("parallel", "arbitrary"),
        name="ret_prompt",
    )(proj, proj, proj, proj, gn, intra, qdec, kdec, cdec)


def _ret_sample_kernel(q_ref, k_ref, v_ref, g_ref, gn_ref, st_ref, intra_ref, qdec_ref, kdec_ref, cdec_ref,
                       y_ref, so_ref, *, n_batch, rows):
    k = k_ref[...]
    qb = q_ref[...].astype(BF16)
    kb = k.astype(BF16)
    vb = v_ref[...].astype(BF16)
    s = _dot_nt(qb, kb) * intra_ref[0]
    kd = k * kdec_ref[0]
    row_batch = lax.broadcasted_iota(jnp.int32, kd.shape, 0) // rows
    inter = []
    for b in range(n_batch):
        state = st_ref[b, 0]
        inter.append(_dot(qb[b * rows:(b + 1) * rows], state.astype(BF16)))
        kd_b = jnp.where(row_batch == b, kd, 0.0).astype(BF16)
        so_ref[b, 0] = state * cdec_ref[0] + _dot_tn(kd_b, vb)
    o = _dot(s.astype(BF16), vb) + jnp.concatenate(inter, axis=0) * qdec_ref[0]
    y_ref[...] = _groupnorm_gate(o, g_ref[...], gn_ref[...])


def _ret_sample(proj, gn, state, n_batch, rows, t_valid):
    m = n_batch * rows
    log_g = _log_decay()
    r = jnp.arange(m)
    t = (r % rows).astype(F32)
    same = (r[:, None] // rows) == (r[None, :] // rows)
    diff = t[:, None] - t[None, :]
    intra = jnp.where(same[None] & (diff >= 0)[None], jnp.exp(log_g[:, None, None] * jnp.maximum(diff, 0.0)[None]), 0.0)
    ones = jnp.ones((1, 1, HEAD_DIM), F32)
    qdec = jnp.exp(log_g[:, None] * (t[None, :] + 1.0))[:, :, None] * ones
    kdec = jnp.where(t[None, :] < t_valid, jnp.exp(log_g[:, None] * (t_valid - 1.0 - t[None, :])), 0.0)[:, :, None] * ones
    cdec = jnp.exp(log_g * t_valid)[:, None, None] * ones
    col = lambda off: (lambda h: (0, off // HEAD_DIM + h))
    per_head = lambda h: (h, 0, 0)
    return pl.pallas_call(
        functools.partial(_ret_sample_kernel, n_batch=n_batch, rows=rows),
        grid=(RET_HEADS,),
        in_specs=[
            pl.BlockSpec((m, HEAD_DIM), col(COL_RQ)),
            pl.BlockSpec((m, HEAD_DIM), col(COL_RK)),
            pl.BlockSpec((m, HEAD_DIM), col(COL_RV)),
            pl.BlockSpec((m, HEAD_DIM), col(COL_RG)),
            pl.BlockSpec((1, HEAD_DIM), lambda h: (0, h)),
            pl.BlockSpec((n_batch, 1, HEAD_DIM, HEAD_DIM), lambda h: (0, h, 0, 0)),
            pl.BlockSpec((1, m, m), per_head),
            pl.BlockSpec((1, m, HEAD_DIM), per_head),
            pl.BlockSpec((1, m, HEAD_DIM), per_head),
            pl.BlockSpec((1, 1, HEAD_DIM), per_head),
        ],
        out_specs=[
            pl.BlockSpec((m, HEAD_DIM), lambda h: (0, h)),
            pl.BlockSpec((n_batch, 1, HEAD_DIM, HEAD_DIM), lambda h: (0, h, 0, 0)),
        ],
        out_shape=[
            jax.ShapeDtypeStruct((m, RET_HEADS * HEAD_DIM), BF16),
            jax.ShapeDtypeStruct(state.shape, F32),
        ],
        compiler_params=_cparams("parallel"),
        name="ret_sample",
    )(proj, proj, proj, proj, gn, state, intra, qdec, kdec, cdec)


def _kth_largest(count_ge, next_above, smin, smax, n_adm, topk):
    few = n_adm < topk
    lo = jnp.where(few, -jnp.inf, smin)
    c_lo = jnp.where(few, float(topk), n_adm)
    hi = jnp.where(few, -jnp.inf, smax + jnp.maximum(jnp.abs(smax) * 1e-6, 1e-30))

    def bisect(_, state):
        lo, hi, c_lo = state
        mid = 0.5 * lo + 0.5 * hi
        c = count_ge(mid)
        take = c >= topk
        return jnp.where(take, mid, lo), jnp.where(take, hi, mid), jnp.where(take, c, c_lo)

    lo, hi, c_lo = lax.fori_loop(0, BISECT_STEPS, bisect, (lo, hi, c_lo))
    c_lo = jnp.where(few, float(topk), c_lo)

    def unfinished(state):
        return jnp.max(state[2]) > 0.0

    def step_up(state):
        lo, c_lo, active = state
        nxt, n_eq = next_above(lo)
        c_nxt = c_lo - n_eq
        move = active * jnp.where(c_nxt >= topk, 1.0, 0.0)
        lo = jnp.where(move > 0.0, nxt, lo)
        c_lo = jnp.where(move > 0.0, c_nxt, c_lo)
        return lo, c_lo, move * jnp.where(c_nxt > topk, 1.0, 0.0)

    return lax.while_loop(unfinished, step_up, (lo, c_lo, jnp.where(c_lo > topk, 1.0, 0.0)))[0]


def _dsa_prompt_kernel(dq_ref, iq_ref, iwq_ref, kall_ref, vall_ref, ikall_ref, o_ref,
                       kbf, vtb, ika, ikb, keys, z_a, z_b, zmax_a, zmax_b, m_scr, acc_scr, *, tq, topk):
    z_scr, zmax_scr = (z_a, z_b), (zmax_a, zmax_b)
    i = pl.program_id(1)
    n_chunks = kbf.shape[0]
    scale_log2e = HEAD_DIM ** -0.5 * math.log2(math.e)

    @pl.when(i == 0)
    def _():
        lane = lax.broadcasted_iota(jnp.int32, (tq, LANES), 1)
        for c in range(n_chunks):
            rows = slice(c * tq, (c + 1) * tq)
            kbf[c] = kall_ref[rows, :].astype(BF16)
            for kvh in range(DSA_KV_HEADS):
                sl = slice(kvh * HEAD_DIM, (kvh + 1) * HEAD_DIM)
                vtb[c, kvh, :HEAD_DIM, :] = vall_ref[rows, sl].T.astype(BF16)
                vtb[c, kvh, HEAD_DIM:, :] = jnp.ones((ROW_ALIGN, tq), BF16)
            a = ikall_ref[rows, :]
            ika[c] = jnp.where(lane < IDX_DH, a, 0.0).astype(BF16)
            ikb[c] = jnp.where(lane >= IDX_DH, pltpu.roll(a, IDX_DH, 1), 0.0).astype(BF16)

    w_t = iwq_ref[...].T
    iqb = iq_ref[...].astype(BF16)
    qb = (dq_ref[...] * scale_log2e).astype(BF16)
    t_col = i * tq + lax.broadcasted_iota(jnp.int32, (tq, tq), 1)
    s_row0 = lax.broadcasted_iota(jnp.int32, (tq, tq), 0)

    fold = lambda x: x.reshape(tq // SUBLANES, SUBLANES, tq)

    def score_chunk(c, carry):
        smax, smin = carry
        a = ika[c]
        b = ikb[c]
        acc = jnp.zeros((tq, tq), F32)
        for p in range(IDX_HEADS // 2):
            pair = iqb[:, p * LANES:(p + 1) * LANES]
            w0 = w_t[IDX_DH + 2 * p:IDX_DH + 2 * p + 1, :]
            w1 = w_t[IDX_DH + 2 * p + 1:IDX_DH + 2 * p + 2, :]
            acc = acc + jnp.maximum(_dot_nt(a, pair), 0.0) * w0
            acc = acc + jnp.maximum(_dot_nt(b, pair), 0.0) * w1
        admissible = s_row0 + c * tq <= t_col
        keys[c] = jnp.where(admissible, acc, -jnp.inf)
        smax = jnp.maximum(smax, fold(jnp.where(admissible, acc, -jnp.inf)).max(axis=0))
        smin = jnp.minimum(smin, fold(jnp.where(admissible, acc, jnp.inf)).min(axis=0))
        return smax, smin

    smax, smin = lax.fori_loop(0, i + 1, score_chunk, (jnp.full((SUBLANES, tq), -jnp.inf, F32),
                                                      jnp.full((SUBLANES, tq), jnp.inf, F32)))
    smax = smax.max(axis=0, keepdims=True)
    smin = smin.min(axis=0, keepdims=True)
    keys[i + 1] = jnp.full((tq, tq), -jnp.inf, F32)
    n_pairs = (i + 2) // 2

    def count_ge(t):
        def pair(j, cnt):
            for c in (2 * j, 2 * j + 1):
                cnt = cnt + fold(jnp.where(keys[c] >= t, 1.0, 0.0)).sum(axis=0)
            return cnt

        return lax.fori_loop(0, n_pairs, pair, jnp.zeros((SUBLANES, tq), F32)).sum(axis=0, keepdims=True)

    def next_above(lo):
        def pair(j, carry):
            nxt, n_eq = carry
            for c in (2 * j, 2 * j + 1):
                k = keys[c]
                nxt = jnp.minimum(nxt, fold(jnp.where(k > lo, k, jnp.inf)).min(axis=0))
                n_eq = n_eq + fold(jnp.where(k == lo, 1.0, 0.0)).sum(axis=0)
            return nxt, n_eq

        nxt, n_eq = lax.fori_loop(0, n_pairs, pair, (jnp.full((SUBLANES, tq), jnp.inf, F32),
                                                     jnp.zeros((SUBLANES, tq), F32)))
        return nxt.min(axis=0, keepdims=True), n_eq.sum(axis=0, keepdims=True)

    thr = _kth_largest(count_ge, next_above, smin, smax, (t_col[:1, :] + 1).astype(F32), topk)

    m_scr[...] = jnp.full_like(m_scr, NEG)
    acc_scr[...] = jnp.zeros_like(acc_scr)

    kv_of = lambda h: h // (DSA_HEADS // DSA_KV_HEADS)

    def logits(c, slot):
        bias = jnp.where(s_row0 + c * tq <= t_col, jnp.where(keys[c] >= thr, 0.0, NEG), NEG)
        kc = kbf[c]
        for h in range(DSA_HEADS):
            ksl = slice(kv_of(h) * HEAD_DIM, (kv_of(h) + 1) * HEAD_DIM)
            z = _dot_nt(kc[:, ksl], qb[:, h * HEAD_DIM:(h + 1) * HEAD_DIM]) + bias
            z_scr[slot][h] = z
            zmax_scr[slot][h] = z.max(axis=0, keepdims=True)

    def update(c, slot):
        for h in range(DSA_HEADS):
            m_old = m_scr[h]
            m_new = jnp.maximum(m_old, zmax_scr[slot][h])
            p = jnp.exp2(z_scr[slot][h] - m_new)
            acc_scr[h] = acc_scr[h] * jnp.exp2(m_old - m_new) + _dot(vtb[c, kv_of(h)], p.astype(BF16))
            m_scr[h] = m_new

    n = i + 1
    logits(0, 0)

    def two_chunks(k, carry):
        update(2 * k, 0)
        logits(2 * k + 1, 1)
        update(2 * k + 1, 1)
        logits(2 * k + 2, 0)
        return carry

    lax.fori_loop(0, (n - 1) // 2, two_chunks, 0)

    @pl.when(n % 2 == 1)
    def _():
        update(n - 1, 0)

    @pl.when(n % 2 == 0)
    def _():
        update(n - 2, 0)
        logits(n - 1, 1)
        update(n - 1, 1)

    for h in range(DSA_HEADS):
        o = acc_scr[h, :HEAD_DIM, :] * (1.0 / acc_scr[h, HEAD_DIM:HEAD_DIM + 1, :])
        o_ref[:, h * HEAD_DIM:(h + 1) * HEAD_DIM] = o.T.astype(BF16)


def _dsa_prompt(proj, batch, seq, tq):
    nq = seq // tq
    topk = min(TOPK_MAX, seq // 4)
    qw = DSA_HEADS * HEAD_DIM
    kvw = DSA_KV_HEADS * HEAD_DIM
    return pl.pallas_call(
        functools.partial(_dsa_prompt_kernel, tq=tq, topk=topk),
        grid=(batch, nq),
        in_specs=[
            pl.BlockSpec((tq, qw), lambda b, i: (b * nq + i, COL_DQ // qw)),
            pl.BlockSpec((tq, qw), lambda b, i: (b * nq + i, COL_IQ // qw)),
            pl.BlockSpec((tq, LANES), lambda b, i: (b * nq + i, COL_IK // LANES)),
            pl.BlockSpec((seq, kvw), lambda b, i: (b, COL_DK // kvw)),
            pl.BlockSpec((seq, kvw), lambda b, i: (b, COL_DV // kvw)),
            pl.BlockSpec((seq, LANES), lambda b, i: (b, COL_IK // LANES)),
        ],
        out_specs=pl.BlockSpec((tq, qw), lambda b, i: (b * nq + i, 0)),
        out_shape=jax.ShapeDtypeStruct((batch * seq, qw), BF16),
        scratch_shapes=[
            pltpu.VMEM((nq, tq, kvw), BF16),
            pltpu.VMEM((nq, DSA_KV_HEADS, HEAD_DIM + ROW_ALIGN, tq), BF16),
            pltpu.VMEM((nq, tq, LANES), BF16),
            pltpu.VMEM((nq, tq, LANES), BF16),
            pltpu.VMEM((nq + 1, tq, tq), F32),
            pltpu.VMEM((DSA_HEADS, tq, tq), F32),
            pltpu.VMEM((DSA_HEADS, tq, tq), F32),
            pltpu.VMEM((DSA_HEADS, 1, tq), F32),
            pltpu.VMEM((DSA_HEADS, 1, tq), F32),
            pltpu.VMEM((DSA_HEADS, 1, tq), F32),
            pltpu.VMEM((DSA_HEADS, HEAD_DIM + ROW_ALIGN, tq), F32),
        ],
        compiler_params=_cparams("parallel", "arbitrary"),
        name="dsa_prompt",
    )(proj, proj, proj, proj, proj, proj)


def _idx_sample_kernel(pt_ref, iq_ref, iw_ref, iknew_ref, *rest, pages):
    page_refs = rest[:pages]
    sp_ref, sn_ref = rest[pages:]
    iqb = iq_ref[0].astype(BF16)
    w = iw_ref[0]
    rows = iqb.shape[0] // IDX_HEADS

    def scores(ik_t):
        d = jnp.maximum(_dot(iqb, ik_t.astype(BF16)), 0.0) * w
        return d.reshape(rows, IDX_HEADS, LANES).sum(axis=1)

    for p in range(pages):
        sp_ref[0, :, p * PAGE_SIZE:(p + 1) * PAGE_SIZE] = scores(page_refs[p][0])
    t = lax.broadcasted_iota(jnp.int32, (rows, LANES), 0)
    j = lax.broadcasted_iota(jnp.int32, (rows, LANES), 1)
    sn_ref[0] = jnp.where(j <= t, scores(iknew_ref[0]), -jnp.inf)


def _idx_sample(page_table, iq, iw, ik_new, pool_ik, pages):
    n_batch, n_pages = page_table.shape
    rows16 = iq.shape[1]
    rows = rows16 // IDX_HEADS
    past = n_pages * PAGE_SIZE
    page_specs = [pl.BlockSpec((1, IDX_DH, PAGE_SIZE), functools.partial(
        lambda b, g, pt, k: (pt[b, g * pages + k], 0, 0), k=k)) for k in range(pages)]
    grid_spec = pltpu.PrefetchScalarGridSpec(
        num_scalar_prefetch=1,
        grid=(n_batch, n_pages // pages),
        in_specs=[
            pl.BlockSpec((1, rows16, IDX_DH), lambda b, g, pt: (b, 0, 0)),
            pl.BlockSpec((1, rows16, LANES), lambda b, g, pt: (b, 0, 0)),
            pl.BlockSpec((1, IDX_DH, PAGE_SIZE), lambda b, g, pt: (b, 0, 0)),
        ] + page_specs,
        out_specs=[
            pl.BlockSpec((1, rows, pages * PAGE_SIZE), lambda b, g, pt: (b, 0, g)),
            pl.BlockSpec((1, rows, LANES), lambda b, g, pt: (b, 0, 0)),
        ],
    )
    return pl.pallas_call(
        functools.partial(_idx_sample_kernel, pages=pages),
        grid_spec=grid_spec,
        out_shape=[
            jax.ShapeDtypeStruct((n_batch, rows, past), F32),
            jax.ShapeDtypeStruct((n_batch, rows, LANES), F32),
        ],
        compiler_params=_cparams("parallel", "arbitrary"),
        name="idx_sample",
    )(page_table, iq, iw, ik_new, *([pool_ik] * pages))


def _sel_sample_kernel(sp_ref, sn_ref, bp_ref, bn_ref, *, topk):
    n_batch, rows, past = sp_ref.shape
    sp = sp_ref[...].reshape(n_batch * rows, past)
    sn = sn_ref[...].reshape(n_batch * rows, LANES)
    t = lax.broadcasted_iota(jnp.int32, (n_batch * rows, LANES), 0) % rows
    j = lax.broadcasted_iota(jnp.int32, (n_batch * rows, LANES), 1)
    admissible_new = j <= t
    row_sum = lambda a: a.sum(axis=1, keepdims=True)
    row_min = lambda a: a.min(axis=1, keepdims=True)

    def count_ge(thr):
        return row_sum(jnp.where(sp >= thr, 1.0, 0.0)) + row_sum(jnp.where(sn >= thr, 1.0, 0.0))

    def next_above(lo):
        nxt = jnp.minimum(row_min(jnp.where(sp > lo, sp, jnp.inf)), row_min(jnp.where(sn > lo, sn, jnp.inf)))
        return nxt, row_sum(jnp.where(sp == lo, 1.0, 0.0)) + row_sum(jnp.where(sn == lo, 1.0, 0.0))

    smin = jnp.minimum(row_min(sp), row_min(jnp.where(admissible_new, sn, jnp.inf)))
    smax = jnp.maximum(sp.max(axis=1, keepdims=True), sn.max(axis=1, keepdims=True))
    n_adm = float(past) + row_sum(jnp.where(admissible_new, 1.0, 0.0))
    thr = _kth_largest(count_ge, next_above, smin, smax, n_adm, topk)
    bp_ref[...] = jnp.where(sp >= thr, 0.0, NEG).reshape(n_batch, rows, past)
    bn_ref[...] = jnp.where(admissible_new, jnp.where(sn >= thr, 0.0, NEG), NEG).reshape(n_batch, rows, LANES)


def _sel_sample(scores_past, scores_new, topk):
    full = lambda a: pl.BlockSpec(a.shape, lambda: (0,) * a.ndim)
    return pl.pallas_call(
        functools.partial(_sel_sample_kernel, topk=topk),
        in_specs=[full(scores_past), full(scores_new)],
        out_specs=[full(scores_past), full(scores_new)],
        out_shape=[jax.ShapeDtypeStruct(scores_past.shape, F32), jax.ShapeDtypeStruct(scores_new.shape, F32)],
        compiler_params=pltpu.CompilerParams(vmem_limit_bytes=VMEM_LIMIT),
        name="sel_sample",
    )(scores_past, scores_new)


def _dsa_sample_kernel(pt_ref, bp_ref, bn_ref, q_ref, knew_ref, vnew_ref, *rest, pages, steps):
    k_refs = rest[:pages]
    v_refs = rest[pages:2 * pages]
    o_ref, m_scr, l_scr, acc_scr = rest[2 * pages:]
    g = pl.program_id(0) % steps
    scale = HEAD_DIM ** -0.5
    group = DSA_HEADS // DSA_KV_HEADS

    @pl.when(g == 0)
    def _():
        m_scr[...] = jnp.full_like(m_scr, NEG)
        l_scr[...] = jnp.zeros_like(l_scr)
        acc_scr[...] = jnp.zeros_like(acc_scr)

    def head_rows(ref, kvh):
        return ref[0, pl.ds(kvh, PAGE_SIZE, stride=DSA_KV_HEADS), :].astype(BF16)

    def attend(k_pages, v_pages, bias):
        bias_g = jnp.concatenate([bias] * group, axis=0)
        kv_heads = range(DSA_KV_HEADS)
        qb = [q_ref[0, kvh].astype(BF16) for kvh in kv_heads]
        raw = [jnp.concatenate([_dot_nt(qb[kvh], head_rows(k, kvh)) for k in k_pages], axis=1) for kvh in kv_heads]
        pb, alpha = [], []
        for kvh in kv_heads:
            z = jnp.where(bias_g < 0.0, NEG, raw[kvh])
            m_old = m_scr[kvh]
            m_new = jnp.maximum(m_old, z.max(axis=1, keepdims=True))
            alpha.append(jnp.exp((m_old - m_new) * scale))
            p = jnp.exp((z - m_new) * scale)
            l_scr[kvh] = l_scr[kvh] * alpha[kvh] + p.sum(axis=1, keepdims=True)
            m_scr[kvh] = m_new
            pb.append(p.astype(BF16))
        for kvh in kv_heads:
            pv = _dot(pb[kvh][:, :PAGE_SIZE], head_rows(v_pages[0], kvh))
            for n in range(1, len(v_pages)):
                pv = pv + _dot(pb[kvh][:, n * PAGE_SIZE:(n + 1) * PAGE_SIZE], head_rows(v_pages[n], kvh))
            acc_scr[kvh] = acc_scr[kvh] * alpha[kvh] + pv

    attend(k_refs, v_refs, bp_ref[0])

    @pl.when(g == steps - 1)
    def _():
        attend([knew_ref], [vnew_ref], bn_ref[0])
        for kvh in range(DSA_KV_HEADS):
            o_ref[0, kvh] = (acc_scr[kvh] * (1.0 / l_scr[kvh])).astype(BF16)


def _dsa_sample(page_table, bias_past, bias_new, q, k_new, v_new, pool_k, pool_v, steps):
    n_batch, n_pages = page_table.shape
    pages = n_pages // steps
    rows = bias_past.shape[1]
    page_rows = DSA_KV_HEADS * PAGE_SIZE
    grows = q.shape[2]
    page_spec = lambda k: pl.BlockSpec((1, page_rows, HEAD_DIM), functools.partial(
        lambda i, pt, k: (pt[i // steps, (i % steps) * pages + k], 0, 0), k=k))
    per_batch3 = lambda i, pt: (i // steps, 0, 0)
    per_batch4 = lambda i, pt: (i // steps, 0, 0, 0)
    in_specs = [
        pl.BlockSpec((1, rows, pages * PAGE_SIZE), lambda i, pt: (i // steps, 0, i % steps)),
        pl.BlockSpec((1, rows, LANES), per_batch3),
        pl.BlockSpec((1, DSA_KV_HEADS, grows, HEAD_DIM), per_batch4),
        pl.BlockSpec((1, page_rows, HEAD_DIM), per_batch3),
        pl.BlockSpec((1, page_rows, HEAD_DIM), per_batch3),
    ] + [page_spec(k) for k in range(pages)] * 2
    grid_spec = pltpu.PrefetchScalarGridSpec(
        num_scalar_prefetch=1,
        grid=(n_batch * steps,),
        in_specs=in_specs,
        out_specs=pl.BlockSpec((1, DSA_KV_HEADS, grows, HEAD_DIM), per_batch4),
        scratch_shapes=[
            pltpu.VMEM((DSA_KV_HEADS, grows, 1), F32),
            pltpu.VMEM((DSA_KV_HEADS, grows, 1), F32),
            pltpu.VMEM((DSA_KV_HEADS, grows, HEAD_DIM), F32),
        ],
    )
    return pl.pallas_call(
        functools.partial(_dsa_sample_kernel, pages=pages, steps=steps),
        grid_spec=grid_spec,
        out_shape=jax.ShapeDtypeStruct((n_batch, DSA_KV_HEADS, grows, HEAD_DIM), BF16),
        compiler_params=_cparams("arbitrary"),
        name="dsa_sample",
    )(page_table, bias_past, bias_new, q, k_new, v_new, *([pool_k] * pages), *([pool_v] * pages))


def _merge_kernel(x_ref, ry_ref, do_ref, ga_ref, gb_ref, wr_ref, wd_ref, wo_ref, o_ref):
    ya = _dot(ry_ref[...], wr_ref[...])
    yb = _dot(do_ref[...], wd_ref[...])
    merged = _sigmoid(ga_ref[...]) * ya + _sigmoid(gb_ref[...]) * yb
    o_ref[...] = x_ref[...] + _dot(merged.astype(BF16), wo_ref[...])


def _merge(x, ret_y, dsa_o, proj, w_ret_out, w_dsa_out, w_o, tm):
    m, d = x.shape
    w = ret_y.shape[1]
    const = lambda i: (0, 0)
    resident = dict(pipeline_mode=pl.Buffered(1))
    return pl.pallas_call(
        _merge_kernel,
        grid=(m // tm,),
        in_specs=[
            pl.BlockSpec((tm, d), lambda i: (i, 0)),
            pl.BlockSpec((tm, w), lambda i: (i, 0)),
            pl.BlockSpec((tm, w), lambda i: (i, 0)),
            pl.BlockSpec((tm, d), lambda i: (i, COL_GA // d)),
            pl.BlockSpec((tm, d), lambda i: (i, COL_GB // d)),
            pl.BlockSpec((w, d), const, **resident),
            pl.BlockSpec((w, d), const, **resident),
            pl.BlockSpec((d, d), const, **resident),
        ],
        out_specs=pl.BlockSpec((tm, d), lambda i: (i, 0)),
        out_shape=jax.ShapeDtypeStruct((m, d), F32),
        compiler_params=_cparams("parallel"),
        name="merge_out",
    )(x, ret_y, dsa_o, proj, proj, w_ret_out, w_dsa_out, w_o)


def _mem_attn_kernel(h_ref, g_ref, wq_ref, mk_ref, mv_ref, wo_ref, o_ref, *, rows_per_batch):
    h = h_ref[...]
    qm = _dot(_rmsnorm_bf16(h, g_ref[...]), wq_ref[...])
    tm = h.shape[0]
    nk = mk_ref.shape[0]
    scale = HEAD_DIM ** -0.5
    if rows_per_batch is not None:
        rb = lax.broadcasted_iota(jnp.int32, (tm, nk), 0) // rows_per_batch
        kb = lax.broadcasted_iota(jnp.int32, (tm, nk), 1) // MEM_LEN
        same = rb == kb
    outs = []
    for hd in range(MEM_HEADS):
        sl = slice(hd * HEAD_DIM, (hd + 1) * HEAD_DIM)
        z = _dot_nt(qm[:, sl].astype(BF16), mk_ref[:, sl].astype(BF16)) * scale
        if rows_per_batch is not None:
            z = jnp.where(same, z, NEG)
        p = jnp.exp(z - z.max(axis=-1, keepdims=True))
        pv = _dot(p.astype(BF16), mv_ref[:, sl].astype(BF16))
        outs.append(pv * (1.0 / p.sum(axis=-1, keepdims=True)))
    om = jnp.concatenate(outs, axis=1).astype(BF16)
    o_ref[...] = h + _dot(om, wo_ref[...])


def _mem_attn(h, g, w_mq, mk, mv, w_mo, *, tm, batch_tiles, mk_col, mv_col, nk, rows_per_batch):
    m, d = h.shape
    const = lambda i: (0, 0)
    if batch_tiles:
        kmap = lambda col: (lambda i: (i // batch_tiles, col))
    else:
        kmap = lambda col: (lambda i: (0, col))
    return pl.pallas_call(
        functools.partial(_mem_attn_kernel, rows_per_batch=rows_per_batch),
        grid=(m // tm,),
        in_specs=[
            pl.BlockSpec((tm, d), lambda i: (i, 0)),
            pl.BlockSpec((1, d), const),
            pl.BlockSpec((d, MEM_W), const),
            pl.BlockSpec((nk, MEM_W), kmap(mk_col)),
            pl.BlockSpec((nk, MEM_W), kmap(mv_col)),
            pl.BlockSpec((MEM_W, d), const),
        ],
        out_specs=pl.BlockSpec((tm, d), lambda i: (i, 0)),
        out_shape=jax.ShapeDtypeStruct((m, d), F32),
        compiler_params=_cparams("parallel"),
        name="mem_attn",
    )(h, g, w_mq, mk, mv, w_mo)


def _mlp_kernel(h_ref, g_ref, wu_ref, wd_ref, gf_ref, o_ref, *rest):
    u_ref = rest[-1]
    j = pl.program_id(1)

    @pl.when(j == 0)
    def _():
        u_ref[...] = _rmsnorm_bf16(h_ref[...], g_ref[...])
        o_ref[...] = h_ref[...]

    wu, wd = wu_ref[...], wd_ref[...]
    if len(rest) == 3:
        wu, wd = wu.astype(BF16), wd.astype(BF16)
        rest[0][...] = wu
        rest[1][...] = wd
    a = jnp.maximum(_dot(u_ref[...], wu), 0.0)
    o_ref[...] += _dot((a * a).astype(BF16), wd)

    @pl.when(j == pl.num_programs(1) - 1)
    def _():
        y = o_ref[...]
        ms = jnp.mean(y * y, axis=-1, keepdims=True)
        o_ref[...] = (y * lax.rsqrt(ms + EPS)) * gf_ref[...]


def _mlp(h, g, w_up, w_down, g_final, tm, tf):
    m, d = h.shape
    ff = w_up.shape[1]
    emit = w_up.dtype != BF16
    assert not emit or m == tm, "weight copies are written once, by a call with one row tile"
    wu_spec = pl.BlockSpec((d, tf), lambda i, j: (0, j))
    wd_spec = pl.BlockSpec((tf, d), lambda i, j: (j, 0))
    out_specs = [pl.BlockSpec((tm, d), lambda i, j: (i, 0))]
    out_shape = [jax.ShapeDtypeStruct((m, d), F32)]
    if emit:
        out_specs += [wu_spec, wd_spec]
        out_shape += [jax.ShapeDtypeStruct(w_up.shape, BF16), jax.ShapeDtypeStruct(w_down.shape, BF16)]
    out = pl.pallas_call(
        _mlp_kernel,
        grid=(m // tm, ff // tf),
        in_specs=[
            pl.BlockSpec((tm, d), lambda i, j: (i, 0)),
            pl.BlockSpec((1, d), lambda i, j: (0, 0)),
            wu_spec,
            wd_spec,
            pl.BlockSpec((1, d), lambda i, j: (0, 0)),
        ],
        out_specs=out_specs,
        out_shape=out_shape,
        scratch_shapes=[pltpu.VMEM((tm, d), BF16)],
        compiler_params=_cparams("parallel", "arbitrary"),
        name="mlp_final",
    )(h, g, w_up, w_down, g_final)
    return out if emit else out[0]


def _pick_tile(n, pref):
    t = min(n, pref)
    while n % t:
        t //= 2
    return t


def kernel(x_prompt, x_sample, mem_prompt, cache_k, cache_v, cache_idx_k, state_ret, cache_mem_k, cache_mem_v,
           page_table, g_mix, w_in, gn_ret, w_ret_out, w_dsa_out, w_o, g_mem, g_memkv, w_mq, w_mk, w_mv, w_mo,
           g_mlp, w_up, w_down, g_final):
    assert w_in.shape[0] == 1, "one layer"
    batch, seq, d = x_prompt.shape
    n_dec, t_dec, _ = x_sample.shape
    n_pages = page_table.shape[1]
    past = n_pages * PAGE_SIZE
    mem_len = mem_prompt.shape[1]
    assert mem_len == MEM_LEN and t_dec <= SUBLANES
    row = lambda v: v.reshape(1, -1)

    w_ret_out_b, w_dsa_out_b, w_o_b = (w[0].astype(BF16) for w in (w_ret_out, w_dsa_out, w_o))
    w_mq_b, w_mo_b = w_mq[0].astype(BF16), w_mo[0].astype(BF16)
    w_mkv_b = jnp.concatenate([w_mk[0], w_mv[0]], axis=1).astype(BF16)
    g_mix_r, gn_r, g_mem_r, g_memkv_r, g_mlp_r, g_final_r = (
        row(v) for v in (g_mix[0], gn_ret[0], g_mem[0], g_memkv[0], g_mlp[0], g_final))

    rows = SUBLANES
    m_s = n_dec * rows
    xs = jnp.pad(x_sample, ((0, 0), (0, rows - t_dec), (0, 0))).reshape(m_s, d)
    proj_s, k_rows_s, v_rows_s, w_proj_b = _norm_proj_rope(xs, g_mix_r, jnp.swapaxes(w_in[0], 0, 1),
                                                           _rope_tables(past, m_s, rows), m_s, 1)

    m_p = batch * seq
    xp = x_prompt.reshape(m_p, d)
    tm_proj = _pick_tile(seq, 1024)
    proj_p, k_rows_p, v_rows_p = _norm_proj_rope(xp, g_mix_r, w_proj_b, _rope_tables(0, seq, seq), tm_proj,
                                                 seq // tm_proj)
    ret_y_p, ret_state_p = _ret_prompt(proj_p, gn_r, batch, seq, _pick_tile(seq, 256))
    dsa_o_p = _dsa_prompt(proj_p, batch, seq, _pick_tile(seq, 256))
    kv_p = _norm_proj(mem_prompt.reshape(batch * mem_len, d), g_memkv_r, w_mkv_b, mem_len, PROJ_TN)
    tm_t = _pick_tile(seq, 256)
    h_p = _merge(xp, ret_y_p, dsa_o_p, proj_p, w_ret_out_b, w_dsa_out_b, w_o_b, tm_t)
    h_p = _mem_attn(h_p, g_mem_r, w_mq_b, kv_p, kv_p, w_mo_b, tm=tm_t, batch_tiles=seq // tm_t,
                    mk_col=0, mv_col=1, nk=mem_len, rows_per_batch=None)

    ret_y_s, ret_state_s = _ret_sample(proj_s, gn_r, state_ret[0], n_dec, rows, t_dec)

    proj_s3 = proj_s.reshape(n_dec, rows, N_PROJ)
    iq_s = proj_s3[:, :, COL_IQ:COL_IQ + IDX_HEADS * IDX_DH].reshape(n_dec, rows * IDX_HEADS, IDX_DH)
    iw_s = proj_s3[:, :, COL_IK + IDX_DH:COL_IK + IDX_DH + IDX_HEADS].reshape(n_dec, rows * IDX_HEADS, 1)
    iw_s = jnp.broadcast_to(iw_s, (n_dec, rows * IDX_HEADS, LANES))
    kvw = DSA_KV_HEADS * HEAD_DIM
    page_rows = DSA_KV_HEADS * PAGE_SIZE
    pad_keys = lambda a: jnp.pad(a, ((0, 0), (0, PAGE_SIZE - rows), (0, 0)))
    ik_new_t = jnp.swapaxes(pad_keys(proj_s3[:, :, COL_IK:COL_IK + IDX_DH]), 1, 2)
    pool_ik_t = jnp.swapaxes(cache_idx_k, 2, 3).reshape(-1, IDX_DH, PAGE_SIZE)
    new_page = lambda a: jnp.pad(a.reshape(n_dec, DSA_KV_HEADS * rows, HEAD_DIM),
                                 ((0, 0), (0, page_rows - DSA_KV_HEADS * rows), (0, 0)))
    k_new, v_new = new_page(k_rows_s), new_page(v_rows_s)
    scores_past, scores_new = _idx_sample(page_table, iq_s, iw_s, ik_new_t, pool_ik_t, _pick_tile(n_pages, 32))
    group = DSA_HEADS // DSA_KV_HEADS
    dq_s = proj_s3[:, :, COL_DQ:COL_DQ + DSA_HEADS * HEAD_DIM].reshape(n_dec, rows, DSA_KV_HEADS, group, HEAD_DIM)
    dq_s = dq_s.transpose(0, 2, 3, 1, 4).reshape(n_dec, DSA_KV_HEADS, group * rows, HEAD_DIM)
    bias_past, bias_new = _sel_sample(scores_past, scores_new, min(TOPK_MAX, (past + t_dec) // 4))
    dsa_o_s = _dsa_sample(page_table, bias_past, bias_new, dq_s, k_new, v_new,
                          cache_k.reshape(-1, page_rows, HEAD_DIM), cache_v.reshape(-1, page_rows, HEAD_DIM),
                          n_pages // _pick_tile(n_pages, MAX_PAGES_PER_STEP))
    dsa_o_s = dsa_o_s.reshape(n_dec, DSA_KV_HEADS, group, rows, HEAD_DIM).transpose(0, 3, 1, 2, 4)
    dsa_o_s = dsa_o_s.reshape(m_s, DSA_HEADS * HEAD_DIM)

    h_s = _merge(xs, ret_y_s, dsa_o_s, proj_s, w_ret_out_b, w_dsa_out_b, w_o_b, m_s)
    h_s = _mem_attn(h_s, g_mem_r, w_mq_b, cache_mem_k[0].reshape(n_dec * mem_len, MEM_W),
                    cache_mem_v[0].reshape(n_dec * mem_len, MEM_W), w_mo_b, tm=m_s, batch_tiles=0,
                    mk_col=0, mv_col=0, nk=n_dec * mem_len, rows_per_batch=rows)
    y_s, w_up_b, w_down_b = _mlp(h_s, g_mlp_r, w_up[0], w_down[0], g_final_r, m_s, 512)
    y_p = _mlp(h_p, g_mlp_r, w_up_b, w_down_b, g_final_r, _pick_tile(seq, 1024), 512)

    def rows_p(col, width, tail):
        return proj_p[:, col:col + width].reshape((1, batch, seq) + tail)

    def rows_s(col, width, tail):
        return proj_s3[:, :t_dec, col:col + width].reshape((1, n_dec, t_dec) + tail)

    kv_p_rows = lambda a: a.reshape(1, batch, seq, DSA_KV_HEADS, HEAD_DIM)
    kv_s_rows = lambda a: a.reshape(1, n_dec, rows, DSA_KV_HEADS, HEAD_DIM)[:, :, :t_dec]
    return (
        y_p.reshape(batch, seq, d),
        y_s.reshape(n_dec, rows, d)[:, :t_dec],
        ret_state_p[None],
        kv_p_rows(k_rows_p),
        kv_p_rows(v_rows_p),
        rows_p(COL_IK, IDX_DH, (IDX_DH,)),
        kv_p[:, :MEM_W].reshape(1, batch, mem_len, MEM_HEADS, HEAD_DIM),
        kv_p[:, MEM_W:].reshape(1, batch, mem_len, MEM_HEADS, HEAD_DIM),
        ret_state_s[None],
        kv_s_rows(k_rows_s),
        kv_s_rows(v_rows_s),
        rows_s(COL_IK, IDX_DH, (IDX_DH,)),
    )
```

```python
import functools
import math

import jax
import jax.numpy as jnp
import numpy as np
from jax import lax
from jax.experimental import pallas as pl
from jax.experimental.pallas import tpu as pltpu

F32 = jnp.float32
BF16 = jnp.bfloat16

D_MODEL = 2048
RET_HEADS = 8
HEAD_DIM = 128
DSA_HEADS = 8
DSA_KV_HEADS = 2
IDX_HEADS = 16
IDX_DH = 64
TOPK_MAX = 256
PAGE_SIZE = 128
MEM_LEN = 256
MEM_HEADS = 4
MEM_W = MEM_HEADS * HEAD_DIM
D_FF = 4 * D_MODEL
RET_THETA = 10000.0
ROPE_THETA = 500000.0
EPS = 1e-6
LANES = 128
SUBLANES = 8
ROW_ALIGN = 16
VMEM_LIMIT = 56 * 1024 * 1024
NEG = -1e30
BISECT_STEPS = 16
MAX_PAGES_PER_STEP = 32

COL_GA = 0
COL_GB = 2048
COL_RQ = 4096
COL_RK = 5120
COL_RV = 6144
COL_RG = 7168
COL_DQ = 8192
COL_IQ = 9216
COL_DK = 10240
COL_DV = 10496
COL_IK = 10752
N_PROJ = 11264
PROJ_TN = 512
PROJ_TN_BF16 = 1024
ROPE_NONE, ROPE_RET, ROPE_RETK, ROPE_DSA, ROPE_IDX, ROPE_IKW = range(6)
COL_ROPE = ((ROPE_NONE,) * 32 + (ROPE_RET,) * 8 + (ROPE_RETK,) * 8 + (ROPE_NONE,) * 16 + (ROPE_DSA,) * 8
            + (ROPE_IDX,) * 8 + (ROPE_DSA,) * 2 + (ROPE_NONE,) * 2 + (ROPE_IKW,) + (ROPE_NONE,) * 3)
TAB_RET_C, TAB_RET_S, TAB_DSA_C, TAB_DSA_S, TAB_IDX_C, TAB_IDX_S = range(6)
TAB_W = 6 * LANES
DSA_ROT = HEAD_DIM // 4
IDX_ROT = IDX_DH // 4


def _cparams(*sem):
    return pltpu.CompilerParams(dimension_semantics=sem, vmem_limit_bytes=VMEM_LIMIT)


def _dot(a, b):
    return jnp.dot(a, b, preferred_element_type=F32)


def _dot_nt(a, b):
    return lax.dot_general(a, b, (((1,), (1,)), ((), ())), preferred_element_type=F32)


def _dot_tn(a, b):
    return lax.dot_general(a, b, (((0,), (0,)), ((), ())), preferred_element_type=F32)


def _rmsnorm_bf16(x, g):
    ms = jnp.mean(x * x, axis=-1, keepdims=True)
    return ((x * lax.rsqrt(ms + EPS)) * g).astype(BF16)


def _sigmoid(x):
    return 1.0 / (1.0 + jnp.exp(-x))


def _tab(tab_ref, which):
    return tab_ref[:, which * LANES:(which + 1) * LANES]


def _rope_cols(o_ref, g, c, s, half, period):
    a = o_ref[:, g * LANES:(g + 1) * LANES]
    if 2 * half == LANES:
        partner = pltpu.roll(a, half, 1)
    else:
        lane = lax.broadcasted_iota(jnp.int32, (1, LANES), 1)
        first = (lane & (period - 1)) < half
        partner = jnp.where(first, pltpu.roll(a, LANES - half, 1), pltpu.roll(a, half, 1))
    o_ref[:, g * LANES:(g + 1) * LANES] = a * c + partner * s


def _finish_block(dst_ref, tab_ref, plan, kv_group, kr_ref, vr_ref):
    for g, kind in enumerate(plan):
        rope = functools.partial(_rope_cols, dst_ref, g)
        if kind == ROPE_RET:
            rope(_tab(tab_ref, TAB_RET_C), _tab(tab_ref, TAB_RET_S), LANES // 2, LANES)
        elif kind == ROPE_RETK:
            scale = HEAD_DIM ** -0.5
            rope(_tab(tab_ref, TAB_RET_C) * scale, _tab(tab_ref, TAB_RET_S) * scale, LANES // 2, LANES)
        elif kind == ROPE_DSA:
            rope(_tab(tab_ref, TAB_DSA_C), _tab(tab_ref, TAB_DSA_S), DSA_ROT // 2, LANES)
        elif kind == ROPE_IDX:
            rope(_tab(tab_ref, TAB_IDX_C), _tab(tab_ref, TAB_IDX_S), IDX_ROT // 2, IDX_DH)
        elif kind == ROPE_IKW:
            lane = lax.broadcasted_iota(jnp.int32, (1, LANES), 1)
            is_ik = lane < IDX_DH
            rope(jnp.where(is_ik, _tab(tab_ref, TAB_IDX_C), 1.0), jnp.where(is_ik, _tab(tab_ref, TAB_IDX_S), 0.0),
                 IDX_ROT // 2, IDX_DH)
    if kv_group is not None:
        rows = dst_ref.shape[0]
        for kvh in range(DSA_KV_HEADS):
            rows_of_head = pl.ds(kvh, rows, stride=DSA_KV_HEADS)
            gk, gv = kv_group + kvh, kv_group + DSA_KV_HEADS + kvh
            kr_ref[rows_of_head, :] = dst_ref[:, gk * LANES:(gk + 1) * LANES]
            vr_ref[rows_of_head, :] = dst_ref[:, gv * LANES:(gv + 1) * LANES]


def _norm_proj_kernel(x_ref, g_ref, w_ref, tab_ref, o_ref, u_ref, *, kr_ref=None, vr_ref=None, wq_ref=None):
    j = pl.program_id(1)

    @pl.when(j == 0)
    def _():
        u_ref[...] = _rmsnorm_bf16(x_ref[...], g_ref[...])

    if tab_ref is None:
        o_ref[...] = _dot(u_ref[...], w_ref[...])
        return
    wb = w_ref[...]
    if wb.dtype != BF16:
        wb = wb.astype(BF16)
        wq_ref[...] = wb
    o_ref[...] = _dot_nt(u_ref[...], wb)
    groups = o_ref.shape[1] // LANES

    plans = {}
    for b in range(N_PROJ // (groups * LANES)):
        plans.setdefault(COL_ROPE[b * groups:(b + 1) * groups], []).append(b)
    kv_block, kv_group = divmod(COL_DK // LANES, groups)
    for plan, blocks in plans.items():
        if all(kind == ROPE_NONE for kind in plan):
            continue
        cond = functools.reduce(jnp.logical_or, [j == b for b in blocks])

        @pl.when(cond)
        def _(plan=plan, blocks=blocks):
            _finish_block(o_ref, tab_ref, plan, kv_group if kv_block in blocks else None, kr_ref, vr_ref)


def _norm_proj(x, g, w, tm, tn):
    m, d = x.shape
    n = w.shape[1]

    def body(x_ref, g_ref, w_ref, o_ref, u_ref):
        _norm_proj_kernel(x_ref, g_ref, w_ref, None, o_ref, u_ref)

    return pl.pallas_call(
        body,
        grid=(m // tm, n // tn),
        in_specs=[
            pl.BlockSpec((tm, d), lambda i, j: (i, 0)),
            pl.BlockSpec((1, d), lambda i, j: (0, 0)),
            pl.BlockSpec((d, tn), lambda i, j: (0, j)),
        ],
        out_specs=pl.BlockSpec((tm, tn), lambda i, j: (i, j)),
        out_shape=jax.ShapeDtypeStruct((m, n), F32),
        scratch_shapes=[pltpu.VMEM((tm, d), BF16)],
        compiler_params=_cparams("parallel", "arbitrary"),
        name="norm_proj",
    )(x, g, w)


_W_IN_ROW = {"rq": 0, "rk": 1024, "rv": 2048, "rg": 3072, "dq": 4096, "dk": 5120, "iq": 5632, "ik": 6656,
             "ga": 6736, "gb": 8784}
PROJ_SRC_ROWS = tuple(_W_IN_ROW[name] + PROJ_TN * k for name, nblk in (
    ("ga", 4), ("gb", 4), ("rq", 2), ("rk", 2), ("rv", 2), ("rg", 2), ("dq", 2), ("iq", 2), ("dk", 1), ("ik", 1))
    for k in range(nblk))


def _norm_proj_rope(x, g, w_t, tab, tm, pos_blocks):
    m, d = x.shape
    from_f32 = w_t.dtype != BF16
    tn = PROJ_TN if from_f32 else PROJ_TN_BF16
    kv_spec = pl.BlockSpec((DSA_KV_HEADS * tm, HEAD_DIM), lambda i, j, *_: (i, 0))
    kv_rows = jax.ShapeDtypeStruct((DSA_KV_HEADS * m, HEAD_DIM), F32)
    in_specs = [
        pl.BlockSpec((tm, d), lambda i, j, *_: (i, 0)),
        pl.BlockSpec((1, d), lambda i, j, *_: (0, 0)),
        None,
        pl.BlockSpec((tm, TAB_W), lambda i, j, *_: (i % pos_blocks, 0)),
    ]
    out_specs = [pl.BlockSpec((tm, tn), lambda i, j, *_: (i, j)), kv_spec, kv_spec]
    out_shape = [jax.ShapeDtypeStruct((m, N_PROJ), F32), kv_rows, kv_rows]
    scratch = [pltpu.VMEM((tm, d), BF16)]
    if from_f32:
        assert m == tm and all(r % ROW_ALIGN == 0 for r in PROJ_SRC_ROWS)
        src_rows = jnp.asarray([r // ROW_ALIGN for r in PROJ_SRC_ROWS], jnp.int32)
        in_specs[2] = pl.BlockSpec((pl.Element(tn), pl.Element(d)), lambda i, j, src: (src[j] * ROW_ALIGN, 0))
        out_specs.append(pl.BlockSpec((tn, d), lambda i, j, src: (j, 0)))
        out_shape.append(jax.ShapeDtypeStruct((N_PROJ, d), BF16))

        def body(src_ref, x_ref, g_ref, w_ref, tab_ref, o_ref, kr_ref, vr_ref, wq_ref, u_ref):
            _norm_proj_kernel(x_ref, g_ref, w_ref, tab_ref, o_ref, u_ref, kr_ref=kr_ref, vr_ref=vr_ref, wq_ref=wq_ref)

        grid_spec = pltpu.PrefetchScalarGridSpec(num_scalar_prefetch=1, grid=(1, N_PROJ // tn), in_specs=in_specs,
                                                 out_specs=out_specs, scratch_shapes=scratch)
        args = (src_rows, x, g, w_t, tab)
    else:
        in_specs[2] = pl.BlockSpec((tn, d), lambda i, j: (j, 0))

        def body(x_ref, g_ref, w_ref, tab_ref, o_ref, kr_ref, vr_ref, u_ref):
            _norm_proj_kernel(x_ref, g_ref, w_ref, tab_ref, o_ref, u_ref, kr_ref=kr_ref, vr_ref=vr_ref)

        grid_spec = pltpu.PrefetchScalarGridSpec(num_scalar_prefetch=0, grid=(m // tm, N_PROJ // tn),
                                                 in_specs=in_specs, out_specs=out_specs, scratch_shapes=scratch)
        args = (x, g, w_t, tab)
    return pl.pallas_call(
        body,
        grid_spec=grid_spec,
        out_shape=out_shape,
        compiler_params=_cparams("parallel", "arbitrary"),
        name="norm_proj_rope",
    )(*args)


@functools.lru_cache(maxsize=None)
def _rope_tables(start, count, period):
    pos = (start + np.arange(count) % period).astype(np.float64)[:, None]

    def cs(half, theta):
        inv = theta ** (-np.arange(half, dtype=np.float64) / half)
        ang = pos * inv[None, :]
        return np.cos(ang), np.sin(ang)

    ones = lambda n: np.ones((count, n))
    zeros = lambda n: np.zeros((count, n))
    c, s = cs(HEAD_DIM // 2, RET_THETA)
    ret_c = np.concatenate([c, c], 1)
    ret_s = np.concatenate([-s, s], 1)
    c, s = cs(DSA_ROT // 2, ROPE_THETA)
    dsa_c = np.concatenate([c, c, ones(LANES - DSA_ROT)], 1)
    dsa_s = np.concatenate([-s, s, zeros(LANES - DSA_ROT)], 1)
    c, s = cs(IDX_ROT // 2, ROPE_THETA)
    idx_c = np.tile(np.concatenate([c, c, ones(IDX_DH - IDX_ROT)], 1), (1, 2))
    idx_s = np.tile(np.concatenate([-s, s, zeros(IDX_DH - IDX_ROT)], 1), (1, 2))
    return np.concatenate([ret_c, ret_s, dsa_c, dsa_s, idx_c, idx_s], 1).astype(np.float32)


@functools.lru_cache(maxsize=None)
def _decay_tables(rows, n_seq, t_valid):
    log_g = np.log1p(-np.exp2(-5.0 - np.arange(RET_HEADS, dtype=np.float64)))
    r = np.arange(n_seq * rows)
    t = (r % rows).astype(np.float64)
    same = (r[:, None] // rows) == (r[None, :] // rows)
    diff = t[:, None] - t[None, :]
    intra = np.where(same[None] & (diff >= 0)[None], np.exp(log_g[:, None, None] * np.maximum(diff, 0.0)[None]), 0.0)
    ones = np.ones((1, 1, HEAD_DIM))
    qdec = np.exp(log_g[:, None] * (t[None, :] + 1.0))[:, :, None] * ones
    kdec = np.where(t[None, :] < t_valid, np.exp(log_g[:, None] * (t_valid - 1.0 - t[None, :])), 0.0)[:, :, None] * ones
    cdec = np.exp(log_g * t_valid)[:, None, None] * ones
    return tuple(a.astype(np.float32) for a in (intra, qdec, kdec, cdec))


def _groupnorm_gate(o, gate, gn):
    mu = jnp.mean(o, axis=-1, keepdims=True)
    d = o - mu
    var = jnp.mean(d * d, axis=-1, keepdims=True)
    n = d * lax.rsqrt(var + EPS) * gn
    return (gate * _sigmoid(gate) * n).astype(BF16)


def _ret_prompt_kernel(q_ref, k_ref, v_ref, g_ref, gn_ref, intra_ref, qdec_ref, kdec_ref, cdec_ref,
                       y_ref, st_ref, s_scr):
    c = pl.program_id(1)

    @pl.when(c == 0)
    def _():
        s_scr[...] = jnp.zeros_like(s_scr)

    for h in range(RET_HEADS):
        sl = slice(h * HEAD_DIM, (h + 1) * HEAD_DIM)
        k = k_ref[:, sl]
        qb = q_ref[:, sl].astype(BF16)
        kb = k.astype(BF16)
        vb = v_ref[:, sl].astype(BF16)
        s = _dot_nt(qb, kb) * intra_ref[h]
        state = s_scr[h]
        o = _dot(s.astype(BF16), vb) + _dot(qb, state.astype(BF16)) * qdec_ref[h]
        kd = (k * kdec_ref[h]).astype(BF16)
        s_scr[h] = state * cdec_ref[h] + _dot_tn(kd, vb)
        y_ref[:, sl] = _groupnorm_gate(o, g_ref[:, sl], gn_ref[:, sl])

    @pl.when(c == pl.num_programs(1) - 1)
    def _():
        st_ref[0] = s_scr[...]


def _ret_prompt(proj, gn, batch, seq, chunk):
    nc = seq // chunk
    intra, qdec, kdec, cdec = _decay_tables(chunk, 1, chunk)
    w = RET_HEADS * HEAD_DIM
    col = lambda off: (lambda b, c: (b * nc + c, off // w))
    const3 = lambda b, c: (0, 0, 0)
    return pl.pallas_call(
        _ret_prompt_kernel,
        grid=(batch, nc),
        in_specs=[
            pl.BlockSpec((chunk, w), col(COL_RQ)),
            pl.BlockSpec((chunk, w), col(COL_RK)),
            pl.BlockSpec((chunk, w), col(COL_RV)),
            pl.BlockSpec((chunk, w), col(COL_RG)),
            pl.BlockSpec((1, w), lambda b, c: (0, 0)),
            pl.BlockSpec((RET_HEADS, chunk, chunk), const3),
            pl.BlockSpec((RET_HEADS, chunk, HEAD_DIM), const3),
            pl.BlockSpec((RET_HEADS, chunk, HEAD_DIM), const3),
            pl.BlockSpec((RET_HEADS, 1, HEAD_DIM), const3),
        ],
        out_specs=[
            pl.BlockSpec((chunk, w), lambda b, c: (b * nc + c, 0)),
            pl.BlockSpec((1, RET_HEADS, HEAD_DIM, HEAD_DIM), lambda b, c: (b, 0, 0, 0)),
        ],
        out_shape=[
            jax.ShapeDtypeStruct((batch * seq, w), BF16),
            jax.ShapeDtypeStruct((batch, RET_HEADS, HEAD_DIM, HEAD_DIM), F32),
        ],
        scratch_shapes=[pltpu.VMEM((RET_HEADS, HEAD_DIM, HEAD_DIM), F32)],
        compiler_params=_cparams("parallel", "arbitrary"),
        name="ret_prompt",
    )(proj, proj, proj, proj, gn, intra, qdec, kdec, cdec)


def _ret_sample_kernel(q_ref, k_ref, v_ref, g_ref, gn_ref, st_ref, intra_ref, qdec_ref, kdec_ref, cdec_ref,
                       y_ref, so_ref, *, n_batch, rows):
    k = k_ref[...]
    qb = q_ref[...].astype(BF16)
    kb = k.astype(BF16)
    vb = v_ref[...].astype(BF16)
    s = _dot_nt(qb, kb) * intra_ref[0]
    kd = k * kdec_ref[0]
    row_batch = lax.broadcasted_iota(jnp.int32, kd.shape, 0) // rows
    inter = []
    for b in range(n_batch):
        state = st_ref[b, 0]
        inter.append(_dot(qb[b * rows:(b + 1) * rows], state.astype(BF16)))
        kd_b = jnp.where(row_batch == b, kd, 0.0).astype(BF16)
        so_ref[b, 0] = state * cdec_ref[0] + _dot_tn(kd_b, vb)
    o = _dot(s.astype(BF16), vb) + jnp.concatenate(inter, axis=0) * qdec_ref[0]
    y_ref[...] = _groupnorm_gate(o, g_ref[...], gn_ref[...])


def _ret_sample(proj, gn, state, n_batch, rows, t_valid):
    m = n_batch * rows
    intra, qdec, kdec, cdec = _decay_tables(rows, n_batch, t_valid)
    col = lambda off: (lambda h: (0, off // HEAD_DIM + h))
    per_head = lambda h: (h, 0, 0)
    return pl.pallas_call(
        functools.partial(_ret_sample_kernel, n_batch=n_batch, rows=rows),
        grid=(RET_HEADS,),
        in_specs=[
            pl.BlockSpec((m, HEAD_DIM), col(COL_RQ)),
            pl.BlockSpec((m, HEAD_DIM), col(COL_RK)),
            pl.BlockSpec((m, HEAD_DIM), col(COL_RV)),
            pl.BlockSpec((m, HEAD_DIM), col(COL_RG)),
            pl.BlockSpec((1, HEAD_DIM), lambda h: (0, h)),
            pl.BlockSpec((n_batch, 1, HEAD_DIM, HEAD_DIM), lambda h: (0, h, 0, 0)),
            pl.BlockSpec((1, m, m), per_head),
            pl.BlockSpec((1, m, HEAD_DIM), per_head),
            pl.BlockSpec((1, m, HEAD_DIM), per_head),
            pl.BlockSpec((1, 1, HEAD_DIM), per_head),
        ],
        out_specs=[
            pl.BlockSpec((m, HEAD_DIM), lambda h: (0, h)),
            pl.BlockSpec((n_batch, 1, HEAD_DIM, HEAD_DIM), lambda h: (0, h, 0, 0)),
        ],
        out_shape=[
            jax.ShapeDtypeStruct((m, RET_HEADS * HEAD_DIM), BF16),
            jax.ShapeDtypeStruct(state.shape, F32),
        ],
        compiler_params=_cparams("parallel"),
        name="ret_sample",
    )(proj, proj, proj, proj, gn, state, intra, qdec, kdec, cdec)


def _kth_largest(count_ge, next_above, smin, smax, n_adm, topk):
    few = n_adm < topk
    lo = jnp.where(few, -jnp.inf, smin)
    c_lo = jnp.where(few, float(topk), n_adm)
    hi = jnp.where(few, -jnp.inf, smax + jnp.maximum(jnp.abs(smax) * 1e-6, 1e-30))

    def bisect(_, state):
        lo, hi, c_lo = state
        mid = 0.5 * lo + 0.5 * hi
        c = count_ge(mid)
        take = c >= topk
        return jnp.where(take, mid, lo), jnp.where(take, hi, mid), jnp.where(take, c, c_lo)

    lo, hi, c_lo = lax.fori_loop(0, BISECT_STEPS, bisect, (lo, hi, c_lo))
    c_lo = jnp.where(few, float(topk), c_lo)

    def unfinished(state):
        return jnp.max(state[2]) > 0.0

    def step_up(state):
        lo, c_lo, active = state
        nxt, n_eq = next_above(lo)
        c_nxt = c_lo - n_eq
        move = active * jnp.where(c_nxt >= topk, 1.0, 0.0)
        lo = jnp.where(move > 0.0, nxt, lo)
        c_lo = jnp.where(move > 0.0, c_nxt, c_lo)
        return lo, c_lo, move * jnp.where(c_nxt > topk, 1.0, 0.0)

    return lax.while_loop(unfinished, step_up, (lo, c_lo, jnp.where(c_lo > topk, 1.0, 0.0)))[0]


def _dsa_prompt_kernel(dq_ref, iq_ref, iwq_ref, kall_ref, vall_ref, ikall_ref, o_ref,
                       kbf, vtb, ika, ikb, keys, z_a, z_b, zmax_a, zmax_b, m_scr, acc_scr, *, tq, topk):
    z_scr, zmax_scr = (z_a, z_b), (zmax_a, zmax_b)
    i = pl.program_id(1)
    n_chunks = kbf.shape[0]
    scale_log2e = HEAD_DIM ** -0.5 * math.log2(math.e)

    @pl.when(i == 0)
    def _():
        lane = lax.broadcasted_iota(jnp.int32, (tq, LANES), 1)
        for c in range(n_chunks):
            rows = slice(c * tq, (c + 1) * tq)
            kbf[c] = kall_ref[rows, :].astype(BF16)
            for kvh in range(DSA_KV_HEADS):
                sl = slice(kvh * HEAD_DIM, (kvh + 1) * HEAD_DIM)
                vtb[c, kvh, :HEAD_DIM, :] = vall_ref[rows, sl].T.astype(BF16)
                vtb[c, kvh, HEAD_DIM:, :] = jnp.ones((ROW_ALIGN, tq), BF16)
            a = ikall_ref[rows, :]
            ika[c] = jnp.where(lane < IDX_DH, a, 0.0).astype(BF16)
            ikb[c] = jnp.where(lane >= IDX_DH, pltpu.roll(a, IDX_DH, 1), 0.0).astype(BF16)

    w_t = iwq_ref[...].T
    iqb = iq_ref[...].astype(BF16)
    qb = (dq_ref[...] * scale_log2e).astype(BF16)
    t_col = i * tq + lax.broadcasted_iota(jnp.int32, (tq, tq), 1)
    s_row0 = lax.broadcasted_iota(jnp.int32, (tq, tq), 0)

    fold = lambda x: x.reshape(tq // SUBLANES, SUBLANES, tq)

    def score_chunk(c, carry):
        smax, smin = carry
        src = jnp.minimum(c, n_chunks - 1)
        a = ika[src]
        b = ikb[src]
        acc = jnp.zeros((tq, tq), F32)
        for p in range(IDX_HEADS // 2):
            pair = iqb[:, p * LANES:(p + 1) * LANES]
            w0 = w_t[IDX_DH + 2 * p:IDX_DH + 2 * p + 1, :]
            w1 = w_t[IDX_DH + 2 * p + 1:IDX_DH + 2 * p + 2, :]
            acc = acc + jnp.maximum(_dot_nt(a, pair), 0.0) * w0
            acc = acc + jnp.maximum(_dot_nt(b, pair), 0.0) * w1
        admissible = s_row0 + c * tq <= t_col
        keys[c] = jnp.where(admissible, acc, -jnp.inf)
        smax = jnp.maximum(smax, fold(jnp.where(admissible, acc, -jnp.inf)).max(axis=0))
        smin = jnp.minimum(smin, fold(jnp.where(admissible, acc, jnp.inf)).min(axis=0))
        return smax, smin

    n_pairs = (i + 2) // 2

    def score_pair(j, carry):
        return score_chunk(2 * j + 1, score_chunk(2 * j, carry))

    smax, smin = lax.fori_loop(0, n_pairs, score_pair, (jnp.full((SUBLANES, tq), -jnp.inf, F32),
                                                        jnp.full((SUBLANES, tq), jnp.inf, F32)))
    smax = smax.max(axis=0, keepdims=True)
    smin = smin.min(axis=0, keepdims=True)

    def count_ge(t):
        def pair(j, cnt):
            for c in (2 * j, 2 * j + 1):
                cnt = cnt + fold(jnp.where(keys[c] >= t, 1.0, 0.0)).sum(axis=0)
            return cnt

        return lax.fori_loop(0, n_pairs, pair, jnp.zeros((SUBLANES, tq), F32)).sum(axis=0, keepdims=True)

    def next_above(lo):
        def pair(j, carry):
            nxt, n_eq = carry
            for c in (2 * j, 2 * j + 1):
                k = keys[c]
                nxt = jnp.minimum(nxt, fold(jnp.where(k > lo, k, jnp.inf)).min(axis=0))
                n_eq = n_eq + fold(jnp.where(k == lo, 1.0, 0.0)).sum(axis=0)
            return nxt, n_eq

        nxt, n_eq = lax.fori_loop(0, n_pairs, pair, (jnp.full((SUBLANES, tq), jnp.inf, F32),
                                                     jnp.zeros((SUBLANES, tq), F32)))
        return nxt.min(axis=0, keepdims=True), n_eq.sum(axis=0, keepdims=True)

    thr = _kth_largest(count_ge, next_above, smin, smax, (t_col[:1, :] + 1).astype(F32), topk)

    m_scr[...] = jnp.full_like(m_scr, NEG)
    acc_scr[...] = jnp.zeros_like(acc_scr)

    kv_of = lambda h: h // (DSA_HEADS // DSA_KV_HEADS)

    def logits(c, slot):
        bias = jnp.where(s_row0 + c * tq <= t_col, jnp.where(keys[c] >= thr, 0.0, NEG), NEG)
        kc = kbf[c]
        for h in range(DSA_HEADS):
            ksl = slice(kv_of(h) * HEAD_DIM, (kv_of(h) + 1) * HEAD_DIM)
            z = _dot_nt(kc[:, ksl], qb[:, h * HEAD_DIM:(h + 1) * HEAD_DIM]) + bias
            z_scr[slot][h] = z
            zmax_scr[slot][h] = z.max(axis=0, keepdims=True)

    def update(c, slot):
        for h in range(DSA_HEADS):
            m_old = m_scr[h]
            m_new = jnp.maximum(m_old, zmax_scr[slot][h])
            p = jnp.exp2(z_scr[slot][h] - m_new)
            acc_scr[h] = acc_scr[h] * jnp.exp2(m_old - m_new) + _dot(vtb[c, kv_of(h)], p.astype(BF16))
            m_scr[h] = m_new

    n = i + 1
    logits(0, 0)

    def two_chunks(k, carry):
        update(2 * k, 0)
        logits(2 * k + 1, 1)
        update(2 * k + 1, 1)
        logits(2 * k + 2, 0)
        return carry

    lax.fori_loop(0, (n - 1) // 2, two_chunks, 0)

    @pl.when(n % 2 == 1)
    def _():
        update(n - 1, 0)

    @pl.when(n % 2 == 0)
    def _():
        update(n - 2, 0)
        logits(n - 1, 1)
        update(n - 1, 1)

    for h in range(DSA_HEADS):
        o = acc_scr[h, :HEAD_DIM, :] * (1.0 / acc_scr[h, HEAD_DIM:HEAD_DIM + 1, :])
        o_ref[:, h * HEAD_DIM:(h + 1) * HEAD_DIM] = o.T.astype(BF16)


def _dsa_prompt(proj, batch, seq, tq):
    nq = seq // tq
    topk = min(TOPK_MAX, seq // 4)
    qw = DSA_HEADS * HEAD_DIM
    kvw = DSA_KV_HEADS * HEAD_DIM
    return pl.pallas_call(
        functools.partial(_dsa_prompt_kernel, tq=tq, topk=topk),
        grid=(batch, nq),
        in_specs=[
            pl.BlockSpec((tq, qw), lambda b, i: (b * nq + i, COL_DQ // qw)),
            pl.BlockSpec((tq, qw), lambda b, i: (b * nq + i, COL_IQ // qw)),
            pl.BlockSpec((tq, LANES), lambda b, i: (b * nq + i, COL_IK // LANES)),
            pl.BlockSpec((seq, kvw), lambda b, i: (b, COL_DK // kvw)),
            pl.BlockSpec((seq, kvw), lambda b, i: (b, COL_DV // kvw)),
            pl.BlockSpec((seq, LANES), lambda b, i: (b, COL_IK // LANES)),
        ],
        out_specs=pl.BlockSpec((tq, qw), lambda b, i: (b * nq + i, 0)),
        out_shape=jax.ShapeDtypeStruct((batch * seq, qw), BF16),
        scratch_shapes=[
            pltpu.VMEM((nq, tq, kvw), BF16),
            pltpu.VMEM((nq, DSA_KV_HEADS, HEAD_DIM + ROW_ALIGN, tq), BF16),
            pltpu.VMEM((nq, tq, LANES), BF16),
            pltpu.VMEM((nq, tq, LANES), BF16),
            pltpu.VMEM((nq + 1, tq, tq), F32),
            pltpu.VMEM((DSA_HEADS, tq, tq), F32),
            pltpu.VMEM((DSA_HEADS, tq, tq), F32),
            pltpu.VMEM((DSA_HEADS, 1, tq), F32),
            pltpu.VMEM((DSA_HEADS, 1, tq), F32),
            pltpu.VMEM((DSA_HEADS, 1, tq), F32),
            pltpu.VMEM((DSA_HEADS, HEAD_DIM + ROW_ALIGN, tq), F32),
        ],
        compiler_params=_cparams("parallel", "arbitrary"),
        name="dsa_prompt",
    )(proj, proj, proj, proj, proj, proj)


def _idx_sample_kernel(pt_ref, iq_ref, iw_ref, iknew_ref, *rest, pages):
    page_refs = rest[:pages]
    sp_ref, sn_ref = rest[pages:]
    iqb = iq_ref[0].astype(BF16)
    w = iw_ref[0]
    rows = iqb.shape[0] // IDX_HEADS

    def scores(ik_t):
        d = jnp.maximum(_dot(iqb, ik_t.astype(BF16)), 0.0) * w
        return d.reshape(rows, IDX_HEADS, LANES).sum(axis=1)

    for p in range(pages):
        sp_ref[0, :, p * PAGE_SIZE:(p + 1) * PAGE_SIZE] = scores(page_refs[p][0])
    t = lax.broadcasted_iota(jnp.int32, (rows, LANES), 0)
    j = lax.broadcasted_iota(jnp.int32, (rows, LANES), 1)
    sn_ref[0] = jnp.where(j <= t, scores(iknew_ref[0]), -jnp.inf)


def _idx_sample(page_table, iq, iw, ik_new, pool_ik, pages):
    n_batch, n_pages = page_table.shape
    rows16 = iq.shape[1]
    rows = rows16 // IDX_HEADS
    past = n_pages * PAGE_SIZE
    page_specs = [pl.BlockSpec((1, IDX_DH, PAGE_SIZE), functools.partial(
        lambda b, g, pt, k: (pt[b, g * pages + k], 0, 0), k=k)) for k in range(pages)]
    grid_spec = pltpu.PrefetchScalarGridSpec(
        num_scalar_prefetch=1,
        grid=(n_batch, n_pages // pages),
        in_specs=[
            pl.BlockSpec((1, rows16, IDX_DH), lambda b, g, pt: (b, 0, 0)),
            pl.BlockSpec((1, rows16, LANES), lambda b, g, pt: (b, 0, 0)),
            pl.BlockSpec((1, IDX_DH, PAGE_SIZE), lambda b, g, pt: (b, 0, 0)),
        ] + page_specs,
        out_specs=[
            pl.BlockSpec((1, rows, pages * PAGE_SIZE), lambda b, g, pt: (b, 0, g)),
            pl.BlockSpec((1, rows, LANES), lambda b, g, pt: (b, 0, 0)),
        ],
    )
    return pl.pallas_call(
        functools.partial(_idx_sample_kernel, pages=pages),
        grid_spec=grid_spec,
        out_shape=[
            jax.ShapeDtypeStruct((n_batch, rows, past), F32),
            jax.ShapeDtypeStruct((n_batch, rows, LANES), F32),
        ],
        compiler_params=_cparams("parallel", "arbitrary"),
        name="idx_sample",
    )(page_table, iq, iw, ik_new, *([pool_ik] * pages))


def _sel_sample_kernel(sp_ref, sn_ref, bp_ref, bn_ref, *, topk):
    n_batch, rows, past = sp_ref.shape
    sp = sp_ref[...].reshape(n_batch * rows, past)
    sn = sn_ref[...].reshape(n_batch * rows, LANES)
    t = lax.broadcasted_iota(jnp.int32, (n_batch * rows, LANES), 0) % rows
    j = lax.broadcasted_iota(jnp.int32, (n_batch * rows, LANES), 1)
    admissible_new = j <= t
    row_sum = lambda a: a.sum(axis=1, keepdims=True)
    row_min = lambda a: a.min(axis=1, keepdims=True)

    def count_ge(thr):
        return row_sum(jnp.where(sp >= thr, 1.0, 0.0)) + row_sum(jnp.where(sn >= thr, 1.0, 0.0))

    def next_above(lo):
        nxt = jnp.minimum(row_min(jnp.where(sp > lo, sp, jnp.inf)), row_min(jnp.where(sn > lo, sn, jnp.inf)))
        return nxt, row_sum(jnp.where(sp == lo, 1.0, 0.0)) + row_sum(jnp.where(sn == lo, 1.0, 0.0))

    smin = jnp.minimum(row_min(sp), row_min(jnp.where(admissible_new, sn, jnp.inf)))
    smax = jnp.maximum(sp.max(axis=1, keepdims=True), sn.max(axis=1, keepdims=True))
    n_adm = float(past) + row_sum(jnp.where(admissible_new, 1.0, 0.0))
    thr = _kth_largest(count_ge, next_above, smin, smax, n_adm, topk)
    bp_ref[...] = jnp.where(sp >= thr, 0.0, NEG).reshape(n_batch, rows, past)
    bn_ref[...] = jnp.where(admissible_new, jnp.where(sn >= thr, 0.0, NEG), NEG).reshape(n_batch, rows, LANES)


def _sel_sample(scores_past, scores_new, topk):
    full = lambda a: pl.BlockSpec(a.shape, lambda: (0,) * a.ndim)
    return pl.pallas_call(
        functools.partial(_sel_sample_kernel, topk=topk),
        in_specs=[full(scores_past), full(scores_new)],
        out_specs=[full(scores_past), full(scores_new)],
        out_shape=[jax.ShapeDtypeStruct(scores_past.shape, F32), jax.ShapeDtypeStruct(scores_new.shape, F32)],
        compiler_params=pltpu.CompilerParams(vmem_limit_bytes=VMEM_LIMIT),
        name="sel_sample",
    )(scores_past, scores_new)


def _dsa_sample_kernel(pt_ref, bp_ref, bn_ref, q_ref, knew_ref, vnew_ref, *rest, pages, steps):
    k_refs = rest[:pages]
    v_refs = rest[pages:2 * pages]
    o_ref, m_scr, l_scr, acc_scr = rest[2 * pages:]
    g = pl.program_id(0) % steps
    scale = HEAD_DIM ** -0.5
    group = DSA_HEADS // DSA_KV_HEADS

    @pl.when(g == 0)
    def _():
        m_scr[...] = jnp.full_like(m_scr, NEG)
        l_scr[...] = jnp.zeros_like(l_scr)
        acc_scr[...] = jnp.zeros_like(acc_scr)

    def head_rows(ref, kvh):
        return ref[0, pl.ds(kvh, PAGE_SIZE, stride=DSA_KV_HEADS), :].astype(BF16)

    def attend(k_pages, v_pages, bias):
        bias_g = jnp.concatenate([bias] * group, axis=0)
        kv_heads = range(DSA_KV_HEADS)
        qb = [q_ref[0, kvh].astype(BF16) for kvh in kv_heads]
        raw = [jnp.concatenate([_dot_nt(qb[kvh], head_rows(k, kvh)) for k in k_pages], axis=1) for kvh in kv_heads]
        pb, alpha = [], []
        for kvh in kv_heads:
            z = jnp.where(bias_g < 0.0, NEG, raw[kvh])
            m_old = m_scr[kvh]
            m_new = jnp.maximum(m_old, z.max(axis=1, keepdims=True))
            alpha.append(jnp.exp((m_old - m_new) * scale))
            p = jnp.exp((z - m_new) * scale)
            l_scr[kvh] = l_scr[kvh] * alpha[kvh] + p.sum(axis=1, keepdims=True)
            m_scr[kvh] = m_new
            pb.append(p.astype(BF16))
        for kvh in kv_heads:
            pv = _dot(pb[kvh][:, :PAGE_SIZE], head_rows(v_pages[0], kvh))
            for n in range(1, len(v_pages)):
                pv = pv + _dot(pb[kvh][:, n * PAGE_SIZE:(n + 1) * PAGE_SIZE], head_rows(v_pages[n], kvh))
            acc_scr[kvh] = acc_scr[kvh] * alpha[kvh] + pv

    attend(k_refs, v_refs, bp_ref[0])

    @pl.when(g == steps - 1)
    def _():
        attend([knew_ref], [vnew_ref], bn_ref[0])
        for kvh in range(DSA_KV_HEADS):
            o_ref[0, kvh] = (acc_scr[kvh] * (1.0 / l_scr[kvh])).astype(BF16)


def _dsa_sample(page_table, bias_past, bias_new, q, k_new, v_new, pool_k, pool_v, steps):
    n_batch, n_pages = page_table.shape
    pages = n_pages // steps
    rows = bias_past.shape[1]
    page_rows = DSA_KV_HEADS * PAGE_SIZE
    grows = q.shape[2]
    page_spec = lambda k: pl.BlockSpec((1, page_rows, HEAD_DIM), functools.partial(
        lambda i, pt, k: (pt[i // steps, (i % steps) * pages + k], 0, 0), k=k))
    per_batch3 = lambda i, pt: (i // steps, 0, 0)
    per_batch4 = lambda i, pt: (i // steps, 0, 0, 0)
    in_specs = [
        pl.BlockSpec((1, rows, pages * PAGE_SIZE), lambda i, pt: (i // steps, 0, i % steps)),
        pl.BlockSpec((1, rows, LANES), per_batch3),
        pl.BlockSpec((1, DSA_KV_HEADS, grows, HEAD_DIM), per_batch4),
        pl.BlockSpec((1, page_rows, HEAD_DIM), per_batch3),
        pl.BlockSpec((1, page_rows, HEAD_DIM), per_batch3),
    ] + [page_spec(k) for k in range(pages)] * 2
    grid_spec = pltpu.PrefetchScalarGridSpec(
        num_scalar_prefetch=1,
        grid=(n_batch * steps,),
        in_specs=in_specs,
        out_specs=pl.BlockSpec((1, DSA_KV_HEADS, grows, HEAD_DIM), per_batch4),
        scratch_shapes=[
            pltpu.VMEM((DSA_KV_HEADS, grows, 1), F32),
            pltpu.VMEM((DSA_KV_HEADS, grows, 1), F32),
            pltpu.VMEM((DSA_KV_HEADS, grows, HEAD_DIM), F32),
        ],
    )
    return pl.pallas_call(
        functools.partial(_dsa_sample_kernel, pages=pages, steps=steps),
        grid_spec=grid_spec,
        out_shape=jax.ShapeDtypeStruct((n_batch, DSA_KV_HEADS, grows, HEAD_DIM), BF16),
        compiler_params=_cparams("arbitrary"),
        name="dsa_sample",
    )(page_table, bias_past, bias_new, q, k_new, v_new, *([pool_k] * pages), *([pool_v] * pages))


def _merge_kernel(x_ref, ry_ref, do_ref, ga_ref, gb_ref, wr_ref, wd_ref, wo_ref, o_ref):
    ya = _dot(ry_ref[...], wr_ref[...])
    yb = _dot(do_ref[...], wd_ref[...])
    merged = _sigmoid(ga_ref[...]) * ya + _sigmoid(gb_ref[...]) * yb
    o_ref[...] = x_ref[...] + _dot(merged.astype(BF16), wo_ref[...])


def _merge(x, ret_y, dsa_o, proj, w_ret_out, w_dsa_out, w_o, tm):
    m, d = x.shape
    w = ret_y.shape[1]
    const = lambda i: (0, 0)
    resident = dict(pipeline_mode=pl.Buffered(1))
    return pl.pallas_call(
        _merge_kernel,
        grid=(m // tm,),
        in_specs=[
            pl.BlockSpec((tm, d), lambda i: (i, 0)),
            pl.BlockSpec((tm, w), lambda i: (i, 0)),
            pl.BlockSpec((tm, w), lambda i: (i, 0)),
            pl.BlockSpec((tm, d), lambda i: (i, COL_GA // d)),
            pl.BlockSpec((tm, d), lambda i: (i, COL_GB // d)),
            pl.BlockSpec((w, d), const, **resident),
            pl.BlockSpec((w, d), const, **resident),
            pl.BlockSpec((d, d), const, **resident),
        ],
        out_specs=pl.BlockSpec((tm, d), lambda i: (i, 0)),
        out_shape=jax.ShapeDtypeStruct((m, d), F32),
        compiler_params=_cparams("parallel"),
        name="merge_out",
    )(x, ret_y, dsa_o, proj, proj, w_ret_out, w_dsa_out, w_o)


def _mem_attn_kernel(h_ref, g_ref, wq_ref, mk_ref, mv_ref, wo_ref, o_ref, *, rows_per_batch):
    h = h_ref[...]
    qm = _dot(_rmsnorm_bf16(h, g_ref[...]), wq_ref[...])
    tm = h.shape[0]
    nk = mk_ref.shape[0]
    scale = HEAD_DIM ** -0.5
    if rows_per_batch is not None:
        rb = lax.broadcasted_iota(jnp.int32, (tm, nk), 0) // rows_per_batch
        kb = lax.broadcasted_iota(jnp.int32, (tm, nk), 1) // MEM_LEN
        same = rb == kb
    outs = []
    for hd in range(MEM_HEADS):
        sl = slice(hd * HEAD_DIM, (hd + 1) * HEAD_DIM)
        z = _dot_nt(qm[:, sl].astype(BF16), mk_ref[:, sl].astype(BF16)) * scale
        if rows_per_batch is not None:
            z = jnp.where(same, z, NEG)
        p = jnp.exp(z - z.max(axis=-1, keepdims=True))
        pv = _dot(p.astype(BF16), mv_ref[:, sl].astype(BF16))
        outs.append(pv * (1.0 / p.sum(axis=-1, keepdims=True)))
    om = jnp.concatenate(outs, axis=1).astype(BF16)
    o_ref[...] = h + _dot(om, wo_ref[...])


def _mem_attn(h, g, w_mq, mk, mv, w_mo, *, tm, batch_tiles, mk_col, mv_col, nk, rows_per_batch):
    m, d = h.shape
    const = lambda i: (0, 0)
    if batch_tiles:
        kmap = lambda col: (lambda i: (i // batch_tiles, col))
    else:
        kmap = lambda col: (lambda i: (0, col))
    return pl.pallas_call(
        functools.partial(_mem_attn_kernel, rows_per_batch=rows_per_batch),
        grid=(m // tm,),
        in_specs=[
            pl.BlockSpec((tm, d), lambda i: (i, 0)),
            pl.BlockSpec((1, d), const),
            pl.BlockSpec((d, MEM_W), const),
            pl.BlockSpec((nk, MEM_W), kmap(mk_col)),
            pl.BlockSpec((nk, MEM_W), kmap(mv_col)),
            pl.BlockSpec((MEM_W, d), const),
        ],
        out_specs=pl.BlockSpec((tm, d), lambda i: (i, 0)),
        out_shape=jax.ShapeDtypeStruct((m, d), F32),
        compiler_params=_cparams("parallel"),
        name="mem_attn",
    )(h, g, w_mq, mk, mv, w_mo)


def _mlp_kernel(h_ref, g_ref, wu_ref, wd_ref, gf_ref, o_ref, *rest):
    u_ref = rest[-1]
    j = pl.program_id(1)

    @pl.when(j == 0)
    def _():
        u_ref[...] = _rmsnorm_bf16(h_ref[...], g_ref[...])
        o_ref[...] = h_ref[...]

    wu, wd = wu_ref[...], wd_ref[...]
    if len(rest) == 3:
        wu, wd = wu.astype(BF16), wd.astype(BF16)
        rest[0][...] = wu
        rest[1][...] = wd
    a = jnp.maximum(_dot(u_ref[...], wu), 0.0)
    o_ref[...] += _dot((a * a).astype(BF16), wd)

    @pl.when(j == pl.num_programs(1) - 1)
    def _():
        y = o_ref[...]
        ms = jnp.mean(y * y, axis=-1, keepdims=True)
        o_ref[...] = (y * lax.rsqrt(ms + EPS)) * gf_ref[...]


def _mlp(h, g, w_up, w_down, g_final, tm, tf):
    m, d = h.shape
    ff = w_up.shape[1]
    emit = w_up.dtype != BF16
    assert not emit or m == tm, "weight copies are written once, by a call with one row tile"
    wu_spec = pl.BlockSpec((d, tf), lambda i, j: (0, j))
    wd_spec = pl.BlockSpec((tf, d), lambda i, j: (j, 0))
    out_specs = [pl.BlockSpec((tm, d), lambda i, j: (i, 0))]
    out_shape = [jax.ShapeDtypeStruct((m, d), F32)]
    if emit:
        out_specs += [wu_spec, wd_spec]
        out_shape += [jax.ShapeDtypeStruct(w_up.shape, BF16), jax.ShapeDtypeStruct(w_down.shape, BF16)]
    out = pl.pallas_call(
        _mlp_kernel,
        grid=(m // tm, ff // tf),
        in_specs=[
            pl.BlockSpec((tm, d), lambda i, j: (i, 0)),
            pl.BlockSpec((1, d), lambda i, j: (0, 0)),
            wu_spec,
            wd_spec,
            pl.BlockSpec((1, d), lambda i, j: (0, 0)),
        ],
        out_specs=out_specs,
        out_shape=out_shape,
        scratch_shapes=[pltpu.VMEM((tm, d), BF16)],
        compiler_params=_cparams("parallel", "arbitrary"),
        name="mlp_final",
    )(h, g, w_up, w_down, g_final)
    return out if emit else out[0]


def _pick_tile(n, pref):
    t = min(n, pref)
    while n % t:
        t //= 2
    return t


def kernel(x_prompt, x_sample, mem_prompt, cache_k, cache_v, cache_idx_k, state_ret, cache_mem_k, cache_mem_v,
           page_table, g_mix, w_in, gn_ret, w_ret_out, w_dsa_out, w_o, g_mem, g_memkv, w_mq, w_mk, w_mv, w_mo,
           g_mlp, w_up, w_down, g_final):
    assert w_in.shape[0] == 1, "one layer"
    batch, seq, d = x_prompt.shape
    n_dec, t_dec, _ = x_sample.shape
    n_pages = page_table.shape[1]
    past = n_pages * PAGE_SIZE
    mem_len = mem_prompt.shape[1]
    assert mem_len == MEM_LEN and t_dec <= SUBLANES
    row = lambda v: v.reshape(1, -1)

    w_ret_out_b, w_dsa_out_b, w_o_b = (w[0].astype(BF16) for w in (w_ret_out, w_dsa_out, w_o))
    w_mq_b, w_mo_b = w_mq[0].astype(BF16), w_mo[0].astype(BF16)
    w_mkv_b = jnp.concatenate([w_mk[0], w_mv[0]], axis=1).astype(BF16)
    g_mix_r, gn_r, g_mem_r, g_memkv_r, g_mlp_r, g_final_r = (
        row(v) for v in (g_mix[0], gn_ret[0], g_mem[0], g_memkv[0], g_mlp[0], g_final))

    rows = SUBLANES
    m_s = n_dec * rows
    xs = jnp.pad(x_sample, ((0, 0), (0, rows - t_dec), (0, 0))).reshape(m_s, d)
    proj_s, k_rows_s, v_rows_s, w_proj_b = _norm_proj_rope(xs, g_mix_r, jnp.swapaxes(w_in[0], 0, 1),
                                                           _rope_tables(past, m_s, rows), m_s, 1)

    m_p = batch * seq
    xp = x_prompt.reshape(m_p, d)
    tm_proj = _pick_tile(seq, 1024)
    proj_p, k_rows_p, v_rows_p = _norm_proj_rope(xp, g_mix_r, w_proj_b, _rope_tables(0, seq, seq), tm_proj,
                                                 seq // tm_proj)
    ret_y_p, ret_state_p = _ret_prompt(proj_p, gn_r, batch, seq, _pick_tile(seq, 256))
    dsa_o_p = _dsa_prompt(proj_p, batch, seq, _pick_tile(seq, 256))
    kv_p = _norm_proj(mem_prompt.reshape(batch * mem_len, d), g_memkv_r, w_mkv_b, mem_len, PROJ_TN)
    tm_t = _pick_tile(seq, 256)
    h_p = _merge(xp, ret_y_p, dsa_o_p, proj_p, w_ret_out_b, w_dsa_out_b, w_o_b, tm_t)
    h_p = _mem_attn(h_p, g_mem_r, w_mq_b, kv_p, kv_p, w_mo_b, tm=tm_t, batch_tiles=seq // tm_t,
                    mk_col=0, mv_col=1, nk=mem_len, rows_per_batch=None)

    ret_y_s, ret_state_s = _ret_sample(proj_s, gn_r, state_ret[0], n_dec, rows, t_dec)

    proj_s3 = proj_s.reshape(n_dec, rows, N_PROJ)
    iq_s = proj_s3[:, :, COL_IQ:COL_IQ + IDX_HEADS * IDX_DH].reshape(n_dec, rows * IDX_HEADS, IDX_DH)
    iw_s = proj_s3[:, :, COL_IK + IDX_DH:COL_IK + IDX_DH + IDX_HEADS].reshape(n_dec, rows * IDX_HEADS, 1)
    iw_s = jnp.broadcast_to(iw_s, (n_dec, rows * IDX_HEADS, LANES))
    kvw = DSA_KV_HEADS * HEAD_DIM
    page_rows = DSA_KV_HEADS * PAGE_SIZE
    pad_keys = lambda a: jnp.pad(a, ((0, 0), (0, PAGE_SIZE - rows), (0, 0)))
    ik_new_t = jnp.swapaxes(pad_keys(proj_s3[:, :, COL_IK:COL_IK + IDX_DH]), 1, 2)
    pool_ik_t = jnp.swapaxes(cache_idx_k, 2, 3).reshape(-1, IDX_DH, PAGE_SIZE)
    new_page = lambda a: jnp.pad(a.reshape(n_dec, DSA_KV_HEADS * rows, HEAD_DIM),
                                 ((0, 0), (0, page_rows - DSA_KV_HEADS * rows), (0, 0)))
    k_new, v_new = new_page(k_rows_s), new_page(v_rows_s)
    scores_past, scores_new = _idx_sample(page_table, iq_s, iw_s, ik_new_t, pool_ik_t, _pick_tile(n_pages, 32))
    group = DSA_HEADS // DSA_KV_HEADS
    dq_s = proj_s3[:, :, COL_DQ:COL_DQ + DSA_HEADS * HEAD_DIM].reshape(n_dec, rows, DSA_KV_HEADS, group, HEAD_DIM)
    dq_s = dq_s.transpose(0, 2, 3, 1, 4).reshape(n_dec, DSA_KV_HEADS, group * rows, HEAD_DIM)
    bias_past, bias_new = _sel_sample(scores_past, scores_new, min(TOPK_MAX, (past + t_dec) // 4))
    dsa_o_s = _dsa_sample(page_table, bias_past, bias_new, dq_s, k_new, v_new,
                          cache_k.reshape(-1, page_rows, HEAD_DIM), cache_v.reshape(-1, page_rows, HEAD_DIM),
                          n_pages // _pick_tile(n_pages, MAX_PAGES_PER_STEP))
    dsa_o_s = dsa_o_s.reshape(n_dec, DSA_KV_HEADS, group, rows, HEAD_DIM).transpose(0, 3, 1, 2, 4)
    dsa_o_s = dsa_o_s.reshape(m_s, DSA_HEADS * HEAD_DIM)

    h_s = _merge(xs, ret_y_s, dsa_o_s, proj_s, w_ret_out_b, w_dsa_out_b, w_o_b, m_s)
    h_s = _mem_attn(h_s, g_mem_r, w_mq_b, cache_mem_k[0].reshape(n_dec * mem_len, MEM_W),
                    cache_mem_v[0].reshape(n_dec * mem_len, MEM_W), w_mo_b, tm=m_s, batch_tiles=0,
                    mk_col=0, mv_col=0, nk=n_dec * mem_len, rows_per_batch=rows)
    y_s, w_up_b, w_down_b = _mlp(h_s, g_mlp_r, w_up[0], w_down[0], g_final_r, m_s, 512)
    y_p = _mlp(h_p, g_mlp_r, w_up_b, w_down_b, g_final_r, _pick_tile(seq, 1024), 512)

    def rows_p(col, width, tail):
        return proj_p[:, col:col + width].reshape((1, batch, seq) + tail)

    def rows_s(col, width, tail):
        return proj_s3[:, :t_dec, col:col + width].reshape((1, n_dec, t_dec) + tail)

    kv_p_rows = lambda a: a.reshape(1, batch, seq, DSA_KV_HEADS, HEAD_DIM)
    kv_s_rows = lambda a: a.reshape(1, n_dec, rows, DSA_KV_HEADS, HEAD_DIM)[:, :, :t_dec]
    return (
        y_p.reshape(batch, seq, d),
        y_s.reshape(n_dec, rows, d)[:, :t_dec],
        ret_state_p[None],
        kv_p_rows(k_rows_p),
        kv_p_rows(v_rows_p),
        rows_p(COL_IK, IDX_DH, (IDX_DH,)),
        kv_p[:, :MEM_W].reshape(1, batch, mem_len, MEM_HEADS, HEAD_DIM),
        kv_p[:, MEM_W:].reshape(1, batch, mem_len, MEM_HEADS, HEAD_DIM),
        ret_state_s[None],
        kv_s_rows(k_rows_s),
        kv_s_rows(v_rows_s),
        rows_s(COL_IK, IDX_DH, (IDX_DH,)),
    )
```

```python
import functools
import math

import jax
import jax.numpy as jnp
import numpy as np
from jax import lax
from jax.experimental import pallas as pl
from jax.experimental.pallas import tpu as pltpu

F32 = jnp.float32
BF16 = jnp.bfloat16

D_MODEL = 2048
RET_HEADS = 8
HEAD_DIM = 128
DSA_HEADS = 8
DSA_KV_HEADS = 2
IDX_HEADS = 16
IDX_DH = 64
TOPK_MAX = 256
PAGE_SIZE = 128
MEM_LEN = 256
MEM_HEADS = 4
MEM_W = MEM_HEADS * HEAD_DIM
D_FF = 4 * D_MODEL
RET_THETA = 10000.0
ROPE_THETA = 500000.0
EPS = 1e-6
LANES = 128
SUBLANES = 8
ROW_ALIGN = 16
VMEM_LIMIT = 56 * 1024 * 1024
NEG = -1e30
BISECT_STEPS = 16
MAX_PAGES_PER_STEP = 32

COL_GA = 0
COL_GB = 2048
COL_RQ = 4096
COL_RK = 5120
COL_RV = 6144
COL_RG = 7168
COL_DQ = 8192
COL_IQ = 9216
COL_DK = 10240
COL_DV = 10496
COL_IK = 10752
N_PROJ = 11264
PROJ_TN = 512
PROJ_TN_BF16 = 1024
ROPE_NONE, ROPE_RET, ROPE_RETK, ROPE_DSA, ROPE_IDX, ROPE_IKW = range(6)
COL_ROPE = ((ROPE_NONE,) * 32 + (ROPE_RET,) * 8 + (ROPE_RETK,) * 8 + (ROPE_NONE,) * 16 + (ROPE_DSA,) * 8
            + (ROPE_IDX,) * 8 + (ROPE_DSA,) * 2 + (ROPE_NONE,) * 2 + (ROPE_IKW,) + (ROPE_NONE,) * 3)
TAB_RET_C, TAB_RET_S, TAB_DSA_C, TAB_DSA_S, TAB_IDX_C, TAB_IDX_S = range(6)
TAB_W = 6 * LANES
DSA_ROT = HEAD_DIM // 4
IDX_ROT = IDX_DH // 4


def _cparams(*sem):
    return pltpu.CompilerParams(dimension_semantics=sem, vmem_limit_bytes=VMEM_LIMIT)


def _dot(a, b):
    return jnp.dot(a, b, preferred_element_type=F32)


def _dot_nt(a, b):
    return lax.dot_general(a, b, (((1,), (1,)), ((), ())), preferred_element_type=F32)


def _dot_tn(a, b):
    return lax.dot_general(a, b, (((0,), (0,)), ((), ())), preferred_element_type=F32)


def _rmsnorm_bf16(x, g):
    ms = jnp.mean(x * x, axis=-1, keepdims=True)
    return ((x * lax.rsqrt(ms + EPS)) * g).astype(BF16)


def _sigmoid(x):
    return 1.0 / (1.0 + jnp.exp(-x))


def _tab(tab_ref, which):
    return tab_ref[:, which * LANES:(which + 1) * LANES]


def _rope_cols(o_ref, g, c, s, half, period):
    a = o_ref[:, g * LANES:(g + 1) * LANES]
    if 2 * half == LANES:
        partner = pltpu.roll(a, half, 1)
    else:
        lane = lax.broadcasted_iota(jnp.int32, (1, LANES), 1)
        first = (lane & (period - 1)) < half
        partner = jnp.where(first, pltpu.roll(a, LANES - half, 1), pltpu.roll(a, half, 1))
    o_ref[:, g * LANES:(g + 1) * LANES] = a * c + partner * s


def _finish_block(dst_ref, tab_ref, plan, kv_group, kr_ref, vr_ref):
    for g, kind in enumerate(plan):
        rope = functools.partial(_rope_cols, dst_ref, g)
        if kind == ROPE_RET:
            rope(_tab(tab_ref, TAB_RET_C), _tab(tab_ref, TAB_RET_S), LANES // 2, LANES)
        elif kind == ROPE_RETK:
            scale = HEAD_DIM ** -0.5
            rope(_tab(tab_ref, TAB_RET_C) * scale, _tab(tab_ref, TAB_RET_S) * scale, LANES // 2, LANES)
        elif kind == ROPE_DSA:
            rope(_tab(tab_ref, TAB_DSA_C), _tab(tab_ref, TAB_DSA_S), DSA_ROT // 2, LANES)
        elif kind == ROPE_IDX:
            rope(_tab(tab_ref, TAB_IDX_C), _tab(tab_ref, TAB_IDX_S), IDX_ROT // 2, IDX_DH)
        elif kind == ROPE_IKW:
            lane = lax.broadcasted_iota(jnp.int32, (1, LANES), 1)
            is_ik = lane < IDX_DH
            rope(jnp.where(is_ik, _tab(tab_ref, TAB_IDX_C), 1.0), jnp.where(is_ik, _tab(tab_ref, TAB_IDX_S), 0.0),
                 IDX_ROT // 2, IDX_DH)
    if kv_group is not None:
        rows = dst_ref.shape[0]
        for kvh in range(DSA_KV_HEADS):
            rows_of_head = pl.ds(kvh, rows, stride=DSA_KV_HEADS)
            gk, gv = kv_group + kvh, kv_group + DSA_KV_HEADS + kvh
            kr_ref[rows_of_head, :] = dst_ref[:, gk * LANES:(gk + 1) * LANES]
            vr_ref[rows_of_head, :] = dst_ref[:, gv * LANES:(gv + 1) * LANES]


def _norm_proj_kernel(x_ref, g_ref, w_ref, tab_ref, o_ref, u_ref, *, kr_ref=None, vr_ref=None, wq_ref=None):
    j = pl.program_id(1)

    @pl.when(j == 0)
    def _():
        u_ref[...] = _rmsnorm_bf16(x_ref[...], g_ref[...])

    if tab_ref is None:
        o_ref[...] = _dot(u_ref[...], w_ref[...])
        return
    wb = w_ref[...]
    if wb.dtype != BF16:
        wb = wb.astype(BF16)
        wq_ref[...] = wb
    o_ref[...] = _dot_nt(u_ref[...], wb)
    groups = o_ref.shape[1] // LANES

    plans = {}
    for b in range(N_PROJ // (groups * LANES)):
        plans.setdefault(COL_ROPE[b * groups:(b + 1) * groups], []).append(b)
    kv_block, kv_group = divmod(COL_DK // LANES, groups)
    for plan, blocks in plans.items():
        if all(kind == ROPE_NONE for kind in plan):
            continue
        cond = functools.reduce(jnp.logical_or, [j == b for b in blocks])

        @pl.when(cond)
        def _(plan=plan, blocks=blocks):
            _finish_block(o_ref, tab_ref, plan, kv_group if kv_block in blocks else None, kr_ref, vr_ref)


def _norm_proj(x, g, w, tm, tn):
    m, d = x.shape
    n = w.shape[1]

    def body(x_ref, g_ref, w_ref, o_ref, u_ref):
        _norm_proj_kernel(x_ref, g_ref, w_ref, None, o_ref, u_ref)

    return pl.pallas_call(
        body,
        grid=(m // tm, n // tn),
        in_specs=[
            pl.BlockSpec((tm, d), lambda i, j: (i, 0)),
            pl.BlockSpec((1, d), lambda i, j: (0, 0)),
            pl.BlockSpec((d, tn), lambda i, j: (0, j)),
        ],
        out_specs=pl.BlockSpec((tm, tn), lambda i, j: (i, j)),
        out_shape=jax.ShapeDtypeStruct((m, n), F32),
        scratch_shapes=[pltpu.VMEM((tm, d), BF16)],
        compiler_params=_cparams("parallel", "arbitrary"),
        name="norm_proj",
    )(x, g, w)


_W_IN_ROW = {"rq": 0, "rk": 1024, "rv": 2048, "rg": 3072, "dq": 4096, "dk": 5120, "iq": 5632, "ik": 6656,
             "ga": 6736, "gb": 8784}
PROJ_SRC_ROWS = tuple(_W_IN_ROW[name] + PROJ_TN * k for name, nblk in (
    ("ga", 4), ("gb", 4), ("rq", 2), ("rk", 2), ("rv", 2), ("rg", 2), ("dq", 2), ("iq", 2), ("dk", 1), ("ik", 1))
    for k in range(nblk))


def _norm_proj_rope(x, g, w_t, tab, tm, pos_blocks):
    m, d = x.shape
    from_f32 = w_t.dtype != BF16
    tn = PROJ_TN if from_f32 else PROJ_TN_BF16
    kv_spec = pl.BlockSpec((DSA_KV_HEADS * tm, HEAD_DIM), lambda i, j, *_: (i, 0))
    kv_rows = jax.ShapeDtypeStruct((DSA_KV_HEADS * m, HEAD_DIM), F32)
    in_specs = [
        pl.BlockSpec((tm, d), lambda i, j, *_: (i, 0)),
        pl.BlockSpec((1, d), lambda i, j, *_: (0, 0)),
        None,
        pl.BlockSpec((tm, TAB_W), lambda i, j, *_: (i % pos_blocks, 0)),
    ]
    out_specs = [pl.BlockSpec((tm, tn), lambda i, j, *_: (i, j)), kv_spec, kv_spec]
    out_shape = [jax.ShapeDtypeStruct((m, N_PROJ), F32), kv_rows, kv_rows]
    scratch = [pltpu.VMEM((tm, d), BF16)]
    if from_f32:
        assert m == tm and all(r % ROW_ALIGN == 0 for r in PROJ_SRC_ROWS)
        src_rows = jnp.asarray([r // ROW_ALIGN for r in PROJ_SRC_ROWS], jnp.int32)
        in_specs[2] = pl.BlockSpec((pl.Element(tn), pl.Element(d)), lambda i, j, src: (src[j] * ROW_ALIGN, 0))
        out_specs.append(pl.BlockSpec((tn, d), lambda i, j, src: (j, 0)))
        out_shape.append(jax.ShapeDtypeStruct((N_PROJ, d), BF16))

        def body(src_ref, x_ref, g_ref, w_ref, tab_ref, o_ref, kr_ref, vr_ref, wq_ref, u_ref):
            _norm_proj_kernel(x_ref, g_ref, w_ref, tab_ref, o_ref, u_ref, kr_ref=kr_ref, vr_ref=vr_ref, wq_ref=wq_ref)

        grid_spec = pltpu.PrefetchScalarGridSpec(num_scalar_prefetch=1, grid=(1, N_PROJ // tn), in_specs=in_specs,
                                                 out_specs=out_specs, scratch_shapes=scratch)
        args = (src_rows, x, g, w_t, tab)
    else:
        in_specs[2] = pl.BlockSpec((tn, d), lambda i, j: (j, 0))

        def body(x_ref, g_ref, w_ref, tab_ref, o_ref, kr_ref, vr_ref, u_ref):
            _norm_proj_kernel(x_ref, g_ref, w_ref, tab_ref, o_ref, u_ref, kr_ref=kr_ref, vr_ref=vr_ref)

        grid_spec = pltpu.PrefetchScalarGridSpec(num_scalar_prefetch=0, grid=(m // tm, N_PROJ // tn),
                                                 in_specs=in_specs, out_specs=out_specs, scratch_shapes=scratch)
        args = (x, g, w_t, tab)
    return pl.pallas_call(
        body,
        grid_spec=grid_spec,
        out_shape=out_shape,
        compiler_params=_cparams("parallel", "arbitrary"),
        name="norm_proj_rope",
    )(*args)


@functools.lru_cache(maxsize=None)
def _rope_tables(start, count, period):
    pos = (start + np.arange(count) % period).astype(np.float64)[:, None]

    def cs(half, theta):
        inv = theta ** (-np.arange(half, dtype=np.float64) / half)
        ang = pos * inv[None, :]
        return np.cos(ang), np.sin(ang)

    ones = lambda n: np.ones((count, n))
    zeros = lambda n: np.zeros((count, n))
    c, s = cs(HEAD_DIM // 2, RET_THETA)
    ret_c = np.concatenate([c, c], 1)
    ret_s = np.concatenate([-s, s], 1)
    c, s = cs(DSA_ROT // 2, ROPE_THETA)
    dsa_c = np.concatenate([c, c, ones(LANES - DSA_ROT)], 1)
    dsa_s = np.concatenate([-s, s, zeros(LANES - DSA_ROT)], 1)
    c, s = cs(IDX_ROT // 2, ROPE_THETA)
    idx_c = np.tile(np.concatenate([c, c, ones(IDX_DH - IDX_ROT)], 1), (1, 2))
    idx_s = np.tile(np.concatenate([-s, s, zeros(IDX_DH - IDX_ROT)], 1), (1, 2))
    return np.concatenate([ret_c, ret_s, dsa_c, dsa_s, idx_c, idx_s], 1).astype(np.float32)


@functools.lru_cache(maxsize=None)
def _decay_tables(rows, n_seq, t_valid):
    log_g = np.log1p(-np.exp2(-5.0 - np.arange(RET_HEADS, dtype=np.float64)))
    r = np.arange(n_seq * rows)
    t = (r % rows).astype(np.float64)
    same = (r[:, None] // rows) == (r[None, :] // rows)
    diff = t[:, None] - t[None, :]
    intra = np.where(same[None] & (diff >= 0)[None], np.exp(log_g[:, None, None] * np.maximum(diff, 0.0)[None]), 0.0)
    ones = np.ones((1, 1, HEAD_DIM))
    qdec = np.exp(log_g[:, None] * (t[None, :] + 1.0))[:, :, None] * ones
    kdec = np.where(t[None, :] < t_valid, np.exp(log_g[:, None] * (t_valid - 1.0 - t[None, :])), 0.0)[:, :, None] * ones
    cdec = np.exp(log_g * t_valid)[:, None, None] * ones
    return tuple(a.astype(np.float32) for a in (intra, qdec, kdec, cdec))


def _groupnorm_gate(o, gate, gn):
    mu = jnp.mean(o, axis=-1, keepdims=True)
    d = o - mu
    var = jnp.mean(d * d, axis=-1, keepdims=True)
    n = d * lax.rsqrt(var + EPS) * gn
    return (gate * _sigmoid(gate) * n).astype(BF16)


def _ret_prompt_kernel(q_ref, k_ref, v_ref, g_ref, gn_ref, intra_ref, qdec_ref, kdec_ref, cdec_ref,
                       y_ref, st_ref, s_scr):
    c = pl.program_id(1)

    @pl.when(c == 0)
    def _():
        s_scr[...] = jnp.zeros_like(s_scr)

    for h in range(RET_HEADS):
        sl = slice(h * HEAD_DIM, (h + 1) * HEAD_DIM)
        k = k_ref[:, sl]
        qb = q_ref[:, sl].astype(BF16)
        kb = k.astype(BF16)
        vb = v_ref[:, sl].astype(BF16)
        s = _dot_nt(qb, kb) * intra_ref[h]
        state = s_scr[h]
        o = _dot(s.astype(BF16), vb) + _dot(qb, state.astype(BF16)) * qdec_ref[h]
        kd = (k * kdec_ref[h]).astype(BF16)
        s_scr[h] = state * cdec_ref[h] + _dot_tn(kd, vb)
        y_ref[:, sl] = _groupnorm_gate(o, g_ref[:, sl], gn_ref[:, sl])

    @pl.when(c == pl.num_programs(1) - 1)
    def _():
        st_ref[0] = s_scr[...]


def _ret_prompt(proj, gn, batch, seq, chunk):
    nc = seq // chunk
    intra, qdec, kdec, cdec = _decay_tables(chunk, 1, chunk)
    w = RET_HEADS * HEAD_DIM
    col = lambda off: (lambda b, c: (b * nc + c, off // w))
    const3 = lambda b, c: (0, 0, 0)
    return pl.pallas_call(
        _ret_prompt_kernel,
        grid=(batch, nc),
        in_specs=[
            pl.BlockSpec((chunk, w), col(COL_RQ)),
            pl.BlockSpec((chunk, w), col(COL_RK)),
            pl.BlockSpec((chunk, w), col(COL_RV)),
            pl.BlockSpec((chunk, w), col(COL_RG)),
            pl.BlockSpec((1, w), lambda b, c: (0, 0)),
            pl.BlockSpec((RET_HEADS, chunk, chunk), const3),
            pl.BlockSpec((RET_HEADS, chunk, HEAD_DIM), const3),
            pl.BlockSpec((RET_HEADS, chunk, HEAD_DIM), const3),
            pl.BlockSpec((RET_HEADS, 1, HEAD_DIM), const3),
        ],
        out_specs=[
            pl.BlockSpec((chunk, w), lambda b, c: (b * nc + c, 0)),
            pl.BlockSpec((1, RET_HEADS, HEAD_DIM, HEAD_DIM), lambda b, c: (b, 0, 0, 0)),
        ],
        out_shape=[
            jax.ShapeDtypeStruct((batch * seq, w), BF16),
            jax.ShapeDtypeStruct((batch, RET_HEADS, HEAD_DIM, HEAD_DIM), F32),
        ],
        scratch_shapes=[pltpu.VMEM((RET_HEADS, HEAD_DIM, HEAD_DIM), F32)],
        compiler_params=_cparams("parallel", "arbitrary"),
        name="ret_prompt",
    )(proj, proj, proj, proj, gn, intra, qdec, kdec, cdec)


def _ret_sample_kernel(q_ref, k_ref, v_ref, g_ref, gn_ref, st_ref, intra_ref, qdec_ref, kdec_ref, cdec_ref,
                       y_ref, so_ref, *, n_batch, rows):
    k = k_ref[...]
    qb = q_ref[...].astype(BF16)
    kb = k.astype(BF16)
    vb = v_ref[...].astype(BF16)
    s = _dot_nt(qb, kb) * intra_ref[0]
    kd = k * kdec_ref[0]
    row_batch = lax.broadcasted_iota(jnp.int32, kd.shape, 0) // rows
    inter = []
    for b in range(n_batch):
        state = st_ref[b, 0]
        inter.append(_dot(qb[b * rows:(b + 1) * rows], state.astype(BF16)))
        kd_b = jnp.where(row_batch == b, kd, 0.0).astype(BF16)
        so_ref[b, 0] = state * cdec_ref[0] + _dot_tn(kd_b, vb)
    o = _dot(s.astype(BF16), vb) + jnp.concatenate(inter, axis=0) * qdec_ref[0]
    y_ref[...] = _groupnorm_gate(o, g_ref[...], gn_ref[...])


def _ret_sample(proj, gn, state, n_batch, rows, t_valid):
    m = n_batch * rows
    intra, qdec, kdec, cdec = _decay_tables(rows, n_batch, t_valid)
    col = lambda off: (lambda h: (0, off // HEAD_DIM + h))
    per_head = lambda h: (h, 0, 0)
    return pl.pallas_call(
        functools.partial(_ret_sample_kernel, n_batch=n_batch, rows=rows),
        grid=(RET_HEADS,),
        in_specs=[
            pl.BlockSpec((m, HEAD_DIM), col(COL_RQ)),
            pl.BlockSpec((m, HEAD_DIM), col(COL_RK)),
            pl.BlockSpec((m, HEAD_DIM), col(COL_RV)),
            pl.BlockSpec((m, HEAD_DIM), col(COL_RG)),
            pl.BlockSpec((1, HEAD_DIM), lambda h: (0, h)),
            pl.BlockSpec((n_batch, 1, HEAD_DIM, HEAD_DIM), lambda h: (0, h, 0, 0)),
            pl.BlockSpec((1, m, m), per_head),
            pl.BlockSpec((1, m, HEAD_DIM), per_head),
            pl.BlockSpec((1, m, HEAD_DIM), per_head),
            pl.BlockSpec((1, 1, HEAD_DIM), per_head),
        ],
        out_specs=[
            pl.BlockSpec((m, HEAD_DIM), lambda h: (0, h)),
            pl.BlockSpec((n_batch, 1, HEAD_DIM, HEAD_DIM), lambda h: (0, h, 0, 0)),
        ],
        out_shape=[
            jax.ShapeDtypeStruct((m, RET_HEADS * HEAD_DIM), BF16),
            jax.ShapeDtypeStruct(state.shape, F32),
        ],
        compiler_params=_cparams("parallel"),
        name="ret_sample",
    )(proj, proj, proj, proj, gn, state, intra, qdec, kdec, cdec)


def _kth_largest(count_ge, next_above, smin, smax, n_adm, topk):
    few = n_adm < topk
    lo = jnp.where(few, -jnp.inf, smin)
    c_lo = jnp.where(few, float(topk), n_adm)
    hi = jnp.where(few, -jnp.inf, smax + jnp.maximum(jnp.abs(smax) * 1e-6, 1e-30))

    def bisect(_, state):
        lo, hi, c_lo = state
        mid = 0.5 * lo + 0.5 * hi
        c = count_ge(mid)
        take = c >= topk
        return jnp.where(take, mid, lo), jnp.where(take, hi, mid), jnp.where(take, c, c_lo)

    lo, hi, c_lo = lax.fori_loop(0, BISECT_STEPS, bisect, (lo, hi, c_lo))
    c_lo = jnp.where(few, float(topk), c_lo)

    def unfinished(state):
        return jnp.max(state[2]) > 0.0

    def step_up(state):
        lo, c_lo, active = state
        nxt, n_eq = next_above(lo)
        c_nxt = c_lo - n_eq
        move = active * jnp.where(c_nxt >= topk, 1.0, 0.0)
        lo = jnp.where(move > 0.0, nxt, lo)
        c_lo = jnp.where(move > 0.0, c_nxt, c_lo)
        return lo, c_lo, move * jnp.where(c_nxt > topk, 1.0, 0.0)

    return lax.while_loop(unfinished, step_up, (lo, c_lo, jnp.where(c_lo > topk, 1.0, 0.0)))[0]


def _dsa_prompt_kernel(dq_ref, iq_ref, iwq_ref, kall_ref, vall_ref, ikall_ref, o_ref,
                       kbf, vtb, ika, ikb, keys, z_a, z_b, zmax_a, zmax_b, m_scr, acc_scr, *, tq, topk):
    z_scr, zmax_scr = (z_a, z_b), (zmax_a, zmax_b)
    i = pl.program_id(1)
    n_chunks = kbf.shape[0]
    scale_log2e = HEAD_DIM ** -0.5 * math.log2(math.e)

    @pl.when(i == 0)
    def _():
        lane = lax.broadcasted_iota(jnp.int32, (tq, LANES), 1)
        for c in range(n_chunks):
            rows = slice(c * tq, (c + 1) * tq)
            kbf[c] = kall_ref[rows, :].astype(BF16)
            for kvh in range(DSA_KV_HEADS):
                sl = slice(kvh * HEAD_DIM, (kvh + 1) * HEAD_DIM)
                vtb[c, kvh, :HEAD_DIM, :] = vall_ref[rows, sl].T.astype(BF16)
                vtb[c, kvh, HEAD_DIM:, :] = jnp.ones((ROW_ALIGN, tq), BF16)
            a = ikall_ref[rows, :]
            ika[c] = jnp.where(lane < IDX_DH, a, 0.0).astype(BF16)
            ikb[c] = jnp.where(lane >= IDX_DH, pltpu.roll(a, IDX_DH, 1), 0.0).astype(BF16)

    w_t = iwq_ref[...].T
    iqb = iq_ref[...].astype(BF16)
    qb = (dq_ref[...] * scale_log2e).astype(BF16)
    t_col = i * tq + lax.broadcasted_iota(jnp.int32, (tq, tq), 1)
    s_row0 = lax.broadcasted_iota(jnp.int32, (tq, tq), 0)

    fold = lambda x: x.reshape(tq // SUBLANES, SUBLANES, tq)

    def score_chunk(c, carry):
        smax, smin = carry
        src = jnp.minimum(c, n_chunks - 1)
        a = ika[src]
        b = ikb[src]
        acc = jnp.zeros((tq, tq), F32)
        for p in range(IDX_HEADS // 2):
            pair = iqb[:, p * LANES:(p + 1) * LANES]
            w0 = w_t[IDX_DH + 2 * p:IDX_DH + 2 * p + 1, :]
            w1 = w_t[IDX_DH + 2 * p + 1:IDX_DH + 2 * p + 2, :]
            acc = acc + jnp.maximum(_dot_nt(a, pair), 0.0) * w0
            acc = acc + jnp.maximum(_dot_nt(b, pair), 0.0) * w1
        admissible = s_row0 + c * tq <= t_col
        keys[c] = jnp.where(admissible, acc, -jnp.inf)
        smax = jnp.maximum(smax, fold(jnp.where(admissible, acc, -jnp.inf)).max(axis=0))
        smin = jnp.minimum(smin, fold(jnp.where(admissible, acc, jnp.inf)).min(axis=0))
        return smax, smin

    n_pairs = (i + 2) // 2

    def score_pair(j, carry):
        return score_chunk(2 * j + 1, score_chunk(2 * j, carry))

    smax, smin = lax.fori_loop(0, n_pairs, score_pair, (jnp.full((SUBLANES, tq), -jnp.inf, F32),
                                                        jnp.full((SUBLANES, tq), jnp.inf, F32)))
    smax = smax.max(axis=0, keepdims=True)
    smin = smin.min(axis=0, keepdims=True)

    def count_ge(t):
        def pair(j, cnt):
            for c in (2 * j, 2 * j + 1):
                cnt = cnt + fold(jnp.where(keys[c] >= t, 1.0, 0.0)).sum(axis=0)
            return cnt

        return lax.fori_loop(0, n_pairs, pair, jnp.zeros((SUBLANES, tq), F32)).sum(axis=0, keepdims=True)

    def next_above(lo):
        def pair(j, carry):
            nxt, n_eq = carry
            for c in (2 * j, 2 * j + 1):
                k = keys[c]
                nxt = jnp.minimum(nxt, fold(jnp.where(k > lo, k, jnp.inf)).min(axis=0))
                n_eq = n_eq + fold(jnp.where(k == lo, 1.0, 0.0)).sum(axis=0)
            return nxt, n_eq

        nxt, n_eq = lax.fori_loop(0, n_pairs, pair, (jnp.full((SUBLANES, tq), jnp.inf, F32),
                                                     jnp.zeros((SUBLANES, tq), F32)))
        return nxt.min(axis=0, keepdims=True), n_eq.sum(axis=0, keepdims=True)

    thr = _kth_largest(count_ge, next_above, smin, smax, (t_col[:1, :] + 1).astype(F32), topk)

    m_scr[...] = jnp.full_like(m_scr, NEG)
    acc_scr[...] = jnp.zeros_like(acc_scr)

    kv_of = lambda h: h // (DSA_HEADS // DSA_KV_HEADS)

    def logits(c, slot):
        bias = jnp.where(s_row0 + c * tq <= t_col, jnp.where(keys[c] >= thr, 0.0, NEG), NEG)
        kc = kbf[c]
        for h in range(DSA_HEADS):
            ksl = slice(kv_of(h) * HEAD_DIM, (kv_of(h) + 1) * HEAD_DIM)
            z = _dot_nt(kc[:, ksl], qb[:, h * HEAD_DIM:(h + 1) * HEAD_DIM]) + bias
            z_scr[slot][h] = z
            zmax_scr[slot][h] = z.max(axis=0, keepdims=True)

    def update(c, slot):
        for h in range(DSA_HEADS):
            m_old = m_scr[h]
            m_new = jnp.maximum(m_old, zmax_scr[slot][h])
            p = jnp.exp2(z_scr[slot][h] - m_new)
            acc_scr[h] = acc_scr[h] * jnp.exp2(m_old - m_new) + _dot(vtb[c, kv_of(h)], p.astype(BF16))
            m_scr[h] = m_new

    n = i + 1
    logits(0, 0)

    def two_chunks(k, carry):
        update(2 * k, 0)
        logits(2 * k + 1, 1)
        update(2 * k + 1, 1)
        logits(2 * k + 2, 0)
        return carry

    lax.fori_loop(0, (n - 1) // 2, two_chunks, 0)

    @pl.when(n % 2 == 1)
    def _():
        update(n - 1, 0)

    @pl.when(n % 2 == 0)
    def _():
        update(n - 2, 0)
        logits(n - 1, 1)
        update(n - 1, 1)

    for h in range(DSA_HEADS):
        o = acc_scr[h, :HEAD_DIM, :] * (1.0 / acc_scr[h, HEAD_DIM:HEAD_DIM + 1, :])
        o_ref[:, h * HEAD_DIM:(h + 1) * HEAD_DIM] = o.T.astype(BF16)


def _dsa_prompt(proj, batch, seq, tq):
    nq = seq // tq
    topk = min(TOPK_MAX, seq // 4)
    qw = DSA_HEADS * HEAD_DIM
    kvw = DSA_KV_HEADS * HEAD_DIM
    return pl.pallas_call(
        functools.partial(_dsa_prompt_kernel, tq=tq, topk=topk),
        grid=(batch, nq),
        in_specs=[
            pl.BlockSpec((tq, qw), lambda b, i: (b * nq + i, COL_DQ // qw)),
            pl.BlockSpec((tq, qw), lambda b, i: (b * nq + i, COL_IQ // qw)),
            pl.BlockSpec((tq, LANES), lambda b, i: (b * nq + i, COL_IK // LANES)),
            pl.BlockSpec((seq, kvw), lambda b, i: (b, COL_DK // kvw)),
            pl.BlockSpec((seq, kvw), lambda b, i: (b, COL_DV // kvw)),
            pl.BlockSpec((seq, LANES), lambda b, i: (b, COL_IK // LANES)),
        ],
        out_specs=pl.BlockSpec((tq, qw), lambda b, i: (b * nq + i, 0)),
        out_shape=jax.ShapeDtypeStruct((batch * seq, qw), BF16),
        scratch_shapes=[
            pltpu.VMEM((nq, tq, kvw), BF16),
            pltpu.VMEM((nq, DSA_KV_HEADS, HEAD_DIM + ROW_ALIGN, tq), BF16),
            pltpu.VMEM((nq, tq, LANES), BF16),
            pltpu.VMEM((nq, tq, LANES), BF16),
            pltpu.VMEM((nq + 1, tq, tq), F32),
            pltpu.VMEM((DSA_HEADS, tq, tq), F32),
            pltpu.VMEM((DSA_HEADS, tq, tq), F32),
            pltpu.VMEM((DSA_HEADS, 1, tq), F32),
            pltpu.VMEM((DSA_HEADS, 1, tq), F32),
            pltpu.VMEM((DSA_HEADS, 1, tq), F32),
            pltpu.VMEM((DSA_HEADS, HEAD_DIM + ROW_ALIGN, tq), F32),
        ],
        compiler_params=_cparams("parallel", "arbitrary"),
        name="dsa_prompt",
    )(proj, proj, proj, proj, proj, proj)


def _idx_sample_kernel(pt_ref, iq_ref, iw_ref, iknew_ref, *rest, pages):
    page_refs = rest[:pages]
    sp_ref, sn_ref = rest[pages:]
    iqb = iq_ref[0].astype(BF16)
    w = iw_ref[0]
    rows = iqb.shape[0] // IDX_HEADS

    def scores(ik_t):
        d = jnp.maximum(_dot(iqb, ik_t.astype(BF16)), 0.0) * w
        return d.reshape(rows, IDX_HEADS, LANES).sum(axis=1)

    for p in range(pages):
        sp_ref[0, :, p * PAGE_SIZE:(p + 1) * PAGE_SIZE] = scores(page_refs[p][0])
    t = lax.broadcasted_iota(jnp.int32, (rows, LANES), 0)
    j = lax.broadcasted_iota(jnp.int32, (rows, LANES), 1)
    sn_ref[0] = jnp.where(j <= t, scores(iknew_ref[0]), -jnp.inf)


def _idx_sample(page_table, iq, iw, ik_new, pool_ik, pages):
    n_batch, n_pages = page_table.shape
    rows16 = iq.shape[1]
    rows = rows16 // IDX_HEADS
    past = n_pages * PAGE_SIZE
    page_specs = [pl.BlockSpec((1, IDX_DH, PAGE_SIZE), functools.partial(
        lambda b, g, pt, k: (pt[b, g * pages + k], 0, 0), k=k)) for k in range(pages)]
    grid_spec = pltpu.PrefetchScalarGridSpec(
        num_scalar_prefetch=1,
        grid=(n_batch, n_pages // pages),
        in_specs=[
            pl.BlockSpec((1, rows16, IDX_DH), lambda b, g, pt: (b, 0, 0)),
            pl.BlockSpec((1, rows16, LANES), lambda b, g, pt: (b, 0, 0)),
            pl.BlockSpec((1, IDX_DH, PAGE_SIZE), lambda b, g, pt: (b, 0, 0)),
        ] + page_specs,
        out_specs=[
            pl.BlockSpec((1, rows, pages * PAGE_SIZE), lambda b, g, pt: (b, 0, g)),
            pl.BlockSpec((1, rows, LANES), lambda b, g, pt: (b, 0, 0)),
        ],
    )
    return pl.pallas_call(
        functools.partial(_idx_sample_kernel, pages=pages),
        grid_spec=grid_spec,
        out_shape=[
            jax.ShapeDtypeStruct((n_batch, rows, past), F32),
            jax.ShapeDtypeStruct((n_batch, rows, LANES), F32),
        ],
        compiler_params=_cparams("parallel", "arbitrary"),
        name="idx_sample",
    )(page_table, iq, iw, ik_new, *([pool_ik] * pages))


def _sel_sample_kernel(sp_ref, sn_ref, bp_ref, bn_ref, *, topk):
    n_batch, rows, past = sp_ref.shape
    sp = sp_ref[...].reshape(n_batch * rows, past)
    sn = sn_ref[...].reshape(n_batch * rows, LANES)
    t = lax.broadcasted_iota(jnp.int32, (n_batch * rows, LANES), 0) % rows
    j = lax.broadcasted_iota(jnp.int32, (n_batch * rows, LANES), 1)
    admissible_new = j <= t
    row_sum = lambda a: a.sum(axis=1, keepdims=True)
    row_min = lambda a: a.min(axis=1, keepdims=True)

    def count_ge(thr):
        return row_sum(jnp.where(sp >= thr, 1.0, 0.0)) + row_sum(jnp.where(sn >= thr, 1.0, 0.0))

    def next_above(lo):
        nxt = jnp.minimum(row_min(jnp.where(sp > lo, sp, jnp.inf)), row_min(jnp.where(sn > lo, sn, jnp.inf)))
        return nxt, row_sum(jnp.where(sp == lo, 1.0, 0.0)) + row_sum(jnp.where(sn == lo, 1.0, 0.0))

    smin = jnp.minimum(row_min(sp), row_min(jnp.where(admissible_new, sn, jnp.inf)))
    smax = jnp.maximum(sp.max(axis=1, keepdims=True), sn.max(axis=1, keepdims=True))
    n_adm = float(past) + row_sum(jnp.where(admissible_new, 1.0, 0.0))
    thr = _kth_largest(count_ge, next_above, smin, smax, n_adm, topk)
    bp_ref[...] = jnp.where(sp >= thr, 0.0, NEG).reshape(n_batch, rows, past)
    bn_ref[...] = jnp.where(admissible_new, jnp.where(sn >= thr, 0.0, NEG), NEG).reshape(n_batch, rows, LANES)


def _sel_sample(scores_past, scores_new, topk):
    full = lambda a: pl.BlockSpec(a.shape, lambda: (0,) * a.ndim)
    return pl.pallas_call(
        functools.partial(_sel_sample_kernel, topk=topk),
        in_specs=[full(scores_past), full(scores_new)],
        out_specs=[full(scores_past), full(scores_new)],
        out_shape=[jax.ShapeDtypeStruct(scores_past.shape, F32), jax.ShapeDtypeStruct(scores_new.shape, F32)],
        compiler_params=pltpu.CompilerParams(vmem_limit_bytes=VMEM_LIMIT),
        name="sel_sample",
    )(scores_past, scores_new)


def _dsa_sample_kernel(pt_ref, bp_ref, bn_ref, q_ref, knew_ref, vnew_ref, *rest, pages, steps):
    k_refs = rest[:pages]
    v_refs = rest[pages:2 * pages]
    o_ref, m_scr, l_scr, acc_scr = rest[2 * pages:]
    g = pl.program_id(0) % steps
    scale = HEAD_DIM ** -0.5
    group = DSA_HEADS // DSA_KV_HEADS

    @pl.when(g == 0)
    def _():
        m_scr[...] = jnp.full_like(m_scr, NEG)
        l_scr[...] = jnp.zeros_like(l_scr)
        acc_scr[...] = jnp.zeros_like(acc_scr)

    def head_rows(ref, kvh):
        return ref[0, pl.ds(kvh, PAGE_SIZE, stride=DSA_KV_HEADS), :].astype(BF16)

    def attend(k_pages, v_pages, bias):
        bias_g = jnp.concatenate([bias] * group, axis=0)
        kv_heads = range(DSA_KV_HEADS)
        qb = [q_ref[0, kvh].astype(BF16) for kvh in kv_heads]
        raw = [jnp.concatenate([_dot_nt(qb[kvh], head_rows(k, kvh)) for k in k_pages], axis=1) for kvh in kv_heads]
        pb, alpha = [], []
        for kvh in kv_heads:
            z = jnp.where(bias_g < 0.0, NEG, raw[kvh])
            m_old = m_scr[kvh]
            m_new = jnp.maximum(m_old, z.max(axis=1, keepdims=True))
            alpha.append(jnp.exp((m_old - m_new) * scale))
            p = jnp.exp((z - m_new) * scale)
            l_scr[kvh] = l_scr[kvh] * alpha[kvh] + p.sum(axis=1, keepdims=True)
            m_scr[kvh] = m_new
            pb.append(p.astype(BF16))
        for kvh in kv_heads:
            pv = _dot(pb[kvh][:, :PAGE_SIZE], head_rows(v_pages[0], kvh))
            for n in range(1, len(v_pages)):
                pv = pv + _dot(pb[kvh][:, n * PAGE_SIZE:(n + 1) * PAGE_SIZE], head_rows(v_pages[n], kvh))
            acc_scr[kvh] = acc_scr[kvh] * alpha[kvh] + pv

    attend(k_refs, v_refs, bp_ref[0])

    @pl.when(g == steps - 1)
    def _():
        attend([knew_ref], [vnew_ref], bn_ref[0])
        for kvh in range(DSA_KV_HEADS):
            o_ref[0, kvh] = (acc_scr[kvh] * (1.0 / l_scr[kvh])).astype(BF16)


def _dsa_sample(page_table, bias_past, bias_new, q, k_new, v_new, pool_k, pool_v, steps):
    n_batch, n_pages = page_table.shape
    pages = n_pages // steps
    rows = bias_past.shape[1]
    page_rows = DSA_KV_HEADS * PAGE_SIZE
    grows = q.shape[2]
    page_spec = lambda k: pl.BlockSpec((1, page_rows, HEAD_DIM), functools.partial(
        lambda i, pt, k: (pt[i // steps, (i % steps) * pages + k], 0, 0), k=k))
    per_batch3 = lambda i, pt: (i // steps, 0, 0)
    per_batch4 = lambda i, pt: (i // steps, 0, 0, 0)
    in_specs = [
        pl.BlockSpec((1, rows, pages * PAGE_SIZE), lambda i, pt: (i // steps, 0, i % steps)),
        pl.BlockSpec((1, rows, LANES), per_batch3),
        pl.BlockSpec((1, DSA_KV_HEADS, grows, HEAD_DIM), per_batch4),
        pl.BlockSpec((1, page_rows, HEAD_DIM), per_batch3),
        pl.BlockSpec((1, page_rows, HEAD_DIM), per_batch3),
    ] + [page_spec(k) for k in range(pages)] * 2
    grid_spec = pltpu.PrefetchScalarGridSpec(
        num_scalar_prefetch=1,
        grid=(n_batch * steps,),
        in_specs=in_specs,
        out_specs=pl.BlockSpec((1, DSA_KV_HEADS, grows, HEAD_DIM), per_batch4),
        scratch_shapes=[
            pltpu.VMEM((DSA_KV_HEADS, grows, 1), F32),
            pltpu.VMEM((DSA_KV_HEADS, grows, 1), F32),
            pltpu.VMEM((DSA_KV_HEADS, grows, HEAD_DIM), F32),
        ],
    )
    return pl.pallas_call(
        functools.partial(_dsa_sample_kernel, pages=pages, steps=steps),
        grid_spec=grid_spec,
        out_shape=jax.ShapeDtypeStruct((n_batch, DSA_KV_HEADS, grows, HEAD_DIM), BF16),
        compiler_params=_cparams("arbitrary"),
        name="dsa_sample",
    )(page_table, bias_past, bias_new, q, k_new, v_new, *([pool_k] * pages), *([pool_v] * pages))


def _merge_mem_kernel(x_ref, ry_ref, do_ref, ga_ref, gb_ref, wr_ref, wd_ref, wo_ref,
                      g_ref, wq_ref, mk_ref, mv_ref, wmo_ref, o_ref, *, rows_per_batch):
    ya = _dot(ry_ref[...], wr_ref[...])
    yb = _dot(do_ref[...], wd_ref[...])
    merged = _sigmoid(ga_ref[...]) * ya + _sigmoid(gb_ref[...]) * yb
    h = x_ref[...] + _dot(merged.astype(BF16), wo_ref[...])
    qm = _dot(_rmsnorm_bf16(h, g_ref[...]), wq_ref[...])
    tm = h.shape[0]
    nk = mk_ref.shape[0]
    scale = HEAD_DIM ** -0.5
    if rows_per_batch is not None:
        rb = lax.broadcasted_iota(jnp.int32, (tm, nk), 0) // rows_per_batch
        kb = lax.broadcasted_iota(jnp.int32, (tm, nk), 1) // MEM_LEN
        same = rb == kb
    outs = []
    for hd in range(MEM_HEADS):
        sl = slice(hd * HEAD_DIM, (hd + 1) * HEAD_DIM)
        z = _dot_nt(qm[:, sl].astype(BF16), mk_ref[:, sl].astype(BF16)) * scale
        if rows_per_batch is not None:
            z = jnp.where(same, z, NEG)
        p = jnp.exp(z - z.max(axis=-1, keepdims=True))
        pv = _dot(p.astype(BF16), mv_ref[:, sl].astype(BF16))
        outs.append(pv * (1.0 / p.sum(axis=-1, keepdims=True)))
    om = jnp.concatenate(outs, axis=1).astype(BF16)
    o_ref[...] = h + _dot(om, wmo_ref[...])


def _merge_mem(x, ret_y, dsa_o, proj, w_ret_out, w_dsa_out, w_o, g, w_mq, mk, mv, w_mo, *,
               tm, batch_tiles, mk_col, mv_col, nk, rows_per_batch):
    m, d = x.shape
    w = ret_y.shape[1]
    const = lambda i: (0, 0)
    resident = dict(pipeline_mode=pl.Buffered(1))
    if batch_tiles:
        kmap = lambda col: (lambda i: (i // batch_tiles, col))
    else:
        kmap = lambda col: (lambda i: (0, col))
    return pl.pallas_call(
        functools.partial(_merge_mem_kernel, rows_per_batch=rows_per_batch),
        grid=(m // tm,),
        in_specs=[
            pl.BlockSpec((tm, d), lambda i: (i, 0)),
            pl.BlockSpec((tm, w), lambda i: (i, 0)),
            pl.BlockSpec((tm, w), lambda i: (i, 0)),
            pl.BlockSpec((tm, d), lambda i: (i, COL_GA // d)),
            pl.BlockSpec((tm, d), lambda i: (i, COL_GB // d)),
            pl.BlockSpec((w, d), const, **resident),
            pl.BlockSpec((w, d), const, **resident),
            pl.BlockSpec((d, d), const, **resident),
            pl.BlockSpec((1, d), const),
            pl.BlockSpec((d, MEM_W), const, **resident),
            pl.BlockSpec((nk, MEM_W), kmap(mk_col)),
            pl.BlockSpec((nk, MEM_W), kmap(mv_col)),
            pl.BlockSpec((MEM_W, d), const, **resident),
        ],
        out_specs=pl.BlockSpec((tm, d), lambda i: (i, 0)),
        out_shape=jax.ShapeDtypeStruct((m, d), F32),
        compiler_params=_cparams("parallel"),
        name="merge_mem",
    )(x, ret_y, dsa_o, proj, proj, w_ret_out, w_dsa_out, w_o, g, w_mq, mk, mv, w_mo)


def _mlp_kernel(h_ref, g_ref, wu_ref, wd_ref, gf_ref, o_ref, *rest):
    u_ref = rest[-1]
    j = pl.program_id(1)

    @pl.when(j == 0)
    def _():
        u_ref[...] = _rmsnorm_bf16(h_ref[...], g_ref[...])
        o_ref[...] = h_ref[...]

    wu, wd = wu_ref[...], wd_ref[...]
    if len(rest) == 3:
        wu, wd = wu.astype(BF16), wd.astype(BF16)
        rest[0][...] = wu
        rest[1][...] = wd
    a = jnp.maximum(_dot(u_ref[...], wu), 0.0)
    o_ref[...] += _dot((a * a).astype(BF16), wd)

    @pl.when(j == pl.num_programs(1) - 1)
    def _():
        y = o_ref[...]
        ms = jnp.mean(y * y, axis=-1, keepdims=True)
        o_ref[...] = (y * lax.rsqrt(ms + EPS)) * gf_ref[...]


def _mlp(h, g, w_up, w_down, g_final, tm, tf):
    m, d = h.shape
    ff = w_up.shape[1]
    emit = w_up.dtype != BF16
    assert not emit or m == tm, "weight copies are written once, by a call with one row tile"
    wu_spec = pl.BlockSpec((d, tf), lambda i, j: (0, j))
    wd_spec = pl.BlockSpec((tf, d), lambda i, j: (j, 0))
    out_specs = [pl.BlockSpec((tm, d), lambda i, j: (i, 0))]
    out_shape = [jax.ShapeDtypeStruct((m, d), F32)]
    if emit:
        out_specs += [wu_spec, wd_spec]
        out_shape += [jax.ShapeDtypeStruct(w_up.shape, BF16), jax.ShapeDtypeStruct(w_down.shape, BF16)]
    out = pl.pallas_call(
        _mlp_kernel,
        grid=(m // tm, ff // tf),
        in_specs=[
            pl.BlockSpec((tm, d), lambda i, j: (i, 0)),
            pl.BlockSpec((1, d), lambda i, j: (0, 0)),
            wu_spec,
            wd_spec,
            pl.BlockSpec((1, d), lambda i, j: (0, 0)),
        ],
        out_specs=out_specs,
        out_shape=out_shape,
        scratch_shapes=[pltpu.VMEM((tm, d), BF16)],
        compiler_params=_cparams("parallel", "arbitrary"),
        name="mlp_final",
    )(h, g, w_up, w_down, g_final)
    return out if emit else out[0]


def _pick_tile(n, pref):
    t = min(n, pref)
    while n % t:
        t //= 2
    return t


def kernel(x_prompt, x_sample, mem_prompt, cache_k, cache_v, cache_idx_k, state_ret, cache_mem_k, cache_mem_v,
           page_table, g_mix, w_in, gn_ret, w_ret_out, w_dsa_out, w_o, g_mem, g_memkv, w_mq, w_mk, w_mv, w_mo,
           g_mlp, w_up, w_down, g_final):
    assert w_in.shape[0] == 1, "one layer"
    batch, seq, d = x_prompt.shape
    n_dec, t_dec, _ = x_sample.shape
    n_pages = page_table.shape[1]
    past = n_pages * PAGE_SIZE
    mem_len = mem_prompt.shape[1]
    assert mem_len == MEM_LEN and t_dec <= SUBLANES
    row = lambda v: v.reshape(1, -1)

    w_ret_out_b, w_dsa_out_b, w_o_b = (w[0].astype(BF16) for w in (w_ret_out, w_dsa_out, w_o))
    w_mq_b, w_mo_b = w_mq[0].astype(BF16), w_mo[0].astype(BF16)
    w_mkv_b = jnp.concatenate([w_mk[0], w_mv[0]], axis=1).astype(BF16)
    g_mix_r, gn_r, g_mem_r, g_memkv_r, g_mlp_r, g_final_r = (
        row(v) for v in (g_mix[0], gn_ret[0], g_mem[0], g_memkv[0], g_mlp[0], g_final))

    rows = SUBLANES
    m_s = n_dec * rows
    xs = jnp.pad(x_sample, ((0, 0), (0, rows - t_dec), (0, 0))).reshape(m_s, d)
    proj_s, k_rows_s, v_rows_s, w_proj_b = _norm_proj_rope(xs, g_mix_r, jnp.swapaxes(w_in[0], 0, 1),
                                                           _rope_tables(past, m_s, rows), m_s, 1)

    m_p = batch * seq
    xp = x_prompt.reshape(m_p, d)
    tm_proj = _pick_tile(seq, 1024)
    proj_p, k_rows_p, v_rows_p = _norm_proj_rope(xp, g_mix_r, w_proj_b, _rope_tables(0, seq, seq), tm_proj,
                                                 seq // tm_proj)
    ret_y_p, ret_state_p = _ret_prompt(proj_p, gn_r, batch, seq, _pick_tile(seq, 256))
    dsa_o_p = _dsa_prompt(proj_p, batch, seq, _pick_tile(seq, 256))
    kv_p = _norm_proj(mem_prompt.reshape(batch * mem_len, d), g_memkv_r, w_mkv_b, mem_len, PROJ_TN)
    tm_t = _pick_tile(seq, 256)
    h_p = _merge_mem(xp, ret_y_p, dsa_o_p, proj_p, w_ret_out_b, w_dsa_out_b, w_o_b, g_mem_r, w_mq_b, kv_p, kv_p,
                     w_mo_b, tm=tm_t, batch_tiles=seq // tm_t, mk_col=0, mv_col=1, nk=mem_len, rows_per_batch=None)

    ret_y_s, ret_state_s = _ret_sample(proj_s, gn_r, state_ret[0], n_dec, rows, t_dec)

    proj_s3 = proj_s.reshape(n_dec, rows, N_PROJ)
    iq_s = proj_s3[:, :, COL_IQ:COL_IQ + IDX_HEADS * IDX_DH].reshape(n_dec, rows * IDX_HEADS, IDX_DH)
    iw_s = proj_s3[:, :, COL_IK + IDX_DH:COL_IK + IDX_DH + IDX_HEADS].reshape(n_dec, rows * IDX_HEADS, 1)
    iw_s = jnp.broadcast_to(iw_s, (n_dec, rows * IDX_HEADS, LANES))
    kvw = DSA_KV_HEADS * HEAD_DIM
    page_rows = DSA_KV_HEADS * PAGE_SIZE
    pad_keys = lambda a: jnp.pad(a, ((0, 0), (0, PAGE_SIZE - rows), (0, 0)))
    ik_new_t = jnp.swapaxes(pad_keys(proj_s3[:, :, COL_IK:COL_IK + IDX_DH]), 1, 2)
    pool_ik_t = jnp.swapaxes(cache_idx_k, 2, 3).reshape(-1, IDX_DH, PAGE_SIZE)
    new_page = lambda a: jnp.pad(a.reshape(n_dec, DSA_KV_HEADS * rows, HEAD_DIM),
                                 ((0, 0), (0, page_rows - DSA_KV_HEADS * rows), (0, 0)))
    k_new, v_new = new_page(k_rows_s), new_page(v_rows_s)
    scores_past, scores_new = _idx_sample(page_table, iq_s, iw_s, ik_new_t, pool_ik_t, _pick_tile(n_pages, 32))
    group = DSA_HEADS // DSA_KV_HEADS
    dq_s = proj_s3[:, :, COL_DQ:COL_DQ + DSA_HEADS * HEAD_DIM].reshape(n_dec, rows, DSA_KV_HEADS, group, HEAD_DIM)
    dq_s = dq_s.transpose(0, 2, 3, 1, 4).reshape(n_dec, DSA_KV_HEADS, group * rows, HEAD_DIM)
    bias_past, bias_new = _sel_sample(scores_past, scores_new, min(TOPK_MAX, (past + t_dec) // 4))
    dsa_o_s = _dsa_sample(page_table, bias_past, bias_new, dq_s, k_new, v_new,
                          cache_k.reshape(-1, page_rows, HEAD_DIM), cache_v.reshape(-1, page_rows, HEAD_DIM),
                          n_pages // _pick_tile(n_pages, MAX_PAGES_PER_STEP))
    dsa_o_s = dsa_o_s.reshape(n_dec, DSA_KV_HEADS, group, rows, HEAD_DIM).transpose(0, 3, 1, 2, 4)
    dsa_o_s = dsa_o_s.reshape(m_s, DSA_HEADS * HEAD_DIM)

    h_s = _merge_mem(xs, ret_y_s, dsa_o_s, proj_s, w_ret_out_b, w_dsa_out_b, w_o_b, g_mem_r, w_mq_b,
                     cache_mem_k[0].reshape(n_dec * mem_len, MEM_W), cache_mem_v[0].reshape(n_dec * mem_len, MEM_W),
                     w_mo_b, tm=m_s, batch_tiles=0, mk_col=0, mv_col=0, nk=n_dec * mem_len, rows_per_batch=rows)
    y_s, w_up_b, w_down_b = _mlp(h_s, g_mlp_r, w_up[0], w_down[0], g_final_r, m_s, 512)
    y_p = _mlp(h_p, g_mlp_r, w_up_b, w_down_b, g_final_r, _pick_tile(seq, 1024), 512)

    def rows_p(col, width, tail):
        return proj_p[:, col:col + width].reshape((1, batch, seq) + tail)

    def rows_s(col, width, tail):
        return proj_s3[:, :t_dec, col:col + width].reshape((1, n_dec, t_dec) + tail)

    kv_p_rows = lambda a: a.reshape(1, batch, seq, DSA_KV_HEADS, HEAD_DIM)
    kv_s_rows = lambda a: a.reshape(1, n_dec, rows, DSA_KV_HEADS, HEAD_DIM)[:, :, :t_dec]
    return (
        y_p.reshape(batch, seq, d),
        y_s.reshape(n_dec, rows, d)[:, :t_dec],
        ret_state_p[None],
        kv_p_rows(k_rows_p),
        kv_p_rows(v_rows_p),
        rows_p(COL_IK, IDX_DH, (IDX_DH,)),
        kv_p[:, :MEM_W].reshape(1, batch, mem_len, MEM_HEADS, HEAD_DIM),
        kv_p[:, MEM_W:].reshape(1, batch, mem_len, MEM_HEADS, HEAD_DIM),
        ret_state_s[None],
        kv_s_rows(k_rows_s),
        kv_s_rows(v_rows_s),
        rows_s(COL_IK, IDX_DH, (IDX_DH,)),
    )
```

```python
import functools
import math

import jax
import jax.numpy as jnp
import numpy as np
from jax import lax
from jax.experimental import pallas as pl
from jax.experimental.pallas import tpu as pltpu

F32 = jnp.float32
BF16 = jnp.bfloat16

D_MODEL = 2048
RET_HEADS = 8
HEAD_DIM = 128
DSA_HEADS = 8
DSA_KV_HEADS = 2
IDX_HEADS = 16
IDX_DH = 64
TOPK_MAX = 256
PAGE_SIZE = 128
MEM_LEN = 256
MEM_HEADS = 4
MEM_W = MEM_HEADS * HEAD_DIM
D_FF = 4 * D_MODEL
RET_THETA = 10000.0
ROPE_THETA = 500000.0
EPS = 1e-6
LANES = 128
SUBLANES = 8
ROW_ALIGN = 16
VMEM_LIMIT = 56 * 1024 * 1024
NEG = -1e30
BISECT_STEPS = 16
MAX_PAGES_PER_STEP = 32

COL_GA = 0
COL_GB = 2048
COL_RQ = 4096
COL_RK = 5120
COL_RV = 6144
COL_RG = 7168
COL_DQ = 8192
COL_IQ = 9216
COL_DK = 10240
COL_DV = 10496
COL_IK = 10752
N_PROJ = 11264
PROJ_TN = 512
PROJ_TN_BF16 = 1024
ROPE_NONE, ROPE_RET, ROPE_RETK, ROPE_DSA, ROPE_IDX, ROPE_IKW = range(6)
COL_ROPE = ((ROPE_NONE,) * 32 + (ROPE_RET,) * 8 + (ROPE_RETK,) * 8 + (ROPE_NONE,) * 16 + (ROPE_DSA,) * 8
            + (ROPE_IDX,) * 8 + (ROPE_DSA,) * 2 + (ROPE_NONE,) * 2 + (ROPE_IKW,) + (ROPE_NONE,) * 3)
TAB_RET_C, TAB_RET_S, TAB_DSA_C, TAB_DSA_S, TAB_IDX_C, TAB_IDX_S = range(6)
TAB_W = 6 * LANES
DSA_ROT = HEAD_DIM // 4
IDX_ROT = IDX_DH // 4


def _cparams(*sem):
    return pltpu.CompilerParams(dimension_semantics=sem, vmem_limit_bytes=VMEM_LIMIT)


def _dot(a, b):
    return jnp.dot(a, b, preferred_element_type=F32)


def _dot_nt(a, b):
    return lax.dot_general(a, b, (((1,), (1,)), ((), ())), preferred_element_type=F32)


def _dot_tn(a, b):
    return lax.dot_general(a, b, (((0,), (0,)), ((), ())), preferred_element_type=F32)


def _rmsnorm_bf16(x, g):
    ms = jnp.mean(x * x, axis=-1, keepdims=True)
    return ((x * lax.rsqrt(ms + EPS)) * g).astype(BF16)


def _sigmoid(x):
    return 1.0 / (1.0 + jnp.exp(-x))


def _tab(tab_ref, which):
    return tab_ref[:, which * LANES:(which + 1) * LANES]


def _rope_cols(o_ref, g, c, s, half, period):
    a = o_ref[:, g * LANES:(g + 1) * LANES]
    if 2 * half == LANES:
        partner = pltpu.roll(a, half, 1)
    else:
        lane = lax.broadcasted_iota(jnp.int32, (1, LANES), 1)
        first = (lane & (period - 1)) < half
        partner = jnp.where(first, pltpu.roll(a, LANES - half, 1), pltpu.roll(a, half, 1))
    o_ref[:, g * LANES:(g + 1) * LANES] = a * c + partner * s


def _finish_block(dst_ref, tab_ref, plan, kv_group, kr_ref, vr_ref):
    for g, kind in enumerate(plan):
        rope = functools.partial(_rope_cols, dst_ref, g)
        if kind == ROPE_RET:
            rope(_tab(tab_ref, TAB_RET_C), _tab(tab_ref, TAB_RET_S), LANES // 2, LANES)
        elif kind == ROPE_RETK:
            scale = HEAD_DIM ** -0.5
            rope(_tab(tab_ref, TAB_RET_C) * scale, _tab(tab_ref, TAB_RET_S) * scale, LANES // 2, LANES)
        elif kind == ROPE_DSA:
            rope(_tab(tab_ref, TAB_DSA_C), _tab(tab_ref, TAB_DSA_S), DSA_ROT // 2, LANES)
        elif kind == ROPE_IDX:
            rope(_tab(tab_ref, TAB_IDX_C), _tab(tab_ref, TAB_IDX_S), IDX_ROT // 2, IDX_DH)
        elif kind == ROPE_IKW:
            lane = lax.broadcasted_iota(jnp.int32, (1, LANES), 1)
            is_ik = lane < IDX_DH
            rope(jnp.where(is_ik, _tab(tab_ref, TAB_IDX_C), 1.0), jnp.where(is_ik, _tab(tab_ref, TAB_IDX_S), 0.0),
                 IDX_ROT // 2, IDX_DH)
    if kv_group is not None:
        rows = dst_ref.shape[0]
        for kvh in range(DSA_KV_HEADS):
            rows_of_head = pl.ds(kvh, rows, stride=DSA_KV_HEADS)
            gk, gv = kv_group + kvh, kv_group + DSA_KV_HEADS + kvh
            kr_ref[rows_of_head, :] = dst_ref[:, gk * LANES:(gk + 1) * LANES]
            vr_ref[rows_of_head, :] = dst_ref[:, gv * LANES:(gv + 1) * LANES]


def _norm_proj_kernel(x_ref, g_ref, w_ref, tab_ref, o_ref, u_ref, *, kr_ref=None, vr_ref=None, wq_ref=None):
    j = pl.program_id(1)

    @pl.when(j == 0)
    def _():
        u_ref[...] = _rmsnorm_bf16(x_ref[...], g_ref[...])

    if tab_ref is None:
        o_ref[...] = _dot(u_ref[...], w_ref[...])
        return
    wb = w_ref[...]
    if wb.dtype != BF16:
        wb = wb.astype(BF16)
        wq_ref[...] = wb
    o_ref[...] = _dot_nt(u_ref[...], wb)
    groups = o_ref.shape[1] // LANES

    plans = {}
    for b in range(N_PROJ // (groups * LANES)):
        plans.setdefault(COL_ROPE[b * groups:(b + 1) * groups], []).append(b)
    kv_block, kv_group = divmod(COL_DK // LANES, groups)
    for plan, blocks in plans.items():
        if all(kind == ROPE_NONE for kind in plan):
            continue
        cond = functools.reduce(jnp.logical_or, [j == b for b in blocks])

        @pl.when(cond)
        def _(plan=plan, blocks=blocks):
            _finish_block(o_ref, tab_ref, plan, kv_group if kv_block in blocks else None, kr_ref, vr_ref)


def _norm_proj(x, g, w, tm, tn):
    m, d = x.shape
    n = w.shape[1]

    def body(x_ref, g_ref, w_ref, o_ref, u_ref):
        _norm_proj_kernel(x_ref, g_ref, w_ref, None, o_ref, u_ref)

    return pl.pallas_call(
        body,
        grid=(m // tm, n // tn),
        in_specs=[
            pl.BlockSpec((tm, d), lambda i, j: (i, 0)),
            pl.BlockSpec((1, d), lambda i, j: (0, 0)),
            pl.BlockSpec((d, tn), lambda i, j: (0, j)),
        ],
        out_specs=pl.BlockSpec((tm, tn), lambda i, j: (i, j)),
        out_shape=jax.ShapeDtypeStruct((m, n), F32),
        scratch_shapes=[pltpu.VMEM((tm, d), BF16)],
        compiler_params=_cparams("parallel", "arbitrary"),
        name="norm_proj",
    )(x, g, w)


_W_IN_ROW = {"rq": 0, "rk": 1024, "rv": 2048, "rg": 3072, "dq": 4096, "dk": 5120, "iq": 5632, "ik": 6656,
             "ga": 6736, "gb": 8784}
PROJ_SRC_ROWS = tuple(_W_IN_ROW[name] + PROJ_TN * k for name, nblk in (
    ("ga", 4), ("gb", 4), ("rq", 2), ("rk", 2), ("rv", 2), ("rg", 2), ("dq", 2), ("iq", 2), ("dk", 1), ("ik", 1))
    for k in range(nblk))


def _norm_proj_rope(x, g, w_t, tab, tm, pos_blocks):
    m, d = x.shape
    from_f32 = w_t.dtype != BF16
    tn = PROJ_TN if from_f32 else PROJ_TN_BF16
    kv_spec = pl.BlockSpec((DSA_KV_HEADS * tm, HEAD_DIM), lambda i, j, *_: (i, 0))
    kv_rows = jax.ShapeDtypeStruct((DSA_KV_HEADS * m, HEAD_DIM), F32)
    in_specs = [
        pl.BlockSpec((tm, d), lambda i, j, *_: (i, 0)),
        pl.BlockSpec((1, d), lambda i, j, *_: (0, 0)),
        None,
        pl.BlockSpec((tm, TAB_W), lambda i, j, *_: (i % pos_blocks, 0)),
    ]
    out_specs = [pl.BlockSpec((tm, tn), lambda i, j, *_: (i, j)), kv_spec, kv_spec]
    out_shape = [jax.ShapeDtypeStruct((m, N_PROJ), F32), kv_rows, kv_rows]
    scratch = [pltpu.VMEM((tm, d), BF16)]
    if from_f32:
        assert m == tm and all(r % ROW_ALIGN == 0 for r in PROJ_SRC_ROWS)
        src_rows = jnp.asarray([r // ROW_ALIGN for r in PROJ_SRC_ROWS], jnp.int32)
        in_specs[2] = pl.BlockSpec((pl.Element(tn), pl.Element(d)), lambda i, j, src: (src[j] * ROW_ALIGN, 0))
        out_specs.append(pl.BlockSpec((tn, d), lambda i, j, src: (j, 0)))
        out_shape.append(jax.ShapeDtypeStruct((N_PROJ, d), BF16))

        def body(src_ref, x_ref, g_ref, w_ref, tab_ref, o_ref, kr_ref, vr_ref, wq_ref, u_ref):
            _norm_proj_kernel(x_ref, g_ref, w_ref, tab_ref, o_ref, u_ref, kr_ref=kr_ref, vr_ref=vr_ref, wq_ref=wq_ref)

        grid_spec = pltpu.PrefetchScalarGridSpec(num_scalar_prefetch=1, grid=(1, N_PROJ // tn), in_specs=in_specs,
                                                 out_specs=out_specs, scratch_shapes=scratch)
        args = (src_rows, x, g, w_t, tab)
    else:
        in_specs[2] = pl.BlockSpec((tn, d), lambda i, j: (j, 0))

        def body(x_ref, g_ref, w_ref, tab_ref, o_ref, kr_ref, vr_ref, u_ref):
            _norm_proj_kernel(x_ref, g_ref, w_ref, tab_ref, o_ref, u_ref, kr_ref=kr_ref, vr_ref=vr_ref)

        grid_spec = pltpu.PrefetchScalarGridSpec(num_scalar_prefetch=0, grid=(m // tm, N_PROJ // tn),
                                                 in_specs=in_specs, out_specs=out_specs, scratch_shapes=scratch)
        args = (x, g, w_t, tab)
    return pl.pallas_call(
        body,
        grid_spec=grid_spec,
        out_shape=out_shape,
        compiler_params=_cparams("parallel", "arbitrary"),
        name="norm_proj_rope",
    )(*args)


@functools.lru_cache(maxsize=None)
def _rope_tables(start, count, period):
    pos = (start + np.arange(count) % period).astype(np.float64)[:, None]

    def cs(half, theta):
        inv = theta ** (-np.arange(half, dtype=np.float64) / half)
        ang = pos * inv[None, :]
        return np.cos(ang), np.sin(ang)

    ones = lambda n: np.ones((count, n))
    zeros = lambda n: np.zeros((count, n))
    c, s = cs(HEAD_DIM // 2, RET_THETA)
    ret_c = np.concatenate([c, c], 1)
    ret_s = np.concatenate([-s, s], 1)
    c, s = cs(DSA_ROT // 2, ROPE_THETA)
    dsa_c = np.concatenate([c, c, ones(LANES - DSA_ROT)], 1)
    dsa_s = np.concatenate([-s, s, zeros(LANES - DSA_ROT)], 1)
    c, s = cs(IDX_ROT // 2, ROPE_THETA)
    idx_c = np.tile(np.concatenate([c, c, ones(IDX_DH - IDX_ROT)], 1), (1, 2))
    idx_s = np.tile(np.concatenate([-s, s, zeros(IDX_DH - IDX_ROT)], 1), (1, 2))
    return np.concatenate([ret_c, ret_s, dsa_c, dsa_s, idx_c, idx_s], 1).astype(np.float32)


@functools.lru_cache(maxsize=None)
def _decay_tables(rows, n_seq, t_valid):
    log_g = np.log1p(-np.exp2(-5.0 - np.arange(RET_HEADS, dtype=np.float64)))
    r = np.arange(n_seq * rows)
    t = (r % rows).astype(np.float64)
    same = (r[:, None] // rows) == (r[None, :] // rows)
    diff = t[:, None] - t[None, :]
    intra = np.where(same[None] & (diff >= 0)[None], np.exp(log_g[:, None, None] * np.maximum(diff, 0.0)[None]), 0.0)
    ones = np.ones((1, 1, HEAD_DIM))
    qdec = np.exp(log_g[:, None] * (t[None, :] + 1.0))[:, :, None] * ones
    kdec = np.where(t[None, :] < t_valid, np.exp(log_g[:, None] * (t_valid - 1.0 - t[None, :])), 0.0)[:, :, None] * ones
    cdec = np.exp(log_g * t_valid)[:, None, None] * ones
    return tuple(a.astype(np.float32) for a in (intra, qdec, kdec, cdec))


def _groupnorm_gate(o, gate, gn):
    mu = jnp.mean(o, axis=-1, keepdims=True)
    d = o - mu
    var = jnp.mean(d * d, axis=-1, keepdims=True)
    n = d * lax.rsqrt(var + EPS) * gn
    return (gate * _sigmoid(gate) * n).astype(BF16)


def _ret_prompt_kernel(q_ref, k_ref, v_ref, g_ref, gn_ref, intra_ref, qdec_ref, kdec_ref, cdec_ref,
                       y_ref, st_ref, s_scr):
    c = pl.program_id(1)

    @pl.when(c == 0)
    def _():
        s_scr[...] = jnp.zeros_like(s_scr)

    for h in range(RET_HEADS):
        sl = slice(h * HEAD_DIM, (h + 1) * HEAD_DIM)
        k = k_ref[:, sl]
        qb = q_ref[:, sl].astype(BF16)
        kb = k.astype(BF16)
        vb = v_ref[:, sl].astype(BF16)
        s = _dot_nt(qb, kb) * intra_ref[h]
        state = s_scr[h]
        o = _dot(s.astype(BF16), vb) + _dot(qb, state.astype(BF16)) * qdec_ref[h]
        kd = (k * kdec_ref[h]).astype(BF16)
        s_scr[h] = state * cdec_ref[h] + _dot_tn(kd, vb)
        y_ref[:, sl] = _groupnorm_gate(o, g_ref[:, sl], gn_ref[:, sl])

    @pl.when(c == pl.num_programs(1) - 1)
    def _():
        st_ref[0] = s_scr[...]


def _ret_prompt(proj, gn, batch, seq, chunk):
    nc = seq // chunk
    intra, qdec, kdec, cdec = _decay_tables(chunk, 1, chunk)
    w = RET_HEADS * HEAD_DIM
    col = lambda off: (lambda b, c: (b * nc + c, off // w))
    const3 = lambda b, c: (0, 0, 0)
    return pl.pallas_call(
        _ret_prompt_kernel,
        grid=(batch, nc),
        in_specs=[
            pl.BlockSpec((chunk, w), col(COL_RQ)),
            pl.BlockSpec((chunk, w), col(COL_RK)),
            pl.BlockSpec((chunk, w), col(COL_RV)),
            pl.BlockSpec((chunk, w), col(COL_RG)),
            pl.BlockSpec((1, w), lambda b, c: (0, 0)),
            pl.BlockSpec((RET_HEADS, chunk, chunk), const3),
            pl.BlockSpec((RET_HEADS, chunk, HEAD_DIM), const3),
            pl.BlockSpec((RET_HEADS, chunk, HEAD_DIM), const3),
            pl.BlockSpec((RET_HEADS, 1, HEAD_DIM), const3),
        ],
        out_specs=[
            pl.BlockSpec((chunk, w), lambda b, c: (b * nc + c, 0)),
            pl.BlockSpec((1, RET_HEADS, HEAD_DIM, HEAD_DIM), lambda b, c: (b, 0, 0, 0)),
        ],
        out_shape=[
            jax.ShapeDtypeStruct((batch * seq, w), BF16),
            jax.ShapeDtypeStruct((batch, RET_HEADS, HEAD_DIM, HEAD_DIM), F32),
        ],
        scratch_shapes=[pltpu.VMEM((RET_HEADS, HEAD_DIM, HEAD_DIM), F32)],
        compiler_params=_cparams("parallel", "arbitrary"),
        name="ret_prompt",
    )(proj, proj, proj, proj, gn, intra, qdec, kdec, cdec)


def _ret_sample_kernel(q_ref, k_ref, v_ref, g_ref, gn_ref, st_ref, intra_ref, qdec_ref, kdec_ref, cdec_ref,
                       y_ref, so_ref, *, n_batch, rows):
    k = k_ref[...]
    qb = q_ref[...].astype(BF16)
    kb = k.astype(BF16)
    vb = v_ref[...].astype(BF16)
    s = _dot_nt(qb, kb) * intra_ref[0]
    kd = k * kdec_ref[0]
    row_batch = lax.broadcasted_iota(jnp.int32, kd.shape, 0) // rows
    inter = []
    for b in range(n_batch):
        state = st_ref[b, 0]
        inter.append(_dot(qb[b * rows:(b + 1) * rows], state.astype(BF16)))
        kd_b = jnp.where(row_batch == b, kd, 0.0).astype(BF16)
        so_ref[b, 0] = state * cdec_ref[0] + _dot_tn(kd_b, vb)
    o = _dot(s.astype(BF16), vb) + jnp.concatenate(inter, axis=0) * qdec_ref[0]
    y_ref[...] = _groupnorm_gate(o, g_ref[...], gn_ref[...])


def _ret_sample(proj, gn, state, n_batch, rows, t_valid):
    m = n_batch * rows
    intra, qdec, kdec, cdec = _decay_tables(rows, n_batch, t_valid)
    col = lambda off: (lambda h: (0, off // HEAD_DIM + h))
    per_head = lambda h: (h, 0, 0)
    return pl.pallas_call(
        functools.partial(_ret_sample_kernel, n_batch=n_batch, rows=rows),
        grid=(RET_HEADS,),
        in_specs=[
            pl.BlockSpec((m, HEAD_DIM), col(COL_RQ)),
            pl.BlockSpec((m, HEAD_DIM), col(COL_RK)),
            pl.BlockSpec((m, HEAD_DIM), col(COL_RV)),
            pl.BlockSpec((m, HEAD_DIM), col(COL_RG)),
            pl.BlockSpec((1, HEAD_DIM), lambda h: (0, h)),
            pl.BlockSpec((n_batch, 1, HEAD_DIM, HEAD_DIM), lambda h: (0, h, 0, 0)),
            pl.BlockSpec((1, m, m), per_head),
            pl.BlockSpec((1, m, HEAD_DIM), per_head),
            pl.BlockSpec((1, m, HEAD_DIM), per_head),
            pl.BlockSpec((1, 1, HEAD_DIM), per_head),
        ],
        out_specs=[
            pl.BlockSpec((m, HEAD_DIM), lambda h: (0, h)),
            pl.BlockSpec((n_batch, 1, HEAD_DIM, HEAD_DIM), lambda h: (0, h, 0, 0)),
        ],
        out_shape=[
            jax.ShapeDtypeStruct((m, RET_HEADS * HEAD_DIM), BF16),
            jax.ShapeDtypeStruct(state.shape, F32),
        ],
        compiler_params=_cparams("parallel"),
        name="ret_sample",
    )(proj, proj, proj, proj, gn, state, intra, qdec, kdec, cdec)


def _kth_largest(count_ge, next_above, smin, smax, n_adm, topk):
    few = n_adm < topk
    lo = jnp.where(few, -jnp.inf, smin)
    c_lo = jnp.where(few, float(topk), n_adm)
    hi = jnp.where(few, -jnp.inf, smax + jnp.maximum(jnp.abs(smax) * 1e-6, 1e-30))

    def bisect(_, state):
        lo, hi, c_lo = state
        mid = 0.5 * lo + 0.5 * hi
        c = count_ge(mid)
        take = c >= topk
        return jnp.where(take, mid, lo), jnp.where(take, hi, mid), jnp.where(take, c, c_lo)

    lo, hi, c_lo = lax.fori_loop(0, BISECT_STEPS, bisect, (lo, hi, c_lo))
    c_lo = jnp.where(few, float(topk), c_lo)

    def unfinished(state):
        return jnp.max(state[2]) > 0.0

    def step_up(state):
        lo, c_lo, active = state
        nxt, n_eq = next_above(lo)
        c_nxt = c_lo - n_eq
        move = active * jnp.where(c_nxt >= topk, 1.0, 0.0)
        lo = jnp.where(move > 0.0, nxt, lo)
        c_lo = jnp.where(move > 0.0, c_nxt, c_lo)
        return lo, c_lo, move * jnp.where(c_nxt > topk, 1.0, 0.0)

    return lax.while_loop(unfinished, step_up, (lo, c_lo, jnp.where(c_lo > topk, 1.0, 0.0)))[0]


def _dsa_prompt_kernel(dq_ref, iq_ref, iwq_ref, kall_ref, vall_ref, ikall_ref, o_ref,
                       kbf, vtb, ika, ikb, keys, z_a, z_b, zmax_a, zmax_b, m_scr, acc_scr, *, tq, topk):
    z_scr, zmax_scr = (z_a, z_b), (zmax_a, zmax_b)
    i = pl.program_id(1)
    n_chunks = kbf.shape[0]
    scale_log2e = HEAD_DIM ** -0.5 * math.log2(math.e)

    @pl.when(i == 0)
    def _():
        lane = lax.broadcasted_iota(jnp.int32, (tq, LANES), 1)
        for c in range(n_chunks):
            rows = slice(c * tq, (c + 1) * tq)
            kbf[c] = kall_ref[rows, :].astype(BF16)
            for kvh in range(DSA_KV_HEADS):
                sl = slice(kvh * HEAD_DIM, (kvh + 1) * HEAD_DIM)
                vtb[c, kvh, :HEAD_DIM, :] = vall_ref[rows, sl].T.astype(BF16)
                vtb[c, kvh, HEAD_DIM:, :] = jnp.ones((ROW_ALIGN, tq), BF16)
            a = ikall_ref[rows, :]
            ika[c] = jnp.where(lane < IDX_DH, a, 0.0).astype(BF16)
            ikb[c] = jnp.where(lane >= IDX_DH, pltpu.roll(a, IDX_DH, 1), 0.0).astype(BF16)

    w_t = iwq_ref[...].T
    iqb = iq_ref[...].astype(BF16)
    qb = (dq_ref[...] * scale_log2e).astype(BF16)
    t_col = i * tq + lax.broadcasted_iota(jnp.int32, (tq, tq), 1)
    s_row0 = lax.broadcasted_iota(jnp.int32, (tq, tq), 0)

    fold = lambda x: x.reshape(tq // SUBLANES, SUBLANES, tq)

    def score_chunk(c, carry):
        smax, smin = carry
        src = jnp.minimum(c, n_chunks - 1)
        a = ika[src]
        b = ikb[src]
        acc = jnp.zeros((tq, tq), F32)
        for p in range(IDX_HEADS // 2):
            pair = iqb[:, p * LANES:(p + 1) * LANES]
            w0 = w_t[IDX_DH + 2 * p:IDX_DH + 2 * p + 1, :]
            w1 = w_t[IDX_DH + 2 * p + 1:IDX_DH + 2 * p + 2, :]
            acc = acc + jnp.maximum(_dot_nt(a, pair), 0.0) * w0
            acc = acc + jnp.maximum(_dot_nt(b, pair), 0.0) * w1
        admissible = s_row0 + c * tq <= t_col
        keys[c] = jnp.where(admissible, acc, -jnp.inf)
        smax = jnp.maximum(smax, fold(jnp.where(admissible, acc, -jnp.inf)).max(axis=0))
        smin = jnp.minimum(smin, fold(jnp.where(admissible, acc, jnp.inf)).min(axis=0))
        return smax, smin

    n_pairs = (i + 2) // 2

    def score_pair(j, carry):
        return score_chunk(2 * j + 1, score_chunk(2 * j, carry))

    smax, smin = lax.fori_loop(0, n_pairs, score_pair, (jnp.full((SUBLANES, tq), -jnp.inf, F32),
                                                        jnp.full((SUBLANES, tq), jnp.inf, F32)))
    smax = smax.max(axis=0, keepdims=True)
    smin = smin.min(axis=0, keepdims=True)

    def count_ge(t):
        def pair(j, cnt):
            for c in (2 * j, 2 * j + 1):
                cnt = cnt + fold(jnp.where(keys[c] >= t, 1.0, 0.0)).sum(axis=0)
            return cnt

        return lax.fori_loop(0, n_pairs, pair, jnp.zeros((SUBLANES, tq), F32)).sum(axis=0, keepdims=True)

    def next_above(lo):
        def pair(j, carry):
            nxt, n_eq = carry
            for c in (2 * j, 2 * j + 1):
                k = keys[c]
                nxt = jnp.minimum(nxt, fold(jnp.where(k > lo, k, jnp.inf)).min(axis=0))
                n_eq = n_eq + fold(jnp.where(k == lo, 1.0, 0.0)).sum(axis=0)
            return nxt, n_eq

        nxt, n_eq = lax.fori_loop(0, n_pairs, pair, (jnp.full((SUBLANES, tq), jnp.inf, F32),
                                                     jnp.zeros((SUBLANES, tq), F32)))
        return nxt.min(axis=0, keepdims=True), n_eq.sum(axis=0, keepdims=True)

    thr = _kth_largest(count_ge, next_above, smin, smax, (t_col[:1, :] + 1).astype(F32), topk)

    m_scr[...] = jnp.full_like(m_scr, NEG)
    acc_scr[...] = jnp.zeros_like(acc_scr)

    kv_of = lambda h: h // (DSA_HEADS // DSA_KV_HEADS)

    def logits(c, slot):
        bias = jnp.where(s_row0 + c * tq <= t_col, jnp.where(keys[c] >= thr, 0.0, NEG), NEG)
        kc = kbf[c]
        for h in range(DSA_HEADS):
            ksl = slice(kv_of(h) * HEAD_DIM, (kv_of(h) + 1) * HEAD_DIM)
            z = _dot_nt(kc[:, ksl], qb[:, h * HEAD_DIM:(h + 1) * HEAD_DIM]) + bias
            z_scr[slot][h] = z
            zmax_scr[slot][h] = z.max(axis=0, keepdims=True)

    def update(c, slot):
        for h in range(DSA_HEADS):
            m_old = m_scr[h]
            m_new = jnp.maximum(m_old, zmax_scr[slot][h])
            p = jnp.exp2(z_scr[slot][h] - m_new)
            acc_scr[h] = acc_scr[h] * jnp.exp2(m_old - m_new) + _dot(vtb[c, kv_of(h)], p.astype(BF16))
            m_scr[h] = m_new

    n = i + 1
    logits(0, 0)

    def two_chunks(k, carry):
        update(2 * k, 0)
        logits(2 * k + 1, 1)
        update(2 * k + 1, 1)
        logits(2 * k + 2, 0)
        return carry

    lax.fori_loop(0, (n - 1) // 2, two_chunks, 0)

    @pl.when(n % 2 == 1)
    def _():
        update(n - 1, 0)

    @pl.when(n % 2 == 0)
    def _():
        update(n - 2, 0)
        logits(n - 1, 1)
        update(n - 1, 1)

    for h in range(DSA_HEADS):
        o = acc_scr[h, :HEAD_DIM, :] * (1.0 / acc_scr[h, HEAD_DIM:HEAD_DIM + 1, :])
        o_ref[:, h * HEAD_DIM:(h + 1) * HEAD_DIM] = o.T.astype(BF16)


def _dsa_prompt(proj, batch, seq, tq):
    nq = seq // tq
    topk = min(TOPK_MAX, seq // 4)
    qw = DSA_HEADS * HEAD_DIM
    kvw = DSA_KV_HEADS * HEAD_DIM
    return pl.pallas_call(
        functools.partial(_dsa_prompt_kernel, tq=tq, topk=topk),
        grid=(batch, nq),
        in_specs=[
            pl.BlockSpec((tq, qw), lambda b, i: (b * nq + i, COL_DQ // qw)),
            pl.BlockSpec((tq, qw), lambda b, i: (b * nq + i, COL_IQ // qw)),
            pl.BlockSpec((tq, LANES), lambda b, i: (b * nq + i, COL_IK // LANES)),
            pl.BlockSpec((seq, kvw), lambda b, i: (b, COL_DK // kvw)),
            pl.BlockSpec((seq, kvw), lambda b, i: (b, COL_DV // kvw)),
            pl.BlockSpec((seq, LANES), lambda b, i: (b, COL_IK // LANES)),
        ],
        out_specs=pl.BlockSpec((tq, qw), lambda b, i: (b * nq + i, 0)),
        out_shape=jax.ShapeDtypeStruct((batch * seq, qw), BF16),
        scratch_shapes=[
            pltpu.VMEM((nq, tq, kvw), BF16),
            pltpu.VMEM((nq, DSA_KV_HEADS, HEAD_DIM + ROW_ALIGN, tq), BF16),
            pltpu.VMEM((nq, tq, LANES), BF16),
            pltpu.VMEM((nq, tq, LANES), BF16),
            pltpu.VMEM((nq + 1, tq, tq), F32),
            pltpu.VMEM((DSA_HEADS, tq, tq), F32),
            pltpu.VMEM((DSA_HEADS, tq, tq), F32),
            pltpu.VMEM((DSA_HEADS, 1, tq), F32),
            pltpu.VMEM((DSA_HEADS, 1, tq), F32),
            pltpu.VMEM((DSA_HEADS, 1, tq), F32),
            pltpu.VMEM((DSA_HEADS, HEAD_DIM + ROW_ALIGN, tq), F32),
        ],
        compiler_params=_cparams("parallel", "arbitrary"),
        name="dsa_prompt",
    )(proj, proj, proj, proj, proj, proj)


def _idx_sample_kernel(pt_ref, iq_ref, iw_ref, iknew_ref, *rest, pages, t_valid):
    page_refs = rest[:pages]
    sp_ref, sn_ref = rest[pages:]
    iqb = iq_ref[0, :t_valid * IDX_HEADS].astype(BF16)
    w = iw_ref[0, :t_valid * IDX_HEADS]
    rows = sp_ref.shape[1]

    def scores(ik_t):
        d = jnp.maximum(_dot(iqb, ik_t.astype(BF16)), 0.0) * w
        return d.reshape(t_valid, IDX_HEADS, LANES).sum(axis=1)

    sp_ref[0, t_valid:, :] = jnp.zeros((rows - t_valid, pages * PAGE_SIZE), F32)
    for p in range(pages):
        sp_ref[0, :t_valid, p * PAGE_SIZE:(p + 1) * PAGE_SIZE] = scores(page_refs[p][0])
    t = lax.broadcasted_iota(jnp.int32, (t_valid, LANES), 0)
    j = lax.broadcasted_iota(jnp.int32, (t_valid, LANES), 1)
    sn_ref[0, t_valid:, :] = jnp.zeros((rows - t_valid, LANES), F32)
    sn_ref[0, :t_valid, :] = jnp.where(j <= t, scores(iknew_ref[0]), -jnp.inf)


def _idx_sample(page_table, iq, iw, ik_new, pool_ik, pages, t_valid):
    n_batch, n_pages = page_table.shape
    rows16 = iq.shape[1]
    rows = rows16 // IDX_HEADS
    past = n_pages * PAGE_SIZE
    page_specs = [pl.BlockSpec((1, IDX_DH, PAGE_SIZE), functools.partial(
        lambda b, g, pt, k: (pt[b, g * pages + k], 0, 0), k=k)) for k in range(pages)]
    grid_spec = pltpu.PrefetchScalarGridSpec(
        num_scalar_prefetch=1,
        grid=(n_batch, n_pages // pages),
        in_specs=[
            pl.BlockSpec((1, rows16, IDX_DH), lambda b, g, pt: (b, 0, 0)),
            pl.BlockSpec((1, rows16, LANES), lambda b, g, pt: (b, 0, 0)),
            pl.BlockSpec((1, IDX_DH, PAGE_SIZE), lambda b, g, pt: (b, 0, 0)),
        ] + page_specs,
        out_specs=[
            pl.BlockSpec((1, rows, pages * PAGE_SIZE), lambda b, g, pt: (b, 0, g)),
            pl.BlockSpec((1, rows, LANES), lambda b, g, pt: (b, 0, 0)),
        ],
    )
    return pl.pallas_call(
        functools.partial(_idx_sample_kernel, pages=pages, t_valid=t_valid),
        grid_spec=grid_spec,
        out_shape=[
            jax.ShapeDtypeStruct((n_batch, rows, past), F32),
            jax.ShapeDtypeStruct((n_batch, rows, LANES), F32),
        ],
        compiler_params=_cparams("parallel", "arbitrary"),
        name="idx_sample",
    )(page_table, iq, iw, ik_new, *([pool_ik] * pages))


def _sel_sample_kernel(sp_ref, sn_ref, bp_ref, bn_ref, *, topk):
    n_batch, rows, past = sp_ref.shape
    sp = sp_ref[...].reshape(n_batch * rows, past)
    sn = sn_ref[...].reshape(n_batch * rows, LANES)
    t = lax.broadcasted_iota(jnp.int32, (n_batch * rows, LANES), 0) % rows
    j = lax.broadcasted_iota(jnp.int32, (n_batch * rows, LANES), 1)
    admissible_new = j <= t
    row_sum = lambda a: a.sum(axis=1, keepdims=True)
    row_min = lambda a: a.min(axis=1, keepdims=True)

    def count_ge(thr):
        return row_sum(jnp.where(sp >= thr, 1.0, 0.0)) + row_sum(jnp.where(sn >= thr, 1.0, 0.0))

    def next_above(lo):
        nxt = jnp.minimum(row_min(jnp.where(sp > lo, sp, jnp.inf)), row_min(jnp.where(sn > lo, sn, jnp.inf)))
        return nxt, row_sum(jnp.where(sp == lo, 1.0, 0.0)) + row_sum(jnp.where(sn == lo, 1.0, 0.0))

    smin = jnp.minimum(row_min(sp), row_min(jnp.where(admissible_new, sn, jnp.inf)))
    smax = jnp.maximum(sp.max(axis=1, keepdims=True), sn.max(axis=1, keepdims=True))
    n_adm = float(past) + row_sum(jnp.where(admissible_new, 1.0, 0.0))
    thr = _kth_largest(count_ge, next_above, smin, smax, n_adm, topk)
    bp_ref[...] = jnp.where(sp >= thr, 0.0, NEG).reshape(n_batch, rows, past)
    bn_ref[...] = jnp.where(admissible_new, jnp.where(sn >= thr, 0.0, NEG), NEG).reshape(n_batch, rows, LANES)


def _sel_sample(scores_past, scores_new, topk):
    full = lambda a: pl.BlockSpec(a.shape, lambda: (0,) * a.ndim)
    return pl.pallas_call(
        functools.partial(_sel_sample_kernel, topk=topk),
        in_specs=[full(scores_past), full(scores_new)],
        out_specs=[full(scores_past), full(scores_new)],
        out_shape=[jax.ShapeDtypeStruct(scores_past.shape, F32), jax.ShapeDtypeStruct(scores_new.shape, F32)],
        compiler_params=pltpu.CompilerParams(vmem_limit_bytes=VMEM_LIMIT),
        name="sel_sample",
    )(scores_past, scores_new)


def _dsa_sample_kernel(pt_ref, bp_ref, bn_ref, q_ref, knew_ref, vnew_ref, *rest, pages, steps):
    k_refs = rest[:pages]
    v_refs = rest[pages:2 * pages]
    o_ref, m_scr, l_scr, acc_scr = rest[2 * pages:]
    g = pl.program_id(0) % steps
    scale = HEAD_DIM ** -0.5
    group = DSA_HEADS // DSA_KV_HEADS

    @pl.when(g == 0)
    def _():
        m_scr[...] = jnp.full_like(m_scr, NEG)
        l_scr[...] = jnp.zeros_like(l_scr)
        acc_scr[...] = jnp.zeros_like(acc_scr)

    def head_rows(ref, kvh):
        return ref[0, pl.ds(kvh, PAGE_SIZE, stride=DSA_KV_HEADS), :].astype(BF16)

    def attend(k_pages, v_pages, bias):
        bias_g = jnp.concatenate([bias] * group, axis=0)
        kv_heads = range(DSA_KV_HEADS)
        qb = [q_ref[0, kvh].astype(BF16) for kvh in kv_heads]
        raw = [jnp.concatenate([_dot_nt(qb[kvh], head_rows(k, kvh)) for k in k_pages], axis=1) for kvh in kv_heads]
        pb, alpha = [], []
        for kvh in kv_heads:
            z = jnp.where(bias_g < 0.0, NEG, raw[kvh])
            m_old = m_scr[kvh]
            m_new = jnp.maximum(m_old, z.max(axis=1, keepdims=True))
            alpha.append(jnp.exp((m_old - m_new) * scale))
            p = jnp.exp((z - m_new) * scale)
            l_scr[kvh] = l_scr[kvh] * alpha[kvh] + p.sum(axis=1, keepdims=True)
            m_scr[kvh] = m_new
            pb.append(p.astype(BF16))
        for kvh in kv_heads:
            pv = _dot(pb[kvh][:, :PAGE_SIZE], head_rows(v_pages[0], kvh))
            for n in range(1, len(v_pages)):
                pv = pv + _dot(pb[kvh][:, n * PAGE_SIZE:(n + 1) * PAGE_SIZE], head_rows(v_pages[n], kvh))
            acc_scr[kvh] = acc_scr[kvh] * alpha[kvh] + pv

    attend(k_refs, v_refs, bp_ref[0])

    @pl.when(g == steps - 1)
    def _():
        attend([knew_ref], [vnew_ref], bn_ref[0])
        for kvh in range(DSA_KV_HEADS):
            o_ref[0, kvh] = (acc_scr[kvh] * (1.0 / l_scr[kvh])).astype(BF16)


def _dsa_sample(page_table, bias_past, bias_new, q, k_new, v_new, pool_k, pool_v, steps):
    n_batch, n_pages = page_table.shape
    pages = n_pages // steps
    rows = bias_past.shape[1]
    page_rows = DSA_KV_HEADS * PAGE_SIZE
    grows = q.shape[2]
    page_spec = lambda k: pl.BlockSpec((1, page_rows, HEAD_DIM), functools.partial(
        lambda i, pt, k: (pt[i // steps, (i % steps) * pages + k], 0, 0), k=k))
    per_batch3 = lambda i, pt: (i // steps, 0, 0)
    per_batch4 = lambda i, pt: (i // steps, 0, 0, 0)
    in_specs = [
        pl.BlockSpec((1, rows, pages * PAGE_SIZE), lambda i, pt: (i // steps, 0, i % steps)),
        pl.BlockSpec((1, rows, LANES), per_batch3),
        pl.BlockSpec((1, DSA_KV_HEADS, grows, HEAD_DIM), per_batch4),
        pl.BlockSpec((1, page_rows, HEAD_DIM), per_batch3),
        pl.BlockSpec((1, page_rows, HEAD_DIM), per_batch3),
    ] + [page_spec(k) for k in range(pages)] * 2
    grid_spec = pltpu.PrefetchScalarGridSpec(
        num_scalar_prefetch=1,
        grid=(n_batch * steps,),
        in_specs=in_specs,
        out_specs=pl.BlockSpec((1, DSA_KV_HEADS, grows, HEAD_DIM), per_batch4),
        scratch_shapes=[
            pltpu.VMEM((DSA_KV_HEADS, grows, 1), F32),
            pltpu.VMEM((DSA_KV_HEADS, grows, 1), F32),
            pltpu.VMEM((DSA_KV_HEADS, grows, HEAD_DIM), F32),
        ],
    )
    return pl.pallas_call(
        functools.partial(_dsa_sample_kernel, pages=pages, steps=steps),
        grid_spec=grid_spec,
        out_shape=jax.ShapeDtypeStruct((n_batch, DSA_KV_HEADS, grows, HEAD_DIM), BF16),
        compiler_params=_cparams("arbitrary"),
        name="dsa_sample",
    )(page_table, bias_past, bias_new, q, k_new, v_new, *([pool_k] * pages), *([pool_v] * pages))


def _merge_mem_kernel(x_ref, ry_ref, do_ref, ga_ref, gb_ref, wr_ref, wd_ref, wo_ref,
                      g_ref, wq_ref, mk_ref, mv_ref, wmo_ref, o_ref, *, rows_per_batch):
    ya = _dot(ry_ref[...], wr_ref[...])
    yb = _dot(do_ref[...], wd_ref[...])
    merged = _sigmoid(ga_ref[...]) * ya + _sigmoid(gb_ref[...]) * yb
    h = x_ref[...] + _dot(merged.astype(BF16), wo_ref[...])
    qm = _dot(_rmsnorm_bf16(h, g_ref[...]), wq_ref[...])
    tm = h.shape[0]
    nk = mk_ref.shape[0]
    scale = HEAD_DIM ** -0.5
    if rows_per_batch is not None:
        rb = lax.broadcasted_iota(jnp.int32, (tm, nk), 0) // rows_per_batch
        kb = lax.broadcasted_iota(jnp.int32, (tm, nk), 1) // MEM_LEN
        same = rb == kb
    outs = []
    for hd in range(MEM_HEADS):
        sl = slice(hd * HEAD_DIM, (hd + 1) * HEAD_DIM)
        z = _dot_nt(qm[:, sl].astype(BF16), mk_ref[:, sl].astype(BF16)) * scale
        if rows_per_batch is not None:
            z = jnp.where(same, z, NEG)
        p = jnp.exp(z - z.max(axis=-1, keepdims=True))
        pv = _dot(p.astype(BF16), mv_ref[:, sl].astype(BF16))
        outs.append(pv * (1.0 / p.sum(axis=-1, keepdims=True)))
    om = jnp.concatenate(outs, axis=1).astype(BF16)
    o_ref[...] = h + _dot(om, wmo_ref[...])


def _merge_mem(x, ret_y, dsa_o, proj, w_ret_out, w_dsa_out, w_o, g, w_mq, mk, mv, w_mo, *,
               tm, batch_tiles, mk_col, mv_col, nk, rows_per_batch):
    m, d = x.shape
    w = ret_y.shape[1]
    const = lambda i: (0, 0)
    resident = dict(pipeline_mode=pl.Buffered(1))
    if batch_tiles:
        kmap = lambda col: (lambda i: (i // batch_tiles, col))
    else:
        kmap = lambda col: (lambda i: (0, col))
    return pl.pallas_call(
        functools.partial(_merge_mem_kernel, rows_per_batch=rows_per_batch),
        grid=(m // tm,),
        in_specs=[
            pl.BlockSpec((tm, d), lambda i: (i, 0)),
            pl.BlockSpec((tm, w), lambda i: (i, 0)),
            pl.BlockSpec((tm, w), lambda i: (i, 0)),
            pl.BlockSpec((tm, d), lambda i: (i, COL_GA // d)),
            pl.BlockSpec((tm, d), lambda i: (i, COL_GB // d)),
            pl.BlockSpec((w, d), const, **resident),
            pl.BlockSpec((w, d), const, **resident),
            pl.BlockSpec((d, d), const, **resident),
            pl.BlockSpec((1, d), const),
            pl.BlockSpec((d, MEM_W), const, **resident),
            pl.BlockSpec((nk, MEM_W), kmap(mk_col)),
            pl.BlockSpec((nk, MEM_W), kmap(mv_col)),
            pl.BlockSpec((MEM_W, d), const, **resident),
        ],
        out_specs=pl.BlockSpec((tm, d), lambda i: (i, 0)),
        out_shape=jax.ShapeDtypeStruct((m, d), F32),
        compiler_params=_cparams("parallel"),
        name="merge_mem",
    )(x, ret_y, dsa_o, proj, proj, w_ret_out, w_dsa_out, w_o, g, w_mq, mk, mv, w_mo)


def _mlp_kernel(h_ref, g_ref, wu_ref, wd_ref, gf_ref, o_ref, *rest):
    u_ref = rest[-1]
    j = pl.program_id(1)

    @pl.when(j == 0)
    def _():
        u_ref[...] = _rmsnorm_bf16(h_ref[...], g_ref[...])
        o_ref[...] = h_ref[...]

    wu, wd = wu_ref[...], wd_ref[...]
    if len(rest) == 3:
        wu, wd = wu.astype(BF16), wd.astype(BF16)
        rest[0][...] = wu
        rest[1][...] = wd
    a = jnp.maximum(_dot(u_ref[...], wu), 0.0)
    o_ref[...] += _dot((a * a).astype(BF16), wd)

    @pl.when(j == pl.num_programs(1) - 1)
    def _():
        y = o_ref[...]
        ms = jnp.mean(y * y, axis=-1, keepdims=True)
        o_ref[...] = (y * lax.rsqrt(ms + EPS)) * gf_ref[...]


def _mlp(h, g, w_up, w_down, g_final, tm, tf):
    m, d = h.shape
    ff = w_up.shape[1]
    emit = w_up.dtype != BF16
    assert not emit or m == tm, "weight copies are written once, by a call with one row tile"
    wu_spec = pl.BlockSpec((d, tf), lambda i, j: (0, j))
    wd_spec = pl.BlockSpec((tf, d), lambda i, j: (j, 0))
    out_specs = [pl.BlockSpec((tm, d), lambda i, j: (i, 0))]
    out_shape = [jax.ShapeDtypeStruct((m, d), F32)]
    if emit:
        out_specs += [wu_spec, wd_spec]
        out_shape += [jax.ShapeDtypeStruct(w_up.shape, BF16), jax.ShapeDtypeStruct(w_down.shape, BF16)]
    out = pl.pallas_call(
        _mlp_kernel,
        grid=(m // tm, ff // tf),
        in_specs=[
            pl.BlockSpec((tm, d), lambda i, j: (i, 0)),
            pl.BlockSpec((1, d), lambda i, j: (0, 0)),
            wu_spec,
            wd_spec,
            pl.BlockSpec((1, d), lambda i, j: (0, 0)),
        ],
        out_specs=out_specs,
        out_shape=out_shape,
        scratch_shapes=[pltpu.VMEM((tm, d), BF16)],
        compiler_params=_cparams("parallel", "arbitrary"),
        name="mlp_final",
    )(h, g, w_up, w_down, g_final)
    return out if emit else out[0]


def _pick_tile(n, pref):
    t = min(n, pref)
    while n % t:
        t //= 2
    return t


def kernel(x_prompt, x_sample, mem_prompt, cache_k, cache_v, cache_idx_k, state_ret, cache_mem_k, cache_mem_v,
           page_table, g_mix, w_in, gn_ret, w_ret_out, w_dsa_out, w_o, g_mem, g_memkv, w_mq, w_mk, w_mv, w_mo,
           g_mlp, w_up, w_down, g_final):
    assert w_in.shape[0] == 1, "one layer"
    batch, seq, d = x_prompt.shape
    n_dec, t_dec, _ = x_sample.shape
    n_pages = page_table.shape[1]
    past = n_pages * PAGE_SIZE
    mem_len = mem_prompt.shape[1]
    assert mem_len == MEM_LEN and t_dec <= SUBLANES
    row = lambda v: v.reshape(1, -1)

    w_ret_out_b, w_dsa_out_b, w_o_b = (w[0].astype(BF16) for w in (w_ret_out, w_dsa_out, w_o))
    w_mq_b, w_mo_b = w_mq[0].astype(BF16), w_mo[0].astype(BF16)
    w_mkv_b = jnp.concatenate([w_mk[0], w_mv[0]], axis=1).astype(BF16)
    g_mix_r, gn_r, g_mem_r, g_memkv_r, g_mlp_r, g_final_r = (
        row(v) for v in (g_mix[0], gn_ret[0], g_mem[0], g_memkv[0], g_mlp[0], g_final))

    rows = SUBLANES
    m_s = n_dec * rows
    xs = jnp.pad(x_sample, ((0, 0), (0, rows - t_dec), (0, 0))).reshape(m_s, d)
    proj_s, k_rows_s, v_rows_s, w_proj_b = _norm_proj_rope(xs, g_mix_r, jnp.swapaxes(w_in[0], 0, 1),
                                                           _rope_tables(past, m_s, rows), m_s, 1)

    m_p = batch * seq
    xp = x_prompt.reshape(m_p, d)
    tm_proj = _pick_tile(seq, 1024)
    proj_p, k_rows_p, v_rows_p = _norm_proj_rope(xp, g_mix_r, w_proj_b, _rope_tables(0, seq, seq), tm_proj,
                                                 seq // tm_proj)
    ret_y_p, ret_state_p = _ret_prompt(proj_p, gn_r, batch, seq, _pick_tile(seq, 256))
    dsa_o_p = _dsa_prompt(proj_p, batch, seq, _pick_tile(seq, 256))
    kv_p = _norm_proj(mem_prompt.reshape(batch * mem_len, d), g_memkv_r, w_mkv_b, mem_len, PROJ_TN)
    tm_t = _pick_tile(seq, 256)
    h_p = _merge_mem(xp, ret_y_p, dsa_o_p, proj_p, w_ret_out_b, w_dsa_out_b, w_o_b, g_mem_r, w_mq_b, kv_p, kv_p,
                     w_mo_b, tm=tm_t, batch_tiles=seq // tm_t, mk_col=0, mv_col=1, nk=mem_len, rows_per_batch=None)

    ret_y_s, ret_state_s = _ret_sample(proj_s, gn_r, state_ret[0], n_dec, rows, t_dec)

    proj_s3 = proj_s.reshape(n_dec, rows, N_PROJ)
    iq_s = proj_s3[:, :, COL_IQ:COL_IQ + IDX_HEADS * IDX_DH].reshape(n_dec, rows * IDX_HEADS, IDX_DH)
    iw_s = proj_s3[:, :, COL_IK + IDX_DH:COL_IK + IDX_DH + IDX_HEADS].reshape(n_dec, rows * IDX_HEADS, 1)
    iw_s = jnp.broadcast_to(iw_s, (n_dec, rows * IDX_HEADS, LANES))
    kvw = DSA_KV_HEADS * HEAD_DIM
    page_rows = DSA_KV_HEADS * PAGE_SIZE
    pad_keys = lambda a: jnp.pad(a, ((0, 0), (0, PAGE_SIZE - rows), (0, 0)))
    ik_new_t = jnp.swapaxes(pad_keys(proj_s3[:, :, COL_IK:COL_IK + IDX_DH]), 1, 2)
    pool_ik_t = jnp.swapaxes(cache_idx_k, 2, 3).reshape(-1, IDX_DH, PAGE_SIZE)
    new_page = lambda a: jnp.pad(a.reshape(n_dec, DSA_KV_HEADS * rows, HEAD_DIM),
                                 ((0, 0), (0, page_rows - DSA_KV_HEADS * rows), (0, 0)))
    k_new, v_new = new_page(k_rows_s), new_page(v_rows_s)
    scores_past, scores_new = _idx_sample(page_table, iq_s, iw_s, ik_new_t, pool_ik_t,
                                          _pick_tile(n_pages, MAX_PAGES_PER_STEP), t_dec)
    group = DSA_HEADS // DSA_KV_HEADS
    dq_s = proj_s3[:, :, COL_DQ:COL_DQ + DSA_HEADS * HEAD_DIM].reshape(n_dec, rows, DSA_KV_HEADS, group, HEAD_DIM)
    dq_s = dq_s.transpose(0, 2, 3, 1, 4).reshape(n_dec, DSA_KV_HEADS, group * rows, HEAD_DIM)
    bias_past, bias_new = _sel_sample(scores_past, scores_new, min(TOPK_MAX, (past + t_dec) // 4))
    dsa_o_s = _dsa_sample(page_table, bias_past, bias_new, dq_s, k_new, v_new,
                          cache_k.reshape(-1, page_rows, HEAD_DIM), cache_v.reshape(-1, page_rows, HEAD_DIM),
                          n_pages // _pick_tile(n_pages, MAX_PAGES_PER_STEP))
    dsa_o_s = dsa_o_s.reshape(n_dec, DSA_KV_HEADS, group, rows, HEAD_DIM).transpose(0, 3, 1, 2, 4)
    dsa_o_s = dsa_o_s.reshape(m_s, DSA_HEADS * HEAD_DIM)

    h_s = _merge_mem(xs, ret_y_s, dsa_o_s, proj_s, w_ret_out_b, w_dsa_out_b, w_o_b, g_mem_r, w_mq_b,
                     cache_mem_k[0].reshape(n_dec * mem_len, MEM_W), cache_mem_v[0].reshape(n_dec * mem_len, MEM_W),
                     w_mo_b, tm=m_s, batch_tiles=0, mk_col=0, mv_col=0, nk=n_dec * mem_len, rows_per_batch=rows)
    y_s, w_up_b, w_down_b = _mlp(h_s, g_mlp_r, w_up[0], w_down[0], g_final_r, m_s, 512)
    y_p = _mlp(h_p, g_mlp_r, w_up_b, w_down_b, g_final_r, _pick_tile(seq, 1024), 512)

    def rows_p(col, width, tail):
        return proj_p[:, col:col + width].reshape((1, batch, seq) + tail)

    def rows_s(col, width, tail):
        return proj_s3[:, :t_dec, col:col + width].reshape((1, n_dec, t_dec) + tail)

    kv_p_rows = lambda a: a.reshape(1, batch, seq, DSA_KV_HEADS, HEAD_DIM)
    kv_s_rows = lambda a: a.reshape(1, n_dec, rows, DSA_KV_HEADS, HEAD_DIM)[:, :, :t_dec]
    return (
        y_p.reshape(batch, seq, d),
        y_s.reshape(n_dec, rows, d)[:, :t_dec],
        ret_state_p[None],
        kv_p_rows(k_rows_p),
        kv_p_rows(v_rows_p),
        rows_p(COL_IK, IDX_DH, (IDX_DH,)),
        kv_p[:, :MEM_W].reshape(1, batch, mem_len, MEM_HEADS, HEAD_DIM),
        kv_p[:, MEM_W:].reshape(1, batch, mem_len, MEM_HEADS, HEAD_DIM),
        ret_state_s[None],
        kv_s_rows(k_rows_s),
        kv_s_rows(v_rows_s),
        rows_s(COL_IK, IDX_DH, (IDX_DH,)),
    )
```

```python
import functools
import math

import jax
import jax.numpy as jnp
import numpy as np
from jax import lax
from jax.experimental import pallas as pl
from jax.experimental.pallas import tpu as pltpu

F32 = jnp.float32
BF16 = jnp.bfloat16

D_MODEL = 2048
RET_HEADS = 8
HEAD_DIM = 128
DSA_HEADS = 8
DSA_KV_HEADS = 2
IDX_HEADS = 16
IDX_DH = 64
TOPK_MAX = 256
PAGE_SIZE = 128
MEM_LEN = 256
MEM_HEADS = 4
MEM_W = MEM_HEADS * HEAD_DIM
D_FF = 4 * D_MODEL
RET_THETA = 10000.0
ROPE_THETA = 500000.0
EPS = 1e-6
LANES = 128
SUBLANES = 8
ROW_ALIGN = 16
VMEM_LIMIT = 56 * 1024 * 1024
NEG = -1e30
BISECT_STEPS = 16
MAX_PAGES_PER_STEP = 32

COL_GA = 0
COL_GB = 2048
COL_RQ = 4096
COL_RK = 5120
COL_RV = 6144
COL_RG = 7168
COL_DQ = 8192
COL_IQ = 9216
COL_DK = 10240
COL_DV = 10496
COL_IK = 10752
N_PROJ = 11264
PROJ_TN = 512
PROJ_TN_BF16 = 1024
ROPE_NONE, ROPE_RET, ROPE_RETK, ROPE_DSA, ROPE_IDX, ROPE_IKW = range(6)
COL_ROPE = ((ROPE_NONE,) * 32 + (ROPE_RET,) * 8 + (ROPE_RETK,) * 8 + (ROPE_NONE,) * 16 + (ROPE_DSA,) * 8
            + (ROPE_IDX,) * 8 + (ROPE_DSA,) * 2 + (ROPE_NONE,) * 2 + (ROPE_IKW,) + (ROPE_NONE,) * 3)
TAB_RET_C, TAB_RET_S, TAB_DSA_C, TAB_DSA_S, TAB_IDX_C, TAB_IDX_S = range(6)
TAB_W = 6 * LANES
DSA_ROT = HEAD_DIM // 4
IDX_ROT = IDX_DH // 4


def _cparams(*sem):
    return pltpu.CompilerParams(dimension_semantics=sem, vmem_limit_bytes=VMEM_LIMIT)


def _dot(a, b):
    return jnp.dot(a, b, preferred_element_type=F32)


def _dot_nt(a, b):
    return lax.dot_general(a, b, (((1,), (1,)), ((), ())), preferred_element_type=F32)


def _dot_tn(a, b):
    return lax.dot_general(a, b, (((0,), (0,)), ((), ())), preferred_element_type=F32)


def _rmsnorm_bf16(x, g):
    ms = jnp.mean(x * x, axis=-1, keepdims=True)
    return ((x * lax.rsqrt(ms + EPS)) * g).astype(BF16)


def _sigmoid(x):
    return 1.0 / (1.0 + jnp.exp(-x))


def _tab(tab_ref, which):
    return tab_ref[:, which * LANES:(which + 1) * LANES]


def _rope_cols(o_ref, g, c, s, half, period):
    a = o_ref[:, g * LANES:(g + 1) * LANES]
    if 2 * half == LANES:
        partner = pltpu.roll(a, half, 1)
    else:
        lane = lax.broadcasted_iota(jnp.int32, (1, LANES), 1)
        first = (lane & (period - 1)) < half
        partner = jnp.where(first, pltpu.roll(a, LANES - half, 1), pltpu.roll(a, half, 1))
    o_ref[:, g * LANES:(g + 1) * LANES] = a * c + partner * s


def _finish_block(dst_ref, tab_ref, plan, kv_group, kr_ref, vr_ref):
    for g, kind in enumerate(plan):
        rope = functools.partial(_rope_cols, dst_ref, g)
        if kind == ROPE_RET:
            rope(_tab(tab_ref, TAB_RET_C), _tab(tab_ref, TAB_RET_S), LANES // 2, LANES)
        elif kind == ROPE_RETK:
            scale = HEAD_DIM ** -0.5
            rope(_tab(tab_ref, TAB_RET_C) * scale, _tab(tab_ref, TAB_RET_S) * scale, LANES // 2, LANES)
        elif kind == ROPE_DSA:
            rope(_tab(tab_ref, TAB_DSA_C), _tab(tab_ref, TAB_DSA_S), DSA_ROT // 2, LANES)
        elif kind == ROPE_IDX:
            rope(_tab(tab_ref, TAB_IDX_C), _tab(tab_ref, TAB_IDX_S), IDX_ROT // 2, IDX_DH)
        elif kind == ROPE_IKW:
            lane = lax.broadcasted_iota(jnp.int32, (1, LANES), 1)
            is_ik = lane < IDX_DH
            rope(jnp.where(is_ik, _tab(tab_ref, TAB_IDX_C), 1.0), jnp.where(is_ik, _tab(tab_ref, TAB_IDX_S), 0.0),
                 IDX_ROT // 2, IDX_DH)
    if kv_group is not None:
        rows = dst_ref.shape[0]
        for kvh in range(DSA_KV_HEADS):
            rows_of_head = pl.ds(kvh, rows, stride=DSA_KV_HEADS)
            gk, gv = kv_group + kvh, kv_group + DSA_KV_HEADS + kvh
            kr_ref[rows_of_head, :] = dst_ref[:, gk * LANES:(gk + 1) * LANES]
            vr_ref[rows_of_head, :] = dst_ref[:, gv * LANES:(gv + 1) * LANES]


def _norm_proj_kernel(x_ref, g_ref, w_ref, tab_ref, o_ref, u_ref, *, kr_ref=None, vr_ref=None, wq_ref=None):
    j = pl.program_id(1)

    @pl.when(j == 0)
    def _():
        u_ref[...] = _rmsnorm_bf16(x_ref[...], g_ref[...])

    if tab_ref is None:
        o_ref[...] = _dot(u_ref[...], w_ref[...])
        return
    wb = w_ref[...]
    if wb.dtype != BF16:
        wb = wb.astype(BF16)
        wq_ref[...] = wb
    o_ref[...] = _dot_nt(u_ref[...], wb)
    groups = o_ref.shape[1] // LANES

    plans = {}
    for b in range(N_PROJ // (groups * LANES)):
        plans.setdefault(COL_ROPE[b * groups:(b + 1) * groups], []).append(b)
    kv_block, kv_group = divmod(COL_DK // LANES, groups)
    for plan, blocks in plans.items():
        if all(kind == ROPE_NONE for kind in plan):
            continue
        cond = functools.reduce(jnp.logical_or, [j == b for b in blocks])

        @pl.when(cond)
        def _(plan=plan, blocks=blocks):
            _finish_block(o_ref, tab_ref, plan, kv_group if kv_block in blocks else None, kr_ref, vr_ref)


def _norm_proj(x, g, w, tm, tn):
    m, d = x.shape
    n = w.shape[1]

    def body(x_ref, g_ref, w_ref, o_ref, u_ref):
        _norm_proj_kernel(x_ref, g_ref, w_ref, None, o_ref, u_ref)

    return pl.pallas_call(
        body,
        grid=(m // tm, n // tn),
        in_specs=[
            pl.BlockSpec((tm, d), lambda i, j: (i, 0)),
            pl.BlockSpec((1, d), lambda i, j: (0, 0)),
            pl.BlockSpec((d, tn), lambda i, j: (0, j)),
        ],
        out_specs=pl.BlockSpec((tm, tn), lambda i, j: (i, j)),
        out_shape=jax.ShapeDtypeStruct((m, n), F32),
        scratch_shapes=[pltpu.VMEM((tm, d), BF16)],
        compiler_params=_cparams("parallel", "arbitrary"),
        name="norm_proj",
    )(x, g, w)


_W_IN_ROW = {"rq": 0, "rk": 1024, "rv": 2048, "rg": 3072, "dq": 4096, "dk": 5120, "iq": 5632, "ik": 6656,
             "ga": 6736, "gb": 8784}
PROJ_SRC_ROWS = tuple(_W_IN_ROW[name] + PROJ_TN * k for name, nblk in (
    ("ga", 4), ("gb", 4), ("rq", 2), ("rk", 2), ("rv", 2), ("rg", 2), ("dq", 2), ("iq", 2), ("dk", 1), ("ik", 1))
    for k in range(nblk))


def _norm_proj_rope(x, g, w_t, tab, tm, pos_blocks):
    m, d = x.shape
    from_f32 = w_t.dtype != BF16
    tn = PROJ_TN if from_f32 else PROJ_TN_BF16
    kv_spec = pl.BlockSpec((DSA_KV_HEADS * tm, HEAD_DIM), lambda i, j, *_: (i, 0))
    kv_rows = jax.ShapeDtypeStruct((DSA_KV_HEADS * m, HEAD_DIM), F32)
    in_specs = [
        pl.BlockSpec((tm, d), lambda i, j, *_: (i, 0)),
        pl.BlockSpec((1, d), lambda i, j, *_: (0, 0)),
        None,
        pl.BlockSpec((tm, TAB_W), lambda i, j, *_: (i % pos_blocks, 0)),
    ]
    out_specs = [pl.BlockSpec((tm, tn), lambda i, j, *_: (i, j)), kv_spec, kv_spec]
    out_shape = [jax.ShapeDtypeStruct((m, N_PROJ), F32), kv_rows, kv_rows]
    scratch = [pltpu.VMEM((tm, d), BF16)]
    if from_f32:
        assert m == tm and all(r % ROW_ALIGN == 0 for r in PROJ_SRC_ROWS)
        src_rows = jnp.asarray([r // ROW_ALIGN for r in PROJ_SRC_ROWS], jnp.int32)
        in_specs[2] = pl.BlockSpec((pl.Element(tn), pl.Element(d)), lambda i, j, src: (src[j] * ROW_ALIGN, 0))
        out_specs.append(pl.BlockSpec((tn, d), lambda i, j, src: (j, 0)))
        out_shape.append(jax.ShapeDtypeStruct((N_PROJ, d), BF16))

        def body(src_ref, x_ref, g_ref, w_ref, tab_ref, o_ref, kr_ref, vr_ref, wq_ref, u_ref):
            _norm_proj_kernel(x_ref, g_ref, w_ref, tab_ref, o_ref, u_ref, kr_ref=kr_ref, vr_ref=vr_ref, wq_ref=wq_ref)

        grid_spec = pltpu.PrefetchScalarGridSpec(num_scalar_prefetch=1, grid=(1, N_PROJ // tn), in_specs=in_specs,
                                                 out_specs=out_specs, scratch_shapes=scratch)
        args = (src_rows, x, g, w_t, tab)
    else:
        in_specs[2] = pl.BlockSpec((tn, d), lambda i, j: (j, 0))

        def body(x_ref, g_ref, w_ref, tab_ref, o_ref, kr_ref, vr_ref, u_ref):
            _norm_proj_kernel(x_ref, g_ref, w_ref, tab_ref, o_ref, u_ref, kr_ref=kr_ref, vr_ref=vr_ref)

        grid_spec = pltpu.PrefetchScalarGridSpec(num_scalar_prefetch=0, grid=(m // tm, N_PROJ // tn),
                                                 in_specs=in_specs, out_specs=out_specs, scratch_shapes=scratch)
        args = (x, g, w_t, tab)
    return pl.pallas_call(
        body,
        grid_spec=grid_spec,
        out_shape=out_shape,
        compiler_params=_cparams("parallel", "arbitrary"),
        name="norm_proj_rope",
    )(*args)


@functools.lru_cache(maxsize=None)
def _rope_tables(start, count, period):
    pos = (start + np.arange(count) % period).astype(np.float64)[:, None]

    def cs(half, theta):
        inv = theta ** (-np.arange(half, dtype=np.float64) / half)
        ang = pos * inv[None, :]
        return np.cos(ang), np.sin(ang)

    ones = lambda n: np.ones((count, n))
    zeros = lambda n: np.zeros((count, n))
    c, s = cs(HEAD_DIM // 2, RET_THETA)
    ret_c = np.concatenate([c, c], 1)
    ret_s = np.concatenate([-s, s], 1)
    c, s = cs(DSA_ROT // 2, ROPE_THETA)
    dsa_c = np.concatenate([c, c, ones(LANES - DSA_ROT)], 1)
    dsa_s = np.concatenate([-s, s, zeros(LANES - DSA_ROT)], 1)
    c, s = cs(IDX_ROT // 2, ROPE_THETA)
    idx_c = np.tile(np.concatenate([c, c, ones(IDX_DH - IDX_ROT)], 1), (1, 2))
    idx_s = np.tile(np.concatenate([-s, s, zeros(IDX_DH - IDX_ROT)], 1), (1, 2))
    return np.concatenate([ret_c, ret_s, dsa_c, dsa_s, idx_c, idx_s], 1).astype(np.float32)


@functools.lru_cache(maxsize=None)
def _decay_tables(rows, n_seq, t_valid):
    log_g = np.log1p(-np.exp2(-5.0 - np.arange(RET_HEADS, dtype=np.float64)))
    r = np.arange(n_seq * rows)
    t = (r % rows).astype(np.float64)
    same = (r[:, None] // rows) == (r[None, :] // rows)
    diff = t[:, None] - t[None, :]
    intra = np.where(same[None] & (diff >= 0)[None], np.exp(log_g[:, None, None] * np.maximum(diff, 0.0)[None]), 0.0)
    ones = np.ones((1, 1, HEAD_DIM))
    qdec = np.exp(log_g[:, None] * (t[None, :] + 1.0))[:, :, None] * ones
    kdec = np.where(t[None, :] < t_valid, np.exp(log_g[:, None] * (t_valid - 1.0 - t[None, :])), 0.0)[:, :, None] * ones
    cdec = np.exp(log_g * t_valid)[:, None, None] * ones
    return tuple(a.astype(np.float32) for a in (intra, qdec, kdec, cdec))


def _groupnorm_gate(o, gate, gn):
    mu = jnp.mean(o, axis=-1, keepdims=True)
    d = o - mu
    var = jnp.mean(d * d, axis=-1, keepdims=True)
    n = d * lax.rsqrt(var + EPS) * gn
    return (gate * _sigmoid(gate) * n).astype(BF16)


def _ret_prompt_kernel(q_ref, k_ref, v_ref, g_ref, gn_ref, intra_ref, qdec_ref, kdec_ref, cdec_ref,
                       y_ref, st_ref, s_scr):
    c = pl.program_id(1)

    @pl.when(c == 0)
    def _():
        s_scr[...] = jnp.zeros_like(s_scr)

    for h in range(RET_HEADS):
        sl = slice(h * HEAD_DIM, (h + 1) * HEAD_DIM)
        k = k_ref[:, sl]
        qb = q_ref[:, sl].astype(BF16)
        kb = k.astype(BF16)
        vb = v_ref[:, sl].astype(BF16)
        s = _dot_nt(qb, kb) * intra_ref[h]
        state = s_scr[h]
        o = _dot(s.astype(BF16), vb) + _dot(qb, state.astype(BF16)) * qdec_ref[h]
        kd = (k * kdec_ref[h]).astype(BF16)
        s_scr[h] = state * cdec_ref[h] + _dot_tn(kd, vb)
        y_ref[:, sl] = _groupnorm_gate(o, g_ref[:, sl], gn_ref[:, sl])

    @pl.when(c == pl.num_programs(1) - 1)
    def _():
        st_ref[0] = s_scr[...]


def _ret_prompt(proj, gn, batch, seq, chunk):
    nc = seq // chunk
    intra, qdec, kdec, cdec = _decay_tables(chunk, 1, chunk)
    w = RET_HEADS * HEAD_DIM
    col = lambda off: (lambda b, c: (b * nc + c, off // w))
    const3 = lambda b, c: (0, 0, 0)
    return pl.pallas_call(
        _ret_prompt_kernel,
        grid=(batch, nc),
        in_specs=[
            pl.BlockSpec((chunk, w), col(COL_RQ)),
            pl.BlockSpec((chunk, w), col(COL_RK)),
            pl.BlockSpec((chunk, w), col(COL_RV)),
            pl.BlockSpec((chunk, w), col(COL_RG)),
            pl.BlockSpec((1, w), lambda b, c: (0, 0)),
            pl.BlockSpec((RET_HEADS, chunk, chunk), const3),
            pl.BlockSpec((RET_HEADS, chunk, HEAD_DIM), const3),
            pl.BlockSpec((RET_HEADS, chunk, HEAD_DIM), const3),
            pl.BlockSpec((RET_HEADS, 1, HEAD_DIM), const3),
        ],
        out_specs=[
            pl.BlockSpec((chunk, w), lambda b, c: (b * nc + c, 0)),
            pl.BlockSpec((1, RET_HEADS, HEAD_DIM, HEAD_DIM), lambda b, c: (b, 0, 0, 0)),
        ],
        out_shape=[
            jax.ShapeDtypeStruct((batch * seq, w), BF16),
            jax.ShapeDtypeStruct((batch, RET_HEADS, HEAD_DIM, HEAD_DIM), F32),
        ],
        scratch_shapes=[pltpu.VMEM((RET_HEADS, HEAD_DIM, HEAD_DIM), F32)],
        compiler_params=_cparams("parallel", "arbitrary"),
        name="ret_prompt",
    )(proj, proj, proj, proj, gn, intra, qdec, kdec, cdec)


def _ret_sample_kernel(q_ref, k_ref, v_ref, g_ref, gn_ref, st_ref, intra_ref, qdec_ref, kdec_ref, cdec_ref,
                       y_ref, so_ref, *, n_batch, rows):
    k = k_ref[...]
    qb = q_ref[...].astype(BF16)
    kb = k.astype(BF16)
    vb = v_ref[...].astype(BF16)
    s = _dot_nt(qb, kb) * intra_ref[0]
    kd = k * kdec_ref[0]
    row_batch = lax.broadcasted_iota(jnp.int32, kd.shape, 0) // rows
    inter = []
    for b in range(n_batch):
        state = st_ref[b, 0]
        inter.append(_dot(qb[b * rows:(b + 1) * rows], state.astype(BF16)))
        kd_b = jnp.where(row_batch == b, kd, 0.0).astype(BF16)
        so_ref[b, 0] = state * cdec_ref[0] + _dot_tn(kd_b, vb)
    o = _dot(s.astype(BF16), vb) + jnp.concatenate(inter, axis=0) * qdec_ref[0]
    y_ref[...] = _groupnorm_gate(o, g_ref[...], gn_ref[...])


def _ret_sample(proj, gn, state, n_batch, rows, t_valid):
    m = n_batch * rows
    intra, qdec, kdec, cdec = _decay_tables(rows, n_batch, t_valid)
    col = lambda off: (lambda h: (0, off // HEAD_DIM + h))
    per_head = lambda h: (h, 0, 0)
    return pl.pallas_call(
        functools.partial(_ret_sample_kernel, n_batch=n_batch, rows=rows),
        grid=(RET_HEADS,),
        in_specs=[
            pl.BlockSpec((m, HEAD_DIM), col(COL_RQ)),
            pl.BlockSpec((m, HEAD_DIM), col(COL_RK)),
            pl.BlockSpec((m, HEAD_DIM), col(COL_RV)),
            pl.BlockSpec((m, HEAD_DIM), col(COL_RG)),
            pl.BlockSpec((1, HEAD_DIM), lambda h: (0, h)),
            pl.BlockSpec((n_batch, 1, HEAD_DIM, HEAD_DIM), lambda h: (0, h, 0, 0)),
            pl.BlockSpec((1, m, m), per_head),
            pl.BlockSpec((1, m, HEAD_DIM), per_head),
            pl.BlockSpec((1, m, HEAD_DIM), per_head),
            pl.BlockSpec((1, 1, HEAD_DIM), per_head),
        ],
        out_specs=[
            pl.BlockSpec((m, HEAD_DIM), lambda h: (0, h)),
            pl.BlockSpec((n_batch, 1, HEAD_DIM, HEAD_DIM), lambda h: (0, h, 0, 0)),
        ],
        out_shape=[
            jax.ShapeDtypeStruct((m, RET_HEADS * HEAD_DIM), BF16),
            jax.ShapeDtypeStruct(state.shape, F32),
        ],
        compiler_params=_cparams("parallel"),
        name="ret_sample",
    )(proj, proj, proj, proj, gn, state, intra, qdec, kdec, cdec)


def _kth_largest(count_ge, next_above, smin, smax, n_adm, topk):
    few = n_adm < topk
    lo = jnp.where(few, -jnp.inf, smin)
    c_lo = jnp.where(few, float(topk), n_adm)
    hi = jnp.where(few, -jnp.inf, smax + jnp.maximum(jnp.abs(smax) * 1e-6, 1e-30))

    def bisect(_, state):
        lo, hi, c_lo = state
        mid = 0.5 * lo + 0.5 * hi
        c = count_ge(mid)
        take = c >= topk
        return jnp.where(take, mid, lo), jnp.where(take, hi, mid), jnp.where(take, c, c_lo)

    lo, hi, c_lo = lax.fori_loop(0, BISECT_STEPS, bisect, (lo, hi, c_lo))
    c_lo = jnp.where(few, float(topk), c_lo)

    def unfinished(state):
        return jnp.max(state[2]) > 0.0

    def step_up(state):
        lo, c_lo, active = state
        nxt, n_eq = next_above(lo)
        c_nxt = c_lo - n_eq
        move = active * jnp.where(c_nxt >= topk, 1.0, 0.0)
        lo = jnp.where(move > 0.0, nxt, lo)
        c_lo = jnp.where(move > 0.0, c_nxt, c_lo)
        return lo, c_lo, move * jnp.where(c_nxt > topk, 1.0, 0.0)

    state = step_up(step_up((lo, c_lo, jnp.where(c_lo > topk, 1.0, 0.0))))
    return lax.while_loop(unfinished, step_up, state)[0]


def _dsa_prompt_kernel(dq_ref, iq_ref, iwq_ref, kall_ref, vall_ref, ikall_ref, o_ref,
                       kbf, vtb, ika, ikb, keys, z_a, z_b, zmax_a, zmax_b, m_scr, acc_scr, *, tq, topk):
    z_scr, zmax_scr = (z_a, z_b), (zmax_a, zmax_b)
    i = pl.program_id(1)
    n_chunks = kbf.shape[0]
    scale_log2e = HEAD_DIM ** -0.5 * math.log2(math.e)

    @pl.when(i == 0)
    def _():
        lane = lax.broadcasted_iota(jnp.int32, (tq, LANES), 1)
        for c in range(n_chunks):
            rows = slice(c * tq, (c + 1) * tq)
            kbf[c] = kall_ref[rows, :].astype(BF16)
            for kvh in range(DSA_KV_HEADS):
                sl = slice(kvh * HEAD_DIM, (kvh + 1) * HEAD_DIM)
                vtb[c, kvh, :HEAD_DIM, :] = vall_ref[rows, sl].T.astype(BF16)
                vtb[c, kvh, HEAD_DIM:, :] = jnp.ones((ROW_ALIGN, tq), BF16)
            a = ikall_ref[rows, :]
            ika[c] = jnp.where(lane < IDX_DH, a, 0.0).astype(BF16)
            ikb[c] = jnp.where(lane >= IDX_DH, pltpu.roll(a, IDX_DH, 1), 0.0).astype(BF16)

    w_t = iwq_ref[...].T
    iqb = iq_ref[...].astype(BF16)
    qb = (dq_ref[...] * scale_log2e).astype(BF16)
    t_col = i * tq + lax.broadcasted_iota(jnp.int32, (tq, tq), 1)
    s_row0 = lax.broadcasted_iota(jnp.int32, (tq, tq), 0)

    fold = lambda x: x.reshape(tq // SUBLANES, SUBLANES, tq)

    def score_chunk(c, carry):
        smax, smin = carry
        src = jnp.minimum(c, n_chunks - 1)
        a = ika[src]
        b = ikb[src]
        acc = jnp.zeros((tq, tq), F32)
        for p in range(IDX_HEADS // 2):
            pair = iqb[:, p * LANES:(p + 1) * LANES]
            w0 = w_t[IDX_DH + 2 * p:IDX_DH + 2 * p + 1, :]
            w1 = w_t[IDX_DH + 2 * p + 1:IDX_DH + 2 * p + 2, :]
            acc = acc + jnp.maximum(_dot_nt(a, pair), 0.0) * w0
            acc = acc + jnp.maximum(_dot_nt(b, pair), 0.0) * w1
        admissible = s_row0 + c * tq <= t_col
        keys[c] = jnp.where(admissible, acc, -jnp.inf)
        smax = jnp.maximum(smax, fold(jnp.where(admissible, acc, -jnp.inf)).max(axis=0))
        smin = jnp.minimum(smin, fold(jnp.where(admissible, acc, jnp.inf)).min(axis=0))
        return smax, smin

    n_pairs = (i + 2) // 2

    def score_pair(j, carry):
        return score_chunk(2 * j + 1, score_chunk(2 * j, carry))

    smax, smin = lax.fori_loop(0, n_pairs, score_pair, (jnp.full((SUBLANES, tq), -jnp.inf, F32),
                                                        jnp.full((SUBLANES, tq), jnp.inf, F32)))
    smax = smax.max(axis=0, keepdims=True)
    smin = smin.min(axis=0, keepdims=True)

    def count_ge(t):
        def pair(j, cnt):
            for c in (2 * j, 2 * j + 1):
                cnt = cnt + fold(jnp.where(keys[c] >= t, 1.0, 0.0)).sum(axis=0)
            return cnt

        return lax.fori_loop(0, n_pairs, pair, jnp.zeros((SUBLANES, tq), F32)).sum(axis=0, keepdims=True)

    def next_above(lo):
        def pair(j, carry):
            nxt, n_eq = carry
            for c in (2 * j, 2 * j + 1):
                k = keys[c]
                nxt = jnp.minimum(nxt, fold(jnp.where(k > lo, k, jnp.inf)).min(axis=0))
                n_eq = n_eq + fold(jnp.where(k == lo, 1.0, 0.0)).sum(axis=0)
            return nxt, n_eq

        nxt, n_eq = lax.fori_loop(0, n_pairs, pair, (jnp.full((SUBLANES, tq), jnp.inf, F32),
                                                     jnp.zeros((SUBLANES, tq), F32)))
        return nxt.min(axis=0, keepdims=True), n_eq.sum(axis=0, keepdims=True)

    thr = _kth_largest(count_ge, next_above, smin, smax, (t_col[:1, :] + 1).astype(F32), topk)

    m_scr[...] = jnp.full_like(m_scr, NEG)
    acc_scr[...] = jnp.zeros_like(acc_scr)

    kv_of = lambda h: h // (DSA_HEADS // DSA_KV_HEADS)

    def logits(c, slot):
        bias = jnp.where(s_row0 + c * tq <= t_col, jnp.where(keys[c] >= thr, 0.0, NEG), NEG)
        kc = kbf[c]
        for h in range(DSA_HEADS):
            ksl = slice(kv_of(h) * HEAD_DIM, (kv_of(h) + 1) * HEAD_DIM)
            z = _dot_nt(kc[:, ksl], qb[:, h * HEAD_DIM:(h + 1) * HEAD_DIM]) + bias
            z_scr[slot][h] = z
            zmax_scr[slot][h] = z.max(axis=0, keepdims=True)

    def update(c, slot):
        for h in range(DSA_HEADS):
            m_old = m_scr[h]
            m_new = jnp.maximum(m_old, zmax_scr[slot][h])
            p = jnp.exp2(z_scr[slot][h] - m_new)
            acc_scr[h] = acc_scr[h] * jnp.exp2(m_old - m_new) + _dot(vtb[c, kv_of(h)], p.astype(BF16))
            m_scr[h] = m_new

    n = i + 1
    logits(0, 0)

    def two_chunks(k, carry):
        update(2 * k, 0)
        logits(2 * k + 1, 1)
        update(2 * k + 1, 1)
        logits(2 * k + 2, 0)
        return carry

    lax.fori_loop(0, (n - 1) // 2, two_chunks, 0)

    @pl.when(n % 2 == 1)
    def _():
        update(n - 1, 0)

    @pl.when(n % 2 == 0)
    def _():
        update(n - 2, 0)
        logits(n - 1, 1)
        update(n - 1, 1)

    for h in range(DSA_HEADS):
        o = acc_scr[h, :HEAD_DIM, :] * (1.0 / acc_scr[h, HEAD_DIM:HEAD_DIM + 1, :])
        o_ref[:, h * HEAD_DIM:(h + 1) * HEAD_DIM] = o.T.astype(BF16)


def _dsa_prompt(proj, batch, seq, tq):
    nq = seq // tq
    topk = min(TOPK_MAX, seq // 4)
    qw = DSA_HEADS * HEAD_DIM
    kvw = DSA_KV_HEADS * HEAD_DIM
    return pl.pallas_call(
        functools.partial(_dsa_prompt_kernel, tq=tq, topk=topk),
        grid=(batch, nq),
        in_specs=[
            pl.BlockSpec((tq, qw), lambda b, i: (b * nq + i, COL_DQ // qw)),
            pl.BlockSpec((tq, qw), lambda b, i: (b * nq + i, COL_IQ // qw)),
            pl.BlockSpec((tq, LANES), lambda b, i: (b * nq + i, COL_IK // LANES)),
            pl.BlockSpec((seq, kvw), lambda b, i: (b, COL_DK // kvw)),
            pl.BlockSpec((seq, kvw), lambda b, i: (b, COL_DV // kvw)),
            pl.BlockSpec((seq, LANES), lambda b, i: (b, COL_IK // LANES)),
        ],
        out_specs=pl.BlockSpec((tq, qw), lambda b, i: (b * nq + i, 0)),
        out_shape=jax.ShapeDtypeStruct((batch * seq, qw), BF16),
        scratch_shapes=[
            pltpu.VMEM((nq, tq, kvw), BF16),
            pltpu.VMEM((nq, DSA_KV_HEADS, HEAD_DIM + ROW_ALIGN, tq), BF16),
            pltpu.VMEM((nq, tq, LANES), BF16),
            pltpu.VMEM((nq, tq, LANES), BF16),
            pltpu.VMEM((nq + 1, tq, tq), F32),
            pltpu.VMEM((DSA_HEADS, tq, tq), F32),
            pltpu.VMEM((DSA_HEADS, tq, tq), F32),
            pltpu.VMEM((DSA_HEADS, 1, tq), F32),
            pltpu.VMEM((DSA_HEADS, 1, tq), F32),
            pltpu.VMEM((DSA_HEADS, 1, tq), F32),
            pltpu.VMEM((DSA_HEADS, HEAD_DIM + ROW_ALIGN, tq), F32),
        ],
        compiler_params=_cparams("parallel", "arbitrary"),
        name="dsa_prompt",
    )(proj, proj, proj, proj, proj, proj)


def _idx_sample_kernel(pt_ref, iq_ref, iw_ref, iknew_ref, *rest, pages, t_valid):
    page_refs = rest[:pages]
    sp_ref, sn_ref = rest[pages:]
    iqb = iq_ref[0, :t_valid * IDX_HEADS].astype(BF16)
    w = iw_ref[0, :t_valid * IDX_HEADS]
    rows = sp_ref.shape[1]

    def scores(ik_t):
        d = jnp.maximum(_dot(iqb, ik_t.astype(BF16)), 0.0) * w
        return d.reshape(t_valid, IDX_HEADS, LANES).sum(axis=1)

    sp_ref[0, t_valid:, :] = jnp.zeros((rows - t_valid, pages * PAGE_SIZE), F32)
    for p in range(pages):
        sp_ref[0, :t_valid, p * PAGE_SIZE:(p + 1) * PAGE_SIZE] = scores(page_refs[p][0])
    t = lax.broadcasted_iota(jnp.int32, (t_valid, LANES), 0)
    j = lax.broadcasted_iota(jnp.int32, (t_valid, LANES), 1)
    sn_ref[0, t_valid:, :] = jnp.zeros((rows - t_valid, LANES), F32)
    sn_ref[0, :t_valid, :] = jnp.where(j <= t, scores(iknew_ref[0]), -jnp.inf)


def _idx_sample(page_table, iq, iw, ik_new, pool_ik, pages, t_valid):
    n_batch, n_pages = page_table.shape
    rows16 = iq.shape[1]
    rows = rows16 // IDX_HEADS
    past = n_pages * PAGE_SIZE
    page_specs = [pl.BlockSpec((1, IDX_DH, PAGE_SIZE), functools.partial(
        lambda b, g, pt, k: (pt[b, g * pages + k], 0, 0), k=k)) for k in range(pages)]
    grid_spec = pltpu.PrefetchScalarGridSpec(
        num_scalar_prefetch=1,
        grid=(n_batch, n_pages // pages),
        in_specs=[
            pl.BlockSpec((1, rows16, IDX_DH), lambda b, g, pt: (b, 0, 0)),
            pl.BlockSpec((1, rows16, LANES), lambda b, g, pt: (b, 0, 0)),
            pl.BlockSpec((1, IDX_DH, PAGE_SIZE), lambda b, g, pt: (b, 0, 0)),
        ] + page_specs,
        out_specs=[
            pl.BlockSpec((1, rows, pages * PAGE_SIZE), lambda b, g, pt: (b, 0, g)),
            pl.BlockSpec((1, rows, LANES), lambda b, g, pt: (b, 0, 0)),
        ],
    )
    return pl.pallas_call(
        functools.partial(_idx_sample_kernel, pages=pages, t_valid=t_valid),
        grid_spec=grid_spec,
        out_shape=[
            jax.ShapeDtypeStruct((n_batch, rows, past), F32),
            jax.ShapeDtypeStruct((n_batch, rows, LANES), F32),
        ],
        compiler_params=_cparams("parallel", "arbitrary"),
        name="idx_sample",
    )(page_table, iq, iw, ik_new, *([pool_ik] * pages))


def _sel_sample_kernel(sp_ref, sn_ref, bp_ref, bn_ref, *, topk):
    n_batch, rows, past = sp_ref.shape
    sp = sp_ref[...].reshape(n_batch * rows, past)
    sn = sn_ref[...].reshape(n_batch * rows, LANES)
    t = lax.broadcasted_iota(jnp.int32, (n_batch * rows, LANES), 0) % rows
    j = lax.broadcasted_iota(jnp.int32, (n_batch * rows, LANES), 1)
    admissible_new = j <= t
    row_sum = lambda a: a.sum(axis=1, keepdims=True)
    row_min = lambda a: a.min(axis=1, keepdims=True)

    def count_ge(thr):
        return row_sum(jnp.where(sp >= thr, 1.0, 0.0)) + row_sum(jnp.where(sn >= thr, 1.0, 0.0))

    def next_above(lo):
        nxt = jnp.minimum(row_min(jnp.where(sp > lo, sp, jnp.inf)), row_min(jnp.where(sn > lo, sn, jnp.inf)))
        return nxt, row_sum(jnp.where(sp == lo, 1.0, 0.0)) + row_sum(jnp.where(sn == lo, 1.0, 0.0))

    smin = jnp.minimum(row_min(sp), row_min(jnp.where(admissible_new, sn, jnp.inf)))
    smax = jnp.maximum(sp.max(axis=1, keepdims=True), sn.max(axis=1, keepdims=True))
    n_adm = float(past) + row_sum(jnp.where(admissible_new, 1.0, 0.0))
    thr = _kth_largest(count_ge, next_above, smin, smax, n_adm, topk)
    bp_ref[...] = jnp.where(sp >= thr, 0.0, NEG).reshape(n_batch, rows, past)
    bn_ref[...] = jnp.where(admissible_new, jnp.where(sn >= thr, 0.0, NEG), NEG).reshape(n_batch, rows, LANES)


def _sel_sample(scores_past, scores_new, topk):
    full = lambda a: pl.BlockSpec(a.shape, lambda: (0,) * a.ndim)
    return pl.pallas_call(
        functools.partial(_sel_sample_kernel, topk=topk),
        in_specs=[full(scores_past), full(scores_new)],
        out_specs=[full(scores_past), full(scores_new)],
        out_shape=[jax.ShapeDtypeStruct(scores_past.shape, F32), jax.ShapeDtypeStruct(scores_new.shape, F32)],
        compiler_params=pltpu.CompilerParams(vmem_limit_bytes=VMEM_LIMIT),
        name="sel_sample",
    )(scores_past, scores_new)


def _dsa_sample_kernel(pt_ref, bp_ref, bn_ref, q_ref, knew_ref, vnew_ref, *rest, pages, steps):
    k_refs = rest[:pages]
    v_refs = rest[pages:2 * pages]
    o_ref, m_scr, l_scr, acc_scr = rest[2 * pages:]
    g = pl.program_id(0) % steps
    scale = HEAD_DIM ** -0.5
    group = DSA_HEADS // DSA_KV_HEADS

    @pl.when(g == 0)
    def _():
        m_scr[...] = jnp.full_like(m_scr, NEG)
        l_scr[...] = jnp.zeros_like(l_scr)
        acc_scr[...] = jnp.zeros_like(acc_scr)

    def head_rows(ref, kvh):
        return ref[0, pl.ds(kvh, PAGE_SIZE, stride=DSA_KV_HEADS), :].astype(BF16)

    def attend(k_pages, v_pages, bias):
        bias_g = jnp.concatenate([bias] * group, axis=0)
        kv_heads = range(DSA_KV_HEADS)
        qb = [q_ref[0, kvh].astype(BF16) for kvh in kv_heads]
        raw = [jnp.concatenate([_dot_nt(qb[kvh], head_rows(k, kvh)) for k in k_pages], axis=1) for kvh in kv_heads]
        pb, alpha = [], []
        for kvh in kv_heads:
            z = jnp.where(bias_g < 0.0, NEG, raw[kvh])
            m_old = m_scr[kvh]
            m_new = jnp.maximum(m_old, z.max(axis=1, keepdims=True))
            alpha.append(jnp.exp((m_old - m_new) * scale))
            p = jnp.exp((z - m_new) * scale)
            l_scr[kvh] = l_scr[kvh] * alpha[kvh] + p.sum(axis=1, keepdims=True)
            m_scr[kvh] = m_new
            pb.append(p.astype(BF16))
        for kvh in kv_heads:
            pv = _dot(pb[kvh][:, :PAGE_SIZE], head_rows(v_pages[0], kvh))
            for n in range(1, len(v_pages)):
                pv = pv + _dot(pb[kvh][:, n * PAGE_SIZE:(n + 1) * PAGE_SIZE], head_rows(v_pages[n], kvh))
            acc_scr[kvh] = acc_scr[kvh] * alpha[kvh] + pv

    attend(k_refs, v_refs, bp_ref[0])

    @pl.when(g == steps - 1)
    def _():
        attend([knew_ref], [vnew_ref], bn_ref[0])
        for kvh in range(DSA_KV_HEADS):
            o_ref[0, kvh] = (acc_scr[kvh] * (1.0 / l_scr[kvh])).astype(BF16)


def _dsa_sample(page_table, bias_past, bias_new, q, k_new, v_new, pool_k, pool_v, steps):
    n_batch, n_pages = page_table.shape
    pages = n_pages // steps
    rows = bias_past.shape[1]
    page_rows = DSA_KV_HEADS * PAGE_SIZE
    grows = q.shape[2]
    page_spec = lambda k: pl.BlockSpec((1, page_rows, HEAD_DIM), functools.partial(
        lambda i, pt, k: (pt[i // steps, (i % steps) * pages + k], 0, 0), k=k))
    per_batch3 = lambda i, pt: (i // steps, 0, 0)
    per_batch4 = lambda i, pt: (i // steps, 0, 0, 0)
    in_specs = [
        pl.BlockSpec((1, rows, pages * PAGE_SIZE), lambda i, pt: (i // steps, 0, i % steps)),
        pl.BlockSpec((1, rows, LANES), per_batch3),
        pl.BlockSpec((1, DSA_KV_HEADS, grows, HEAD_DIM), per_batch4),
        pl.BlockSpec((1, page_rows, HEAD_DIM), per_batch3),
        pl.BlockSpec((1, page_rows, HEAD_DIM), per_batch3),
    ] + [page_spec(k) for k in range(pages)] * 2
    grid_spec = pltpu.PrefetchScalarGridSpec(
        num_scalar_prefetch=1,
        grid=(n_batch * steps,),
        in_specs=in_specs,
        out_specs=pl.BlockSpec((1, DSA_KV_HEADS, grows, HEAD_DIM), per_batch4),
        scratch_shapes=[
            pltpu.VMEM((DSA_KV_HEADS, grows, 1), F32),
            pltpu.VMEM((DSA_KV_HEADS, grows, 1), F32),
            pltpu.VMEM((DSA_KV_HEADS, grows, HEAD_DIM), F32),
        ],
    )
    return pl.pallas_call(
        functools.partial(_dsa_sample_kernel, pages=pages, steps=steps),
        grid_spec=grid_spec,
        out_shape=jax.ShapeDtypeStruct((n_batch, DSA_KV_HEADS, grows, HEAD_DIM), BF16),
        compiler_params=_cparams("arbitrary"),
        name="dsa_sample",
    )(page_table, bias_past, bias_new, q, k_new, v_new, *([pool_k] * pages), *([pool_v] * pages))


def _merge_mem_kernel(x_ref, ry_ref, do_ref, ga_ref, gb_ref, wr_ref, wd_ref, wo_ref,
                      g_ref, wq_ref, mk_ref, mv_ref, wmo_ref, o_ref, *, rows_per_batch):
    ya = _dot(ry_ref[...], wr_ref[...])
    yb = _dot(do_ref[...], wd_ref[...])
    merged = _sigmoid(ga_ref[...]) * ya + _sigmoid(gb_ref[...]) * yb
    h = x_ref[...] + _dot(merged.astype(BF16), wo_ref[...])
    qm = _dot(_rmsnorm_bf16(h, g_ref[...]), wq_ref[...])
    tm = h.shape[0]
    nk = mk_ref.shape[0]
    scale = HEAD_DIM ** -0.5
    if rows_per_batch is not None:
        rb = lax.broadcasted_iota(jnp.int32, (tm, nk), 0) // rows_per_batch
        kb = lax.broadcasted_iota(jnp.int32, (tm, nk), 1) // MEM_LEN
        same = rb == kb
    outs = []
    for hd in range(MEM_HEADS):
        sl = slice(hd * HEAD_DIM, (hd + 1) * HEAD_DIM)
        z = _dot_nt(qm[:, sl].astype(BF16), mk_ref[:, sl].astype(BF16)) * scale
        if rows_per_batch is not None:
            z = jnp.where(same, z, NEG)
        p = jnp.exp(z - z.max(axis=-1, keepdims=True))
        pv = _dot(p.astype(BF16), mv_ref[:, sl].astype(BF16))
        outs.append(pv * (1.0 / p.sum(axis=-1, keepdims=True)))
    om = jnp.concatenate(outs, axis=1).astype(BF16)
    o_ref[...] = h + _dot(om, wmo_ref[...])


def _merge_mem(x, ret_y, dsa_o, proj, w_ret_out, w_dsa_out, w_o, g, w_mq, mk, mv, w_mo, *,
               tm, batch_tiles, mk_col, mv_col, nk, rows_per_batch):
    m, d = x.shape
    w = ret_y.shape[1]
    const = lambda i: (0, 0)
    resident = dict(pipeline_mode=pl.Buffered(1))
    if batch_tiles:
        kmap = lambda col: (lambda i: (i // batch_tiles, col))
    else:
        kmap = lambda col: (lambda i: (0, col))
    return pl.pallas_call(
        functools.partial(_merge_mem_kernel, rows_per_batch=rows_per_batch),
        grid=(m // tm,),
        in_specs=[
            pl.BlockSpec((tm, d), lambda i: (i, 0)),
            pl.BlockSpec((tm, w), lambda i: (i, 0)),
            pl.BlockSpec((tm, w), lambda i: (i, 0)),
            pl.BlockSpec((tm, d), lambda i: (i, COL_GA // d)),
            pl.BlockSpec((tm, d), lambda i: (i, COL_GB // d)),
            pl.BlockSpec((w, d), const, **resident),
            pl.BlockSpec((w, d), const, **resident),
            pl.BlockSpec((d, d), const, **resident),
            pl.BlockSpec((1, d), const),
            pl.BlockSpec((d, MEM_W), const, **resident),
            pl.BlockSpec((nk, MEM_W), kmap(mk_col)),
            pl.BlockSpec((nk, MEM_W), kmap(mv_col)),
            pl.BlockSpec((MEM_W, d), const, **resident),
        ],
        out_specs=pl.BlockSpec((tm, d), lambda i: (i, 0)),
        out_shape=jax.ShapeDtypeStruct((m, d), F32),
        compiler_params=_cparams("parallel"),
        name="merge_mem",
    )(x, ret_y, dsa_o, proj, proj, w_ret_out, w_dsa_out, w_o, g, w_mq, mk, mv, w_mo)


def _mlp_kernel(h_ref, g_ref, wu_ref, wd_ref, gf_ref, o_ref, *rest):
    u_ref = rest[-1]
    j = pl.program_id(1)

    @pl.when(j == 0)
    def _():
        u_ref[...] = _rmsnorm_bf16(h_ref[...], g_ref[...])
        o_ref[...] = h_ref[...]

    wu, wd = wu_ref[...], wd_ref[...]
    if len(rest) == 3:
        wu, wd = wu.astype(BF16), wd.astype(BF16)
        rest[0][...] = wu
        rest[1][...] = wd
    a = jnp.maximum(_dot(u_ref[...], wu), 0.0)
    o_ref[...] += _dot((a * a).astype(BF16), wd)

    @pl.when(j == pl.num_programs(1) - 1)
    def _():
        y = o_ref[...]
        ms = jnp.mean(y * y, axis=-1, keepdims=True)
        o_ref[...] = (y * lax.rsqrt(ms + EPS)) * gf_ref[...]


def _mlp(h, g, w_up, w_down, g_final, tm, tf):
    m, d = h.shape
    ff = w_up.shape[1]
    emit = w_up.dtype != BF16
    assert not emit or m == tm, "weight copies are written once, by a call with one row tile"
    wu_spec = pl.BlockSpec((d, tf), lambda i, j: (0, j))
    wd_spec = pl.BlockSpec((tf, d), lambda i, j: (j, 0))
    out_specs = [pl.BlockSpec((tm, d), lambda i, j: (i, 0))]
    out_shape = [jax.ShapeDtypeStruct((m, d), F32)]
    if emit:
        out_specs += [wu_spec, wd_spec]
        out_shape += [jax.ShapeDtypeStruct(w_up.shape, BF16), jax.ShapeDtypeStruct(w_down.shape, BF16)]
    out = pl.pallas_call(
        _mlp_kernel,
        grid=(m // tm, ff // tf),
        in_specs=[
            pl.BlockSpec((tm, d), lambda i, j: (i, 0)),
            pl.BlockSpec((1, d), lambda i, j: (0, 0)),
            wu_spec,
            wd_spec,
            pl.BlockSpec((1, d), lambda i, j: (0, 0)),
        ],
        out_specs=out_specs,
        out_shape=out_shape,
        scratch_shapes=[pltpu.VMEM((tm, d), BF16)],
        compiler_params=_cparams("parallel", "arbitrary"),
        name="mlp_final",
    )(h, g, w_up, w_down, g_final)
    return out if emit else out[0]


def _pick_tile(n, pref):
    t = min(n, pref)
    while n % t:
        t //= 2
    return t


def kernel(x_prompt, x_sample, mem_prompt, cache_k, cache_v, cache_idx_k, state_ret, cache_mem_k, cache_mem_v,
           page_table, g_mix, w_in, gn_ret, w_ret_out, w_dsa_out, w_o, g_mem, g_memkv, w_mq, w_mk, w_mv, w_mo,
           g_mlp, w_up, w_down, g_final):
    assert w_in.shape[0] == 1, "one layer"
    batch, seq, d = x_prompt.shape
    n_dec, t_dec, _ = x_sample.shape
    n_pages = page_table.shape[1]
    past = n_pages * PAGE_SIZE
    mem_len = mem_prompt.shape[1]
    assert mem_len == MEM_LEN and t_dec <= SUBLANES
    row = lambda v: v.reshape(1, -1)

    w_ret_out_b, w_dsa_out_b, w_o_b = (w[0].astype(BF16) for w in (w_ret_out, w_dsa_out, w_o))
    w_mq_b, w_mo_b = w_mq[0].astype(BF16), w_mo[0].astype(BF16)
    w_mkv_b = jnp.concatenate([w_mk[0], w_mv[0]], axis=1).astype(BF16)
    g_mix_r, gn_r, g_mem_r, g_memkv_r, g_mlp_r, g_final_r = (
        row(v) for v in (g_mix[0], gn_ret[0], g_mem[0], g_memkv[0], g_mlp[0], g_final))

    rows = SUBLANES
    m_s = n_dec * rows
    xs = jnp.pad(x_sample, ((0, 0), (0, rows - t_dec), (0, 0))).reshape(m_s, d)
    proj_s, k_rows_s, v_rows_s, w_proj_b = _norm_proj_rope(xs, g_mix_r, jnp.swapaxes(w_in[0], 0, 1),
                                                           _rope_tables(past, m_s, rows), m_s, 1)

    m_p = batch * seq
    xp = x_prompt.reshape(m_p, d)
    tm_proj = _pick_tile(seq, 1024)
    proj_p, k_rows_p, v_rows_p = _norm_proj_rope(xp, g_mix_r, w_proj_b, _rope_tables(0, seq, seq), tm_proj,
                                                 seq // tm_proj)
    ret_y_p, ret_state_p = _ret_prompt(proj_p, gn_r, batch, seq, _pick_tile(seq, 256))
    dsa_o_p = _dsa_prompt(proj_p, batch, seq, _pick_tile(seq, 256))
    kv_p = _norm_proj(mem_prompt.reshape(batch * mem_len, d), g_memkv_r, w_mkv_b, mem_len, PROJ_TN)
    tm_t = _pick_tile(seq, 256)
    h_p = _merge_mem(xp, ret_y_p, dsa_o_p, proj_p, w_ret_out_b, w_dsa_out_b, w_o_b, g_mem_r, w_mq_b, kv_p, kv_p,
                     w_mo_b, tm=tm_t, batch_tiles=seq // tm_t, mk_col=0, mv_col=1, nk=mem_len, rows_per_batch=None)

    ret_y_s, ret_state_s = _ret_sample(proj_s, gn_r, state_ret[0], n_dec, rows, t_dec)

    proj_s3 = proj_s.reshape(n_dec, rows, N_PROJ)
    iq_s = proj_s3[:, :, COL_IQ:COL_IQ + IDX_HEADS * IDX_DH].reshape(n_dec, rows * IDX_HEADS, IDX_DH)
    iw_s = proj_s3[:, :, COL_IK + IDX_DH:COL_IK + IDX_DH + IDX_HEADS].reshape(n_dec, rows * IDX_HEADS, 1)
    iw_s = jnp.broadcast_to(iw_s, (n_dec, rows * IDX_HEADS, LANES))
    kvw = DSA_KV_HEADS * HEAD_DIM
    page_rows = DSA_KV_HEADS * PAGE_SIZE
    pad_keys = lambda a: jnp.pad(a, ((0, 0), (0, PAGE_SIZE - rows), (0, 0)))
    ik_new_t = jnp.swapaxes(pad_keys(proj_s3[:, :, COL_IK:COL_IK + IDX_DH]), 1, 2)
    pool_ik_t = jnp.swapaxes(cache_idx_k, 2, 3).reshape(-1, IDX_DH, PAGE_SIZE)
    new_page = lambda a: jnp.pad(a.reshape(n_dec, DSA_KV_HEADS * rows, HEAD_DIM),
                                 ((0, 0), (0, page_rows - DSA_KV_HEADS * rows), (0, 0)))
    k_new, v_new = new_page(k_rows_s), new_page(v_rows_s)
    scores_past, scores_new = _idx_sample(page_table, iq_s, iw_s, ik_new_t, pool_ik_t,
                                          _pick_tile(n_pages, MAX_PAGES_PER_STEP), t_dec)
    group = DSA_HEADS // DSA_KV_HEADS
    dq_s = proj_s3[:, :, COL_DQ:COL_DQ + DSA_HEADS * HEAD_DIM].reshape(n_dec, rows, DSA_KV_HEADS, group, HEAD_DIM)
    dq_s = dq_s.transpose(0, 2, 3, 1, 4).reshape(n_dec, DSA_KV_HEADS, group * rows, HEAD_DIM)
    bias_past, bias_new = _sel_sample(scores_past, scores_new, min(TOPK_MAX, (past + t_dec) // 4))
    dsa_o_s = _dsa_sample(page_table, bias_past, bias_new, dq_s, k_new, v_new,
                          cache_k.reshape(-1, page_rows, HEAD_DIM), cache_v.reshape(-1, page_rows, HEAD_DIM),
                          n_pages // _pick_tile(n_pages, MAX_PAGES_PER_STEP))
    dsa_o_s = dsa_o_s.reshape(n_dec, DSA_KV_HEADS, group, rows, HEAD_DIM).transpose(0, 3, 1, 2, 4)
    dsa_o_s = dsa_o_s.reshape(m_s, DSA_HEADS * HEAD_DIM)

    h_s = _merge_mem(xs, ret_y_s, dsa_o_s, proj_s, w_ret_out_b, w_dsa_out_b, w_o_b, g_mem_r, w_mq_b,
                     cache_mem_k[0].reshape(n_dec * mem_len, MEM_W), cache_mem_v[0].reshape(n_dec * mem_len, MEM_W),
                     w_mo_b, tm=m_s, batch_tiles=0, mk_col=0, mv_col=0, nk=n_dec * mem_len, rows_per_batch=rows)
    y_s, w_up_b, w_down_b = _mlp(h_s, g_mlp_r, w_up[0], w_down[0], g_final_r, m_s, 512)
    y_p = _mlp(h_p, g_mlp_r, w_up_b, w_down_b, g_final_r, _pick_tile(seq, 1024), 512)

    def rows_p(col, width, tail):
        return proj_p[:, col:col + width].reshape((1, batch, seq) + tail)

    def rows_s(col, width, tail):
        return proj_s3[:, :t_dec, col:col + width].reshape((1, n_dec, t_dec) + tail)

    kv_p_rows = lambda a: a.reshape(1, batch, seq, DSA_KV_HEADS, HEAD_DIM)
    kv_s_rows = lambda a: a.reshape(1, n_dec, rows, DSA_KV_HEADS, HEAD_DIM)[:, :, :t_dec]
    return (
        y_p.reshape(batch, seq, d),
        y_s.reshape(n_dec, rows, d)[:, :t_dec],
        ret_state_p[None],
        kv_p_rows(k_rows_p),
        kv_p_rows(v_rows_p),
        rows_p(COL_IK, IDX_DH, (IDX_DH,)),
        kv_p[:, :MEM_W].reshape(1, batch, mem_len, MEM_HEADS, HEAD_DIM),
        kv_p[:, MEM_W:].reshape(1, batch, mem_len, MEM_HEADS, HEAD_DIM),
        ret_state_s[None],
        kv_s_rows(k_rows_s),
        kv_s_rows(v_rows_s),
        rows_s(COL_IK, IDX_DH, (IDX_DH,)),
    )
```

```python
import functools
import math

import jax
import jax.numpy as jnp
import numpy as np
from jax import lax
from jax.experimental import pallas as pl
from jax.experimental.pallas import tpu as pltpu

F32 = jnp.float32
BF16 = jnp.bfloat16

D_MODEL = 2048
RET_HEADS = 8
HEAD_DIM = 128
DSA_HEADS = 8
DSA_KV_HEADS = 2
IDX_HEADS = 16
IDX_DH = 64
TOPK_MAX = 256
PAGE_SIZE = 128
MEM_LEN = 256
MEM_HEADS = 4
MEM_W = MEM_HEADS * HEAD_DIM
D_FF = 4 * D_MODEL
RET_THETA = 10000.0
ROPE_THETA = 500000.0
EPS = 1e-6
LANES = 128
SUBLANES = 8
ROW_ALIGN = 16
VMEM_LIMIT = 56 * 1024 * 1024
NEG = -1e30
BISECT_STEPS = 16
MAX_PAGES_PER_STEP = 32

COL_GA = 0
COL_GB = 2048
COL_RQ = 4096
COL_RK = 5120
COL_RV = 6144
COL_RG = 7168
COL_DQ = 8192
COL_IQ = 9216
COL_DK = 10240
COL_DV = 10496
COL_IK = 10752
N_PROJ = 11264
PROJ_TN = 512
PROJ_TN_BF16 = 1024
ROPE_NONE, ROPE_RET, ROPE_RETK, ROPE_DSA, ROPE_IDX, ROPE_IKW = range(6)
COL_ROPE = ((ROPE_NONE,) * 32 + (ROPE_RET,) * 8 + (ROPE_RETK,) * 8 + (ROPE_NONE,) * 16 + (ROPE_DSA,) * 8
            + (ROPE_IDX,) * 8 + (ROPE_DSA,) * 2 + (ROPE_NONE,) * 2 + (ROPE_IKW,) + (ROPE_NONE,) * 3)
TAB_RET_C, TAB_RET_S, TAB_DSA_C, TAB_DSA_S, TAB_IDX_C, TAB_IDX_S = range(6)
TAB_W = 6 * LANES
DSA_ROT = HEAD_DIM // 4
IDX_ROT = IDX_DH // 4


def _cparams(*sem):
    return pltpu.CompilerParams(dimension_semantics=sem, vmem_limit_bytes=VMEM_LIMIT)


def _dot(a, b):
    return jnp.dot(a, b, preferred_element_type=F32)


def _dot_nt(a, b):
    return lax.dot_general(a, b, (((1,), (1,)), ((), ())), preferred_element_type=F32)


def _dot_tn(a, b):
    return lax.dot_general(a, b, (((0,), (0,)), ((), ())), preferred_element_type=F32)


def _rmsnorm_bf16(x, g):
    ms = jnp.mean(x * x, axis=-1, keepdims=True)
    return ((x * lax.rsqrt(ms + EPS)) * g).astype(BF16)


def _sigmoid(x):
    return 1.0 / (1.0 + jnp.exp(-x))


def _tab(tab_ref, which):
    return tab_ref[:, which * LANES:(which + 1) * LANES]


def _rope_cols(o_ref, g, c, s, half, period):
    a = o_ref[:, g * LANES:(g + 1) * LANES]
    if 2 * half == LANES:
        partner = pltpu.roll(a, half, 1)
    else:
        lane = lax.broadcasted_iota(jnp.int32, (1, LANES), 1)
        first = (lane & (period - 1)) < half
        partner = jnp.where(first, pltpu.roll(a, LANES - half, 1), pltpu.roll(a, half, 1))
    o_ref[:, g * LANES:(g + 1) * LANES] = a * c + partner * s


def _finish_block(dst_ref, tab_ref, plan, kv_group, kr_ref, vr_ref):
    for g, kind in enumerate(plan):
        rope = functools.partial(_rope_cols, dst_ref, g)
        if kind == ROPE_RET:
            rope(_tab(tab_ref, TAB_RET_C), _tab(tab_ref, TAB_RET_S), LANES // 2, LANES)
        elif kind == ROPE_RETK:
            scale = HEAD_DIM ** -0.5
            rope(_tab(tab_ref, TAB_RET_C) * scale, _tab(tab_ref, TAB_RET_S) * scale, LANES // 2, LANES)
        elif kind == ROPE_DSA:
            rope(_tab(tab_ref, TAB_DSA_C), _tab(tab_ref, TAB_DSA_S), DSA_ROT // 2, LANES)
        elif kind == ROPE_IDX:
            rope(_tab(tab_ref, TAB_IDX_C), _tab(tab_ref, TAB_IDX_S), IDX_ROT // 2, IDX_DH)
        elif kind == ROPE_IKW:
            lane = lax.broadcasted_iota(jnp.int32, (1, LANES), 1)
            is_ik = lane < IDX_DH
            rope(jnp.where(is_ik, _tab(tab_ref, TAB_IDX_C), 1.0), jnp.where(is_ik, _tab(tab_ref, TAB_IDX_S), 0.0),
                 IDX_ROT // 2, IDX_DH)
    if kv_group is not None:
        rows = dst_ref.shape[0]
        for kvh in range(DSA_KV_HEADS):
            rows_of_head = pl.ds(kvh, rows, stride=DSA_KV_HEADS)
            gk, gv = kv_group + kvh, kv_group + DSA_KV_HEADS + kvh
            kr_ref[rows_of_head, :] = dst_ref[:, gk * LANES:(gk + 1) * LANES]
            vr_ref[rows_of_head, :] = dst_ref[:, gv * LANES:(gv + 1) * LANES]


def _norm_proj_kernel(x_ref, g_ref, w_ref, tab_ref, o_ref, u_ref, *, kr_ref=None, vr_ref=None, wq_ref=None):
    j = pl.program_id(1)

    @pl.when(j == 0)
    def _():
        u_ref[...] = _rmsnorm_bf16(x_ref[...], g_ref[...])

    if tab_ref is None:
        o_ref[...] = _dot(u_ref[...], w_ref[...])
        return
    wb = w_ref[...]
    if wb.dtype != BF16:
        wb = wb.astype(BF16)
        wq_ref[...] = wb
    o_ref[...] = _dot_nt(u_ref[...], wb)
    groups = o_ref.shape[1] // LANES

    plans = {}
    for b in range(N_PROJ // (groups * LANES)):
        plans.setdefault(COL_ROPE[b * groups:(b + 1) * groups], []).append(b)
    kv_block, kv_group = divmod(COL_DK // LANES, groups)
    for plan, blocks in plans.items():
        if all(kind == ROPE_NONE for kind in plan):
            continue
        cond = functools.reduce(jnp.logical_or, [j == b for b in blocks])

        @pl.when(cond)
        def _(plan=plan, blocks=blocks):
            _finish_block(o_ref, tab_ref, plan, kv_group if kv_block in blocks else None, kr_ref, vr_ref)


def _norm_proj(x, g, w, tm, tn):
    m, d = x.shape
    n = w.shape[1]

    def body(x_ref, g_ref, w_ref, o_ref, u_ref):
        _norm_proj_kernel(x_ref, g_ref, w_ref, None, o_ref, u_ref)

    return pl.pallas_call(
        body,
        grid=(m // tm, n // tn),
        in_specs=[
            pl.BlockSpec((tm, d), lambda i, j: (i, 0)),
            pl.BlockSpec((1, d), lambda i, j: (0, 0)),
            pl.BlockSpec((d, tn), lambda i, j: (0, j)),
        ],
        out_specs=pl.BlockSpec((tm, tn), lambda i, j: (i, j)),
        out_shape=jax.ShapeDtypeStruct((m, n), F32),
        scratch_shapes=[pltpu.VMEM((tm, d), BF16)],
        compiler_params=_cparams("parallel", "arbitrary"),
        name="norm_proj",
    )(x, g, w)


_W_IN_ROW = {"rq": 0, "rk": 1024, "rv": 2048, "rg": 3072, "dq": 4096, "dk": 5120, "iq": 5632, "ik": 6656,
             "ga": 6736, "gb": 8784}
PROJ_SRC_ROWS = tuple(_W_IN_ROW[name] + PROJ_TN * k for name, nblk in (
    ("ga", 4), ("gb", 4), ("rq", 2), ("rk", 2), ("rv", 2), ("rg", 2), ("dq", 2), ("iq", 2), ("dk", 1), ("ik", 1))
    for k in range(nblk))


def _norm_proj_rope(x, g, w_t, tab, tm, pos_blocks):
    m, d = x.shape
    from_f32 = w_t.dtype != BF16
    tn = PROJ_TN if from_f32 else PROJ_TN_BF16
    kv_spec = pl.BlockSpec((DSA_KV_HEADS * tm, HEAD_DIM), lambda i, j, *_: (i, 0))
    kv_rows = jax.ShapeDtypeStruct((DSA_KV_HEADS * m, HEAD_DIM), F32)
    in_specs = [
        pl.BlockSpec((tm, d), lambda i, j, *_: (i, 0)),
        pl.BlockSpec((1, d), lambda i, j, *_: (0, 0)),
        None,
        pl.BlockSpec((tm, TAB_W), lambda i, j, *_: (i % pos_blocks, 0)),
    ]
    out_specs = [pl.BlockSpec((tm, tn), lambda i, j, *_: (i, j)), kv_spec, kv_spec]
    out_shape = [jax.ShapeDtypeStruct((m, N_PROJ), F32), kv_rows, kv_rows]
    scratch = [pltpu.VMEM((tm, d), BF16)]
    if from_f32:
        assert m == tm and all(r % ROW_ALIGN == 0 for r in PROJ_SRC_ROWS)
        src_rows = jnp.asarray([r // ROW_ALIGN for r in PROJ_SRC_ROWS], jnp.int32)
        in_specs[2] = pl.BlockSpec((pl.Element(tn), pl.Element(d)), lambda i, j, src: (src[j] * ROW_ALIGN, 0))
        out_specs.append(pl.BlockSpec((tn, d), lambda i, j, src: (j, 0)))
        out_shape.append(jax.ShapeDtypeStruct((N_PROJ, d), BF16))

        def body(src_ref, x_ref, g_ref, w_ref, tab_ref, o_ref, kr_ref, vr_ref, wq_ref, u_ref):
            _norm_proj_kernel(x_ref, g_ref, w_ref, tab_ref, o_ref, u_ref, kr_ref=kr_ref, vr_ref=vr_ref, wq_ref=wq_ref)

        grid_spec = pltpu.PrefetchScalarGridSpec(num_scalar_prefetch=1, grid=(1, N_PROJ // tn), in_specs=in_specs,
                                                 out_specs=out_specs, scratch_shapes=scratch)
        args = (src_rows, x, g, w_t, tab)
    else:
        in_specs[2] = pl.BlockSpec((tn, d), lambda i, j: (j, 0))

        def body(x_ref, g_ref, w_ref, tab_ref, o_ref, kr_ref, vr_ref, u_ref):
            _norm_proj_kernel(x_ref, g_ref, w_ref, tab_ref, o_ref, u_ref, kr_ref=kr_ref, vr_ref=vr_ref)

        grid_spec = pltpu.PrefetchScalarGridSpec(num_scalar_prefetch=0, grid=(m // tm, N_PROJ // tn),
                                                 in_specs=in_specs, out_specs=out_specs, scratch_shapes=scratch)
        args = (x, g, w_t, tab)
    return pl.pallas_call(
        body,
        grid_spec=grid_spec,
        out_shape=out_shape,
        compiler_params=_cparams("parallel", "arbitrary"),
        name="norm_proj_rope",
    )(*args)


@functools.lru_cache(maxsize=None)
def _rope_tables(start, count, period):
    pos = (start + np.arange(count) % period).astype(np.float64)[:, None]

    def cs(half, theta):
        inv = theta ** (-np.arange(half, dtype=np.float64) / half)
        ang = pos * inv[None, :]
        return np.cos(ang), np.sin(ang)

    ones = lambda n: np.ones((count, n))
    zeros = lambda n: np.zeros((count, n))
    c, s = cs(HEAD_DIM // 2, RET_THETA)
    ret_c = np.concatenate([c, c], 1)
    ret_s = np.concatenate([-s, s], 1)
    c, s = cs(DSA_ROT // 2, ROPE_THETA)
    dsa_c = np.concatenate([c, c, ones(LANES - DSA_ROT)], 1)
    dsa_s = np.concatenate([-s, s, zeros(LANES - DSA_ROT)], 1)
    c, s = cs(IDX_ROT // 2, ROPE_THETA)
    idx_c = np.tile(np.concatenate([c, c, ones(IDX_DH - IDX_ROT)], 1), (1, 2))
    idx_s = np.tile(np.concatenate([-s, s, zeros(IDX_DH - IDX_ROT)], 1), (1, 2))
    return np.concatenate([ret_c, ret_s, dsa_c, dsa_s, idx_c, idx_s], 1).astype(np.float32)


@functools.lru_cache(maxsize=None)
def _decay_tables(rows, n_seq, t_valid):
    log_g = np.log1p(-np.exp2(-5.0 - np.arange(RET_HEADS, dtype=np.float64)))
    r = np.arange(n_seq * rows)
    t = (r % rows).astype(np.float64)
    same = (r[:, None] // rows) == (r[None, :] // rows)
    diff = t[:, None] - t[None, :]
    intra = np.where(same[None] & (diff >= 0)[None], np.exp(log_g[:, None, None] * np.maximum(diff, 0.0)[None]), 0.0)
    ones = np.ones((1, 1, HEAD_DIM))
    qdec = np.exp(log_g[:, None] * (t[None, :] + 1.0))[:, :, None] * ones
    kdec = np.where(t[None, :] < t_valid, np.exp(log_g[:, None] * (t_valid - 1.0 - t[None, :])), 0.0)[:, :, None] * ones
    cdec = np.exp(log_g * t_valid)[:, None, None] * ones
    return tuple(a.astype(np.float32) for a in (intra, qdec, kdec, cdec))


def _groupnorm_gate(o, gate, gn):
    mu = jnp.mean(o, axis=-1, keepdims=True)
    d = o - mu
    var = jnp.mean(d * d, axis=-1, keepdims=True)
    n = d * lax.rsqrt(var + EPS) * gn
    return (gate * _sigmoid(gate) * n).astype(BF16)


def _ret_prompt_kernel(q_ref, k_ref, v_ref, g_ref, gn_ref, intra_ref, qdec_ref, kdec_ref, cdec_ref,
                       y_ref, st_ref, s_scr):
    c = pl.program_id(1)

    @pl.when(c == 0)
    def _():
        s_scr[...] = jnp.zeros_like(s_scr)

    for h in range(RET_HEADS):
        sl = slice(h * HEAD_DIM, (h + 1) * HEAD_DIM)
        k = k_ref[:, sl]
        qb = q_ref[:, sl].astype(BF16)
        kb = k.astype(BF16)
        vb = v_ref[:, sl].astype(BF16)
        s = _dot_nt(qb, kb) * intra_ref[h]
        state = s_scr[h]
        o = _dot(s.astype(BF16), vb) + _dot(qb, state.astype(BF16)) * qdec_ref[h]
        kd = (k * kdec_ref[h]).astype(BF16)
        s_scr[h] = state * cdec_ref[h] + _dot_tn(kd, vb)
        y_ref[:, sl] = _groupnorm_gate(o, g_ref[:, sl], gn_ref[:, sl])

    @pl.when(c == pl.num_programs(1) - 1)
    def _():
        st_ref[0] = s_scr[...]


def _ret_prompt(proj, gn, batch, seq, chunk):
    nc = seq // chunk
    intra, qdec, kdec, cdec = _decay_tables(chunk, 1, chunk)
    w = RET_HEADS * HEAD_DIM
    col = lambda off: (lambda b, c: (b * nc + c, off // w))
    const3 = lambda b, c: (0, 0, 0)
    return pl.pallas_call(
        _ret_prompt_kernel,
        grid=(batch, nc),
        in_specs=[
            pl.BlockSpec((chunk, w), col(COL_RQ)),
            pl.BlockSpec((chunk, w), col(COL_RK)),
            pl.BlockSpec((chunk, w), col(COL_RV)),
            pl.BlockSpec((chunk, w), col(COL_RG)),
            pl.BlockSpec((1, w), lambda b, c: (0, 0)),
            pl.BlockSpec((RET_HEADS, chunk, chunk), const3),
            pl.BlockSpec((RET_HEADS, chunk, HEAD_DIM), const3),
            pl.BlockSpec((RET_HEADS, chunk, HEAD_DIM), const3),
            pl.BlockSpec((RET_HEADS, 1, HEAD_DIM), const3),
        ],
        out_specs=[
            pl.BlockSpec((chunk, w), lambda b, c: (b * nc + c, 0)),
            pl.BlockSpec((1, RET_HEADS, HEAD_DIM, HEAD_DIM), lambda b, c: (b, 0, 0, 0)),
        ],
        out_shape=[
            jax.ShapeDtypeStruct((batch * seq, w), BF16),
            jax.ShapeDtypeStruct((batch, RET_HEADS, HEAD_DIM, HEAD_DIM), F32),
        ],
        scratch_shapes=[pltpu.VMEM((RET_HEADS, HEAD_DIM, HEAD_DIM), F32)],
        compiler_params=_cparams("parallel", "arbitrary"),
        name="ret_prompt",
    )(proj, proj, proj, proj, gn, intra, qdec, kdec, cdec)


def _ret_sample_kernel(q_ref, k_ref, v_ref, g_ref, gn_ref, st_ref, intra_ref, qdec_ref, kdec_ref, cdec_ref,
                       y_ref, so_ref, *, n_batch, rows):
    k = k_ref[...]
    qb = q_ref[...].astype(BF16)
    kb = k.astype(BF16)
    vb = v_ref[...].astype(BF16)
    s = _dot_nt(qb, kb) * intra_ref[0]
    kd = k * kdec_ref[0]
    row_batch = lax.broadcasted_iota(jnp.int32, kd.shape, 0) // rows
    inter = []
    for b in range(n_batch):
        state = st_ref[b, 0]
        inter.append(_dot(qb[b * rows:(b + 1) * rows], state.astype(BF16)))
        kd_b = jnp.where(row_batch == b, kd, 0.0).astype(BF16)
        so_ref[b, 0] = state * cdec_ref[0] + _dot_tn(kd_b, vb)
    o = _dot(s.astype(BF16), vb) + jnp.concatenate(inter, axis=0) * qdec_ref[0]
    y_ref[...] = _groupnorm_gate(o, g_ref[...], gn_ref[...])


def _ret_sample(proj, gn, state, n_batch, rows, t_valid):
    m = n_batch * rows
    intra, qdec, kdec, cdec = _decay_tables(rows, n_batch, t_valid)
    col = lambda off: (lambda h: (0, off // HEAD_DIM + h))
    per_head = lambda h: (h, 0, 0)
    return pl.pallas_call(
        functools.partial(_ret_sample_kernel, n_batch=n_batch, rows=rows),
        grid=(RET_HEADS,),
        in_specs=[
            pl.BlockSpec((m, HEAD_DIM), col(COL_RQ)),
            pl.BlockSpec((m, HEAD_DIM), col(COL_RK)),
            pl.BlockSpec((m, HEAD_DIM), col(COL_RV)),
            pl.BlockSpec((m, HEAD_DIM), col(COL_RG)),
            pl.BlockSpec((1, HEAD_DIM), lambda h: (0, h)),
            pl.BlockSpec((n_batch, 1, HEAD_DIM, HEAD_DIM), lambda h: (0, h, 0, 0)),
            pl.BlockSpec((1, m, m), per_head),
            pl.BlockSpec((1, m, HEAD_DIM), per_head),
            pl.BlockSpec((1, m, HEAD_DIM), per_head),
            pl.BlockSpec((1, 1, HEAD_DIM), per_head),
        ],
        out_specs=[
            pl.BlockSpec((m, HEAD_DIM), lambda h: (0, h)),
            pl.BlockSpec((n_batch, 1, HEAD_DIM, HEAD_DIM), lambda h: (0, h, 0, 0)),
        ],
        out_shape=[
            jax.ShapeDtypeStruct((m, RET_HEADS * HEAD_DIM), BF16),
            jax.ShapeDtypeStruct(state.shape, F32),
        ],
        compiler_params=_cparams("parallel"),
        name="ret_sample",
    )(proj, proj, proj, proj, gn, state, intra, qdec, kdec, cdec)


def _kth_largest(count_ge, next_above, smin, smax, n_adm, topk):
    few = n_adm < topk
    lo = jnp.where(few, -jnp.inf, smin)
    c_lo = jnp.where(few, float(topk), n_adm)
    hi = jnp.where(few, -jnp.inf, smax + jnp.maximum(jnp.abs(smax) * 1e-6, 1e-30))

    def bisect(_, state):
        lo, hi, c_lo = state
        mid = 0.5 * lo + 0.5 * hi
        c = count_ge(mid)
        take = c >= topk
        return jnp.where(take, mid, lo), jnp.where(take, hi, mid), jnp.where(take, c, c_lo)

    lo, hi, c_lo = lax.fori_loop(0, BISECT_STEPS, bisect, (lo, hi, c_lo))
    c_lo = jnp.where(few, float(topk), c_lo)

    def unfinished(state):
        return jnp.max(state[2]) > 0.0

    def step_up(state):
        lo, c_lo, active = state
        nxt, n_eq = next_above(lo)
        c_nxt = c_lo - n_eq
        move = active * jnp.where(c_nxt >= topk, 1.0, 0.0)
        lo = jnp.where(move > 0.0, nxt, lo)
        c_lo = jnp.where(move > 0.0, c_nxt, c_lo)
        return lo, c_lo, move * jnp.where(c_nxt > topk, 1.0, 0.0)

    return lax.while_loop(unfinished, step_up, (lo, c_lo, jnp.where(c_lo > topk, 1.0, 0.0)))[0]


def _dsa_prompt_kernel(dq_ref, iq_ref, iwq_ref, kall_ref, vall_ref, ikall_ref, o_ref,
                       kbf, vtb, ika, ikb, keys, z_a, z_b, zmax_a, zmax_b, m_scr, acc_scr, *, tq, topk):
    z_scr, zmax_scr = (z_a, z_b), (zmax_a, zmax_b)
    i = pl.program_id(1)
    n_chunks = kbf.shape[0]
    scale_log2e = HEAD_DIM ** -0.5 * math.log2(math.e)

    @pl.when(i == 0)
    def _():
        lane = lax.broadcasted_iota(jnp.int32, (tq, LANES), 1)
        for c in range(n_chunks):
            rows = slice(c * tq, (c + 1) * tq)
            kbf[c] = kall_ref[rows, :].astype(BF16)
            for kvh in range(DSA_KV_HEADS):
                sl = slice(kvh * HEAD_DIM, (kvh + 1) * HEAD_DIM)
                vtb[c, kvh, :HEAD_DIM, :] = vall_ref[rows, sl].T.astype(BF16)
                vtb[c, kvh, HEAD_DIM:, :] = jnp.ones((ROW_ALIGN, tq), BF16)
            a = ikall_ref[rows, :]
            ika[c] = jnp.where(lane < IDX_DH, a, 0.0).astype(BF16)
            ikb[c] = jnp.where(lane >= IDX_DH, pltpu.roll(a, IDX_DH, 1), 0.0).astype(BF16)

    w_t = iwq_ref[...].T
    iqb = iq_ref[...].astype(BF16)
    qb = (dq_ref[...] * scale_log2e).astype(BF16)
    t_col = i * tq + lax.broadcasted_iota(jnp.int32, (tq, tq), 1)
    s_row0 = lax.broadcasted_iota(jnp.int32, (tq, tq), 0)

    fold = lambda x: x.reshape(tq // SUBLANES, SUBLANES, tq)

    def score_chunk(c, carry):
        smax, smin = carry
        src = jnp.minimum(c, n_chunks - 1)
        a = ika[src]
        b = ikb[src]
        acc = jnp.zeros((tq, tq), F32)
        for p in range(IDX_HEADS // 2):
            pair = iqb[:, p * LANES:(p + 1) * LANES]
            w0 = w_t[IDX_DH + 2 * p:IDX_DH + 2 * p + 1, :]
            w1 = w_t[IDX_DH + 2 * p + 1:IDX_DH + 2 * p + 2, :]
            acc = acc + jnp.maximum(_dot_nt(a, pair), 0.0) * w0
            acc = acc + jnp.maximum(_dot_nt(b, pair), 0.0) * w1
        admissible = s_row0 + c * tq <= t_col
        keys[c] = jnp.where(admissible, acc, -jnp.inf)
        smax = jnp.maximum(smax, fold(jnp.where(admissible, acc, -jnp.inf)).max(axis=0))
        smin = jnp.minimum(smin, fold(jnp.where(admissible, acc, jnp.inf)).min(axis=0))
        return smax, smin

    n_pairs = (i + 2) // 2

    def score_pair(j, carry):
        return score_chunk(2 * j + 1, score_chunk(2 * j, carry))

    smax, smin = lax.fori_loop(0, n_pairs, score_pair, (jnp.full((SUBLANES, tq), -jnp.inf, F32),
                                                        jnp.full((SUBLANES, tq), jnp.inf, F32)))
    smax = smax.max(axis=0, keepdims=True)
    smin = smin.min(axis=0, keepdims=True)

    def count_ge(t):
        def pair(j, cnt):
            for c in (2 * j, 2 * j + 1):
                cnt = cnt + fold(jnp.where(keys[c] >= t, 1.0, 0.0)).sum(axis=0)
            return cnt

        return lax.fori_loop(0, n_pairs, pair, jnp.zeros((SUBLANES, tq), F32)).sum(axis=0, keepdims=True)

    def next_above(lo):
        def pair(j, carry):
            nxt, n_eq = carry
            for c in (2 * j, 2 * j + 1):
                k = keys[c]
                nxt = jnp.minimum(nxt, fold(jnp.where(k > lo, k, jnp.inf)).min(axis=0))
                n_eq = n_eq + fold(jnp.where(k == lo, 1.0, 0.0)).sum(axis=0)
            return nxt, n_eq

        nxt, n_eq = lax.fori_loop(0, n_pairs, pair, (jnp.full((SUBLANES, tq), jnp.inf, F32),
                                                     jnp.zeros((SUBLANES, tq), F32)))
        return nxt.min(axis=0, keepdims=True), n_eq.sum(axis=0, keepdims=True)

    thr = _kth_largest(count_ge, next_above, smin, smax, (t_col[:1, :] + 1).astype(F32), topk)

    m_scr[...] = jnp.full_like(m_scr, NEG)
    acc_scr[...] = jnp.zeros_like(acc_scr)

    kv_of = lambda h: h // (DSA_HEADS // DSA_KV_HEADS)

    def logits(c, slot):
        bias = jnp.where(s_row0 + c * tq <= t_col, jnp.where(keys[c] >= thr, 0.0, NEG), NEG)
        kc = kbf[c]
        for h in range(DSA_HEADS):
            ksl = slice(kv_of(h) * HEAD_DIM, (kv_of(h) + 1) * HEAD_DIM)
            z = _dot_nt(kc[:, ksl], qb[:, h * HEAD_DIM:(h + 1) * HEAD_DIM]) + bias
            z_scr[slot][h] = z
            zmax_scr[slot][h] = z.max(axis=0, keepdims=True)

    def update(c, slot):
        for h in range(DSA_HEADS):
            m_old = m_scr[h]
            m_new = jnp.maximum(m_old, zmax_scr[slot][h])
            p = jnp.exp2(z_scr[slot][h] - m_new)
            acc_scr[h] = acc_scr[h] * jnp.exp2(m_old - m_new) + _dot(vtb[c, kv_of(h)], p.astype(BF16))
            m_scr[h] = m_new

    n = i + 1
    logits(0, 0)

    def two_chunks(k, carry):
        logits(2 * k + 1, 1)
        update(2 * k, 0)
        logits(2 * k + 2, 0)
        update(2 * k + 1, 1)
        return carry

    lax.fori_loop(0, (n - 1) // 2, two_chunks, 0)

    @pl.when(n % 2 == 1)
    def _():
        update(n - 1, 0)

    @pl.when(n % 2 == 0)
    def _():
        logits(n - 1, 1)
        update(n - 2, 0)
        update(n - 1, 1)

    for h in range(DSA_HEADS):
        o = acc_scr[h, :HEAD_DIM, :] * (1.0 / acc_scr[h, HEAD_DIM:HEAD_DIM + 1, :])
        o_ref[:, h * HEAD_DIM:(h + 1) * HEAD_DIM] = o.T.astype(BF16)


def _dsa_prompt(proj, batch, seq, tq):
    nq = seq // tq
    topk = min(TOPK_MAX, seq // 4)
    qw = DSA_HEADS * HEAD_DIM
    kvw = DSA_KV_HEADS * HEAD_DIM
    return pl.pallas_call(
        functools.partial(_dsa_prompt_kernel, tq=tq, topk=topk),
        grid=(batch, nq),
        in_specs=[
            pl.BlockSpec((tq, qw), lambda b, i: (b * nq + i, COL_DQ // qw)),
            pl.BlockSpec((tq, qw), lambda b, i: (b * nq + i, COL_IQ // qw)),
            pl.BlockSpec((tq, LANES), lambda b, i: (b * nq + i, COL_IK // LANES)),
            pl.BlockSpec((seq, kvw), lambda b, i: (b, COL_DK // kvw)),
            pl.BlockSpec((seq, kvw), lambda b, i: (b, COL_DV // kvw)),
            pl.BlockSpec((seq, LANES), lambda b, i: (b, COL_IK // LANES)),
        ],
        out_specs=pl.BlockSpec((tq, qw), lambda b, i: (b * nq + i, 0)),
        out_shape=jax.ShapeDtypeStruct((batch * seq, qw), BF16),
        scratch_shapes=[
            pltpu.VMEM((nq, tq, kvw), BF16),
            pltpu.VMEM((nq, DSA_KV_HEADS, HEAD_DIM + ROW_ALIGN, tq), BF16),
            pltpu.VMEM((nq, tq, LANES), BF16),
            pltpu.VMEM((nq, tq, LANES), BF16),
            pltpu.VMEM((nq + 1, tq, tq), F32),
            pltpu.VMEM((DSA_HEADS, tq, tq), F32),
            pltpu.VMEM((DSA_HEADS, tq, tq), F32),
            pltpu.VMEM((DSA_HEADS, 1, tq), F32),
            pltpu.VMEM((DSA_HEADS, 1, tq), F32),
            pltpu.VMEM((DSA_HEADS, 1, tq), F32),
            pltpu.VMEM((DSA_HEADS, HEAD_DIM + ROW_ALIGN, tq), F32),
        ],
        compiler_params=_cparams("parallel", "arbitrary"),
        name="dsa_prompt",
    )(proj, proj, proj, proj, proj, proj)


def _idx_sample_kernel(pt_ref, iq_ref, iw_ref, iknew_ref, *rest, pages, t_valid):
    page_refs = rest[:pages]
    sp_ref, sn_ref = rest[pages:]
    iqb = iq_ref[0, :t_valid * IDX_HEADS].astype(BF16)
    w = iw_ref[0, :t_valid * IDX_HEADS]
    rows = sp_ref.shape[1]

    def scores(ik_t):
        d = jnp.maximum(_dot(iqb, ik_t.astype(BF16)), 0.0) * w
        return d.reshape(t_valid, IDX_HEADS, LANES).sum(axis=1)

    sp_ref[0, t_valid:, :] = jnp.zeros((rows - t_valid, pages * PAGE_SIZE), F32)
    for p in range(pages):
        sp_ref[0, :t_valid, p * PAGE_SIZE:(p + 1) * PAGE_SIZE] = scores(page_refs[p][0])
    t = lax.broadcasted_iota(jnp.int32, (t_valid, LANES), 0)
    j = lax.broadcasted_iota(jnp.int32, (t_valid, LANES), 1)
    sn_ref[0, t_valid:, :] = jnp.zeros((rows - t_valid, LANES), F32)
    sn_ref[0, :t_valid, :] = jnp.where(j <= t, scores(iknew_ref[0]), -jnp.inf)


def _idx_sample(page_table, iq, iw, ik_new, pool_ik, pages, t_valid):
    n_batch, n_pages = page_table.shape
    rows16 = iq.shape[1]
    rows = rows16 // IDX_HEADS
    past = n_pages * PAGE_SIZE
    page_specs = [pl.BlockSpec((1, IDX_DH, PAGE_SIZE), functools.partial(
        lambda b, g, pt, k: (pt[b, g * pages + k], 0, 0), k=k)) for k in range(pages)]
    grid_spec = pltpu.PrefetchScalarGridSpec(
        num_scalar_prefetch=1,
        grid=(n_batch, n_pages // pages),
        in_specs=[
            pl.BlockSpec((1, rows16, IDX_DH), lambda b, g, pt: (b, 0, 0)),
            pl.BlockSpec((1, rows16, LANES), lambda b, g, pt: (b, 0, 0)),
            pl.BlockSpec((1, IDX_DH, PAGE_SIZE), lambda b, g, pt: (b, 0, 0)),
        ] + page_specs,
        out_specs=[
            pl.BlockSpec((1, rows, pages * PAGE_SIZE), lambda b, g, pt: (b, 0, g)),
            pl.BlockSpec((1, rows, LANES), lambda b, g, pt: (b, 0, 0)),
        ],
    )
    return pl.pallas_call(
        functools.partial(_idx_sample_kernel, pages=pages, t_valid=t_valid),
        grid_spec=grid_spec,
        out_shape=[
            jax.ShapeDtypeStruct((n_batch, rows, past), F32),
            jax.ShapeDtypeStruct((n_batch, rows, LANES), F32),
        ],
        compiler_params=_cparams("parallel", "arbitrary"),
        name="idx_sample",
    )(page_table, iq, iw, ik_new, *([pool_ik] * pages))


def _sel_sample_kernel(sp_ref, sn_ref, bp_ref, bn_ref, *, topk):
    n_batch, rows, past = sp_ref.shape
    sp = sp_ref[...].reshape(n_batch * rows, past)
    sn = sn_ref[...].reshape(n_batch * rows, LANES)
    t = lax.broadcasted_iota(jnp.int32, (n_batch * rows, LANES), 0) % rows
    j = lax.broadcasted_iota(jnp.int32, (n_batch * rows, LANES), 1)
    admissible_new = j <= t
    row_sum = lambda a: a.sum(axis=1, keepdims=True)
    row_min = lambda a: a.min(axis=1, keepdims=True)

    def count_ge(thr):
        return row_sum(jnp.where(sp >= thr, 1.0, 0.0)) + row_sum(jnp.where(sn >= thr, 1.0, 0.0))

    def next_above(lo):
        nxt = jnp.minimum(row_min(jnp.where(sp > lo, sp, jnp.inf)), row_min(jnp.where(sn > lo, sn, jnp.inf)))
        return nxt, row_sum(jnp.where(sp == lo, 1.0, 0.0)) + row_sum(jnp.where(sn == lo, 1.0, 0.0))

    smin = jnp.minimum(row_min(sp), row_min(jnp.where(admissible_new, sn, jnp.inf)))
    smax = jnp.maximum(sp.max(axis=1, keepdims=True), sn.max(axis=1, keepdims=True))
    n_adm = float(past) + row_sum(jnp.where(admissible_new, 1.0, 0.0))
    thr = _kth_largest(count_ge, next_above, smin, smax, n_adm, topk)
    bp_ref[...] = jnp.where(sp >= thr, 0.0, NEG).reshape(n_batch, rows, past)
    bn_ref[...] = jnp.where(admissible_new, jnp.where(sn >= thr, 0.0, NEG), NEG).reshape(n_batch, rows, LANES)


def _sel_sample(scores_past, scores_new, topk):
    full = lambda a: pl.BlockSpec(a.shape, lambda: (0,) * a.ndim)
    return pl.pallas_call(
        functools.partial(_sel_sample_kernel, topk=topk),
        in_specs=[full(scores_past), full(scores_new)],
        out_specs=[full(scores_past), full(scores_new)],
        out_shape=[jax.ShapeDtypeStruct(scores_past.shape, F32), jax.ShapeDtypeStruct(scores_new.shape, F32)],
        compiler_params=pltpu.CompilerParams(vmem_limit_bytes=VMEM_LIMIT),
        name="sel_sample",
    )(scores_past, scores_new)


def _dsa_sample_kernel(pt_ref, bp_ref, bn_ref, q_ref, knew_ref, vnew_ref, *rest, pages, steps):
    k_refs = rest[:pages]
    v_refs = rest[pages:2 * pages]
    o_ref, m_scr, l_scr, acc_scr = rest[2 * pages:]
    g = pl.program_id(0) % steps
    scale = HEAD_DIM ** -0.5
    group = DSA_HEADS // DSA_KV_HEADS

    @pl.when(g == 0)
    def _():
        m_scr[...] = jnp.full_like(m_scr, NEG)
        l_scr[...] = jnp.zeros_like(l_scr)
        acc_scr[...] = jnp.zeros_like(acc_scr)

    def head_rows(ref, kvh):
        return ref[0, pl.ds(kvh, PAGE_SIZE, stride=DSA_KV_HEADS), :].astype(BF16)

    def attend(k_pages, v_pages, bias):
        bias_g = jnp.concatenate([bias] * group, axis=0)
        kv_heads = range(DSA_KV_HEADS)
        qb = [q_ref[0, kvh].astype(BF16) for kvh in kv_heads]
        raw = [jnp.concatenate([_dot_nt(qb[kvh], head_rows(k, kvh)) for k in k_pages], axis=1) for kvh in kv_heads]
        pb, alpha = [], []
        for kvh in kv_heads:
            z = jnp.where(bias_g < 0.0, NEG, raw[kvh])
            m_old = m_scr[kvh]
            m_new = jnp.maximum(m_old, z.max(axis=1, keepdims=True))
            alpha.append(jnp.exp((m_old - m_new) * scale))
            p = jnp.exp((z - m_new) * scale)
            l_scr[kvh] = l_scr[kvh] * alpha[kvh] + p.sum(axis=1, keepdims=True)
            m_scr[kvh] = m_new
            pb.append(p.astype(BF16))
        for kvh in kv_heads:
            pv = _dot(pb[kvh][:, :PAGE_SIZE], head_rows(v_pages[0], kvh))
            for n in range(1, len(v_pages)):
                pv = pv + _dot(pb[kvh][:, n * PAGE_SIZE:(n + 1) * PAGE_SIZE], head_rows(v_pages[n], kvh))
            acc_scr[kvh] = acc_scr[kvh] * alpha[kvh] + pv

    attend(k_refs, v_refs, bp_ref[0])

    @pl.when(g == steps - 1)
    def _():
        attend([knew_ref], [vnew_ref], bn_ref[0])
        for kvh in range(DSA_KV_HEADS):
            o_ref[0, kvh] = (acc_scr[kvh] * (1.0 / l_scr[kvh])).astype(BF16)


def _dsa_sample(page_table, bias_past, bias_new, q, k_new, v_new, pool_k, pool_v, steps):
    n_batch, n_pages = page_table.shape
    pages = n_pages // steps
    rows = bias_past.shape[1]
    page_rows = DSA_KV_HEADS * PAGE_SIZE
    grows = q.shape[2]
    page_spec = lambda k: pl.BlockSpec((1, page_rows, HEAD_DIM), functools.partial(
        lambda i, pt, k: (pt[i // steps, (i % steps) * pages + k], 0, 0), k=k))
    per_batch3 = lambda i, pt: (i // steps, 0, 0)
    per_batch4 = lambda i, pt: (i // steps, 0, 0, 0)
    in_specs = [
        pl.BlockSpec((1, rows, pages * PAGE_SIZE), lambda i, pt: (i // steps, 0, i % steps)),
        pl.BlockSpec((1, rows, LANES), per_batch3),
        pl.BlockSpec((1, DSA_KV_HEADS, grows, HEAD_DIM), per_batch4),
        pl.BlockSpec((1, page_rows, HEAD_DIM), per_batch3),
        pl.BlockSpec((1, page_rows, HEAD_DIM), per_batch3),
    ] + [page_spec(k) for k in range(pages)] * 2
    grid_spec = pltpu.PrefetchScalarGridSpec(
        num_scalar_prefetch=1,
        grid=(n_batch * steps,),
        in_specs=in_specs,
        out_specs=pl.BlockSpec((1, DSA_KV_HEADS, grows, HEAD_DIM), per_batch4),
        scratch_shapes=[
            pltpu.VMEM((DSA_KV_HEADS, grows, 1), F32),
            pltpu.VMEM((DSA_KV_HEADS, grows, 1), F32),
            pltpu.VMEM((DSA_KV_HEADS, grows, HEAD_DIM), F32),
        ],
    )
    return pl.pallas_call(
        functools.partial(_dsa_sample_kernel, pages=pages, steps=steps),
        grid_spec=grid_spec,
        out_shape=jax.ShapeDtypeStruct((n_batch, DSA_KV_HEADS, grows, HEAD_DIM), BF16),
        compiler_params=_cparams("arbitrary"),
        name="dsa_sample",
    )(page_table, bias_past, bias_new, q, k_new, v_new, *([pool_k] * pages), *([pool_v] * pages))


def _merge_mem_kernel(x_ref, ry_ref, do_ref, ga_ref, gb_ref, wr_ref, wd_ref, wo_ref,
                      g_ref, wq_ref, mk_ref, mv_ref, wmo_ref, o_ref, *, rows_per_batch):
    ya = _dot(ry_ref[...], wr_ref[...])
    yb = _dot(do_ref[...], wd_ref[...])
    merged = _sigmoid(ga_ref[...]) * ya + _sigmoid(gb_ref[...]) * yb
    h = x_ref[...] + _dot(merged.astype(BF16), wo_ref[...])
    qm = _dot(_rmsnorm_bf16(h, g_ref[...]), wq_ref[...])
    tm = h.shape[0]
    nk = mk_ref.shape[0]
    scale = HEAD_DIM ** -0.5
    if rows_per_batch is not None:
        rb = lax.broadcasted_iota(jnp.int32, (tm, nk), 0) // rows_per_batch
        kb = lax.broadcasted_iota(jnp.int32, (tm, nk), 1) // MEM_LEN
        same = rb == kb
    outs = []
    for hd in range(MEM_HEADS):
        sl = slice(hd * HEAD_DIM, (hd + 1) * HEAD_DIM)
        z = _dot_nt(qm[:, sl].astype(BF16), mk_ref[:, sl].astype(BF16)) * scale
        if rows_per_batch is not None:
            z = jnp.where(same, z, NEG)
        p = jnp.exp(z - z.max(axis=-1, keepdims=True))
        pv = _dot(p.astype(BF16), mv_ref[:, sl].astype(BF16))
        outs.append(pv * (1.0 / p.sum(axis=-1, keepdims=True)))
    om = jnp.concatenate(outs, axis=1).astype(BF16)
    o_ref[...] = h + _dot(om, wmo_ref[...])


def _merge_mem(x, ret_y, dsa_o, proj, w_ret_out, w_dsa_out, w_o, g, w_mq, mk, mv, w_mo, *,
               tm, batch_tiles, mk_col, mv_col, nk, rows_per_batch):
    m, d = x.shape
    w = ret_y.shape[1]
    const = lambda i: (0, 0)
    resident = dict(pipeline_mode=pl.Buffered(1))
    if batch_tiles:
        kmap = lambda col: (lambda i: (i // batch_tiles, col))
    else:
        kmap = lambda col: (lambda i: (0, col))
    return pl.pallas_call(
        functools.partial(_merge_mem_kernel, rows_per_batch=rows_per_batch),
        grid=(m // tm,),
        in_specs=[
            pl.BlockSpec((tm, d), lambda i: (i, 0)),
            pl.BlockSpec((tm, w), lambda i: (i, 0)),
            pl.BlockSpec((tm, w), lambda i: (i, 0)),
            pl.BlockSpec((tm, d), lambda i: (i, COL_GA // d)),
            pl.BlockSpec((tm, d), lambda i: (i, COL_GB // d)),
            pl.BlockSpec((w, d), const, **resident),
            pl.BlockSpec((w, d), const, **resident),
            pl.BlockSpec((d, d), const, **resident),
            pl.BlockSpec((1, d), const),
            pl.BlockSpec((d, MEM_W), const, **resident),
            pl.BlockSpec((nk, MEM_W), kmap(mk_col)),
            pl.BlockSpec((nk, MEM_W), kmap(mv_col)),
            pl.BlockSpec((MEM_W, d), const, **resident),
        ],
        out_specs=pl.BlockSpec((tm, d), lambda i: (i, 0)),
        out_shape=jax.ShapeDtypeStruct((m, d), F32),
        compiler_params=_cparams("parallel"),
        name="merge_mem",
    )(x, ret_y, dsa_o, proj, proj, w_ret_out, w_dsa_out, w_o, g, w_mq, mk, mv, w_mo)


def _mlp_kernel(h_ref, g_ref, wu_ref, wd_ref, gf_ref, o_ref, *rest):
    u_ref = rest[-1]
    j = pl.program_id(1)

    @pl.when(j == 0)
    def _():
        u_ref[...] = _rmsnorm_bf16(h_ref[...], g_ref[...])
        o_ref[...] = h_ref[...]

    wu, wd = wu_ref[...], wd_ref[...]
    if len(rest) == 3:
        wu, wd = wu.astype(BF16), wd.astype(BF16)
        rest[0][...] = wu
        rest[1][...] = wd
    a = jnp.maximum(_dot(u_ref[...], wu), 0.0)
    o_ref[...] += _dot((a * a).astype(BF16), wd)

    @pl.when(j == pl.num_programs(1) - 1)
    def _():
        y = o_ref[...]
        ms = jnp.mean(y * y, axis=-1, keepdims=True)
        o_ref[...] = (y * lax.rsqrt(ms + EPS)) * gf_ref[...]


def _mlp(h, g, w_up, w_down, g_final, tm, tf):
    m, d = h.shape
    ff = w_up.shape[1]
    emit = w_up.dtype != BF16
    assert not emit or m == tm, "weight copies are written once, by a call with one row tile"
    wu_spec = pl.BlockSpec((d, tf), lambda i, j: (0, j))
    wd_spec = pl.BlockSpec((tf, d), lambda i, j: (j, 0))
    out_specs = [pl.BlockSpec((tm, d), lambda i, j: (i, 0))]
    out_shape = [jax.ShapeDtypeStruct((m, d), F32)]
    if emit:
        out_specs += [wu_spec, wd_spec]
        out_shape += [jax.ShapeDtypeStruct(w_up.shape, BF16), jax.ShapeDtypeStruct(w_down.shape, BF16)]
    out = pl.pallas_call(
        _mlp_kernel,
        grid=(m // tm, ff // tf),
        in_specs=[
            pl.BlockSpec((tm, d), lambda i, j: (i, 0)),
            pl.BlockSpec((1, d), lambda i, j: (0, 0)),
            wu_spec,
            wd_spec,
            pl.BlockSpec((1, d), lambda i, j: (0, 0)),
        ],
        out_specs=out_specs,
        out_shape=out_shape,
        scratch_shapes=[pltpu.VMEM((tm, d), BF16)],
        compiler_params=_cparams("parallel", "arbitrary"),
        name="mlp_final",
    )(h, g, w_up, w_down, g_final)
    return out if emit else out[0]


def _pick_tile(n, pref):
    t = min(n, pref)
    while n % t:
        t //= 2
    return t


def kernel(x_prompt, x_sample, mem_prompt, cache_k, cache_v, cache_idx_k, state_ret, cache_mem_k, cache_mem_v,
           page_table, g_mix, w_in, gn_ret, w_ret_out, w_dsa_out, w_o, g_mem, g_memkv, w_mq, w_mk, w_mv, w_mo,
           g_mlp, w_up, w_down, g_final):
    assert w_in.shape[0] == 1, "one layer"
    batch, seq, d = x_prompt.shape
    n_dec, t_dec, _ = x_sample.shape
    n_pages = page_table.shape[1]
    past = n_pages * PAGE_SIZE
    mem_len = mem_prompt.shape[1]
    assert mem_len == MEM_LEN and t_dec <= SUBLANES
    row = lambda v: v.reshape(1, -1)

    w_ret_out_b, w_dsa_out_b, w_o_b = (w[0].astype(BF16) for w in (w_ret_out, w_dsa_out, w_o))
    w_mq_b, w_mo_b = w_mq[0].astype(BF16), w_mo[0].astype(BF16)
    w_mkv_b = jnp.concatenate([w_mk[0], w_mv[0]], axis=1).astype(BF16)
    g_mix_r, gn_r, g_mem_r, g_memkv_r, g_mlp_r, g_final_r = (
        row(v) for v in (g_mix[0], gn_ret[0], g_mem[0], g_memkv[0], g_mlp[0], g_final))

    rows = SUBLANES
    m_s = n_dec * rows
    xs = jnp.pad(x_sample, ((0, 0), (0, rows - t_dec), (0, 0))).reshape(m_s, d)
    proj_s, k_rows_s, v_rows_s, w_proj_b = _norm_proj_rope(xs, g_mix_r, jnp.swapaxes(w_in[0], 0, 1),
                                                           _rope_tables(past, m_s, rows), m_s, 1)

    m_p = batch * seq
    xp = x_prompt.reshape(m_p, d)
    tm_proj = _pick_tile(seq, 1024)
    proj_p, k_rows_p, v_rows_p = _norm_proj_rope(xp, g_mix_r, w_proj_b, _rope_tables(0, seq, seq), tm_proj,
                                                 seq // tm_proj)
    ret_y_p, ret_state_p = _ret_prompt(proj_p, gn_r, batch, seq, _pick_tile(seq, 256))
    dsa_o_p = _dsa_prompt(proj_p, batch, seq, _pick_tile(seq, 256))
    kv_p = _norm_proj(mem_prompt.reshape(batch * mem_len, d), g_memkv_r, w_mkv_b, mem_len, PROJ_TN)
    tm_t = _pick_tile(seq, 256)
    h_p = _merge_mem(xp, ret_y_p, dsa_o_p, proj_p, w_ret_out_b, w_dsa_out_b, w_o_b, g_mem_r, w_mq_b, kv_p, kv_p,
                     w_mo_b, tm=tm_t, batch_tiles=seq // tm_t, mk_col=0, mv_col=1, nk=mem_len, rows_per_batch=None)

    ret_y_s, ret_state_s = _ret_sample(proj_s, gn_r, state_ret[0], n_dec, rows, t_dec)

    proj_s3 = proj_s.reshape(n_dec, rows, N_PROJ)
    iq_s = proj_s3[:, :, COL_IQ:COL_IQ + IDX_HEADS * IDX_DH].reshape(n_dec, rows * IDX_HEADS, IDX_DH)
    iw_s = proj_s3[:, :, COL_IK + IDX_DH:COL_IK + IDX_DH + IDX_HEADS].reshape(n_dec, rows * IDX_HEADS, 1)
    iw_s = jnp.broadcast_to(iw_s, (n_dec, rows * IDX_HEADS, LANES))
    kvw = DSA_KV_HEADS * HEAD_DIM
    page_rows = DSA_KV_HEADS * PAGE_SIZE
    pad_keys = lambda a: jnp.pad(a, ((0, 0), (0, PAGE_SIZE - rows), (0, 0)))
    ik_new_t = jnp.swapaxes(pad_keys(proj_s3[:, :, COL_IK:COL_IK + IDX_DH]), 1, 2)
    pool_ik_t = jnp.swapaxes(cache_idx_k, 2, 3).reshape(-1, IDX_DH, PAGE_SIZE)
    new_page = lambda a: jnp.pad(a.reshape(n_dec, DSA_KV_HEADS * rows, HEAD_DIM),
                                 ((0, 0), (0, page_rows - DSA_KV_HEADS * rows), (0, 0)))
    k_new, v_new = new_page(k_rows_s), new_page(v_rows_s)
    scores_past, scores_new = _idx_sample(page_table, iq_s, iw_s, ik_new_t, pool_ik_t,
                                          _pick_tile(n_pages, MAX_PAGES_PER_STEP), t_dec)
    group = DSA_HEADS // DSA_KV_HEADS
    dq_s = proj_s3[:, :, COL_DQ:COL_DQ + DSA_HEADS * HEAD_DIM].reshape(n_dec, rows, DSA_KV_HEADS, group, HEAD_DIM)
    dq_s = dq_s.transpose(0, 2, 3, 1, 4).reshape(n_dec, DSA_KV_HEADS, group * rows, HEAD_DIM)
    bias_past, bias_new = _sel_sample(scores_past, scores_new, min(TOPK_MAX, (past + t_dec) // 4))
    dsa_o_s = _dsa_sample(page_table, bias_past, bias_new, dq_s, k_new, v_new,
                          cache_k.reshape(-1, page_rows, HEAD_DIM), cache_v.reshape(-1, page_rows, HEAD_DIM),
                          n_pages // _pick_tile(n_pages, MAX_PAGES_PER_STEP))
    dsa_o_s = dsa_o_s.reshape(n_dec, DSA_KV_HEADS, group, rows, HEAD_DIM).transpose(0, 3, 1, 2, 4)
    dsa_o_s = dsa_o_s.reshape(m_s, DSA_HEADS * HEAD_DIM)

    h_s = _merge_mem(xs, ret_y_s, dsa_o_s, proj_s, w_ret_out_b, w_dsa_out_b, w_o_b, g_mem_r, w_mq_b,
                     cache_mem_k[0].reshape(n_dec * mem_len, MEM_W), cache_mem_v[0].reshape(n_dec * mem_len, MEM_W),
                     w_mo_b, tm=m_s, batch_tiles=0, mk_col=0, mv_col=0, nk=n_dec * mem_len, rows_per_batch=rows)
    y_s, w_up_b, w_down_b = _mlp(h_s, g_mlp_r, w_up[0], w_down[0], g_final_r, m_s, 512)
    y_p = _mlp(h_p, g_mlp_r, w_up_b, w_down_b, g_final_r, _pick_tile(seq, 1024), 512)

    def rows_p(col, width, tail):
        return proj_p[:, col:col + width].reshape((1, batch, seq) + tail)

    def rows_s(col, width, tail):
        return proj_s3[:, :t_dec, col:col + width].reshape((1, n_dec, t_dec) + tail)

    kv_p_rows = lambda a: a.reshape(1, batch, seq, DSA_KV_HEADS, HEAD_DIM)
    kv_s_rows = lambda a: a.reshape(1, n_dec, rows, DSA_KV_HEADS, HEAD_DIM)[:, :, :t_dec]
    return (
        y_p.reshape(batch, seq, d),
        y_s.reshape(n_dec, rows, d)[:, :t_dec],
        ret_state_p[None],
        kv_p_rows(k_rows_p),
        kv_p_rows(v_rows_p),
        rows_p(COL_IK, IDX_DH, (IDX_DH,)),
        kv_p[:, :MEM_W].reshape(1, batch, mem_len, MEM_HEADS, HEAD_DIM),
        kv_p[:, MEM_W:].reshape(1, batch, mem_len, MEM_HEADS, HEAD_DIM),
        ret_state_s[None],
        kv_s_rows(k_rows_s),
        kv_s_rows(v_rows_s),
        rows_s(COL_IK, IDX_DH, (IDX_DH,)),
    )
```
